```python
import jax, jax.numpy as jnp
from jax import lax
import numpy as np

D_MODEL = 2048
BATCH = 8
SEQ = 4096
DEPTH = 4

D_FF = 5632
POOL_WINDOWS = (2, 4, 8, 16)
N_POOL_GROUPS = len(POOL_WINDOWS)
POOL_WIDTH = 1024
POOL_GROUP_DIM = POOL_WIDTH // N_POOL_GROUPS
SGU_HEADS = 8
SGU_HEAD_DIM = 128
SGU_WIDTH = SGU_HEADS * SGU_HEAD_DIM
CHUNK = 128
IN_PROJ_WIDTH = POOL_WIDTH + 2 * SGU_WIDTH + 2 * D_MODEL
MACARON_WEIGHT = 0.5
EPS = 1e-6

kernel_name = "hybrid_pool_sgu_macaron_block"


def rmsnorm(x, g):
    xf = x.astype(jnp.float32)
    var = jnp.mean(xf * xf, axis=-1, keepdims=True)
    return (xf * lax.rsqrt(var + EPS)).astype(x.dtype) * g


def swiglu(h, w_up, w_down):
    gate, up = jnp.split(h @ w_up, 2, axis=-1)
    return (jax.nn.silu(gate) * up) @ w_down


def pool_mixer(p, w_group, scale):
    B, S, _ = p.shape
    maxw = POOL_WINDOWS[-1]
    pf = p.astype(jnp.float32)
    cs = jnp.cumsum(pf, axis=1)
    cs_pad = jnp.pad(cs, ((0, 0), (maxw, 0), (0, 0)))
    pos = jnp.arange(1, S + 1, dtype=jnp.int32)
    outs = []
    for g, w in enumerate(POOL_WINDOWS):
        sl = slice(g * POOL_GROUP_DIM, (g + 1) * POOL_GROUP_DIM)
        prev = cs_pad[:, maxw - w: maxw - w + S, sl]
        cnt = jnp.minimum(pos, w).astype(jnp.float32)[None, :, None]
        outs.append((cs[:, :, sl] - prev) / cnt - pf[:, :, sl])
    d = jnp.stack(outs, axis=2).astype(p.dtype)
    y = jnp.einsum('bsgc,gcd->bsgd', d, w_group)
    return y.reshape(B, S, POOL_WIDTH) * scale


def spatial_gating(u, v, v_gain, w_s, b_s):
    B, S, _ = u.shape
    n_chunks = S // CHUNK
    v = rmsnorm(v, v_gain)
    vc = v.reshape(B, n_chunks, CHUNK, SGU_HEADS, SGU_HEAD_DIM)
    w = w_s * jnp.tril(jnp.ones((CHUNK, CHUNK), dtype=w_s.dtype))
    s = jnp.einsum('hts,bnshc->bnthc', w, vc) + b_s.T[None, None, :, :, None]
    return u * s.reshape(B, S, SGU_WIDTH)


def _fwd_setup_inputs(seed: int = 0) -> dict:
    key = jax.random.key(seed)
    ks = jax.random.split(key, 32)
    L, D = DEPTH, D_MODEL

    def dense(k, shape, fan_in):
        return jax.random.normal(k, shape, jnp.float32) * (fan_in ** -0.5)

    def gain(k, shape):
        return 1.0 + 0.05 * jax.random.normal(k, shape, jnp.float32)

    return {
        "x": jax.random.normal(ks[0], (BATCH, SEQ, D), jnp.float32),
        "g_ffn1_pre": gain(ks[1], (L, D)),
        "w_ffn1_up": dense(ks[2], (L, D, 2 * D_FF), D),
        "w_ffn1_down": dense(ks[3], (L, D_FF, D), D_FF),
        "g_ffn1_post": gain(ks[4], (L, D)),
        "g_mix_pre": gain(ks[5], (L, D)),
        "w_in": dense(ks[6], (L, D, IN_PROJ_WIDTH), D),
        "pool_group_w": dense(ks[7], (L, N_POOL_GROUPS, POOL_GROUP_DIM, POOL_GROUP_DIM), POOL_GROUP_DIM),
        "pool_scale": gain(ks[8], (L, POOL_WIDTH)),
        "w_pool_out": dense(ks[9], (L, POOL_WIDTH, D), POOL_WIDTH),
        "sgu_v_gain": gain(ks[10], (L, SGU_WIDTH)),
        "sgu_w_s": dense(ks[11], (L, SGU_HEADS, CHUNK, CHUNK), CHUNK),
        "sgu_b_s": gain(ks[12], (L, SGU_HEADS, CHUNK)),
        "w_sgu_out": dense(ks[13], (L, SGU_WIDTH, D), SGU_WIDTH),
        "w_out": dense(ks[14], (L, D, D), D),
        "g_mix_post": gain(ks[15], (L, D)),
        "g_ffn2_pre": gain(ks[16], (L, D)),
        "w_ffn2_up": dense(ks[17], (L, D, 2 * D_FF), D),
        "w_ffn2_down": dense(ks[18], (L, D_FF, D), D_FF),
        "g_ffn2_post": gain(ks[19], (L, D)),
    }


def _fwd_reference(x, g_ffn1_pre, w_ffn1_up, w_ffn1_down, g_ffn1_post, g_mix_pre, w_in,
              pool_group_w, pool_scale, w_pool_out, sgu_v_gain, sgu_w_s, sgu_b_s,
              w_sgu_out, w_out, g_mix_post, g_ffn2_pre, w_ffn2_up, w_ffn2_down, g_ffn2_post):
    splits = (POOL_WIDTH, POOL_WIDTH + SGU_WIDTH, POOL_WIDTH + 2 * SGU_WIDTH,
              POOL_WIDTH + 2 * SGU_WIDTH + D_MODEL)
    for i in range(DEPTH):
        f = swiglu(rmsnorm(x, g_ffn1_pre[i]), w_ffn1_up[i], w_ffn1_down[i])
        x = x + MACARON_WEIGHT * rmsnorm(f, g_ffn1_post[i])

        h = rmsnorm(x, g_mix_pre[i])
        p, u, v, ga, gb = jnp.split(h @ w_in[i], splits, axis=-1)
        y_a = pool_mixer(p, pool_group_w[i], pool_scale[i]) @ w_pool_out[i]
        y_b = spatial_gating(jax.nn.gelu(u), jax.nn.gelu(v), sgu_v_gain[i],
                             sgu_w_s[i], sgu_b_s[i]) @ w_sgu_out[i]
        m = jax.nn.sigmoid(ga) * y_a + jax.nn.sigmoid(gb) * y_b
        x = x + rmsnorm(m @ w_out[i], g_mix_post[i])

        f = swiglu(rmsnorm(x, g_ffn2_pre[i]), w_ffn2_up[i], w_ffn2_down[i])
        x = x + MACARON_WEIGHT * rmsnorm(f, g_ffn2_post[i])
    return x


import jax as _jax
import jax.numpy as _jnp

TWIN_FORMAT = 'train_step'
FWD_PARAMS = ['x', 'g_ffn1_pre', 'w_ffn1_up', 'w_ffn1_down', 'g_ffn1_post', 'g_mix_pre', 'w_in', 'pool_group_w', 'pool_scale', 'w_pool_out', 'sgu_v_gain', 'sgu_w_s', 'sgu_b_s', 'w_sgu_out', 'w_out', 'g_mix_post', 'g_ffn2_pre', 'w_ffn2_up', 'w_ffn2_down', 'g_ffn2_post']
TWIN_WEIGHTS = ['g_ffn1_pre', 'w_ffn1_up', 'w_ffn1_down', 'g_ffn1_post', 'g_mix_pre', 'w_in', 'pool_group_w', 'pool_scale', 'w_pool_out', 'sgu_v_gain', 'sgu_w_s', 'sgu_b_s', 'w_sgu_out', 'w_out', 'g_mix_post', 'g_ffn2_pre', 'w_ffn2_up', 'w_ffn2_down', 'g_ffn2_post']
TWIN_DIFF_INPUT = 'x'
TWIN_INPUTS = ['x', 'g_ffn1_pre', 'w_ffn1_up', 'w_ffn1_down', 'g_ffn1_post', 'g_mix_pre', 'w_in', 'pool_group_w', 'pool_scale', 'w_pool_out', 'sgu_v_gain', 'sgu_w_s', 'sgu_b_s', 'w_sgu_out', 'w_out', 'g_mix_post', 'g_ffn2_pre', 'w_ffn2_up', 'w_ffn2_down', 'g_ffn2_post', 'loss_target', 'm_g_ffn1_pre', 'm_w_ffn1_up', 'm_w_ffn1_down', 'm_g_ffn1_post', 'm_g_mix_pre', 'm_w_in', 'm_pool_group_w', 'm_pool_scale', 'm_w_pool_out', 'm_sgu_v_gain', 'm_sgu_w_s', 'm_sgu_b_s', 'm_w_sgu_out', 'm_w_out', 'm_g_mix_post', 'm_g_ffn2_pre', 'm_w_ffn2_up', 'm_w_ffn2_down', 'm_g_ffn2_post', 'v_g_ffn1_pre', 'v_w_ffn1_up', 'v_w_ffn1_down', 'v_g_ffn1_post', 'v_g_mix_pre', 'v_w_in', 'v_pool_group_w', 'v_pool_scale', 'v_w_pool_out', 'v_sgu_v_gain', 'v_sgu_w_s', 'v_sgu_b_s', 'v_w_sgu_out', 'v_w_out', 'v_g_mix_post', 'v_g_ffn2_pre', 'v_w_ffn2_up', 'v_w_ffn2_down', 'v_g_ffn2_post']
TWIN_OUTPUTS = ['loss', 'grad_x', 'grad_g_ffn1_pre', 'grad_w_ffn1_up', 'grad_w_ffn1_down', 'grad_g_ffn1_post', 'grad_g_mix_pre', 'grad_w_in', 'grad_pool_group_w', 'grad_pool_scale', 'grad_w_pool_out', 'grad_sgu_v_gain', 'grad_sgu_w_s', 'grad_sgu_b_s', 'grad_w_sgu_out', 'grad_w_out', 'grad_g_mix_post', 'grad_g_ffn2_pre', 'grad_w_ffn2_up', 'grad_w_ffn2_down', 'grad_g_ffn2_post', 'delta_g_ffn1_pre', 'delta_w_ffn1_up', 'delta_w_ffn1_down', 'delta_g_ffn1_post', 'delta_g_mix_pre', 'delta_w_in', 'delta_pool_group_w', 'delta_pool_scale', 'delta_w_pool_out', 'delta_sgu_v_gain', 'delta_sgu_w_s', 'delta_sgu_b_s', 'delta_w_sgu_out', 'delta_w_out', 'delta_g_mix_post', 'delta_g_ffn2_pre', 'delta_w_ffn2_up', 'delta_w_ffn2_down', 'delta_g_ffn2_post', 'new_m_g_ffn1_pre', 'new_m_w_ffn1_up', 'new_m_w_ffn1_down', 'new_m_g_ffn1_post', 'new_m_g_mix_pre', 'new_m_w_in', 'new_m_pool_group_w', 'new_m_pool_scale', 'new_m_w_pool_out', 'new_m_sgu_v_gain', 'new_m_sgu_w_s', 'new_m_sgu_b_s', 'new_m_w_sgu_out', 'new_m_w_out', 'new_m_g_mix_post', 'new_m_g_ffn2_pre', 'new_m_w_ffn2_up', 'new_m_w_ffn2_down', 'new_m_g_ffn2_post', 'new_v_g_ffn1_pre', 'new_v_w_ffn1_up', 'new_v_w_ffn1_down', 'new_v_g_ffn1_post', 'new_v_g_mix_pre', 'new_v_w_in', 'new_v_pool_group_w', 'new_v_pool_scale', 'new_v_w_pool_out', 'new_v_sgu_v_gain', 'new_v_sgu_w_s', 'new_v_sgu_b_s', 'new_v_w_sgu_out', 'new_v_w_out', 'new_v_g_mix_post', 'new_v_g_ffn2_pre', 'new_v_w_ffn2_up', 'new_v_w_ffn2_down', 'new_v_g_ffn2_post']
TWIN_LEAF_KINDS = {'loss': 'loss', 'grad_x': 'grad_x', 'grad_g_ffn1_pre': 'grad_w', 'grad_w_ffn1_up': 'grad_w', 'grad_w_ffn1_down': 'grad_w', 'grad_g_ffn1_post': 'grad_w', 'grad_g_mix_pre': 'grad_w', 'grad_w_in': 'grad_w', 'grad_pool_group_w': 'grad_w', 'grad_pool_scale': 'grad_w', 'grad_w_pool_out': 'grad_w', 'grad_sgu_v_gain': 'grad_w', 'grad_sgu_w_s': 'grad_w', 'grad_sgu_b_s': 'grad_w', 'grad_w_sgu_out': 'grad_w', 'grad_w_out': 'grad_w', 'grad_g_mix_post': 'grad_w', 'grad_g_ffn2_pre': 'grad_w', 'grad_w_ffn2_up': 'grad_w', 'grad_w_ffn2_down': 'grad_w', 'grad_g_ffn2_post': 'grad_w', 'delta_g_ffn1_pre': 'delta_w', 'delta_w_ffn1_up': 'delta_w', 'delta_w_ffn1_down': 'delta_w', 'delta_g_ffn1_post': 'delta_w', 'delta_g_mix_pre': 'delta_w', 'delta_w_in': 'delta_w', 'delta_pool_group_w': 'delta_w', 'delta_pool_scale': 'delta_w', 'delta_w_pool_out': 'delta_w', 'delta_sgu_v_gain': 'delta_w', 'delta_sgu_w_s': 'delta_w', 'delta_sgu_b_s': 'delta_w', 'delta_w_sgu_out': 'delta_w', 'delta_w_out': 'delta_w', 'delta_g_mix_post': 'delta_w', 'delta_g_ffn2_pre': 'delta_w', 'delta_w_ffn2_up': 'delta_w', 'delta_w_ffn2_down': 'delta_w', 'delta_g_ffn2_post': 'delta_w', 'new_m_g_ffn1_pre': 'new_m', 'new_m_w_ffn1_up': 'new_m', 'new_m_w_ffn1_down': 'new_m', 'new_m_g_ffn1_post': 'new_m', 'new_m_g_mix_pre': 'new_m', 'new_m_w_in': 'new_m', 'new_m_pool_group_w': 'new_m', 'new_m_pool_scale': 'new_m', 'new_m_w_pool_out': 'new_m', 'new_m_sgu_v_gain': 'new_m', 'new_m_sgu_w_s': 'new_m', 'new_m_sgu_b_s': 'new_m', 'new_m_w_sgu_out': 'new_m', 'new_m_w_out': 'new_m', 'new_m_g_mix_post': 'new_m', 'new_m_g_ffn2_pre': 'new_m', 'new_m_w_ffn2_up': 'new_m', 'new_m_w_ffn2_down': 'new_m', 'new_m_g_ffn2_post': 'new_m', 'new_v_g_ffn1_pre': 'new_v', 'new_v_w_ffn1_up': 'new_v', 'new_v_w_ffn1_down': 'new_v', 'new_v_g_ffn1_post': 'new_v', 'new_v_g_mix_pre': 'new_v', 'new_v_w_in': 'new_v', 'new_v_pool_group_w': 'new_v', 'new_v_pool_scale': 'new_v', 'new_v_w_pool_out': 'new_v', 'new_v_sgu_v_gain': 'new_v', 'new_v_sgu_w_s': 'new_v', 'new_v_sgu_b_s': 'new_v', 'new_v_w_sgu_out': 'new_v', 'new_v_w_out': 'new_v', 'new_v_g_mix_post': 'new_v', 'new_v_g_ffn2_pre': 'new_v', 'new_v_w_ffn2_up': 'new_v', 'new_v_w_ffn2_down': 'new_v', 'new_v_g_ffn2_post': 'new_v'}


def _forward(args):
    return _fwd_reference(*[args[k] for k in FWD_PARAMS])


def _output_shape():
    def fwd():
        inp = _fwd_setup_inputs(0)
        return _fwd_reference(*[inp[k] for k in FWD_PARAMS])
    out = _jax.eval_shape(fwd)
    return out.shape, out.dtype

N_MICROBATCH = 1
ADAM_LR = 0.001
ADAM_B1 = 0.9
ADAM_B2 = 0.999
ADAM_EPS = 1e-08
ADAM_WD = 0.01
ADAM_STEP = 10
PER_EXAMPLE_BATCH_AXIS = {'x': 0, 'loss_target': 0}
SHARED_INPUTS = []
_WEIGHT_DTYPES = {'g_ffn1_pre': _jnp.float32, 'w_ffn1_up': _jnp.float32, 'w_ffn1_down': _jnp.float32, 'g_ffn1_post': _jnp.float32, 'g_mix_pre': _jnp.float32, 'w_in': _jnp.float32, 'pool_group_w': _jnp.float32, 'pool_scale': _jnp.float32, 'w_pool_out': _jnp.float32, 'sgu_v_gain': _jnp.float32, 'sgu_w_s': _jnp.float32, 'sgu_b_s': _jnp.float32, 'w_sgu_out': _jnp.float32, 'w_out': _jnp.float32, 'g_mix_post': _jnp.float32, 'g_ffn2_pre': _jnp.float32, 'w_ffn2_up': _jnp.float32, 'w_ffn2_down': _jnp.float32, 'g_ffn2_post': _jnp.float32}
MOMENT_SCALE = {'g_ffn1_pre': 4.701017e-01, 'w_ffn1_up': 1.926269e-01, 'w_ffn1_down': 3.455249e-01, 'g_ffn1_post': 3.936478e+00, 'g_mix_pre': 7.774517e-01, 'w_in': 4.131424e-01, 'pool_group_w': 8.354734e-01, 'pool_scale': 8.493333e-01, 'w_pool_out': 6.121716e-01, 'sgu_v_gain': 2.255897e-01, 'sgu_w_s': 2.209092e-01, 'sgu_b_s': 3.353792e-01, 'w_sgu_out': 1.458939e+00, 'w_out': 1.560558e+00, 'g_mix_post': 1.612795e+01, 'g_ffn2_pre': 3.957072e-01, 'w_ffn2_up': 1.668895e-01, 'w_ffn2_down': 3.224065e-01, 'g_ffn2_post': 3.979748e+00}


def _to_microbatches(a, axis):
    t = _jnp.moveaxis(a, axis, 0)
    t = t.reshape((N_MICROBATCH, t.shape[0] // N_MICROBATCH) + t.shape[1:])
    return _jnp.moveaxis(t, 1, axis + 1)


def setup_inputs(seed: int = 0) -> dict:
    inp = _fwd_setup_inputs(seed)
    key = _jax.random.fold_in(_jax.random.key(seed), 7919)
    shape, _ = _output_shape()
    out = dict(inp)
    out["loss_target"] = _jax.random.normal(_jax.random.fold_in(key, 0), shape, _jnp.float32)
    for i, name in enumerate(TWIN_WEIGHTS):
        w = inp[name].astype(_jnp.float32)
        if MOMENT_SCALE is None:
            s = _jnp.sqrt(_jnp.mean(_jnp.square(w)) + 1e-30)
        else:
            s = MOMENT_SCALE[name]
        km, kv = _jax.random.split(_jax.random.fold_in(key, i + 1))
        out[name] = w
        out["m_" + name] = s * _jax.random.normal(km, w.shape, _jnp.float32)
        out["v_" + name] = (s * s) * _jax.random.uniform(kv, w.shape, _jnp.float32, 0.5, 1.5)
    if N_MICROBATCH > 1:
        for name, axis in PER_EXAMPLE_BATCH_AXIS.items():
            out[name] = _to_microbatches(out[name], axis)
    return {'x': out['x'], 'g_ffn1_pre': out['g_ffn1_pre'], 'w_ffn1_up': out['w_ffn1_up'], 'w_ffn1_down': out['w_ffn1_down'], 'g_ffn1_post': out['g_ffn1_post'], 'g_mix_pre': out['g_mix_pre'], 'w_in': out['w_in'], 'pool_group_w': out['pool_group_w'], 'pool_scale': out['pool_scale'], 'w_pool_out': out['w_pool_out'], 'sgu_v_gain': out['sgu_v_gain'], 'sgu_w_s': out['sgu_w_s'], 'sgu_b_s': out['sgu_b_s'], 'w_sgu_out': out['w_sgu_out'], 'w_out': out['w_out'], 'g_mix_post': out['g_mix_post'], 'g_ffn2_pre': out['g_ffn2_pre'], 'w_ffn2_up': out['w_ffn2_up'], 'w_ffn2_down': out['w_ffn2_down'], 'g_ffn2_post': out['g_ffn2_post'], 'loss_target': out['loss_target'], 'm_g_ffn1_pre': out['m_g_ffn1_pre'], 'm_w_ffn1_up': out['m_w_ffn1_up'], 'm_w_ffn1_down': out['m_w_ffn1_down'], 'm_g_ffn1_post': out['m_g_ffn1_post'], 'm_g_mix_pre': out['m_g_mix_pre'], 'm_w_in': out['m_w_in'], 'm_pool_group_w': out['m_pool_group_w'], 'm_pool_scale': out['m_pool_scale'], 'm_w_pool_out': out['m_w_pool_out'], 'm_sgu_v_gain': out['m_sgu_v_gain'], 'm_sgu_w_s': out['m_sgu_w_s'], 'm_sgu_b_s': out['m_sgu_b_s'], 'm_w_sgu_out': out['m_w_sgu_out'], 'm_w_out': out['m_w_out'], 'm_g_mix_post': out['m_g_mix_post'], 'm_g_ffn2_pre': out['m_g_ffn2_pre'], 'm_w_ffn2_up': out['m_w_ffn2_up'], 'm_w_ffn2_down': out['m_w_ffn2_down'], 'm_g_ffn2_post': out['m_g_ffn2_post'], 'v_g_ffn1_pre': out['v_g_ffn1_pre'], 'v_w_ffn1_up': out['v_w_ffn1_up'], 'v_w_ffn1_down': out['v_w_ffn1_down'], 'v_g_ffn1_post': out['v_g_ffn1_post'], 'v_g_mix_pre': out['v_g_mix_pre'], 'v_w_in': out['v_w_in'], 'v_pool_group_w': out['v_pool_group_w'], 'v_pool_scale': out['v_pool_scale'], 'v_w_pool_out': out['v_w_pool_out'], 'v_sgu_v_gain': out['v_sgu_v_gain'], 'v_sgu_w_s': out['v_sgu_w_s'], 'v_sgu_b_s': out['v_sgu_b_s'], 'v_w_sgu_out': out['v_w_sgu_out'], 'v_w_out': out['v_w_out'], 'v_g_mix_post': out['v_g_mix_post'], 'v_g_ffn2_pre': out['v_g_ffn2_pre'], 'v_w_ffn2_up': out['v_w_ffn2_up'], 'v_w_ffn2_down': out['v_w_ffn2_down'], 'v_g_ffn2_post': out['v_g_ffn2_post']}


def _loss(weights, diff, rest, loss_target):
    with _jax.named_scope("forward"):
        args = {**rest, TWIN_DIFF_INPUT: diff, **{k: w.astype(_WEIGHT_DTYPES[k]) for k, w in weights.items()}}
        y = _forward(args)
    with _jax.named_scope("loss_head"):
        err = _jnp.square(y.astype(_jnp.float32) - loss_target)
        return 0.5 * _jnp.sum(_jnp.mean(err, axis=-1)) if err.ndim else 0.5 * err


def _adamw(w, g, m, v):
    m = ADAM_B1 * m + (1.0 - ADAM_B1) * g
    v = ADAM_B2 * v + (1.0 - ADAM_B2) * _jnp.square(g)
    m_hat = m / (1.0 - ADAM_B1 ** ADAM_STEP)
    v_hat = v / (1.0 - ADAM_B2 ** ADAM_STEP)
    delta = -ADAM_LR * (m_hat / (_jnp.sqrt(v_hat) + ADAM_EPS) + ADAM_WD * w)
    return delta, m, v


def reference(x, g_ffn1_pre, w_ffn1_up, w_ffn1_down, g_ffn1_post, g_mix_pre, w_in, pool_group_w, pool_scale, w_pool_out, sgu_v_gain, sgu_w_s, sgu_b_s, w_sgu_out, w_out, g_mix_post, g_ffn2_pre, w_ffn2_up, w_ffn2_down, g_ffn2_post, loss_target, m_g_ffn1_pre, m_w_ffn1_up, m_w_ffn1_down, m_g_ffn1_post, m_g_mix_pre, m_w_in, m_pool_group_w, m_pool_scale, m_w_pool_out, m_sgu_v_gain, m_sgu_w_s, m_sgu_b_s, m_w_sgu_out, m_w_out, m_g_mix_post, m_g_ffn2_pre, m_w_ffn2_up, m_w_ffn2_down, m_g_ffn2_post, v_g_ffn1_pre, v_w_ffn1_up, v_w_ffn1_down, v_g_ffn1_post, v_g_mix_pre, v_w_in, v_pool_group_w, v_pool_scale, v_w_pool_out, v_sgu_v_gain, v_sgu_w_s, v_sgu_b_s, v_w_sgu_out, v_w_out, v_g_mix_post, v_g_ffn2_pre, v_w_ffn2_up, v_w_ffn2_down, v_g_ffn2_post):
    given = dict(x=x, g_ffn1_pre=g_ffn1_pre, w_ffn1_up=w_ffn1_up, w_ffn1_down=w_ffn1_down, g_ffn1_post=g_ffn1_post, g_mix_pre=g_mix_pre, w_in=w_in, pool_group_w=pool_group_w, pool_scale=pool_scale, w_pool_out=w_pool_out, sgu_v_gain=sgu_v_gain, sgu_w_s=sgu_w_s, sgu_b_s=sgu_b_s, w_sgu_out=w_sgu_out, w_out=w_out, g_mix_post=g_mix_post, g_ffn2_pre=g_ffn2_pre, w_ffn2_up=w_ffn2_up, w_ffn2_down=w_ffn2_down, g_ffn2_post=g_ffn2_post, loss_target=loss_target, m_g_ffn1_pre=m_g_ffn1_pre, m_w_ffn1_up=m_w_ffn1_up, m_w_ffn1_down=m_w_ffn1_down, m_g_ffn1_post=m_g_ffn1_post, m_g_mix_pre=m_g_mix_pre, m_w_in=m_w_in, m_pool_group_w=m_pool_group_w, m_pool_scale=m_pool_scale, m_w_pool_out=m_w_pool_out, m_sgu_v_gain=m_sgu_v_gain, m_sgu_w_s=m_sgu_w_s, m_sgu_b_s=m_sgu_b_s, m_w_sgu_out=m_w_sgu_out, m_w_out=m_w_out, m_g_mix_post=m_g_mix_post, m_g_ffn2_pre=m_g_ffn2_pre, m_w_ffn2_up=m_w_ffn2_up, m_w_ffn2_down=m_w_ffn2_down, m_g_ffn2_post=m_g_ffn2_post, v_g_ffn1_pre=v_g_ffn1_pre, v_w_ffn1_up=v_w_ffn1_up, v_w_ffn1_down=v_w_ffn1_down, v_g_ffn1_post=v_g_ffn1_post, v_g_mix_pre=v_g_mix_pre, v_w_in=v_w_in, v_pool_group_w=v_pool_group_w, v_pool_scale=v_pool_scale, v_w_pool_out=v_w_pool_out, v_sgu_v_gain=v_sgu_v_gain, v_sgu_w_s=v_sgu_w_s, v_sgu_b_s=v_sgu_b_s, v_w_sgu_out=v_w_sgu_out, v_w_out=v_w_out, v_g_mix_post=v_g_mix_post, v_g_ffn2_pre=v_g_ffn2_pre, v_w_ffn2_up=v_w_ffn2_up, v_w_ffn2_down=v_w_ffn2_down, v_g_ffn2_post=v_g_ffn2_post)
    weights = {n: given[n] for n in TWIN_WEIGHTS}
    shared = {n: given[n] for n in SHARED_INPUTS}
    per_example = {n: given[n] for n in ['x']}
    grad_fn = _jax.value_and_grad(_loss, argnums=(0, 1))

    def one_microbatch(ex, loss_target):
        ex = dict(ex)
        diff = ex.pop(TWIN_DIFF_INPUT)
        return grad_fn(weights, diff, {**shared, **ex}, loss_target)

    if N_MICROBATCH == 1:
        loss, (grad_w, grad_x) = one_microbatch(per_example, given["loss_target"])
    else:
        def body(carry, xs):
            loss_sum, grad_sum = carry
            l_k, (gw_k, gx_k) = one_microbatch(xs[0], xs[1])
            with _jax.named_scope("update"):
                return (loss_sum + l_k, _jax.tree.map(_jnp.add, grad_sum, gw_k)), gx_k

        init = (_jnp.zeros((), _jnp.float32), _jax.tree.map(_jnp.zeros_like, weights))
        (loss, grad_w), grad_x = _jax.lax.scan(body, init, (per_example, given["loss_target"]))
    with _jax.named_scope("update"):
        delta_w, new_m, new_v = {}, {}, {}
        for n in TWIN_WEIGHTS:
            delta_w[n], new_m[n], new_v[n] = _adamw(weights[n], grad_w[n], given["m_" + n], given["v_" + n])
    return (loss, grad_x, *[grad_w[n] for n in TWIN_WEIGHTS], *[delta_w[n] for n in TWIN_WEIGHTS],
            *[new_m[n] for n in TWIN_WEIGHTS], *[new_v[n] for n in TWIN_WEIGHTS])
```

```python
import math

import jax
import jax.numpy as jnp
from jax import lax
from jax.experimental import pallas as pl
from jax.experimental.pallas import tpu as pltpu

F32 = jnp.float32
BF16 = jnp.bfloat16
MESH = pl.DeviceIdType.MESH

EPS = 1e-6
MACARON_WEIGHT = 0.5
POOL_WINDOWS = (2, 4, 8, 16)
POOL_HALO = 16
ADAM_LR = 0.001
ADAM_B1 = 0.9
ADAM_B2 = 0.999
ADAM_EPS = 1e-08
ADAM_WD = 0.01
ADAM_STEP = 10
GELU_K = math.sqrt(2.0 / math.pi)
GELU_C = 0.044715

N_CHIPS = 4
N_DEV = 8
V7X_VMEM_BYTES = 64 * 1024 * 1024
VMEM_LIMIT = (V7X_VMEM_BYTES * 3) // 4
LANE = 128
SUBLANE = 8

WEIGHTS = ['g_ffn1_pre', 'w_ffn1_up', 'w_ffn1_down', 'g_ffn1_post', 'g_mix_pre', 'w_in', 'pool_group_w',
           'pool_scale', 'w_pool_out', 'sgu_v_gain', 'sgu_w_s', 'sgu_b_s', 'w_sgu_out', 'w_out', 'g_mix_post',
           'g_ffn2_pre', 'w_ffn2_up', 'w_ffn2_down', 'g_ffn2_post']
BIG_AXIS = {'w_ffn1_up': 1, 'w_ffn1_down': 0, 'w_in': 1, 'pool_group_w': 1, 'w_pool_out': 1, 'w_sgu_out': 1,
            'w_out': 0, 'w_ffn2_up': 1, 'w_ffn2_down': 0}
BIG = list(BIG_AXIS)
SMALL = [n for n in WEIGHTS if n not in BIG_AXIS]


def _pick(dim, cands):
    for c in cands:
        if dim % c == 0:
            return c
    return dim


def _params(*sem):
    return pltpu.CompilerParams(dimension_semantics=sem if sem else None, vmem_limit_bytes=VMEM_LIMIT)


def _sigmoid(x):
    return 1.0 / (1.0 + jnp.exp(-x))


def _gelu(x):
    t = jnp.tanh(GELU_K * (x + GELU_C * (x * x * x)))
    return x * (0.5 * (1.0 + t)), t


def _gelu_grad(x, t):
    return 0.5 * (1.0 + t) + (0.5 * x) * (1.0 - t * t) * (GELU_K * (1.0 + (3.0 * GELU_C) * (x * x)))


def matmul(a, b, *, ta=False, tb=False, out_dtype=BF16, name):
    m_dim, k_dim = (a.shape[1], a.shape[0]) if ta else a.shape
    n_dim = b.shape[0] if tb else b.shape[1]
    tm = _pick(m_dim, (1024, 512, 256, 128))
    tn = _pick(n_dim, (1024, 512, 256, 128))
    tk = _pick(k_dim, (1024, 512, 256, 128))
    nk = k_dim // tk
    dims = (((0 if ta else 1,), (1 if tb else 0,)), ((), ()))

    def body(a_ref, b_ref, o_ref, acc_ref):
        k = pl.program_id(2)

        @pl.when(k == 0)
        def _():
            acc_ref[...] = jnp.zeros_like(acc_ref)

        acc_ref[...] += lax.dot_general(a_ref[...], b_ref[...], dims, preferred_element_type=F32)

        @pl.when(k == nk - 1)
        def _():
            o_ref[...] = acc_ref[...].astype(o_ref.dtype)

    a_spec = pl.BlockSpec((tk, tm), lambda i, j, k: (k, i)) if ta else pl.BlockSpec((tm, tk), lambda i, j, k: (i, k))
    b_spec = pl.BlockSpec((tn, tk), lambda i, j, k: (j, k)) if tb else pl.BlockSpec((tk, tn), lambda i, j, k: (k, j))
    return pl.pallas_call(
        body, name=name, grid=(m_dim // tm, n_dim // tn, nk),
        in_specs=[a_spec, b_spec], out_specs=pl.BlockSpec((tm, tn), lambda i, j, k: (i, j)),
        out_shape=jax.ShapeDtypeStruct((m_dim, n_dim), out_dtype),
        scratch_shapes=[pltpu.VMEM((tm, tn), F32)],
        compiler_params=_params("parallel", "parallel", "arbitrary"),
    )(a, b)


ROW_TILES = (256, 128, 64, 32, 16, 8)


def _row(tr, width):
    return pl.BlockSpec((tr, width), lambda i: (i, 0))


def _vec(width):
    return pl.BlockSpec((1, width), lambda i: (0, 0))


def rms_fwd(x, g, name):
    t_dim, d = x.shape
    tr = _pick(t_dim, ROW_TILES)

    def body(x_ref, g_ref, o_ref):
        xv = x_ref[...]
        r = lax.rsqrt(jnp.mean(xv * xv, axis=-1, keepdims=True) + EPS)
        o_ref[...] = ((xv * r) * g_ref[...]).astype(o_ref.dtype)

    return pl.pallas_call(
        body, name=name, grid=(t_dim // tr,), in_specs=[_row(tr, d), _vec(d)], out_specs=_row(tr, d),
        out_shape=jax.ShapeDtypeStruct((t_dim, d), BF16), compiler_params=_params("parallel"),
    )(x, g.reshape(1, d))


def res_rms_fwd(x, f, g, weight, name):
    t_dim, d = x.shape
    tr = _pick(t_dim, ROW_TILES)

    def body(x_ref, f_ref, g_ref, o_ref):
        fv = f_ref[...]
        r = lax.rsqrt(jnp.mean(fv * fv, axis=-1, keepdims=True) + EPS)
        o_ref[...] = x_ref[...] + weight * ((fv * r) * g_ref[...])

    return pl.pallas_call(
        body, name=name, grid=(t_dim // tr,), in_specs=[_row(tr, d), _row(tr, d), _vec(d)], out_specs=_row(tr, d),
        out_shape=jax.ShapeDtypeStruct((t_dim, d), F32), compiler_params=_params("parallel"),
    )(x, f, g.reshape(1, d))


def rms_bwd(f, g, dy, weight, resid, out_dtype, name):
    t_dim, d = f.shape
    tr = _pick(t_dim, ROW_TILES)
    has_resid = resid is not None

    def body(*refs):
        if has_resid:
            f_ref, g_ref, dy_ref, res_ref, o_ref, dg_ref = refs
        else:
            f_ref, g_ref, dy_ref, o_ref, dg_ref = refs

        @pl.when(pl.program_id(0) == 0)
        def _():
            dg_ref[...] = jnp.zeros_like(dg_ref)

        fv = f_ref[...]
        r = lax.rsqrt(jnp.mean(fv * fv, axis=-1, keepdims=True) + EPS)
        n = fv * r
        dyw = dy_ref[...] * weight
        dn = dyw * g_ref[...]
        df = r * (dn - n * jnp.mean(dn * n, axis=-1, keepdims=True))
        if has_resid:
            df = df + res_ref[...]
        o_ref[...] = df.astype(o_ref.dtype)
        dg_ref[...] += jnp.sum(dyw * n, axis=0, keepdims=True)

    ins = [f, g.reshape(1, d), dy] + ([resid] if has_resid else [])
    in_specs = [_row(tr, d), _vec(d), _row(tr, d)] + ([_row(tr, d)] if has_resid else [])
    out, dg = pl.pallas_call(
        body, name=name, grid=(t_dim // tr,), in_specs=in_specs, out_specs=[_row(tr, d), _vec(d)],
        out_shape=[jax.ShapeDtypeStruct((t_dim, d), out_dtype), jax.ShapeDtypeStruct((1, d), F32)],
        compiler_params=_params("arbitrary"),
    )(*ins)
    return out, dg.reshape(d)


def loss_grad(y, target, name):
    t_dim, d = y.shape
    tr = _pick(t_dim, ROW_TILES)
    inv_d = 1.0 / d

    def body(y_ref, t_ref, dy_ref, s_ref):
        @pl.when(pl.program_id(0) == 0)
        def _():
            s_ref[...] = jnp.zeros_like(s_ref)

        e = y_ref[...] - t_ref[...]
        dy_ref[...] = e * inv_d
        s_ref[...] += jnp.sum(e * e)

    dy, s = pl.pallas_call(
        body, name=name, grid=(t_dim // tr,), in_specs=[_row(tr, d), _row(tr, d)],
        out_specs=[_row(tr, d), pl.BlockSpec((SUBLANE, LANE), lambda i: (0, 0))],
        out_shape=[jax.ShapeDtypeStruct((t_dim, d), F32), jax.ShapeDtypeStruct((SUBLANE, LANE), F32)],
        compiler_params=_params("arbitrary"),
    )(y, target)
    return dy, s[0, 0]


def swiglu_fwd(z, name):
    t_dim, f2 = z.shape
    f = f2 // 2
    tr = _pick(t_dim, (512,) + ROW_TILES)
    tc = _pick(f, (512, 256, 128))
    nf = f // tc

    def body(g_ref, u_ref, a_ref):
        g = g_ref[...].astype(F32)
        a_ref[...] = (g * _sigmoid(g) * u_ref[...].astype(F32)).astype(a_ref.dtype)

    return pl.pallas_call(
        body, name=name, grid=(t_dim // tr, nf),
        in_specs=[pl.BlockSpec((tr, tc), lambda i, j: (i, j)), pl.BlockSpec((tr, tc), lambda i, j: (i, j + nf))],
        out_specs=pl.BlockSpec((tr, tc), lambda i, j: (i, j)),
        out_shape=jax.ShapeDtypeStruct((t_dim, f), BF16), compiler_params=_params("parallel", "parallel"),
    )(z, z)


def swiglu_bwd(z, da, name):
    t_dim, f2 = z.shape
    f = f2 // 2
    tr = _pick(t_dim, (512,) + ROW_TILES)
    tc = _pick(f, (512, 256, 128))
    nf = f // tc

    def body(g_ref, u_ref, da_ref, o_ref):
        j = pl.program_id(1)
        g = g_ref[...].astype(F32)
        da = da_ref[...].astype(F32)
        s = _sigmoid(g)

        @pl.when(j < nf)
        def _():
            o_ref[...] = (da * u_ref[...].astype(F32) * (s * (1.0 + g * (1.0 - s)))).astype(o_ref.dtype)

        @pl.when(j >= nf)
        def _():
            o_ref[...] = (da * (g * s)).astype(o_ref.dtype)

    return pl.pallas_call(
        body, name=name, grid=(t_dim // tr, 2 * nf),
        in_specs=[pl.BlockSpec((tr, tc), lambda i, j: (i, j % nf)),
                  pl.BlockSpec((tr, tc), lambda i, j: (i, j % nf + nf)),
                  pl.BlockSpec((tr, tc), lambda i, j: (i, j % nf))],
        out_specs=pl.BlockSpec((tr, tc), lambda i, j: (i, j)),
        out_shape=jax.ShapeDtypeStruct((t_dim, f2), BF16), compiler_params=_params("parallel", "parallel"),
    )(z, z, da)


def gate_fwd(z, ya, yb, off_a, off_b, name):
    t_dim, d = ya.shape
    tr = _pick(t_dim, (512,) + ROW_TILES)
    tc = math.gcd(math.gcd(off_a, off_b), _pick(d, (512, 256, 128)))
    ja, jb = off_a // tc, off_b // tc

    def body(ga_ref, gb_ref, ya_ref, yb_ref, m_ref):
        m = _sigmoid(ga_ref[...]) * ya_ref[...].astype(F32) + _sigmoid(gb_ref[...]) * yb_ref[...].astype(F32)
        m_ref[...] = m.astype(m_ref.dtype)

    blk = pl.BlockSpec((tr, tc), lambda i, j: (i, j))
    return pl.pallas_call(
        body, name=name, grid=(t_dim // tr, d // tc),
        in_specs=[pl.BlockSpec((tr, tc), lambda i, j: (i, j + ja)), pl.BlockSpec((tr, tc), lambda i, j: (i, j + jb)),
                  blk, blk],
        out_specs=blk, out_shape=jax.ShapeDtypeStruct((t_dim, d), BF16),
        compiler_params=_params("parallel", "parallel"),
    )(z, z, ya, yb)


def gate_bwd(z, ya, yb, dm, off_a, off_b, name):
    t_dim, d = ya.shape
    tr = _pick(t_dim, (512,) + ROW_TILES)
    tc = math.gcd(math.gcd(off_a, off_b), _pick(d, (512, 256, 128)))
    ja, jb = off_a // tc, off_b // tc

    def body(ga_ref, gb_ref, ya_ref, yb_ref, dm_ref, dya_ref, dyb_ref, dga_ref, dgb_ref):
        dm = dm_ref[...].astype(F32)
        sa = _sigmoid(ga_ref[...])
        sb = _sigmoid(gb_ref[...])
        dya_ref[...] = (dm * sa).astype(BF16)
        dyb_ref[...] = (dm * sb).astype(BF16)
        dga_ref[...] = (dm * ya_ref[...].astype(F32) * (sa * (1.0 - sa))).astype(BF16)
        dgb_ref[...] = (dm * yb_ref[...].astype(F32) * (sb * (1.0 - sb))).astype(BF16)

    blk = pl.BlockSpec((tr, tc), lambda i, j: (i, j))
    out = jax.ShapeDtypeStruct((t_dim, d), BF16)
    return pl.pallas_call(
        body, name=name, grid=(t_dim // tr, d // tc),
        in_specs=[pl.BlockSpec((tr, tc), lambda i, j: (i, j + ja)), pl.BlockSpec((tr, tc), lambda i, j: (i, j + jb)),
                  blk, blk, blk],
        out_specs=[blk] * 4, out_shape=[out] * 4, compiler_params=_params("parallel", "parallel"),
    )(z, z, ya, yb, dm)


def _window_sums(e, n_rows, forward):
    def shifted(v, k):
        return pltpu.roll(v, (n_rows - k) if forward else k, 0)

    s2 = e + shifted(e, 1)
    s4 = s2 + shifted(s2, 2)
    s8 = s4 + shifted(s4, 4)
    s16 = s8 + shifted(s8, 8)
    return (s2, s4, s8, s16)


def _pool_rows(t_dim):
    return _pick(t_dim, (256, 128, 64, 32, 16))


def pool_fwd(z, w_group, scale, pw, name):
    t_dim = z.shape[0]
    n_groups, c, _ = w_group.shape
    tr = _pool_rows(t_dim)
    per = tr // POOL_HALO

    def body(cur_ref, prev_ref, w_ref, scale_ref, d_ref, e_ref, yp_ref):
        i = pl.program_id(0)
        cur = cur_ref[...]
        prev = jnp.where(i > 0, prev_ref[...], 0.0)
        ext = jnp.concatenate([prev, cur], axis=0)
        sums = _window_sums(ext, tr + POOL_HALO, forward=False)
        pos = (i * tr + 1 + lax.broadcasted_iota(jnp.int32, (tr, 1), 0)).astype(F32)
        for g, w in enumerate(POOL_WINDOWS):
            cols = slice(g * c, (g + 1) * c)
            cnt = jnp.minimum(pos, float(w))
            d = (sums[g][POOL_HALO:, cols] / cnt - cur[:, cols]).astype(BF16)
            e = jnp.dot(d, w_ref[g], preferred_element_type=F32)
            d_ref[:, cols] = d
            e_ref[:, cols] = e.astype(BF16)
            yp_ref[:, cols] = (e * scale_ref[:, cols]).astype(BF16)

    out = jax.ShapeDtypeStruct((t_dim, pw), BF16)
    return pl.pallas_call(
        body, name=name, grid=(t_dim // tr,),
        in_specs=[_row(tr, pw), pl.BlockSpec((POOL_HALO, pw), lambda i: (jnp.maximum(i * per - 1, 0), 0)),
                  pl.BlockSpec((n_groups, c, c), lambda i: (0, 0, 0)), _vec(pw)],
        out_specs=[_row(tr, pw)] * 3, out_shape=[out] * 3, compiler_params=_params("parallel"),
    )(z, z, w_group, scale.reshape(1, pw))


def pool_bwd(dyp, e, d, w_group, scale, name):
    t_dim, pw = dyp.shape
    n_groups, c, _ = w_group.shape
    tr = _pool_rows(t_dim)
    per = tr // POOL_HALO
    n_tiles = t_dim // tr
    last_halo = t_dim // POOL_HALO - 1
    nt_dims = (((1,), (1,)), ((), ()))
    tn_dims = (((0,), (0,)), ((), ()))

    def body(dyp_ref, nxt_ref, e_ref, d_ref, w_ref, scale_ref, dp_ref, dw_ref, dscale_ref):
        i = pl.program_id(0)

        @pl.when(i == 0)
        def _():
            dw_ref[...] = jnp.zeros_like(dw_ref)
            dscale_ref[...] = jnp.zeros_like(dscale_ref)

        dyp_v = dyp_ref[...].astype(F32)
        dscale_ref[...] += jnp.sum(dyp_v * e_ref[...].astype(F32), axis=0, keepdims=True)
        de_cur = dyp_v * scale_ref[...]
        de_nxt = jnp.where(i < n_tiles - 1, nxt_ref[...].astype(F32) * scale_ref[...], 0.0)
        de = jnp.concatenate([de_cur, de_nxt], axis=0).astype(BF16)
        pos = (i * tr + 1 + lax.broadcasted_iota(jnp.int32, (tr + POOL_HALO, 1), 0)).astype(F32)
        for g, w in enumerate(POOL_WINDOWS):
            cols = slice(g * c, (g + 1) * c)
            de_g = de[:, cols]
            dd = lax.dot_general(de_g, w_ref[g], nt_dims, preferred_element_type=F32)
            dw_ref[g] += lax.dot_general(d_ref[:, cols], de_g[:tr], tn_dims, preferred_element_type=F32)
            q = dd / jnp.minimum(pos, float(w))
            win = _window_sums(q, tr + POOL_HALO, forward=True)[g]
            dp_ref[:, cols] = (win[:tr] - dd[:tr]).astype(BF16)

    return pl.pallas_call(
        body, name=name, grid=(n_tiles,),
        in_specs=[_row(tr, pw), pl.BlockSpec((POOL_HALO, pw), lambda i: (jnp.minimum((i + 1) * per, last_halo), 0)),
                  _row(tr, pw), _row(tr, pw), pl.BlockSpec((n_groups, c, c), lambda i: (0, 0, 0)), _vec(pw)],
        out_specs=[_row(tr, pw), pl.BlockSpec((n_groups, c, c), lambda i: (0, 0, 0)), _vec(pw)],
        out_shape=[jax.ShapeDtypeStruct((t_dim, pw), BF16), jax.ShapeDtypeStruct((n_groups, c, c), F32),
                   jax.ShapeDtypeStruct((1, pw), F32)],
        compiler_params=_params("arbitrary"),
    )(dyp, dyp, e, d, w_group, scale.reshape(1, pw))


def _sgu_rows(t_dim, chunk):
    return chunk * _pick(t_dim // chunk, (2, 1))


def _tril(chunk):
    return lax.broadcasted_iota(jnp.int32, (chunk, chunk), 0) >= lax.broadcasted_iota(jnp.int32, (chunk, chunk), 1)


def sgu_fwd(z, gain, w_s, b_s, pw, sw, name):
    t_dim = z.shape[0]
    n_heads, chunk, _ = w_s.shape
    hd = sw // n_heads
    tr = _sgu_rows(t_dim, chunk)
    ju = pw // sw

    def body(u_ref, v_ref, gain_ref, w_ref, bt_ref, sg_ref):
        ug, _ = _gelu(u_ref[...])
        vg, _ = _gelu(v_ref[...])
        r = lax.rsqrt(jnp.mean(vg * vg, axis=-1, keepdims=True) + EPS)
        vn = ((vg * r) * gain_ref[...]).astype(BF16)
        tri = _tril(chunk)
        for h in range(n_heads):
            wm = jnp.where(tri, w_ref[h], 0.0).astype(BF16)
            cols = slice(h * hd, (h + 1) * hd)
            for ch in range(tr // chunk):
                rows = slice(ch * chunk, (ch + 1) * chunk)
                s = jnp.dot(wm, vn[rows, cols], preferred_element_type=F32) + bt_ref[:, h:h + 1]
                sg_ref[rows, cols] = (ug[rows, cols] * s).astype(BF16)

    return pl.pallas_call(
        body, name=name, grid=(t_dim // tr,),
        in_specs=[pl.BlockSpec((tr, sw), lambda i: (i, ju)), pl.BlockSpec((tr, sw), lambda i: (i, ju + 1)), _vec(sw),
                  pl.BlockSpec((n_heads, chunk, chunk), lambda i: (0, 0, 0)),
                  pl.BlockSpec((chunk, n_heads), lambda i: (0, 0))],
        out_specs=_row(tr, sw), out_shape=jax.ShapeDtypeStruct((t_dim, sw), BF16),
        compiler_params=_params("parallel"),
    )(z, z, gain.reshape(1, sw), w_s, b_s.T)


def sgu_bwd(z, dsg, gain, w_s, b_s, pw, sw, name):
    t_dim = z.shape[0]
    n_heads, chunk, _ = w_s.shape
    hd = sw // n_heads
    tr = _sgu_rows(t_dim, chunk)
    ju = pw // sw
    nt_dims = (((1,), (1,)), ((), ()))
    tn_dims = (((0,), (0,)), ((), ()))

    def body(u_ref, v_ref, dsg_ref, gain_ref, w_ref, bt_ref, du_ref, dv_ref, dw_ref, dbt_ref, dgain_ref,
             dvn_ref, dug_ref):
        @pl.when(pl.program_id(0) == 0)
        def _():
            dw_ref[...] = jnp.zeros_like(dw_ref)
            dbt_ref[...] = jnp.zeros_like(dbt_ref)
            dgain_ref[...] = jnp.zeros_like(dgain_ref)

        u = u_ref[...]
        v = v_ref[...]
        ug, tu = _gelu(u)
        vg, tv = _gelu(v)
        r = lax.rsqrt(jnp.mean(vg * vg, axis=-1, keepdims=True) + EPS)
        n = vg * r
        gain_v = gain_ref[...]
        vn = (n * gain_v).astype(BF16)
        dsg_v = dsg_ref[...].astype(F32)
        tri = _tril(chunk)
        for h in range(n_heads):
            wm = jnp.where(tri, w_ref[h], 0.0).astype(BF16)
            cols = slice(h * hd, (h + 1) * hd)
            for ch in range(tr // chunk):
                rows = slice(ch * chunk, (ch + 1) * chunk)
                vn_b = vn[rows, cols]
                s = jnp.dot(wm, vn_b, preferred_element_type=F32) + bt_ref[:, h:h + 1]
                dsg_b = dsg_v[rows, cols]
                dug_ref[rows, cols] = dsg_b * s
                ds = dsg_b * ug[rows, cols]
                ds_b = ds.astype(BF16)
                dw_ref[h] += jnp.where(tri, lax.dot_general(ds_b, vn_b, nt_dims, preferred_element_type=F32), 0.0)
                dbt_ref[:, h:h + 1] += jnp.sum(ds, axis=1, keepdims=True)
                dvn_ref[rows, cols] = lax.dot_general(wm, ds_b, tn_dims, preferred_element_type=F32)
        dvn = dvn_ref[...]
        dgain_ref[...] += jnp.sum(dvn * n, axis=0, keepdims=True)
        dn = dvn * gain_v
        dvg = r * (dn - n * jnp.mean(dn * n, axis=-1, keepdims=True))
        dv_ref[...] = (dvg * _gelu_grad(v, tv)).astype(BF16)
        du_ref[...] = (dug_ref[...] * _gelu_grad(u, tu)).astype(BF16)

    full_w = pl.BlockSpec((n_heads, chunk, chunk), lambda i: (0, 0, 0))
    full_b = pl.BlockSpec((chunk, n_heads), lambda i: (0, 0))
    du, dv, dw, dbt, dgain = pl.pallas_call(
        body, name=name, grid=(t_dim // tr,),
        in_specs=[pl.BlockSpec((tr, sw), lambda i: (i, ju)), pl.BlockSpec((tr, sw), lambda i: (i, ju + 1)),
                  _row(tr, sw), _vec(sw), full_w, full_b],
        out_specs=[_row(tr, sw), _row(tr, sw), full_w, full_b, _vec(sw)],
        out_shape=[jax.ShapeDtypeStruct((t_dim, sw), BF16), jax.ShapeDtypeStruct((t_dim, sw), BF16),
                   jax.ShapeDtypeStruct((n_heads, chunk, chunk), F32), jax.ShapeDtypeStruct((chunk, n_heads), F32),
                   jax.ShapeDtypeStruct((1, sw), F32)],
        scratch_shapes=[pltpu.VMEM((tr, sw), F32), pltpu.VMEM((tr, sw), F32)],
        compiler_params=_params("arbitrary"),
    )(z, z, dsg, gain.reshape(1, sw), w_s, b_s.T)
    return du, dv, dw, dbt.T, dgain.reshape(sw)


HBM_SPEC = pl.BlockSpec(memory_space=pltpu.HBM)


def _slot(ref, axis, k, n):
    idx = [slice(None)] * len(ref.shape)
    idx[axis] = pl.ds(k * n, n)
    return ref.at[tuple(idx)]


def _chip_of(k, core):
    return (k // 2, k % 2, core)


def all_gather_weights(shards, axes, name):
    nt = len(shards)
    full = []
    for s, ax in zip(shards, axes):
        shape = list(s.shape)
        shape[ax] *= N_CHIPS
        full.append(jax.ShapeDtypeStruct(tuple(shape), s.dtype))

    def body(*refs):
        ins, outs = refs[:nt], refs[nt:2 * nt]
        send_sems, recv_sems, local_sems = refs[2 * nt:]
        core = lax.axis_index("c")
        me = 2 * lax.axis_index("x") + lax.axis_index("y")

        def remote(t, src_chip, to_chip):
            n = ins[t].shape[axes[t]]
            return pltpu.make_async_remote_copy(
                src_ref=ins[t], dst_ref=_slot(outs[t], axes[t], src_chip, n),
                send_sem=send_sems.at[t, to_chip], recv_sem=recv_sems.at[t, src_chip],
                device_id=_chip_of(to_chip, core), device_id_type=MESH)

        for k in range(N_CHIPS):
            @pl.when(me == k)
            def _(k=k):
                others = [j for j in range(N_CHIPS) if j != k]
                local = [pltpu.make_async_copy(ins[t], _slot(outs[t], axes[t], k, ins[t].shape[axes[t]]),
                                               local_sems.at[t]) for t in range(nt)]
                sends = [remote(t, k, j) for t in range(nt) for j in others]
                for cp in local + sends:
                    cp.start()
                for t in range(nt):
                    for j in others:
                        remote(t, j, k).wait_recv()
                for cp in sends:
                    cp.wait_send()
                for cp in local:
                    cp.wait()

    return pl.pallas_call(
        body, name=name, out_shape=full, in_specs=[HBM_SPEC] * nt, out_specs=[HBM_SPEC] * nt,
        scratch_shapes=[pltpu.SemaphoreType.DMA((nt, N_CHIPS)), pltpu.SemaphoreType.DMA((nt, N_CHIPS)),
                        pltpu.SemaphoreType.DMA((nt,))],
        compiler_params=pltpu.CompilerParams(has_side_effects=True),
    )(*shards)


def scatter_gradients(grads, axes, name):
    nt = len(grads)
    pieces = []
    for g, ax in zip(grads, axes):
        shape = list(g.shape)
        shape[ax] //= N_CHIPS
        pieces.append(tuple(shape))
    outs_shape = [jax.ShapeDtypeStruct((N_CHIPS,) + p, g.dtype) for p, g in zip(pieces, grads)]

    def body(*refs):
        ins, outs = refs[:nt], refs[nt:2 * nt]
        send_sems, recv_sems, local_sems = refs[2 * nt:]
        core = lax.axis_index("c")
        me = 2 * lax.axis_index("x") + lax.axis_index("y")

        def remote(t, src_chip, to_chip):
            return pltpu.make_async_remote_copy(
                src_ref=_slot(ins[t], axes[t], to_chip, pieces[t][axes[t]]), dst_ref=outs[t].at[src_chip],
                send_sem=send_sems.at[t, to_chip], recv_sem=recv_sems.at[t, src_chip],
                device_id=_chip_of(to_chip, core), device_id_type=MESH)

        for k in range(N_CHIPS):
            @pl.when(me == k)
            def _(k=k):
                others = [j for j in range(N_CHIPS) if j != k]
                local = [pltpu.make_async_copy(_slot(ins[t], axes[t], k, pieces[t][axes[t]]), outs[t].at[k],
                                               local_sems.at[t]) for t in range(nt)]
                sends = [remote(t, k, j) for t in range(nt) for j in others]
                for cp in local + sends:
                    cp.start()
                for t in range(nt):
                    for j in others:
                        remote(t, j, k).wait_recv()
                for cp in sends:
                    cp.wait_send()
                for cp in local:
                    cp.wait()

    return pl.pallas_call(
        body, name=name, out_shape=outs_shape, in_specs=[HBM_SPEC] * nt, out_specs=[HBM_SPEC] * nt,
        scratch_shapes=[pltpu.SemaphoreType.DMA((nt, N_CHIPS)), pltpu.SemaphoreType.DMA((nt, N_CHIPS)),
                        pltpu.SemaphoreType.DMA((nt,))],
        compiler_params=pltpu.CompilerParams(has_side_effects=True),
    )(*grads)


def swap_with_sibling(arrs, name):
    nt = len(arrs)

    def body(*refs):
        ins, outs = refs[:nt], refs[nt:2 * nt]
        send_sems, recv_sems = refs[2 * nt:]
        sibling = (lax.axis_index("x"), lax.axis_index("y"), 1 - lax.axis_index("c"))
        copies = [pltpu.make_async_remote_copy(src_ref=ins[t], dst_ref=outs[t], send_sem=send_sems.at[t],
                                               recv_sem=recv_sems.at[t], device_id=sibling, device_id_type=MESH)
                  for t in range(nt)]
        for cp in copies:
            cp.start()
        for cp in copies:
            cp.wait()

    return pl.pallas_call(
        body, name=name, out_shape=[jax.ShapeDtypeStruct(a.shape, a.dtype) for a in arrs],
        in_specs=[HBM_SPEC] * nt, out_specs=[HBM_SPEC] * nt,
        scratch_shapes=[pltpu.SemaphoreType.DMA((nt,)), pltpu.SemaphoreType.DMA((nt,))],
        compiler_params=pltpu.CompilerParams(has_side_effects=True),
    )(*arrs)


def sum_pieces(stack, name):
    _, r, c = stack.shape
    tr = _pick(r, ROW_TILES)
    tc = _pick(c, (512, 256, 128))

    def body(s_ref, o_ref):
        acc = s_ref[0].astype(F32)
        for k in range(1, N_CHIPS):
            acc = acc + s_ref[k].astype(F32)
        o_ref[...] = acc

    return pl.pallas_call(
        body, name=name, grid=(r // tr, c // tc),
        in_specs=[pl.BlockSpec((N_CHIPS, tr, tc), lambda i, j: (0, i, j))],
        out_specs=pl.BlockSpec((tr, tc), lambda i, j: (i, j)),
        out_shape=jax.ShapeDtypeStruct((r, c), F32), compiler_params=_params("parallel", "parallel"),
    )(stack)


def all_reduce_small(x, name):
    rows, lanes = x.shape

    def body(x_ref, sum_ref, gath_ref, send_sems, recv_sems, local_sem):
        cx, cy, cc = lax.axis_index("x"), lax.axis_index("y"), lax.axis_index("c")
        me, sibling = (cx, cy, cc), (cx, cy, 1 - cc)
        chips = [(1 - cx, cy), (cx, 1 - cy), (1 - cx, 1 - cy)]

        def block(px, py, pc):
            return gath_ref.at[pl.ds(pl.multiple_of((4 * px + 2 * py + pc) * rows, SUBLANE), rows), :]

        def copy(k, blk, to, src=None):
            return pltpu.make_async_remote_copy(
                src_ref=block(*blk) if src is None else src, dst_ref=block(*blk),
                send_sem=send_sems.at[k], recv_sem=recv_sems.at[k], device_id=to, device_id_type=MESH)

        mine = pltpu.make_async_copy(x_ref, block(*me), local_sem)
        mine.start()
        first = [copy(0, me, sibling, src=x_ref)]
        first += [copy(1 + j, me, (*chip, cc), src=x_ref) for j, chip in enumerate(chips)]
        for cp in first:
            cp.start()
        passed = [copy(4 + j, (*chip, cc), sibling) for j, chip in enumerate(chips)]
        for j, chip in enumerate(chips):
            copy(1 + j, (*chip, cc), me).wait_recv()
            passed[j].start()
        copy(0, sibling, me).wait_recv()
        for j, chip in enumerate(chips):
            copy(4 + j, (*chip, 1 - cc), me).wait_recv()
        for cp in first + passed:
            cp.wait_send()
        mine.wait()
        acc = gath_ref[pl.ds(0, rows), :]
        for k in range(1, N_DEV):
            acc = acc + gath_ref[pl.ds(k * rows, rows), :]
        sum_ref[...] = acc

    return pl.pallas_call(
        body, name=name, out_shape=jax.ShapeDtypeStruct((rows, lanes), F32),
        in_specs=[pl.BlockSpec(memory_space=pltpu.VMEM)], out_specs=pl.BlockSpec(memory_space=pltpu.VMEM),
        scratch_shapes=[pltpu.VMEM((N_DEV * rows, lanes), F32), pltpu.SemaphoreType.DMA((7,)),
                        pltpu.SemaphoreType.DMA((7,)), pltpu.SemaphoreType.DMA],
        compiler_params=pltpu.CompilerParams(has_side_effects=True, vmem_limit_bytes=VMEM_LIMIT),
    )(x)


def _adamw(w, g, m, v):
    m = ADAM_B1 * m + (1.0 - ADAM_B1) * g
    v = ADAM_B2 * v + (1.0 - ADAM_B2) * (g * g)
    m_hat = m / (1.0 - ADAM_B1 ** ADAM_STEP)
    v_hat = v / (1.0 - ADAM_B2 ** ADAM_STEP)
    delta = -ADAM_LR * (m_hat / (jnp.sqrt(v_hat) + ADAM_EPS) + ADAM_WD * w)
    return delta, m, v


def adam_big(p_own, p_sib, w, m, v, layer, stacks, name):
    r, c = p_own.shape
    tr = _pick(r, ROW_TILES)
    tc = _pick(c, (512, 256, 128))

    def body(p_ref, q_ref, w_ref, m_ref, v_ref, *rest):
        g_out, d_out, m_out, v_out = rest[4:]
        g = p_ref[...] + q_ref[...]
        delta, m_new, v_new = _adamw(w_ref[...], g, m_ref[...], v_ref[...])
        g_out[...] = g
        d_out[...] = delta
        m_out[...] = m_new
        v_out[...] = v_new

    flat = pl.BlockSpec((tr, tc), lambda i, j: (i, j))
    layered = pl.BlockSpec((None, tr, tc), lambda i, j: (layer, i, j))
    anyspace = pl.BlockSpec(memory_space=pl.ANY)
    out = jax.ShapeDtypeStruct(w.shape, F32)
    return pl.pallas_call(
        body, name=name, grid=(r // tr, c // tc),
        in_specs=[flat, flat, layered, layered, layered] + [anyspace] * 4,
        out_specs=[layered] * 4, out_shape=[out] * 4, input_output_aliases={5: 0, 6: 1, 7: 2, 8: 3},
        compiler_params=_params("parallel", "parallel"),
    )(p_own, p_sib, w, m, v, *stacks)


def adam_small(g, w, m, v, name):
    rows, lanes = g.shape
    tr = _pick(rows, (512,) + ROW_TILES)

    def body(g_ref, w_ref, m_ref, v_ref, d_out, m_out, v_out):
        delta, m_new, v_new = _adamw(w_ref[...], g_ref[...], m_ref[...], v_ref[...])
        d_out[...] = delta
        m_out[...] = m_new
        v_out[...] = v_new

    out = jax.ShapeDtypeStruct((rows, lanes), F32)
    return pl.pallas_call(
        body, name=name, grid=(rows // tr,), in_specs=[_row(tr, lanes)] * 4, out_specs=[_row(tr, lanes)] * 3,
        out_shape=[out] * 3, compiler_params=_params("parallel"),
    )(g, w, m, v)


def ffn_fwd(x, g_pre, w_up, w_down, g_post, tag):
    h = rms_fwd(x, g_pre, name=f"{tag}_rms_fwd")
    z = matmul(h, w_up, out_dtype=BF16, name=f"{tag}_up")
    a = swiglu_fwd(z, name=f"{tag}_swiglu_fwd")
    f = matmul(a, w_down, out_dtype=F32, name=f"{tag}_down")
    y = res_rms_fwd(x, f, g_post, MACARON_WEIGHT, name=f"{tag}_res_fwd")
    return y, (x, h, z, a, f)


def ffn_bwd(dy, saved, g_pre, w_up, w_down, g_post, tag):
    x, h, z, a, f = saved
    df, dg_post = rms_bwd(f, g_post, dy, MACARON_WEIGHT, None, BF16, name=f"{tag}_res_bwd")
    da = matmul(df, w_down, tb=True, out_dtype=BF16, name=f"{tag}_down_dx")
    dw_down = matmul(a, df, ta=True, out_dtype=BF16, name=f"{tag}_down_dw")
    dz = swiglu_bwd(z, da, name=f"{tag}_swiglu_bwd")
    dh = matmul(dz, w_up, tb=True, out_dtype=F32, name=f"{tag}_up_dx")
    dw_up = matmul(h, dz, ta=True, out_dtype=BF16, name=f"{tag}_up_dw")
    dx, dg_pre = rms_bwd(x, g_pre, dh, 1.0, dy, F32, name=f"{tag}_rms_bwd")
    return dx, dw_up, dw_down, dg_pre, dg_post


def mixer_fwd(x, wt, sm, dims):
    pw, sw, d = dims
    h = rms_fwd(x, sm['g_mix_pre'], name="mix_rms_fwd")
    z = matmul(h, wt['w_in'], out_dtype=F32, name="mix_in")
    dd, e, yp = pool_fwd(z, wt['pool_group_w'], sm['pool_scale'], pw, name="mix_pool_fwd")
    ya = matmul(yp, wt['w_pool_out'], out_dtype=BF16, name="mix_pool_out")
    sg = sgu_fwd(z, sm['sgu_v_gain'], sm['sgu_w_s'], sm['sgu_b_s'], pw, sw, name="mix_sgu_fwd")
    yb = matmul(sg, wt['w_sgu_out'], out_dtype=BF16, name="mix_sgu_out")
    m = gate_fwd(z, ya, yb, pw + 2 * sw, pw + 2 * sw + d, name="mix_gate_fwd")
    o = matmul(m, wt['w_out'], out_dtype=F32, name="mix_out")
    y = res_rms_fwd(x, o, sm['g_mix_post'], 1.0, name="mix_res_fwd")
    return y, (x, h, z, dd, e, yp, sg, ya, yb, m, o)


def mixer_bwd(dy, saved, wt, sm, dims):
    pw, sw, d = dims
    x, h, z, dd, e, yp, sg, ya, yb, m, o = saved
    grads = {}
    do, grads['g_mix_post'] = rms_bwd(o, sm['g_mix_post'], dy, 1.0, None, BF16, name="mix_res_bwd")
    dm = matmul(do, wt['w_out'], tb=True, out_dtype=BF16, name="mix_out_dx")
    grads['w_out'] = matmul(m, do, ta=True, out_dtype=BF16, name="mix_out_dw")
    dya, dyb, dga, dgb = gate_bwd(z, ya, yb, dm, pw + 2 * sw, pw + 2 * sw + d, name="mix_gate_bwd")
    dyp = matmul(dya, wt['w_pool_out'], tb=True, out_dtype=BF16, name="mix_pool_out_dx")
    grads['w_pool_out'] = matmul(yp, dya, ta=True, out_dtype=BF16, name="mix_pool_out_dw")
    dp, dwg, dscale = pool_bwd(dyp, e, dd, wt['pool_group_w'], sm['pool_scale'], name="mix_pool_bwd")
    grads['pool_group_w'] = dwg.astype(BF16)
    grads['pool_scale'] = dscale.reshape(pw)
    dsg = matmul(dyb, wt['w_sgu_out'], tb=True, out_dtype=BF16, name="mix_sgu_out_dx")
    grads['w_sgu_out'] = matmul(sg, dyb, ta=True, out_dtype=BF16, name="mix_sgu_out_dw")
    du, dv, grads['sgu_w_s'], grads['sgu_b_s'], grads['sgu_v_gain'] = sgu_bwd(
        z, dsg, sm['sgu_v_gain'], sm['sgu_w_s'], sm['sgu_b_s'], pw, sw, name="mix_sgu_bwd")
    dz = jnp.concatenate([dp, du, dv, dga, dgb], axis=1)
    dh = matmul(dz, wt['w_in'], tb=True, out_dtype=F32, name="mix_in_dx")
    grads['w_in'] = matmul(h, dz, ta=True, out_dtype=BF16, name="mix_in_dw")
    dx, grads['g_mix_pre'] = rms_bwd(x, sm['g_mix_pre'], dh, 1.0, dy, F32, name="mix_rms_bwd")
    return dx, grads


def _as_rows(a):
    return a.reshape(a.shape[0], -1, a.shape[-1])


def kernel(x, g_ffn1_pre, w_ffn1_up, w_ffn1_down, g_ffn1_post, g_mix_pre, w_in, pool_group_w, pool_scale, w_pool_out, sgu_v_gain, sgu_w_s, sgu_b_s, w_sgu_out, w_out, g_mix_post, g_ffn2_pre, w_ffn2_up, w_ffn2_down, g_ffn2_post, loss_target, m_g_ffn1_pre, m_w_ffn1_up, m_w_ffn1_down, m_g_ffn1_post, m_g_mix_pre, m_w_in, m_pool_group_w, m_pool_scale, m_w_pool_out, m_sgu_v_gain, m_sgu_w_s, m_sgu_b_s, m_w_sgu_out, m_w_out, m_g_mix_post, m_g_ffn2_pre, m_w_ffn2_up, m_w_ffn2_down, m_g_ffn2_post, v_g_ffn1_pre, v_w_ffn1_up, v_w_ffn1_down, v_g_ffn1_post, v_g_mix_pre, v_w_in, v_pool_group_w, v_pool_scale, v_w_pool_out, v_sgu_v_gain, v_sgu_w_s, v_sgu_b_s, v_w_sgu_out, v_w_out, v_g_mix_post, v_g_ffn2_pre, v_w_ffn2_up, v_w_ffn2_down, v_g_ffn2_post):
    w = dict(zip(WEIGHTS, (g_ffn1_pre, w_ffn1_up, w_ffn1_down, g_ffn1_post, g_mix_pre, w_in, pool_group_w, pool_scale, w_pool_out, sgu_v_gain, sgu_w_s, sgu_b_s, w_sgu_out, w_out, g_mix_post, g_ffn2_pre, w_ffn2_up, w_ffn2_down, g_ffn2_post)))
    mom = dict(zip(WEIGHTS, (m_g_ffn1_pre, m_w_ffn1_up, m_w_ffn1_down, m_g_ffn1_post, m_g_mix_pre, m_w_in, m_pool_group_w, m_pool_scale, m_w_pool_out, m_sgu_v_gain, m_sgu_w_s, m_sgu_b_s, m_w_sgu_out, m_w_out, m_g_mix_post, m_g_ffn2_pre, m_w_ffn2_up, m_w_ffn2_down, m_g_ffn2_post)))
    var = dict(zip(WEIGHTS, (v_g_ffn1_pre, v_w_ffn1_up, v_w_ffn1_down, v_g_ffn1_post, v_g_mix_pre, v_w_in, v_pool_group_w, v_pool_scale, v_w_pool_out, v_sgu_v_gain, v_sgu_w_s, v_sgu_b_s, v_w_sgu_out, v_w_out, v_g_mix_post, v_g_ffn2_pre, v_w_ffn2_up, v_w_ffn2_down, v_g_ffn2_post)))
    depth = g_ffn1_pre.shape[0]
    d = x.shape[-1]
    pw = pool_scale.shape[-1]
    sw = sgu_v_gain.shape[-1]
    dims = (pw, sw, d)
    axes = [BIG_AXIS[n] for n in BIG]

    shards_bf16 = {n: w[n].astype(BF16) for n in BIG}
    full = []
    for l in range(depth):
        gathered = all_gather_weights([shards_bf16[n][l] for n in BIG], axes, name="gather_weights")
        full.append(dict(zip(BIG, gathered)))
    small = [{n: w[n][l] for n in SMALL} for l in range(depth)]

    act = x[0]
    saved = []
    for l in range(depth):
        wt, sm = full[l], small[l]
        act, s1 = ffn_fwd(act, sm['g_ffn1_pre'], wt['w_ffn1_up'], wt['w_ffn1_down'], sm['g_ffn1_post'], "ffn1")
        act, s2 = mixer_fwd(act, wt, sm, dims)
        act, s3 = ffn_fwd(act, sm['g_ffn2_pre'], wt['w_ffn2_up'], wt['w_ffn2_down'], sm['g_ffn2_post'], "ffn2")
        saved.append((s1, s2, s3))
    dact, sq_sum = loss_grad(act, loss_target[0], name="loss_grad")
    loss = lax.psum(0.5 * sq_sum / d, ("x", "y", "c"))

    rows3 = {n: (_as_rows(w[n]), _as_rows(mom[n]), _as_rows(var[n])) for n in BIG}
    stacks = {n: tuple(lax.empty(rows3[n][0].shape, F32) for _ in range(4)) for n in BIG}
    small_grads = [None] * depth
    for l in reversed(range(depth)):
        wt, sm = full[l], small[l]
        s1, s2, s3 = saved[l]
        g = {}
        dact, g['w_ffn2_up'], g['w_ffn2_down'], g['g_ffn2_pre'], g['g_ffn2_post'] = ffn_bwd(
            dact, s3, sm['g_ffn2_pre'], wt['w_ffn2_up'], wt['w_ffn2_down'], sm['g_ffn2_post'], "ffn2")
        dact, gm = mixer_bwd(dact, s2, wt, sm, dims)
        g.update(gm)
        dact, g['w_ffn1_up'], g['w_ffn1_down'], g['g_ffn1_pre'], g['g_ffn1_post'] = ffn_bwd(
            dact, s1, sm['g_ffn1_pre'], wt['w_ffn1_up'], wt['w_ffn1_down'], sm['g_ffn1_post'], "ffn1")
        small_grads[l] = {n: g[n] for n in SMALL}

        pieces = scatter_gradients([g[n] for n in BIG], axes, name="scatter_gradients")
        plane = [sum_pieces(p.reshape(N_CHIPS, -1, p.shape[-1]), name=f"sum_{n}") for n, p in zip(BIG, pieces)]
        other = swap_with_sibling(plane, name="swap_plane_sums")
        for n, p_own, p_sib in zip(BIG, plane, other):
            stacks[n] = tuple(adam_big(p_own, p_sib, *rows3[n], l, stacks[n], name=f"adam_{n}"))

    def flat(tree):
        v = jnp.concatenate([tree[n].reshape(-1).astype(F32) for n in SMALL])
        pad = (-v.shape[0]) % (SUBLANE * LANE)
        return jnp.pad(v, (0, pad)).reshape(-1, LANE)

    g_small = all_reduce_small(flat({n: jnp.stack([small_grads[l][n] for l in range(depth)]) for n in SMALL}),
                               name="all_reduce_small")
    d_small, m_small, v_small = adam_small(g_small, flat(w), flat(mom), flat(var), name="adam_small")

    def unflat(block):
        v, out, at = block.reshape(-1), {}, 0
        for n in SMALL:
            out[n] = v[at:at + w[n].size].reshape(w[n].shape)
            at += w[n].size
        return out

    result = [{}, {}, {}, {}]
    for tree, block in zip(result, (g_small, d_small, m_small, v_small)):
        tree.update(unflat(block))
    for n in BIG:
        for tree, stack in zip(result, stacks[n]):
            tree[n] = stack.reshape(w[n].shape)
    return (loss, dact.reshape(x.shape), *[tree[n] for tree in result for n in WEIGHTS])
```

```python
import math

import jax
import jax.numpy as jnp
from jax import lax
from jax.experimental import pallas as pl
from jax.experimental.pallas import tpu as pltpu

F32 = jnp.float32
BF16 = jnp.bfloat16
MESH = pl.DeviceIdType.MESH

EPS = 1e-6
MACARON_WEIGHT = 0.5
POOL_WINDOWS = (2, 4, 8, 16)
POOL_HALO = 16
ADAM_LR = 0.001
ADAM_B1 = 0.9
ADAM_B2 = 0.999
ADAM_EPS = 1e-08
ADAM_WD = 0.01
ADAM_STEP = 10
GELU_K = math.sqrt(2.0 / math.pi)
GELU_C = 0.044715

N_CHIPS = 4
N_DEV = 8
V7X_VMEM_BYTES = 64 * 1024 * 1024
VMEM_LIMIT = (V7X_VMEM_BYTES * 3) // 4
LANE = 128
SUBLANE = 8

WEIGHTS = ['g_ffn1_pre', 'w_ffn1_up', 'w_ffn1_down', 'g_ffn1_post', 'g_mix_pre', 'w_in', 'pool_group_w',
           'pool_scale', 'w_pool_out', 'sgu_v_gain', 'sgu_w_s', 'sgu_b_s', 'w_sgu_out', 'w_out', 'g_mix_post',
           'g_ffn2_pre', 'w_ffn2_up', 'w_ffn2_down', 'g_ffn2_post']
BIG_AXIS = {'w_ffn1_up': 1, 'w_ffn1_down': 0, 'w_in': 1, 'pool_group_w': 1, 'w_pool_out': 1, 'w_sgu_out': 1,
            'w_out': 0, 'w_ffn2_up': 1, 'w_ffn2_down': 0}
BIG = list(BIG_AXIS)
BLOCK_WEIGHTS = [['w_ffn1_up', 'w_ffn1_down'], ['w_in', 'pool_group_w', 'w_pool_out', 'w_sgu_out', 'w_out'],
                 ['w_ffn2_up', 'w_ffn2_down']]
SMALL = [n for n in WEIGHTS if n not in BIG_AXIS]


def _pick(dim, cands):
    for c in cands:
        if dim % c == 0:
            return c
    return dim


def _params(*sem):
    return pltpu.CompilerParams(dimension_semantics=sem if sem else None, vmem_limit_bytes=VMEM_LIMIT)


def _sigmoid(x):
    return 1.0 / (1.0 + jnp.exp(-x))


def _gelu(x):
    t = jnp.tanh(GELU_K * (x + GELU_C * (x * x * x)))
    return x * (0.5 * (1.0 + t)), t


def _gelu_grad(x, t):
    return 0.5 * (1.0 + t) + (0.5 * x) * (1.0 - t * t) * (GELU_K * (1.0 + (3.0 * GELU_C) * (x * x)))


def matmul(a, b, *, ta=False, tb=False, out_dtype=BF16, name):
    m_dim, k_dim = (a.shape[1], a.shape[0]) if ta else a.shape
    n_dim = b.shape[0] if tb else b.shape[1]
    tm = _pick(m_dim, (1024, 512, 256, 128))
    tn = _pick(n_dim, (1024, 512, 256, 128))
    tk = _pick(k_dim, (1024, 512, 256, 128))
    nk = k_dim // tk
    dims = (((0 if ta else 1,), (1 if tb else 0,)), ((), ()))

    def body(a_ref, b_ref, o_ref, acc_ref):
        k = pl.program_id(2)

        @pl.when(k == 0)
        def _():
            acc_ref[...] = jnp.zeros_like(acc_ref)

        acc_ref[...] += lax.dot_general(a_ref[...], b_ref[...], dims, preferred_element_type=F32)

        @pl.when(k == nk - 1)
        def _():
            o_ref[...] = acc_ref[...].astype(o_ref.dtype)

    a_spec = pl.BlockSpec((tk, tm), lambda i, j, k: (k, i)) if ta else pl.BlockSpec((tm, tk), lambda i, j, k: (i, k))
    b_spec = pl.BlockSpec((tn, tk), lambda i, j, k: (j, k)) if tb else pl.BlockSpec((tk, tn), lambda i, j, k: (k, j))
    return pl.pallas_call(
        body, name=name, grid=(m_dim // tm, n_dim // tn, nk),
        in_specs=[a_spec, b_spec], out_specs=pl.BlockSpec((tm, tn), lambda i, j, k: (i, j)),
        out_shape=jax.ShapeDtypeStruct((m_dim, n_dim), out_dtype),
        scratch_shapes=[pltpu.VMEM((tm, tn), F32)],
        compiler_params=_params("parallel", "parallel", "arbitrary"),
    )(a, b)


ROW_TILES = (256, 128, 64, 32, 16, 8)


def _row(tr, width):
    return pl.BlockSpec((tr, width), lambda i: (i, 0))


def _vec(width):
    return pl.BlockSpec((1, width), lambda i: (0, 0))


def rms_fwd(x, g, name, deps=()):
    t_dim, d = x.shape
    tr = _pick(t_dim, ROW_TILES)

    def body(x_ref, g_ref, *rest):
        o_ref = rest[-1]
        xv = x_ref[...]
        r = lax.rsqrt(jnp.mean(xv * xv, axis=-1, keepdims=True) + EPS)
        o_ref[...] = ((xv * r) * g_ref[...]).astype(o_ref.dtype)

    return pl.pallas_call(
        body, name=name, grid=(t_dim // tr,), in_specs=[_row(tr, d), _vec(d)] + [ANY_SPEC] * len(deps),
        out_specs=_row(tr, d), out_shape=jax.ShapeDtypeStruct((t_dim, d), BF16), compiler_params=_params("parallel"),
    )(x, g.reshape(1, d), *deps)


def res_rms_fwd(x, f, g, weight, name):
    t_dim, d = x.shape
    tr = _pick(t_dim, ROW_TILES)

    def body(x_ref, f_ref, g_ref, o_ref):
        fv = f_ref[...]
        r = lax.rsqrt(jnp.mean(fv * fv, axis=-1, keepdims=True) + EPS)
        o_ref[...] = x_ref[...] + weight * ((fv * r) * g_ref[...])

    return pl.pallas_call(
        body, name=name, grid=(t_dim // tr,), in_specs=[_row(tr, d), _row(tr, d), _vec(d)], out_specs=_row(tr, d),
        out_shape=jax.ShapeDtypeStruct((t_dim, d), F32), compiler_params=_params("parallel"),
    )(x, f, g.reshape(1, d))


def rms_bwd(f, g, dy, weight, resid, out_dtype, name, deps=()):
    t_dim, d = f.shape
    tr = _pick(t_dim, ROW_TILES)
    has_resid = resid is not None

    def body(*refs):
        o_ref, dg_ref = refs[-2:]
        if has_resid:
            f_ref, g_ref, dy_ref, res_ref = refs[:4]
        else:
            f_ref, g_ref, dy_ref = refs[:3]

        @pl.when(pl.program_id(0) == 0)
        def _():
            dg_ref[...] = jnp.zeros_like(dg_ref)

        fv = f_ref[...]
        r = lax.rsqrt(jnp.mean(fv * fv, axis=-1, keepdims=True) + EPS)
        n = fv * r
        dyw = dy_ref[...] * weight
        dn = dyw * g_ref[...]
        df = r * (dn - n * jnp.mean(dn * n, axis=-1, keepdims=True))
        if has_resid:
            df = df + res_ref[...]
        o_ref[...] = df.astype(o_ref.dtype)
        dg_ref[...] += jnp.sum(dyw * n, axis=0, keepdims=True)

    ins = [f, g.reshape(1, d), dy] + ([resid] if has_resid else []) + list(deps)
    in_specs = [_row(tr, d), _vec(d), _row(tr, d)] + ([_row(tr, d)] if has_resid else []) + [ANY_SPEC] * len(deps)
    out, dg = pl.pallas_call(
        body, name=name, grid=(t_dim // tr,), in_specs=in_specs, out_specs=[_row(tr, d), _vec(d)],
        out_shape=[jax.ShapeDtypeStruct((t_dim, d), out_dtype), jax.ShapeDtypeStruct((1, d), F32)],
        compiler_params=_params("arbitrary"),
    )(*ins)
    return out, dg.reshape(d)


def loss_grad(y, target, name):
    t_dim, d = y.shape
    tr = _pick(t_dim, ROW_TILES)
    inv_d = 1.0 / d

    def body(y_ref, t_ref, dy_ref, s_ref):
        @pl.when(pl.program_id(0) == 0)
        def _():
            s_ref[...] = jnp.zeros_like(s_ref)

        e = y_ref[...] - t_ref[...]
        dy_ref[...] = e * inv_d
        s_ref[...] += jnp.sum(e * e)

    dy, s = pl.pallas_call(
        body, name=name, grid=(t_dim // tr,), in_specs=[_row(tr, d), _row(tr, d)],
        out_specs=[_row(tr, d), pl.BlockSpec((SUBLANE, LANE), lambda i: (0, 0))],
        out_shape=[jax.ShapeDtypeStruct((t_dim, d), F32), jax.ShapeDtypeStruct((SUBLANE, LANE), F32)],
        compiler_params=_params("arbitrary"),
    )(y, target)
    return dy, s[0, 0]


def swiglu_fwd(z, name):
    t_dim, f2 = z.shape
    f = f2 // 2
    tr = _pick(t_dim, (512,) + ROW_TILES)
    tc = _pick(f, (512, 256, 128))
    nf = f // tc

    def body(g_ref, u_ref, a_ref):
        g = g_ref[...].astype(F32)
        a_ref[...] = (g * _sigmoid(g) * u_ref[...].astype(F32)).astype(a_ref.dtype)

    return pl.pallas_call(
        body, name=name, grid=(t_dim // tr, nf),
        in_specs=[pl.BlockSpec((tr, tc), lambda i, j: (i, j)), pl.BlockSpec((tr, tc), lambda i, j: (i, j + nf))],
        out_specs=pl.BlockSpec((tr, tc), lambda i, j: (i, j)),
        out_shape=jax.ShapeDtypeStruct((t_dim, f), BF16), compiler_params=_params("parallel", "parallel"),
    )(z, z)


def swiglu_bwd(z, da, name):
    t_dim, f2 = z.shape
    f = f2 // 2
    tr = _pick(t_dim, ROW_TILES)
    tc = _pick(f, (512, 256, 128))

    def body(z_ref, da_ref, o_ref):
        for c in range(f // tc):
            lo = c * tc
            g = z_ref[:, lo:lo + tc].astype(F32)
            u = z_ref[:, f + lo:f + lo + tc].astype(F32)
            da = da_ref[:, lo:lo + tc].astype(F32)
            s = _sigmoid(g)
            o_ref[:, lo:lo + tc] = (da * u * (s * (1.0 + g * (1.0 - s)))).astype(o_ref.dtype)
            o_ref[:, f + lo:f + lo + tc] = (da * (g * s)).astype(o_ref.dtype)

    return pl.pallas_call(
        body, name=name, grid=(t_dim // tr,), in_specs=[_row(tr, f2), _row(tr, f)], out_specs=_row(tr, f2),
        out_shape=jax.ShapeDtypeStruct((t_dim, f2), BF16), compiler_params=_params("parallel"),
    )(z, da)


def gate_fwd(z, ya, yb, off_a, off_b, name):
    t_dim, d = ya.shape
    tr = _pick(t_dim, (512,) + ROW_TILES)
    tc = math.gcd(math.gcd(off_a, off_b), _pick(d, (512, 256, 128)))
    ja, jb = off_a // tc, off_b // tc

    def body(ga_ref, gb_ref, ya_ref, yb_ref, m_ref):
        m = _sigmoid(ga_ref[...]) * ya_ref[...].astype(F32) + _sigmoid(gb_ref[...]) * yb_ref[...].astype(F32)
        m_ref[...] = m.astype(m_ref.dtype)

    blk = pl.BlockSpec((tr, tc), lambda i, j: (i, j))
    return pl.pallas_call(
        body, name=name, grid=(t_dim // tr, d // tc),
        in_specs=[pl.BlockSpec((tr, tc), lambda i, j: (i, j + ja)), pl.BlockSpec((tr, tc), lambda i, j: (i, j + jb)),
                  blk, blk],
        out_specs=blk, out_shape=jax.ShapeDtypeStruct((t_dim, d), BF16),
        compiler_params=_params("parallel", "parallel"),
    )(z, z, ya, yb)


def gate_bwd(z, ya, yb, dm, off_a, off_b, name):
    t_dim, d = ya.shape
    tr = _pick(t_dim, (512,) + ROW_TILES)
    tc = math.gcd(math.gcd(off_a, off_b), _pick(d, (512, 256, 128)))
    ja, jb = off_a // tc, off_b // tc

    def body(ga_ref, gb_ref, ya_ref, yb_ref, dm_ref, dya_ref, dyb_ref, dga_ref, dgb_ref):
        dm = dm_ref[...].astype(F32)
        sa = _sigmoid(ga_ref[...])
        sb = _sigmoid(gb_ref[...])
        dya_ref[...] = (dm * sa).astype(BF16)
        dyb_ref[...] = (dm * sb).astype(BF16)
        dga_ref[...] = (dm * ya_ref[...].astype(F32) * (sa * (1.0 - sa))).astype(BF16)
        dgb_ref[...] = (dm * yb_ref[...].astype(F32) * (sb * (1.0 - sb))).astype(BF16)

    blk = pl.BlockSpec((tr, tc), lambda i, j: (i, j))
    out = jax.ShapeDtypeStruct((t_dim, d), BF16)
    return pl.pallas_call(
        body, name=name, grid=(t_dim // tr, d // tc),
        in_specs=[pl.BlockSpec((tr, tc), lambda i, j: (i, j + ja)), pl.BlockSpec((tr, tc), lambda i, j: (i, j + jb)),
                  blk, blk, blk],
        out_specs=[blk] * 4, out_shape=[out] * 4, compiler_params=_params("parallel", "parallel"),
    )(z, z, ya, yb, dm)


def _window_sums(e, n_rows, forward):
    def shifted(v, k):
        return pltpu.roll(v, (n_rows - k) if forward else k, 0)

    s2 = e + shifted(e, 1)
    s4 = s2 + shifted(s2, 2)
    s8 = s4 + shifted(s4, 4)
    s16 = s8 + shifted(s8, 8)
    return (s2, s4, s8, s16)


def _pool_rows(t_dim):
    return _pick(t_dim, (256, 128, 64, 32, 16))


def pool_fwd(z, w_group, scale, pw, name):
    t_dim = z.shape[0]
    n_groups, c, _ = w_group.shape
    tr = _pool_rows(t_dim)
    per = tr // POOL_HALO

    def body(cur_ref, prev_ref, w_ref, scale_ref, d_ref, e_ref, yp_ref):
        i = pl.program_id(0)
        cur = cur_ref[...]
        prev = jnp.where(i > 0, prev_ref[...], 0.0)
        ext = jnp.concatenate([prev, cur], axis=0)
        sums = _window_sums(ext, tr + POOL_HALO, forward=False)
        pos = (i * tr + 1 + lax.broadcasted_iota(jnp.int32, (tr, 1), 0)).astype(F32)
        for g, w in enumerate(POOL_WINDOWS):
            cols = slice(g * c, (g + 1) * c)
            cnt = jnp.minimum(pos, float(w))
            d = (sums[g][POOL_HALO:, cols] / cnt - cur[:, cols]).astype(BF16)
            e = jnp.dot(d, w_ref[g], preferred_element_type=F32)
            d_ref[:, cols] = d
            e_ref[:, cols] = e.astype(BF16)
            yp_ref[:, cols] = (e * scale_ref[:, cols]).astype(BF16)

    out = jax.ShapeDtypeStruct((t_dim, pw), BF16)
    return pl.pallas_call(
        body, name=name, grid=(t_dim // tr,),
        in_specs=[_row(tr, pw), pl.BlockSpec((POOL_HALO, pw), lambda i: (jnp.maximum(i * per - 1, 0), 0)),
                  pl.BlockSpec((n_groups, c, c), lambda i: (0, 0, 0)), _vec(pw)],
        out_specs=[_row(tr, pw)] * 3, out_shape=[out] * 3, compiler_params=_params("parallel"),
    )(z, z, w_group, scale.reshape(1, pw))


def pool_bwd(dyp, e, d, w_group, scale, name):
    t_dim, pw = dyp.shape
    n_groups, c, _ = w_group.shape
    tr = _pool_rows(t_dim)
    per = tr // POOL_HALO
    n_tiles = t_dim // tr
    last_halo = t_dim // POOL_HALO - 1
    nt_dims = (((1,), (1,)), ((), ()))
    tn_dims = (((0,), (0,)), ((), ()))

    def body(dyp_ref, nxt_ref, e_ref, d_ref, w_ref, scale_ref, dp_ref, dw_ref, dscale_ref):
        i = pl.program_id(0)

        @pl.when(i == 0)
        def _():
            dw_ref[...] = jnp.zeros_like(dw_ref)
            dscale_ref[...] = jnp.zeros_like(dscale_ref)

        dyp_v = dyp_ref[...].astype(F32)
        dscale_ref[...] += jnp.sum(dyp_v * e_ref[...].astype(F32), axis=0, keepdims=True)
        de_cur = dyp_v * scale_ref[...]
        de_nxt = jnp.where(i < n_tiles - 1, nxt_ref[...].astype(F32) * scale_ref[...], 0.0)
        de = jnp.concatenate([de_cur, de_nxt], axis=0).astype(BF16)
        pos = (i * tr + 1 + lax.broadcasted_iota(jnp.int32, (tr + POOL_HALO, 1), 0)).astype(F32)
        for g, w in enumerate(POOL_WINDOWS):
            cols = slice(g * c, (g + 1) * c)
            de_g = de[:, cols]
            dd = lax.dot_general(de_g, w_ref[g], nt_dims, preferred_element_type=F32)
            dw_ref[g] += lax.dot_general(d_ref[:, cols], de_g[:tr], tn_dims, preferred_element_type=F32)
            q = dd / jnp.minimum(pos, float(w))
            win = _window_sums(q, tr + POOL_HALO, forward=True)[g]
            dp_ref[:, cols] = (win[:tr] - dd[:tr]).astype(BF16)

    return pl.pallas_call(
        body, name=name, grid=(n_tiles,),
        in_specs=[_row(tr, pw), pl.BlockSpec((POOL_HALO, pw), lambda i: (jnp.minimum((i + 1) * per, last_halo), 0)),
                  _row(tr, pw), _row(tr, pw), pl.BlockSpec((n_groups, c, c), lambda i: (0, 0, 0)), _vec(pw)],
        out_specs=[_row(tr, pw), pl.BlockSpec((n_groups, c, c), lambda i: (0, 0, 0)), _vec(pw)],
        out_shape=[jax.ShapeDtypeStruct((t_dim, pw), BF16), jax.ShapeDtypeStruct((n_groups, c, c), F32),
                   jax.ShapeDtypeStruct((1, pw), F32)],
        compiler_params=_params("arbitrary"),
    )(dyp, dyp, e, d, w_group, scale.reshape(1, pw))


def _sgu_rows(t_dim, chunk):
    return chunk * _pick(t_dim // chunk, (2, 1))


def _tril(chunk):
    return lax.broadcasted_iota(jnp.int32, (chunk, chunk), 0) >= lax.broadcasted_iota(jnp.int32, (chunk, chunk), 1)


def sgu_fwd(z, gain, w_s, b_s, pw, sw, name):
    t_dim = z.shape[0]
    n_heads, chunk, _ = w_s.shape
    hd = sw // n_heads
    tr = _sgu_rows(t_dim, chunk)
    ju = pw // sw

    def body(u_ref, v_ref, gain_ref, w_ref, bt_ref, sg_ref):
        ug, _ = _gelu(u_ref[...])
        vg, _ = _gelu(v_ref[...])
        r = lax.rsqrt(jnp.mean(vg * vg, axis=-1, keepdims=True) + EPS)
        vn = ((vg * r) * gain_ref[...]).astype(BF16)
        tri = _tril(chunk)
        for h in range(n_heads):
            wm = jnp.where(tri, w_ref[h], 0.0).astype(BF16)
            cols = slice(h * hd, (h + 1) * hd)
            for ch in range(tr // chunk):
                rows = slice(ch * chunk, (ch + 1) * chunk)
                s = jnp.dot(wm, vn[rows, cols], preferred_element_type=F32) + bt_ref[:, h:h + 1]
                sg_ref[rows, cols] = (ug[rows, cols] * s).astype(BF16)

    return pl.pallas_call(
        body, name=name, grid=(t_dim // tr,),
        in_specs=[pl.BlockSpec((tr, sw), lambda i: (i, ju)), pl.BlockSpec((tr, sw), lambda i: (i, ju + 1)), _vec(sw),
                  pl.BlockSpec((n_heads, chunk, chunk), lambda i: (0, 0, 0)),
                  pl.BlockSpec((chunk, n_heads), lambda i: (0, 0))],
        out_specs=_row(tr, sw), out_shape=jax.ShapeDtypeStruct((t_dim, sw), BF16),
        compiler_params=_params("parallel"),
    )(z, z, gain.reshape(1, sw), w_s, b_s.T)


def sgu_bwd(z, dsg, gain, w_s, b_s, pw, sw, name):
    t_dim = z.shape[0]
    n_heads, chunk, _ = w_s.shape
    hd = sw // n_heads
    tr = _sgu_rows(t_dim, chunk)
    ju = pw // sw
    nt_dims = (((1,), (1,)), ((), ()))
    tn_dims = (((0,), (0,)), ((), ()))

    def body(u_ref, v_ref, dsg_ref, gain_ref, w_ref, bt_ref, du_ref, dv_ref, dw_ref, dbt_ref, dgain_ref,
             dvn_ref, dug_ref):
        @pl.when(pl.program_id(0) == 0)
        def _():
            dw_ref[...] = jnp.zeros_like(dw_ref)
            dbt_ref[...] = jnp.zeros_like(dbt_ref)
            dgain_ref[...] = jnp.zeros_like(dgain_ref)

        u = u_ref[...]
        v = v_ref[...]
        ug, tu = _gelu(u)
        vg, tv = _gelu(v)
        r = lax.rsqrt(jnp.mean(vg * vg, axis=-1, keepdims=True) + EPS)
        n = vg * r
        gain_v = gain_ref[...]
        vn = (n * gain_v).astype(BF16)
        dsg_v = dsg_ref[...].astype(F32)
        tri = _tril(chunk)
        for h in range(n_heads):
            wm = jnp.where(tri, w_ref[h], 0.0).astype(BF16)
            cols = slice(h * hd, (h + 1) * hd)
            for ch in range(tr // chunk):
                rows = slice(ch * chunk, (ch + 1) * chunk)
                vn_b = vn[rows, cols]
                s = jnp.dot(wm, vn_b, preferred_element_type=F32) + bt_ref[:, h:h + 1]
                dsg_b = dsg_v[rows, cols]
                dug_ref[rows, cols] = dsg_b * s
                ds = dsg_b * ug[rows, cols]
                ds_b = ds.astype(BF16)
                dw_ref[h] += jnp.where(tri, lax.dot_general(ds_b, vn_b, nt_dims, preferred_element_type=F32), 0.0)
                dbt_ref[:, h:h + 1] += jnp.sum(ds, axis=1, keepdims=True)
                dvn_ref[rows, cols] = lax.dot_general(wm, ds_b, tn_dims, preferred_element_type=F32)
        dvn = dvn_ref[...]
        dgain_ref[...] += jnp.sum(dvn * n, axis=0, keepdims=True)
        dn = dvn * gain_v
        dvg = r * (dn - n * jnp.mean(dn * n, axis=-1, keepdims=True))
        dv_ref[...] = (dvg * _gelu_grad(v, tv)).astype(BF16)
        du_ref[...] = (dug_ref[...] * _gelu_grad(u, tu)).astype(BF16)

    full_w = pl.BlockSpec((n_heads, chunk, chunk), lambda i: (0, 0, 0))
    full_b = pl.BlockSpec((chunk, n_heads), lambda i: (0, 0))
    du, dv, dw, dbt, dgain = pl.pallas_call(
        body, name=name, grid=(t_dim // tr,),
        in_specs=[pl.BlockSpec((tr, sw), lambda i: (i, ju)), pl.BlockSpec((tr, sw), lambda i: (i, ju + 1)),
                  _row(tr, sw), _vec(sw), full_w, full_b],
        out_specs=[_row(tr, sw), _row(tr, sw), full_w, full_b, _vec(sw)],
        out_shape=[jax.ShapeDtypeStruct((t_dim, sw), BF16), jax.ShapeDtypeStruct((t_dim, sw), BF16),
                   jax.ShapeDtypeStruct((n_heads, chunk, chunk), F32), jax.ShapeDtypeStruct((chunk, n_heads), F32),
                   jax.ShapeDtypeStruct((1, sw), F32)],
        scratch_shapes=[pltpu.VMEM((tr, sw), F32), pltpu.VMEM((tr, sw), F32)],
        compiler_params=_params("arbitrary"),
    )(z, z, dsg, gain.reshape(1, sw), w_s, b_s.T)
    return du, dv, dw, dbt.T, dgain.reshape(sw)


HBM_SPEC = pl.BlockSpec(memory_space=pltpu.HBM)
SEM_SPEC = pl.BlockSpec(memory_space=pltpu.SEMAPHORE)
ANY_SPEC = pl.BlockSpec(memory_space=pl.ANY)
DATAFLOW = pltpu.SideEffectType.DATAFLOW_SIDE_EFFECTING


def _hbm(a):
    return pltpu.with_memory_space_constraint(a, pltpu.HBM)


def _slot(ref, axis, k, n):
    idx = [slice(None)] * len(ref.shape)
    idx[axis] = pl.ds(k * n, n)
    return ref.at[tuple(idx)]


def _chip_of(k, core):
    return (k // 2, k % 2, core)


def _my_chip():
    return 2 * lax.axis_index("x") + lax.axis_index("y")


class Exchange:
    def __init__(self, gather, axes):
        self.gather, self.axes = gather, axes

    def land_shape(self, t, src_shape):
        shape = list(src_shape)
        if self.gather:
            shape[self.axes[t]] *= N_CHIPS
            return tuple(shape)
        shape[self.axes[t]] //= N_CHIPS
        return (N_CHIPS,) + tuple(shape)

    def views(self, t, src_ref, land_ref, src_chip, dst_chip):
        ax = self.axes[t]
        if self.gather:
            return src_ref, _slot(land_ref, ax, src_chip, src_ref.shape[ax])
        return _slot(src_ref, ax, dst_chip, src_ref.shape[ax] // N_CHIPS), land_ref.at[src_chip]


def place_own(srcs, ex, name, deps=()):
    nt = len(srcs)

    def body(*refs):
        ins, outs, sems = refs[:nt], refs[nt + len(deps):2 * nt + len(deps)], refs[-1]
        me = _my_chip()
        for k in range(N_CHIPS):
            @pl.when(me == k)
            def _(k=k):
                copies = [pltpu.make_async_copy(*ex.views(t, ins[t], outs[t], k, k), sems.at[t]) for t in range(nt)]
                for cp in copies:
                    cp.start()
                for cp in copies:
                    cp.wait()

    return pl.pallas_call(
        body, name=name, out_shape=[jax.ShapeDtypeStruct(ex.land_shape(t, s.shape), s.dtype) for t, s in enumerate(srcs)],
        in_specs=[HBM_SPEC] * nt + [ANY_SPEC] * len(deps), out_specs=[HBM_SPEC] * nt,
        scratch_shapes=[pltpu.SemaphoreType.DMA((nt,))],
    )(*srcs, *deps)


def exchange_start(srcs, lands, ex, groups, name):
    nt, ng = len(srcs), len(groups)

    def body(*refs):
        ins, land_refs, outs = refs[:nt], refs[nt:2 * nt], refs[2 * nt:]
        sems, token = outs[:2 * ng], outs[-1]
        core = lax.axis_index("c")
        me = _my_chip()
        for k in range(N_CHIPS):
            @pl.when(me == k)
            def _(k=k):
                for gi, group in enumerate(groups):
                    for ti, t in enumerate(group):
                        for j in range(N_CHIPS):
                            if j != k:
                                src, dst = ex.views(t, ins[t], land_refs[t], k, j)
                                pltpu.make_async_remote_copy(
                                    src_ref=src, dst_ref=dst, send_sem=sems[2 * gi].at[ti * N_CHIPS + j],
                                    recv_sem=sems[2 * gi + 1].at[ti * N_CHIPS + k], device_id=_chip_of(j, core),
                                    device_id_type=MESH).start()
        token[...] = jnp.zeros_like(token)

    sem_shapes = []
    for group in groups:
        sem_shapes += [pltpu.SemaphoreType.DMA((len(group) * N_CHIPS,))] * 2
    thru = [pltpu.HBM(a.shape, a.dtype) for a in list(srcs) + list(lands)]
    outs = pl.pallas_call(
        body, name=name, out_shape=sem_shapes + thru + [jax.ShapeDtypeStruct((SUBLANE, LANE), F32)],
        in_specs=[HBM_SPEC] * (2 * nt),
        out_specs=[SEM_SPEC] * (2 * ng) + [HBM_SPEC] * (2 * nt) + [pl.BlockSpec(memory_space=pltpu.VMEM)],
        input_output_aliases={i: 2 * ng + i for i in range(2 * nt)},
        compiler_params=pltpu.CompilerParams(has_side_effects=DATAFLOW),
    )(*[_hbm(a) for a in srcs], *[_hbm(a) for a in lands])
    sems = [(outs[2 * gi], outs[2 * gi + 1]) for gi in range(ng)]
    return sems, outs[2 * ng:2 * ng + nt], outs[2 * ng + nt:2 * ng + 2 * nt], outs[-1]


def exchange_wait(srcs, lands, tensors, ex, sems, after, name):
    n = len(srcs)
    send_sems, recv_sems = sems

    def body(*refs):
        ins, land_refs, send_ref, recv_ref = refs[:n], refs[n:2 * n], refs[2 * n], refs[2 * n + 1]
        core = lax.axis_index("c")
        me = _my_chip()
        for k in range(N_CHIPS):
            @pl.when(me == k)
            def _(k=k):
                for ti, t in enumerate(tensors):
                    for j in range(N_CHIPS):
                        if j != k:
                            src, _ = ex.views(t, ins[ti], land_refs[ti], k, j)
                            _, dst = ex.views(t, ins[ti], land_refs[ti], j, k)
                            copy = pltpu.make_async_remote_copy(
                                src_ref=src, dst_ref=dst, send_sem=send_ref.at[ti * N_CHIPS + j],
                                recv_sem=recv_ref.at[ti * N_CHIPS + j],
                                device_id=_chip_of(j, core), device_id_type=MESH)
                            copy.wait_send()
                            copy.wait_recv()

    thru = [pltpu.HBM(a.shape, a.dtype) for a in list(srcs) + list(lands)]
    outs = pl.pallas_call(
        body, name=name, out_shape=thru, in_specs=[HBM_SPEC] * (2 * n) + [SEM_SPEC, SEM_SPEC] + [ANY_SPEC] * len(after),
        out_specs=[HBM_SPEC] * (2 * n), input_output_aliases={i: i for i in range(2 * n)},
        compiler_params=pltpu.CompilerParams(has_side_effects=DATAFLOW),
    )(*srcs, *lands, send_sems, recv_sems, *after)
    return outs[n:]


def swap_with_sibling(arrs, name):
    nt = len(arrs)

    def body(*refs):
        ins, outs = refs[:nt], refs[nt:2 * nt]
        send_sems, recv_sems = refs[2 * nt:]
        sibling = (lax.axis_index("x"), lax.axis_index("y"), 1 - lax.axis_index("c"))
        copies = [pltpu.make_async_remote_copy(src_ref=ins[t], dst_ref=outs[t], send_sem=send_sems.at[t],
                                               recv_sem=recv_sems.at[t], device_id=sibling, device_id_type=MESH)
                  for t in range(nt)]
        for cp in copies:
            cp.start()
        for cp in copies:
            cp.wait()

    return pl.pallas_call(
        body, name=name, out_shape=[jax.ShapeDtypeStruct(a.shape, a.dtype) for a in arrs],
        in_specs=[HBM_SPEC] * nt, out_specs=[HBM_SPEC] * nt,
        scratch_shapes=[pltpu.SemaphoreType.DMA((nt,)), pltpu.SemaphoreType.DMA((nt,))],
        compiler_params=pltpu.CompilerParams(has_side_effects=True),
    )(*arrs)


def sum_pieces(stack, name):
    _, r, c = stack.shape
    tr = _pick(r, ROW_TILES)
    tc = _pick(c, (512, 256, 128))

    def body(s_ref, o_ref):
        acc = s_ref[0].astype(F32)
        for k in range(1, N_CHIPS):
            acc = acc + s_ref[k].astype(F32)
        o_ref[...] = acc

    return pl.pallas_call(
        body, name=name, grid=(r // tr, c // tc),
        in_specs=[pl.BlockSpec((N_CHIPS, tr, tc), lambda i, j: (0, i, j))],
        out_specs=pl.BlockSpec((tr, tc), lambda i, j: (i, j)),
        out_shape=jax.ShapeDtypeStruct((r, c), F32), compiler_params=_params("parallel", "parallel"),
    )(stack)


def all_reduce_small(x, name):
    rows, lanes = x.shape

    def body(x_ref, sum_ref, gath_ref, send_sems, recv_sems, local_sem):
        cx, cy, cc = lax.axis_index("x"), lax.axis_index("y"), lax.axis_index("c")
        me, sibling = (cx, cy, cc), (cx, cy, 1 - cc)
        chips = [(1 - cx, cy), (cx, 1 - cy), (1 - cx, 1 - cy)]

        def block(px, py, pc):
            return gath_ref.at[pl.ds(pl.multiple_of((4 * px + 2 * py + pc) * rows, SUBLANE), rows), :]

        def copy(k, blk, to, src=None):
            return pltpu.make_async_remote_copy(
                src_ref=block(*blk) if src is None else src, dst_ref=block(*blk),
                send_sem=send_sems.at[k], recv_sem=recv_sems.at[k], device_id=to, device_id_type=MESH)

        mine = pltpu.make_async_copy(x_ref, block(*me), local_sem)
        mine.start()
        first = [copy(0, me, sibling, src=x_ref)]
        first += [copy(1 + j, me, (*chip, cc), src=x_ref) for j, chip in enumerate(chips)]
        for cp in first:
            cp.start()
        passed = [copy(4 + j, (*chip, cc), sibling) for j, chip in enumerate(chips)]
        for j, chip in enumerate(chips):
            copy(1 + j, (*chip, cc), me).wait_recv()
            passed[j].start()
        copy(0, sibling, me).wait_recv()
        for j, chip in enumerate(chips):
            copy(4 + j, (*chip, 1 - cc), me).wait_recv()
        for cp in first + passed:
            cp.wait_send()
        mine.wait()
        acc = gath_ref[pl.ds(0, rows), :]
        for k in range(1, N_DEV):
            acc = acc + gath_ref[pl.ds(k * rows, rows), :]
        sum_ref[...] = acc

    return pl.pallas_call(
        body, name=name, out_shape=jax.ShapeDtypeStruct((rows, lanes), F32),
        in_specs=[pl.BlockSpec(memory_space=pltpu.VMEM)], out_specs=pl.BlockSpec(memory_space=pltpu.VMEM),
        scratch_shapes=[pltpu.VMEM((N_DEV * rows, lanes), F32), pltpu.SemaphoreType.DMA((7,)),
                        pltpu.SemaphoreType.DMA((7,)), pltpu.SemaphoreType.DMA],
        compiler_params=pltpu.CompilerParams(has_side_effects=True, vmem_limit_bytes=VMEM_LIMIT),
    )(x)


def _adamw(w, g, m, v):
    m = ADAM_B1 * m + (1.0 - ADAM_B1) * g
    v = ADAM_B2 * v + (1.0 - ADAM_B2) * (g * g)
    m_hat = m / (1.0 - ADAM_B1 ** ADAM_STEP)
    v_hat = v / (1.0 - ADAM_B2 ** ADAM_STEP)
    delta = -ADAM_LR * (m_hat / (jnp.sqrt(v_hat) + ADAM_EPS) + ADAM_WD * w)
    return delta, m, v


def adam_big(p_own, p_sib, w, m, v, layer, stacks, name):
    r, c = p_own.shape
    tr = _pick(r, ROW_TILES)
    tc = _pick(c, (512, 256, 128))

    def body(p_ref, q_ref, w_ref, m_ref, v_ref, *rest):
        g_out, d_out, m_out, v_out = rest[4:]
        g = p_ref[...] + q_ref[...]
        delta, m_new, v_new = _adamw(w_ref[...], g, m_ref[...], v_ref[...])
        g_out[...] = g
        d_out[...] = delta
        m_out[...] = m_new
        v_out[...] = v_new

    flat = pl.BlockSpec((tr, tc), lambda i, j: (i, j))
    layered = pl.BlockSpec((None, tr, tc), lambda i, j: (layer, i, j))
    anyspace = pl.BlockSpec(memory_space=pl.ANY)
    out = jax.ShapeDtypeStruct(w.shape, F32)
    return pl.pallas_call(
        body, name=name, grid=(r // tr, c // tc),
        in_specs=[flat, flat, layered, layered, layered] + [anyspace] * 4,
        out_specs=[layered] * 4, out_shape=[out] * 4, input_output_aliases={5: 0, 6: 1, 7: 2, 8: 3},
        compiler_params=_params("parallel", "parallel"),
    )(p_own, p_sib, w, m, v, *stacks)


def adam_small(g, w, m, v, name):
    rows, lanes = g.shape
    tr = _pick(rows, (512,) + ROW_TILES)

    def body(g_ref, w_ref, m_ref, v_ref, d_out, m_out, v_out):
        delta, m_new, v_new = _adamw(w_ref[...], g_ref[...], m_ref[...], v_ref[...])
        d_out[...] = delta
        m_out[...] = m_new
        v_out[...] = v_new

    out = jax.ShapeDtypeStruct((rows, lanes), F32)
    return pl.pallas_call(
        body, name=name, grid=(rows // tr,), in_specs=[_row(tr, lanes)] * 4, out_specs=[_row(tr, lanes)] * 3,
        out_shape=[out] * 3, compiler_params=_params("parallel"),
    )(g, w, m, v)


def ffn_fwd(x, g_pre, w_up, w_down, g_post, tag, deps=()):
    h = rms_fwd(x, g_pre, name=f"{tag}_rms_fwd", deps=deps)
    z = matmul(h, w_up, out_dtype=BF16, name=f"{tag}_up")
    a = swiglu_fwd(z, name=f"{tag}_swiglu_fwd")
    f = matmul(a, w_down, out_dtype=F32, name=f"{tag}_down")
    y = res_rms_fwd(x, f, g_post, MACARON_WEIGHT, name=f"{tag}_res_fwd")
    return y, (x, h, z, a, f)


def ffn_bwd(dy, saved, g_pre, w_up, w_down, g_post, tag, deps=()):
    x, h, z, a, f = saved
    df, dg_post = rms_bwd(f, g_post, dy, MACARON_WEIGHT, None, BF16, name=f"{tag}_res_bwd", deps=deps)
    da = matmul(df, w_down, tb=True, out_dtype=BF16, name=f"{tag}_down_dx")
    dw_down = matmul(a, df, ta=True, out_dtype=BF16, name=f"{tag}_down_dw")
    dz = swiglu_bwd(z, da, name=f"{tag}_swiglu_bwd")
    dh = matmul(dz, w_up, tb=True, out_dtype=F32, name=f"{tag}_up_dx")
    dw_up = matmul(h, dz, ta=True, out_dtype=BF16, name=f"{tag}_up_dw")
    dx, dg_pre = rms_bwd(x, g_pre, dh, 1.0, dy, F32, name=f"{tag}_rms_bwd")
    return dx, dw_up, dw_down, dg_pre, dg_post


def mixer_fwd(x, wt, sm, dims):
    pw, sw, d = dims
    h = rms_fwd(x, sm['g_mix_pre'], name="mix_rms_fwd")
    z = matmul(h, wt['w_in'], out_dtype=F32, name="mix_in")
    dd, e, yp = pool_fwd(z, wt['pool_group_w'], sm['pool_scale'], pw, name="mix_pool_fwd")
    ya = matmul(yp, wt['w_pool_out'], out_dtype=BF16, name="mix_pool_out")
    sg = sgu_fwd(z, sm['sgu_v_gain'], sm['sgu_w_s'], sm['sgu_b_s'], pw, sw, name="mix_sgu_fwd")
    yb = matmul(sg, wt['w_sgu_out'], out_dtype=BF16, name="mix_sgu_out")
    m = gate_fwd(z, ya, yb, pw + 2 * sw, pw + 2 * sw + d, name="mix_gate_fwd")
    o = matmul(m, wt['w_out'], out_dtype=F32, name="mix_out")
    y = res_rms_fwd(x, o, sm['g_mix_post'], 1.0, name="mix_res_fwd")
    return y, (x, h, z, dd, e, yp, sg, ya, yb, m, o)


def mixer_bwd(dy, saved, wt, sm, dims, deps=()):
    pw, sw, d = dims
    x, h, z, dd, e, yp, sg, ya, yb, m, o = saved
    grads = {}
    do, grads['g_mix_post'] = rms_bwd(o, sm['g_mix_post'], dy, 1.0, None, BF16, name="mix_res_bwd", deps=deps)
    dm = matmul(do, wt['w_out'], tb=True, out_dtype=BF16, name="mix_out_dx")
    grads['w_out'] = matmul(m, do, ta=True, out_dtype=BF16, name="mix_out_dw")
    dya, dyb, dga, dgb = gate_bwd(z, ya, yb, dm, pw + 2 * sw, pw + 2 * sw + d, name="mix_gate_bwd")
    dyp = matmul(dya, wt['w_pool_out'], tb=True, out_dtype=BF16, name="mix_pool_out_dx")
    grads['w_pool_out'] = matmul(yp, dya, ta=True, out_dtype=BF16, name="mix_pool_out_dw")
    dp, dwg, dscale = pool_bwd(dyp, e, dd, wt['pool_group_w'], sm['pool_scale'], name="mix_pool_bwd")
    grads['pool_group_w'] = dwg.astype(BF16)
    grads['pool_scale'] = dscale.reshape(pw)
    dsg = matmul(dyb, wt['w_sgu_out'], tb=True, out_dtype=BF16, name="mix_sgu_out_dx")
    grads['w_sgu_out'] = matmul(sg, dyb, ta=True, out_dtype=BF16, name="mix_sgu_out_dw")
    du, dv, grads['sgu_w_s'], grads['sgu_b_s'], grads['sgu_v_gain'] = sgu_bwd(
        z, dsg, sm['sgu_v_gain'], sm['sgu_w_s'], sm['sgu_b_s'], pw, sw, name="mix_sgu_bwd")
    dz = jnp.concatenate([dp, du, dv, dga, dgb], axis=1)
    dh = matmul(dz, wt['w_in'], tb=True, out_dtype=F32, name="mix_in_dx")
    grads['w_in'] = matmul(h, dz, ta=True, out_dtype=BF16, name="mix_in_dw")
    dx, grads['g_mix_pre'] = rms_bwd(x, sm['g_mix_pre'], dh, 1.0, dy, F32, name="mix_rms_bwd")
    return dx, grads


def _as_rows(a):
    return a.reshape(a.shape[0], -1, a.shape[-1])


def kernel(x, g_ffn1_pre, w_ffn1_up, w_ffn1_down, g_ffn1_post, g_mix_pre, w_in, pool_group_w, pool_scale, w_pool_out, sgu_v_gain, sgu_w_s, sgu_b_s, w_sgu_out, w_out, g_mix_post, g_ffn2_pre, w_ffn2_up, w_ffn2_down, g_ffn2_post, loss_target, m_g_ffn1_pre, m_w_ffn1_up, m_w_ffn1_down, m_g_ffn1_post, m_g_mix_pre, m_w_in, m_pool_group_w, m_pool_scale, m_w_pool_out, m_sgu_v_gain, m_sgu_w_s, m_sgu_b_s, m_w_sgu_out, m_w_out, m_g_mix_post, m_g_ffn2_pre, m_w_ffn2_up, m_w_ffn2_down, m_g_ffn2_post, v_g_ffn1_pre, v_w_ffn1_up, v_w_ffn1_down, v_g_ffn1_post, v_g_mix_pre, v_w_in, v_pool_group_w, v_pool_scale, v_w_pool_out, v_sgu_v_gain, v_sgu_w_s, v_sgu_b_s, v_w_sgu_out, v_w_out, v_g_mix_post, v_g_ffn2_pre, v_w_ffn2_up, v_w_ffn2_down, v_g_ffn2_post):
    w = dict(zip(WEIGHTS, (g_ffn1_pre, w_ffn1_up, w_ffn1_down, g_ffn1_post, g_mix_pre, w_in, pool_group_w, pool_scale, w_pool_out, sgu_v_gain, sgu_w_s, sgu_b_s, w_sgu_out, w_out, g_mix_post, g_ffn2_pre, w_ffn2_up, w_ffn2_down, g_ffn2_post)))
    mom = dict(zip(WEIGHTS, (m_g_ffn1_pre, m_w_ffn1_up, m_w_ffn1_down, m_g_ffn1_post, m_g_mix_pre, m_w_in, m_pool_group_w, m_pool_scale, m_w_pool_out, m_sgu_v_gain, m_sgu_w_s, m_sgu_b_s, m_w_sgu_out, m_w_out, m_g_mix_post, m_g_ffn2_pre, m_w_ffn2_up, m_w_ffn2_down, m_g_ffn2_post)))
    var = dict(zip(WEIGHTS, (v_g_ffn1_pre, v_w_ffn1_up, v_w_ffn1_down, v_g_ffn1_post, v_g_mix_pre, v_w_in, v_pool_group_w, v_pool_scale, v_w_pool_out, v_sgu_v_gain, v_sgu_w_s, v_sgu_b_s, v_w_sgu_out, v_w_out, v_g_mix_post, v_g_ffn2_pre, v_w_ffn2_up, v_w_ffn2_down, v_g_ffn2_post)))
    depth = g_ffn1_pre.shape[0]
    d = x.shape[-1]
    pw = pool_scale.shape[-1]
    sw = sgu_v_gain.shape[-1]
    dims = (pw, sw, d)
    axes = [BIG_AXIS[n] for n in BIG]

    shards_bf16 = {n: w[n].astype(BF16) for n in BIG}
    gather = Exchange(True, axes)
    group_ids = [[BIG.index(n) for n in group] for group in BLOCK_WEIGHTS]

    def start_gather(l, deps=()):
        shards = [shards_bf16[n][l] for n in BIG]
        lands = place_own(shards, gather, name="place_shards", deps=deps)
        return exchange_start(shards, lands, gather, group_ids, name=f"gather_start_l{l}")

    def finish_gather(started, l, gi, after):
        sems, srcs, lands, _ = started
        ids = group_ids[gi]
        got = exchange_wait([srcs[t] for t in ids], [lands[t] for t in ids], ids, gather, sems[gi], (after,),
                            name=f"gather_wait_l{l}_b{gi}")
        return dict(zip(BLOCK_WEIGHTS[gi], got))

    small = [{n: w[n][l] for n in SMALL} for l in range(depth)]

    act = x[0]
    saved, full = [], []
    started = start_gather(0)
    for l in range(depth):
        ahead = start_gather(l + 1, (started[3],)) if l + 1 < depth else None
        deps = (ahead[3],) if ahead is not None else ()
        sm, wt = small[l], {}
        wt.update(finish_gather(started, l, 0, act))
        act, s1 = ffn_fwd(act, sm['g_ffn1_pre'], wt['w_ffn1_up'], wt['w_ffn1_down'], sm['g_ffn1_post'], "ffn1", deps)
        wt.update(finish_gather(started, l, 1, act))
        act, s2 = mixer_fwd(act, wt, sm, dims)
        wt.update(finish_gather(started, l, 2, act))
        act, s3 = ffn_fwd(act, sm['g_ffn2_pre'], wt['w_ffn2_up'], wt['w_ffn2_down'], sm['g_ffn2_post'], "ffn2")
        saved.append((s1, s2, s3))
        full.append(wt)
        started = ahead
    dact, sq_sum = loss_grad(act, loss_target[0], name="loss_grad")
    loss = lax.psum(0.5 * sq_sum / d, ("x", "y", "c"))

    rows3 = {n: (_as_rows(w[n]), _as_rows(mom[n]), _as_rows(var[n])) for n in BIG}
    stacks = {n: tuple(lax.empty(rows3[n][0].shape, F32) for _ in range(4)) for n in BIG}
    small_grads = [{} for _ in range(depth)]

    def start_scatter(names, grads, l, block):
        ex = Exchange(False, [BIG_AXIS[n] for n in names])
        srcs = [grads[n] for n in names]
        lands = place_own(srcs, ex, name=f"place_pieces_{len(names)}")
        return (names, ex, l, block) + exchange_start(srcs, lands, ex, [list(range(len(names)))],
                                                      name=f"scatter_start_l{l}_b{block}")

    def finish_scatter(pending, after):
        names, ex, l, block, sems, srcs, lands, _ = pending
        pieces = exchange_wait(srcs, lands, list(range(len(names))), ex, sems[0], after,
                               name=f"scatter_wait_l{l}_b{block}")
        plane = [sum_pieces(p.reshape(N_CHIPS, -1, p.shape[-1]), name=f"sum_{n}") for n, p in zip(names, pieces)]
        other = swap_with_sibling(plane, name=f"swap_plane_sums_{len(names)}")
        for n, p_own, p_sib in zip(names, plane, other):
            stacks[n] = tuple(adam_big(p_own, p_sib, *rows3[n], l, stacks[n], name=f"adam_{n}"))
        return stacks[names[-1]][0]

    pending, done = None, ()
    for l in reversed(range(depth)):
        wt, sm = full[l], small[l]
        s1, s2, s3 = saved[l]
        for block in (2, 1, 0):
            deps = (pending[-1],) if pending is not None else ()
            g = {}
            if block == 2:
                dnew, g['w_ffn2_up'], g['w_ffn2_down'], g['g_ffn2_pre'], g['g_ffn2_post'] = ffn_bwd(
                    dact, s3, sm['g_ffn2_pre'], wt['w_ffn2_up'], wt['w_ffn2_down'], sm['g_ffn2_post'], "ffn2", deps)
            elif block == 1:
                dnew, g = mixer_bwd(dact, s2, wt, sm, dims, deps)
            else:
                dnew, g['w_ffn1_up'], g['w_ffn1_down'], g['g_ffn1_pre'], g['g_ffn1_post'] = ffn_bwd(
                    dact, s1, sm['g_ffn1_pre'], wt['w_ffn1_up'], wt['w_ffn1_down'], sm['g_ffn1_post'], "ffn1", deps)
            small_grads[l].update({n: g[n] for n in g if n in SMALL})
            if pending is not None:
                done = (finish_scatter(pending, (dnew,) + done),)
            pending = start_scatter(BLOCK_WEIGHTS[block], g, l, block)
            dact = dnew
    finish_scatter(pending, (dact,) + done)

    def flat(tree):
        v = jnp.concatenate([tree[n].reshape(-1).astype(F32) for n in SMALL])
        pad = (-v.shape[0]) % (SUBLANE * LANE)
        return jnp.pad(v, (0, pad)).reshape(-1, LANE)

    g_small = all_reduce_small(flat({n: jnp.stack([small_grads[l][n] for l in range(depth)]) for n in SMALL}),
                               name="all_reduce_small")
    d_small, m_small, v_small = adam_small(g_small, flat(w), flat(mom), flat(var), name="adam_small")

    def unflat(block):
        v, out, at = block.reshape(-1), {}, 0
        for n in SMALL:
            out[n] = v[at:at + w[n].size].reshape(w[n].shape)
            at += w[n].size
        return out

    result = [{}, {}, {}, {}]
    for tree, block in zip(result, (g_small, d_small, m_small, v_small)):
        tree.update(unflat(block))
    for n in BIG:
        for tree, stack in zip(result, stacks[n]):
            tree[n] = stack.reshape(w[n].shape)
    return (loss, dact.reshape(x.shape), *[tree[n] for tree in result for n in WEIGHTS])
```

```python
import math

import jax
import jax.numpy as jnp
from jax import lax
from jax.experimental import pallas as pl
from jax.experimental.pallas import tpu as pltpu

F32 = jnp.float32
BF16 = jnp.bfloat16
MESH = pl.DeviceIdType.MESH

EPS = 1e-6
MACARON_WEIGHT = 0.5
POOL_WINDOWS = (2, 4, 8, 16)
POOL_HALO = 16
ADAM_LR = 0.001
ADAM_B1 = 0.9
ADAM_B2 = 0.999
ADAM_EPS = 1e-08
ADAM_WD = 0.01
ADAM_STEP = 10
GELU_K = math.sqrt(2.0 / math.pi)
GELU_C = 0.044715

N_CHIPS = 4
N_DEV = 8
V7X_VMEM_BYTES = 64 * 1024 * 1024
VMEM_LIMIT = (V7X_VMEM_BYTES * 3) // 4
LANE = 128
SUBLANE = 8

WEIGHTS = ['g_ffn1_pre', 'w_ffn1_up', 'w_ffn1_down', 'g_ffn1_post', 'g_mix_pre', 'w_in', 'pool_group_w',
           'pool_scale', 'w_pool_out', 'sgu_v_gain', 'sgu_w_s', 'sgu_b_s', 'w_sgu_out', 'w_out', 'g_mix_post',
           'g_ffn2_pre', 'w_ffn2_up', 'w_ffn2_down', 'g_ffn2_post']
BIG_AXIS = {'w_ffn1_up': 1, 'w_ffn1_down': 0, 'w_in': 1, 'pool_group_w': 1, 'w_pool_out': 1, 'w_sgu_out': 1,
            'w_out': 0, 'w_ffn2_up': 1, 'w_ffn2_down': 0}
BIG = list(BIG_AXIS)
BLOCK_WEIGHTS = [['w_ffn1_up', 'w_ffn1_down'], ['w_in', 'pool_group_w', 'w_pool_out', 'w_sgu_out', 'w_out'],
                 ['w_ffn2_up', 'w_ffn2_down']]
SMALL = [n for n in WEIGHTS if n not in BIG_AXIS]


def _pick(dim, cands):
    for c in cands:
        if dim % c == 0:
            return c
    return dim


def _params(*sem):
    return pltpu.CompilerParams(dimension_semantics=sem if sem else None, vmem_limit_bytes=VMEM_LIMIT)


def _sigmoid(x):
    return 1.0 / (1.0 + jnp.exp(-x))


def _gelu(x):
    t = jnp.tanh(GELU_K * (x + GELU_C * (x * x * x)))
    return x * (0.5 * (1.0 + t)), t


def _gelu_grad(x, t):
    return 0.5 * (1.0 + t) + (0.5 * x) * (1.0 - t * t) * (GELU_K * (1.0 + (3.0 * GELU_C) * (x * x)))


MATMUL_MN_TILES = (1024, 1408, 512, 256, 128)
MATMUL_K_TILES = (2048, 2816, 1024, 512, 256, 128)


def matmul(a, b, *, ta=False, tb=False, out_dtype=BF16, name):
    m_dim, k_dim = (a.shape[1], a.shape[0]) if ta else a.shape
    n_dim = b.shape[0] if tb else b.shape[1]
    tm = _pick(m_dim, MATMUL_MN_TILES)
    tn = _pick(n_dim, MATMUL_MN_TILES)
    tk = _pick(k_dim, MATMUL_K_TILES)
    nk = k_dim // tk
    dims = (((0 if ta else 1,), (1 if tb else 0,)), ((), ()))

    def body(a_ref, b_ref, o_ref, acc_ref):
        k = pl.program_id(2)

        @pl.when(k == 0)
        def _():
            acc_ref[...] = jnp.zeros_like(acc_ref)

        acc_ref[...] += lax.dot_general(a_ref[...], b_ref[...], dims, preferred_element_type=F32)

        @pl.when(k == nk - 1)
        def _():
            o_ref[...] = acc_ref[...].astype(o_ref.dtype)

    a_spec = pl.BlockSpec((tk, tm), lambda i, j, k: (k, i)) if ta else pl.BlockSpec((tm, tk), lambda i, j, k: (i, k))
    b_spec = pl.BlockSpec((tn, tk), lambda i, j, k: (j, k)) if tb else pl.BlockSpec((tk, tn), lambda i, j, k: (k, j))
    return pl.pallas_call(
        body, name=name, grid=(m_dim // tm, n_dim // tn, nk),
        in_specs=[a_spec, b_spec], out_specs=pl.BlockSpec((tm, tn), lambda i, j, k: (i, j)),
        out_shape=jax.ShapeDtypeStruct((m_dim, n_dim), out_dtype),
        scratch_shapes=[pltpu.VMEM((tm, tn), F32)],
        compiler_params=_params("parallel", "parallel", "arbitrary"),
    )(a, b)


ROW_TILES = (256, 128, 64, 32, 16, 8)


def _row(tr, width):
    return pl.BlockSpec((tr, width), lambda i: (i, 0))


def _vec(width):
    return pl.BlockSpec((1, width), lambda i: (0, 0))


def rms_fwd(x, g, name, deps=()):
    t_dim, d = x.shape
    tr = _pick(t_dim, ROW_TILES)

    def body(x_ref, g_ref, *rest):
        o_ref = rest[-1]
        xv = x_ref[...]
        r = lax.rsqrt(jnp.mean(xv * xv, axis=-1, keepdims=True) + EPS)
        o_ref[...] = ((xv * r) * g_ref[...]).astype(o_ref.dtype)

    return pl.pallas_call(
        body, name=name, grid=(t_dim // tr,), in_specs=[_row(tr, d), _vec(d)] + [ANY_SPEC] * len(deps),
        out_specs=_row(tr, d), out_shape=jax.ShapeDtypeStruct((t_dim, d), BF16), compiler_params=_params("parallel"),
    )(x, g.reshape(1, d), *deps)


def res_rms_fwd(x, f, g, weight, name):
    t_dim, d = x.shape
    tr = _pick(t_dim, ROW_TILES)

    def body(x_ref, f_ref, g_ref, o_ref):
        fv = f_ref[...]
        r = lax.rsqrt(jnp.mean(fv * fv, axis=-1, keepdims=True) + EPS)
        o_ref[...] = x_ref[...] + weight * ((fv * r) * g_ref[...])

    return pl.pallas_call(
        body, name=name, grid=(t_dim // tr,), in_specs=[_row(tr, d), _row(tr, d), _vec(d)], out_specs=_row(tr, d),
        out_shape=jax.ShapeDtypeStruct((t_dim, d), F32), compiler_params=_params("parallel"),
    )(x, f, g.reshape(1, d))


def rms_bwd(f, g, dy, weight, resid, out_dtype, name, deps=()):
    t_dim, d = f.shape
    tr = _pick(t_dim, ROW_TILES)
    has_resid = resid is not None

    def body(*refs):
        o_ref, dg_ref = refs[-2:]
        if has_resid:
            f_ref, g_ref, dy_ref, res_ref = refs[:4]
        else:
            f_ref, g_ref, dy_ref = refs[:3]

        @pl.when(pl.program_id(0) == 0)
        def _():
            dg_ref[...] = jnp.zeros_like(dg_ref)

        fv = f_ref[...]
        r = lax.rsqrt(jnp.mean(fv * fv, axis=-1, keepdims=True) + EPS)
        n = fv * r
        dyw = dy_ref[...] * weight
        dn = dyw * g_ref[...]
        df = r * (dn - n * jnp.mean(dn * n, axis=-1, keepdims=True))
        if has_resid:
            df = df + res_ref[...]
        o_ref[...] = df.astype(o_ref.dtype)
        dg_ref[...] += jnp.sum(dyw * n, axis=0, keepdims=True)

    ins = [f, g.reshape(1, d), dy] + ([resid] if has_resid else []) + list(deps)
    in_specs = [_row(tr, d), _vec(d), _row(tr, d)] + ([_row(tr, d)] if has_resid else []) + [ANY_SPEC] * len(deps)
    out, dg = pl.pallas_call(
        body, name=name, grid=(t_dim // tr,), in_specs=in_specs, out_specs=[_row(tr, d), _vec(d)],
        out_shape=[jax.ShapeDtypeStruct((t_dim, d), out_dtype), jax.ShapeDtypeStruct((1, d), F32)],
        compiler_params=_params("arbitrary"),
    )(*ins)
    return out, dg.reshape(d)


def loss_grad(y, target, name):
    t_dim, d = y.shape
    tr = _pick(t_dim, ROW_TILES)
    inv_d = 1.0 / d

    def body(y_ref, t_ref, dy_ref, s_ref):
        @pl.when(pl.program_id(0) == 0)
        def _():
            s_ref[...] = jnp.zeros_like(s_ref)

        e = y_ref[...] - t_ref[...]
        dy_ref[...] = e * inv_d
        s_ref[...] += jnp.sum(e * e)

    dy, s = pl.pallas_call(
        body, name=name, grid=(t_dim // tr,), in_specs=[_row(tr, d), _row(tr, d)],
        out_specs=[_row(tr, d), pl.BlockSpec((SUBLANE, LANE), lambda i: (0, 0))],
        out_shape=[jax.ShapeDtypeStruct((t_dim, d), F32), jax.ShapeDtypeStruct((SUBLANE, LANE), F32)],
        compiler_params=_params("arbitrary"),
    )(y, target)
    return dy, s[0, 0]


def swiglu_fwd(z, name):
    t_dim, f2 = z.shape
    f = f2 // 2
    tr = _pick(t_dim, (512,) + ROW_TILES)
    tc = _pick(f, (512, 256, 128))
    nf = f // tc

    def body(g_ref, u_ref, a_ref):
        g = g_ref[...].astype(F32)
        a_ref[...] = (g * _sigmoid(g) * u_ref[...].astype(F32)).astype(a_ref.dtype)

    return pl.pallas_call(
        body, name=name, grid=(t_dim // tr, nf),
        in_specs=[pl.BlockSpec((tr, tc), lambda i, j: (i, j)), pl.BlockSpec((tr, tc), lambda i, j: (i, j + nf))],
        out_specs=pl.BlockSpec((tr, tc), lambda i, j: (i, j)),
        out_shape=jax.ShapeDtypeStruct((t_dim, f), BF16), compiler_params=_params("parallel", "parallel"),
    )(z, z)


def swiglu_bwd(z, da, name):
    t_dim, f2 = z.shape
    f = f2 // 2
    tr = _pick(t_dim, ROW_TILES)
    tc = _pick(f, (512, 256, 128))

    def body(z_ref, da_ref, o_ref):
        for c in range(f // tc):
            lo = c * tc
            g = z_ref[:, lo:lo + tc].astype(F32)
            u = z_ref[:, f + lo:f + lo + tc].astype(F32)
            da = da_ref[:, lo:lo + tc].astype(F32)
            s = _sigmoid(g)
            o_ref[:, lo:lo + tc] = (da * u * (s * (1.0 + g * (1.0 - s)))).astype(o_ref.dtype)
            o_ref[:, f + lo:f + lo + tc] = (da * (g * s)).astype(o_ref.dtype)

    return pl.pallas_call(
        body, name=name, grid=(t_dim // tr,), in_specs=[_row(tr, f2), _row(tr, f)], out_specs=_row(tr, f2),
        out_shape=jax.ShapeDtypeStruct((t_dim, f2), BF16), compiler_params=_params("parallel"),
    )(z, da)


def gate_fwd(z, ya, yb, off_a, off_b, name):
    t_dim, d = ya.shape
    tr = _pick(t_dim, (512,) + ROW_TILES)
    tc = math.gcd(math.gcd(off_a, off_b), _pick(d, (512, 256, 128)))
    ja, jb = off_a // tc, off_b // tc

    def body(ga_ref, gb_ref, ya_ref, yb_ref, m_ref):
        m = _sigmoid(ga_ref[...]) * ya_ref[...].astype(F32) + _sigmoid(gb_ref[...]) * yb_ref[...].astype(F32)
        m_ref[...] = m.astype(m_ref.dtype)

    blk = pl.BlockSpec((tr, tc), lambda i, j: (i, j))
    return pl.pallas_call(
        body, name=name, grid=(t_dim // tr, d // tc),
        in_specs=[pl.BlockSpec((tr, tc), lambda i, j: (i, j + ja)), pl.BlockSpec((tr, tc), lambda i, j: (i, j + jb)),
                  blk, blk],
        out_specs=blk, out_shape=jax.ShapeDtypeStruct((t_dim, d), BF16),
        compiler_params=_params("parallel", "parallel"),
    )(z, z, ya, yb)


def gate_bwd(z, ya, yb, dm, off_a, off_b, name):
    t_dim, d = ya.shape
    tr = _pick(t_dim, (512,) + ROW_TILES)
    tc = math.gcd(math.gcd(off_a, off_b), _pick(d, (512, 256, 128)))
    ja, jb = off_a // tc, off_b // tc

    def body(ga_ref, gb_ref, ya_ref, yb_ref, dm_ref, dya_ref, dyb_ref, dga_ref, dgb_ref):
        dm = dm_ref[...].astype(F32)
        sa = _sigmoid(ga_ref[...])
        sb = _sigmoid(gb_ref[...])
        dya_ref[...] = (dm * sa).astype(BF16)
        dyb_ref[...] = (dm * sb).astype(BF16)
        dga_ref[...] = (dm * ya_ref[...].astype(F32) * (sa * (1.0 - sa))).astype(BF16)
        dgb_ref[...] = (dm * yb_ref[...].astype(F32) * (sb * (1.0 - sb))).astype(BF16)

    blk = pl.BlockSpec((tr, tc), lambda i, j: (i, j))
    out = jax.ShapeDtypeStruct((t_dim, d), BF16)
    return pl.pallas_call(
        body, name=name, grid=(t_dim // tr, d // tc),
        in_specs=[pl.BlockSpec((tr, tc), lambda i, j: (i, j + ja)), pl.BlockSpec((tr, tc), lambda i, j: (i, j + jb)),
                  blk, blk, blk],
        out_specs=[blk] * 4, out_shape=[out] * 4, compiler_params=_params("parallel", "parallel"),
    )(z, z, ya, yb, dm)


def _window_sums(e, n_rows, forward):
    def shifted(v, k):
        return pltpu.roll(v, (n_rows - k) if forward else k, 0)

    s2 = e + shifted(e, 1)
    s4 = s2 + shifted(s2, 2)
    s8 = s4 + shifted(s4, 4)
    s16 = s8 + shifted(s8, 8)
    return (s2, s4, s8, s16)


def _pool_rows(t_dim):
    return _pick(t_dim, (256, 128, 64, 32, 16))


def pool_fwd(z, w_group, scale, pw, name):
    t_dim = z.shape[0]
    n_groups, c, _ = w_group.shape
    tr = _pool_rows(t_dim)
    per = tr // POOL_HALO

    def body(cur_ref, prev_ref, w_ref, scale_ref, d_ref, e_ref, yp_ref):
        i = pl.program_id(0)
        cur = cur_ref[...]
        prev = jnp.where(i > 0, prev_ref[...], 0.0)
        ext = jnp.concatenate([prev, cur], axis=0)
        sums = _window_sums(ext, tr + POOL_HALO, forward=False)
        pos = (i * tr + 1 + lax.broadcasted_iota(jnp.int32, (tr, 1), 0)).astype(F32)
        for g, w in enumerate(POOL_WINDOWS):
            cols = slice(g * c, (g + 1) * c)
            cnt = jnp.minimum(pos, float(w))
            d = (sums[g][POOL_HALO:, cols] / cnt - cur[:, cols]).astype(BF16)
            e = jnp.dot(d, w_ref[g], preferred_element_type=F32)
            d_ref[:, cols] = d
            e_ref[:, cols] = e.astype(BF16)
            yp_ref[:, cols] = (e * scale_ref[:, cols]).astype(BF16)

    out = jax.ShapeDtypeStruct((t_dim, pw), BF16)
    return pl.pallas_call(
        body, name=name, grid=(t_dim // tr,),
        in_specs=[_row(tr, pw), pl.BlockSpec((POOL_HALO, pw), lambda i: (jnp.maximum(i * per - 1, 0), 0)),
                  pl.BlockSpec((n_groups, c, c), lambda i: (0, 0, 0)), _vec(pw)],
        out_specs=[_row(tr, pw)] * 3, out_shape=[out] * 3, compiler_params=_params("parallel"),
    )(z, z, w_group, scale.reshape(1, pw))


def pool_bwd(dyp, e, d, w_group, scale, name):
    t_dim, pw = dyp.shape
    n_groups, c, _ = w_group.shape
    tr = _pool_rows(t_dim)
    per = tr // POOL_HALO
    n_tiles = t_dim // tr
    last_halo = t_dim // POOL_HALO - 1
    nt_dims = (((1,), (1,)), ((), ()))
    tn_dims = (((0,), (0,)), ((), ()))

    def body(dyp_ref, nxt_ref, e_ref, d_ref, w_ref, scale_ref, dp_ref, dw_ref, dscale_ref):
        i = pl.program_id(0)

        @pl.when(i == 0)
        def _():
            dw_ref[...] = jnp.zeros_like(dw_ref)
            dscale_ref[...] = jnp.zeros_like(dscale_ref)

        dyp_v = dyp_ref[...].astype(F32)
        dscale_ref[...] += jnp.sum(dyp_v * e_ref[...].astype(F32), axis=0, keepdims=True)
        de_cur = dyp_v * scale_ref[...]
        de_nxt = jnp.where(i < n_tiles - 1, nxt_ref[...].astype(F32) * scale_ref[...], 0.0)
        de = jnp.concatenate([de_cur, de_nxt], axis=0).astype(BF16)
        pos = (i * tr + 1 + lax.broadcasted_iota(jnp.int32, (tr + POOL_HALO, 1), 0)).astype(F32)
        for g, w in enumerate(POOL_WINDOWS):
            cols = slice(g * c, (g + 1) * c)
            de_g = de[:, cols]
            dd = lax.dot_general(de_g, w_ref[g], nt_dims, preferred_element_type=F32)
            dw_ref[g] += lax.dot_general(d_ref[:, cols], de_g[:tr], tn_dims, preferred_element_type=F32)
            q = dd / jnp.minimum(pos, float(w))
            win = _window_sums(q, tr + POOL_HALO, forward=True)[g]
            dp_ref[:, cols] = (win[:tr] - dd[:tr]).astype(BF16)

    return pl.pallas_call(
        body, name=name, grid=(n_tiles,),
        in_specs=[_row(tr, pw), pl.BlockSpec((POOL_HALO, pw), lambda i: (jnp.minimum((i + 1) * per, last_halo), 0)),
                  _row(tr, pw), _row(tr, pw), pl.BlockSpec((n_groups, c, c), lambda i: (0, 0, 0)), _vec(pw)],
        out_specs=[_row(tr, pw), pl.BlockSpec((n_groups, c, c), lambda i: (0, 0, 0)), _vec(pw)],
        out_shape=[jax.ShapeDtypeStruct((t_dim, pw), BF16), jax.ShapeDtypeStruct((n_groups, c, c), F32),
                   jax.ShapeDtypeStruct((1, pw), F32)],
        compiler_params=_params("arbitrary"),
    )(dyp, dyp, e, d, w_group, scale.reshape(1, pw))


def _sgu_rows(t_dim, chunk):
    return chunk * _pick(t_dim // chunk, (2, 1))


def _tril(chunk):
    return lax.broadcasted_iota(jnp.int32, (chunk, chunk), 0) >= lax.broadcasted_iota(jnp.int32, (chunk, chunk), 1)


def sgu_fwd(z, gain, w_s, b_s, pw, sw, name):
    t_dim = z.shape[0]
    n_heads, chunk, _ = w_s.shape
    hd = sw // n_heads
    tr = _sgu_rows(t_dim, chunk)
    ju = pw // sw

    def body(u_ref, v_ref, gain_ref, w_ref, bt_ref, sg_ref):
        ug, _ = _gelu(u_ref[...])
        vg, _ = _gelu(v_ref[...])
        r = lax.rsqrt(jnp.mean(vg * vg, axis=-1, keepdims=True) + EPS)
        vn = ((vg * r) * gain_ref[...]).astype(BF16)
        tri = _tril(chunk)
        for h in range(n_heads):
            wm = jnp.where(tri, w_ref[h], 0.0).astype(BF16)
            cols = slice(h * hd, (h + 1) * hd)
            for ch in range(tr // chunk):
                rows = slice(ch * chunk, (ch + 1) * chunk)
                s = jnp.dot(wm, vn[rows, cols], preferred_element_type=F32) + bt_ref[:, h:h + 1]
                sg_ref[rows, cols] = (ug[rows, cols] * s).astype(BF16)

    return pl.pallas_call(
        body, name=name, grid=(t_dim // tr,),
        in_specs=[pl.BlockSpec((tr, sw), lambda i: (i, ju)), pl.BlockSpec((tr, sw), lambda i: (i, ju + 1)), _vec(sw),
                  pl.BlockSpec((n_heads, chunk, chunk), lambda i: (0, 0, 0)),
                  pl.BlockSpec((chunk, n_heads), lambda i: (0, 0))],
        out_specs=_row(tr, sw), out_shape=jax.ShapeDtypeStruct((t_dim, sw), BF16),
        compiler_params=_params("parallel"),
    )(z, z, gain.reshape(1, sw), w_s, b_s.T)


def sgu_bwd(z, dsg, gain, w_s, b_s, pw, sw, name):
    t_dim = z.shape[0]
    n_heads, chunk, _ = w_s.shape
    hd = sw // n_heads
    tr = _sgu_rows(t_dim, chunk)
    ju = pw // sw
    nt_dims = (((1,), (1,)), ((), ()))
    tn_dims = (((0,), (0,)), ((), ()))

    def body(u_ref, v_ref, dsg_ref, gain_ref, w_ref, bt_ref, du_ref, dv_ref, dw_ref, dbt_ref, dgain_ref,
             dvn_ref, dug_ref):
        @pl.when(pl.program_id(0) == 0)
        def _():
            dw_ref[...] = jnp.zeros_like(dw_ref)
            dbt_ref[...] = jnp.zeros_like(dbt_ref)
            dgain_ref[...] = jnp.zeros_like(dgain_ref)

        u = u_ref[...]
        v = v_ref[...]
        ug, tu = _gelu(u)
        vg, tv = _gelu(v)
        r = lax.rsqrt(jnp.mean(vg * vg, axis=-1, keepdims=True) + EPS)
        n = vg * r
        gain_v = gain_ref[...]
        vn = (n * gain_v).astype(BF16)
        dsg_v = dsg_ref[...].astype(F32)
        tri = _tril(chunk)
        for h in range(n_heads):
            wm = jnp.where(tri, w_ref[h], 0.0).astype(BF16)
            cols = slice(h * hd, (h + 1) * hd)
            for ch in range(tr // chunk):
                rows = slice(ch * chunk, (ch + 1) * chunk)
                vn_b = vn[rows, cols]
                s = jnp.dot(wm, vn_b, preferred_element_type=F32) + bt_ref[:, h:h + 1]
                dsg_b = dsg_v[rows, cols]
                dug_ref[rows, cols] = dsg_b * s
                ds = dsg_b * ug[rows, cols]
                ds_b = ds.astype(BF16)
                dw_ref[h] += jnp.where(tri, lax.dot_general(ds_b, vn_b, nt_dims, preferred_element_type=F32), 0.0)
                dbt_ref[:, h:h + 1] += jnp.sum(ds, axis=1, keepdims=True)
                dvn_ref[rows, cols] = lax.dot_general(wm, ds_b, tn_dims, preferred_element_type=F32)
        dvn = dvn_ref[...]
        dgain_ref[...] += jnp.sum(dvn * n, axis=0, keepdims=True)
        dn = dvn * gain_v
        dvg = r * (dn - n * jnp.mean(dn * n, axis=-1, keepdims=True))
        dv_ref[...] = (dvg * _gelu_grad(v, tv)).astype(BF16)
        du_ref[...] = (dug_ref[...] * _gelu_grad(u, tu)).astype(BF16)

    full_w = pl.BlockSpec((n_heads, chunk, chunk), lambda i: (0, 0, 0))
    full_b = pl.BlockSpec((chunk, n_heads), lambda i: (0, 0))
    du, dv, dw, dbt, dgain = pl.pallas_call(
        body, name=name, grid=(t_dim // tr,),
        in_specs=[pl.BlockSpec((tr, sw), lambda i: (i, ju)), pl.BlockSpec((tr, sw), lambda i: (i, ju + 1)),
                  _row(tr, sw), _vec(sw), full_w, full_b],
        out_specs=[_row(tr, sw), _row(tr, sw), full_w, full_b, _vec(sw)],
        out_shape=[jax.ShapeDtypeStruct((t_dim, sw), BF16), jax.ShapeDtypeStruct((t_dim, sw), BF16),
                   jax.ShapeDtypeStruct((n_heads, chunk, chunk), F32), jax.ShapeDtypeStruct((chunk, n_heads), F32),
                   jax.ShapeDtypeStruct((1, sw), F32)],
        scratch_shapes=[pltpu.VMEM((tr, sw), F32), pltpu.VMEM((tr, sw), F32)],
        compiler_params=_params("arbitrary"),
    )(z, z, dsg, gain.reshape(1, sw), w_s, b_s.T)
    return du, dv, dw, dbt.T, dgain.reshape(sw)


HBM_SPEC = pl.BlockSpec(memory_space=pltpu.HBM)
SEM_SPEC = pl.BlockSpec(memory_space=pltpu.SEMAPHORE)
ANY_SPEC = pl.BlockSpec(memory_space=pl.ANY)
DATAFLOW = pltpu.SideEffectType.DATAFLOW_SIDE_EFFECTING


def _hbm(a):
    return pltpu.with_memory_space_constraint(a, pltpu.HBM)


def _slot(ref, axis, k, n):
    idx = [slice(None)] * len(ref.shape)
    idx[axis] = pl.ds(k * n, n)
    return ref.at[tuple(idx)]


def _chip_of(k, core):
    return (k // 2, k % 2, core)


def _my_chip():
    return 2 * lax.axis_index("x") + lax.axis_index("y")


class Exchange:
    def __init__(self, gather, axes):
        self.gather, self.axes = gather, axes
        self.per = 1 if gather else 2

    def bufs(self, t, refs):
        return refs[t * self.per:(t + 1) * self.per]

    def views(self, t, bufs, src_chip, dst_chip):
        ax = self.axes[t]
        if self.gather:
            slot = _slot(bufs[0], ax, src_chip, bufs[0].shape[ax] // N_CHIPS)
            return slot, slot
        return _slot(bufs[0], ax, dst_chip, bufs[0].shape[ax] // N_CHIPS), bufs[1].at[src_chip]


def exchange_start(bufs, ex, groups, name, deps=()):
    nb, ng = len(bufs), len(groups)

    def body(*refs):
        ins, outs = refs[:nb], refs[nb + len(deps):]
        sems, token = outs[:2 * ng], outs[-1]
        core = lax.axis_index("c")
        me = _my_chip()
        for k in range(N_CHIPS):
            @pl.when(me == k)
            def _(k=k):
                for gi, group in enumerate(groups):
                    for ti, t in enumerate(group):
                        for j in range(N_CHIPS):
                            if j != k:
                                src, dst = ex.views(t, ex.bufs(t, ins), k, j)
                                pltpu.make_async_remote_copy(
                                    src_ref=src, dst_ref=dst, send_sem=sems[2 * gi].at[ti * N_CHIPS + j],
                                    recv_sem=sems[2 * gi + 1].at[ti * N_CHIPS + k], device_id=_chip_of(j, core),
                                    device_id_type=MESH).start()
        token[...] = jnp.zeros_like(token)

    sem_shapes = []
    for group in groups:
        sem_shapes += [pltpu.SemaphoreType.DMA((len(group) * N_CHIPS,))] * 2
    thru = [pltpu.HBM(a.shape, a.dtype) for a in bufs]
    outs = pl.pallas_call(
        body, name=name, out_shape=sem_shapes + thru + [jax.ShapeDtypeStruct((SUBLANE, LANE), F32)],
        in_specs=[HBM_SPEC] * nb + [ANY_SPEC] * len(deps),
        out_specs=[SEM_SPEC] * (2 * ng) + [HBM_SPEC] * nb + [pl.BlockSpec(memory_space=pltpu.VMEM)],
        input_output_aliases={i: 2 * ng + i for i in range(nb)},
        compiler_params=pltpu.CompilerParams(has_side_effects=DATAFLOW),
    )(*[_hbm(a) for a in bufs], *deps)
    sems = [(outs[2 * gi], outs[2 * gi + 1]) for gi in range(ng)]
    return sems, outs[2 * ng:2 * ng + nb], outs[-1]


def exchange_wait(bufs, tensors, ex, sems, after, name):
    nb = len(bufs)
    send_sems, recv_sems = sems

    def body(*refs):
        ins, send_ref, recv_ref = refs[:nb], refs[nb], refs[nb + 1]
        core = lax.axis_index("c")
        me = _my_chip()
        for k in range(N_CHIPS):
            @pl.when(me == k)
            def _(k=k):
                for ti, t in enumerate(tensors):
                    for j in range(N_CHIPS):
                        if j != k:
                            src, _ = ex.views(t, ex.bufs(ti, ins), k, j)
                            _, dst = ex.views(t, ex.bufs(ti, ins), j, k)
                            copy = pltpu.make_async_remote_copy(
                                src_ref=src, dst_ref=dst, send_sem=send_ref.at[ti * N_CHIPS + j],
                                recv_sem=recv_ref.at[ti * N_CHIPS + j],
                                device_id=_chip_of(j, core), device_id_type=MESH)
                            copy.wait_send()
                            copy.wait_recv()

    return pl.pallas_call(
        body, name=name, out_shape=[pltpu.HBM(a.shape, a.dtype) for a in bufs],
        in_specs=[HBM_SPEC] * nb + [SEM_SPEC, SEM_SPEC] + [ANY_SPEC] * len(after),
        out_specs=[HBM_SPEC] * nb, input_output_aliases={i: i for i in range(nb)},
        compiler_params=pltpu.CompilerParams(has_side_effects=DATAFLOW),
    )(*bufs, send_sems, recv_sems, *after)


def swap_with_sibling(arrs, name):
    nt = len(arrs)

    def body(*refs):
        ins, outs = refs[:nt], refs[nt:2 * nt]
        send_sems, recv_sems = refs[2 * nt:]
        sibling = (lax.axis_index("x"), lax.axis_index("y"), 1 - lax.axis_index("c"))
        copies = [pltpu.make_async_remote_copy(src_ref=ins[t], dst_ref=outs[t], send_sem=send_sems.at[t],
                                               recv_sem=recv_sems.at[t], device_id=sibling, device_id_type=MESH)
                  for t in range(nt)]
        for cp in copies:
            cp.start()
        for cp in copies:
            cp.wait()

    return pl.pallas_call(
        body, name=name, out_shape=[jax.ShapeDtypeStruct(a.shape, a.dtype) for a in arrs],
        in_specs=[HBM_SPEC] * nt, out_specs=[HBM_SPEC] * nt,
        scratch_shapes=[pltpu.SemaphoreType.DMA((nt,)), pltpu.SemaphoreType.DMA((nt,))],
        compiler_params=pltpu.CompilerParams(has_side_effects=True),
    )(*arrs)


def sum_pieces(stack, full, axis, me, name):
    _, a_dim, r, c = stack.shape
    tr = _pick(r, ROW_TILES)
    tc = _pick(c, (512, 256, 128))
    nr, nc = r // tr, c // tc

    def body(me_ref, own_ref, s1_ref, s2_ref, s3_ref, o_ref):
        acc = own_ref[...].astype(F32)
        for ref in (s1_ref, s2_ref, s3_ref):
            acc = acc + ref[...].astype(F32)
        o_ref[...] = acc

    def own_map(a, i, j, me_ref):
        return (a, me_ref[0] * nr + i, j) if axis == 1 else (a, i, me_ref[0] * nc + j)

    def from_chip(step):
        return pl.BlockSpec((None, None, tr, tc), lambda a, i, j, me_ref: ((me_ref[0] + step) % N_CHIPS, a, i, j))

    return pl.pallas_call(
        body, name=name, out_shape=jax.ShapeDtypeStruct((a_dim, r, c), F32),
        grid_spec=pltpu.PrefetchScalarGridSpec(
            num_scalar_prefetch=1, grid=(a_dim, nr, nc),
            in_specs=[pl.BlockSpec((None, tr, tc), own_map), from_chip(1), from_chip(2), from_chip(3)],
            out_specs=pl.BlockSpec((None, tr, tc), lambda a, i, j, me_ref: (a, i, j))),
        compiler_params=_params("parallel", "parallel", "parallel"),
    )(me, full, stack, stack, stack)


def all_reduce_small(x, name):
    rows, lanes = x.shape

    def body(x_ref, sum_ref, gath_ref, send_sems, recv_sems, local_sem):
        cx, cy, cc = lax.axis_index("x"), lax.axis_index("y"), lax.axis_index("c")
        me, sibling = (cx, cy, cc), (cx, cy, 1 - cc)
        chips = [(1 - cx, cy), (cx, 1 - cy), (1 - cx, 1 - cy)]

        def block(px, py, pc):
            return gath_ref.at[pl.ds(pl.multiple_of((4 * px + 2 * py + pc) * rows, SUBLANE), rows), :]

        def copy(k, blk, to, src=None):
            return pltpu.make_async_remote_copy(
                src_ref=block(*blk) if src is None else src, dst_ref=block(*blk),
                send_sem=send_sems.at[k], recv_sem=recv_sems.at[k], device_id=to, device_id_type=MESH)

        mine = pltpu.make_async_copy(x_ref, block(*me), local_sem)
        mine.start()
        first = [copy(0, me, sibling, src=x_ref)]
        first += [copy(1 + j, me, (*chip, cc), src=x_ref) for j, chip in enumerate(chips)]
        for cp in first:
            cp.start()
        passed = [copy(4 + j, (*chip, cc), sibling) for j, chip in enumerate(chips)]
        for j, chip in enumerate(chips):
            copy(1 + j, (*chip, cc), me).wait_recv()
            passed[j].start()
        copy(0, sibling, me).wait_recv()
        for j, chip in enumerate(chips):
            copy(4 + j, (*chip, 1 - cc), me).wait_recv()
        for cp in first + passed:
            cp.wait_send()
        mine.wait()
        acc = gath_ref[pl.ds(0, rows), :]
        for k in range(1, N_DEV):
            acc = acc + gath_ref[pl.ds(k * rows, rows), :]
        sum_ref[...] = acc

    return pl.pallas_call(
        body, name=name, out_shape=jax.ShapeDtypeStruct((rows, lanes), F32),
        in_specs=[pl.BlockSpec(memory_space=pltpu.VMEM)], out_specs=pl.BlockSpec(memory_space=pltpu.VMEM),
        scratch_shapes=[pltpu.VMEM((N_DEV * rows, lanes), F32), pltpu.SemaphoreType.DMA((7,)),
                        pltpu.SemaphoreType.DMA((7,)), pltpu.SemaphoreType.DMA],
        compiler_params=pltpu.CompilerParams(has_side_effects=True, vmem_limit_bytes=VMEM_LIMIT),
    )(x)


def _adamw(w, g, m, v):
    m = ADAM_B1 * m + (1.0 - ADAM_B1) * g
    v = ADAM_B2 * v + (1.0 - ADAM_B2) * (g * g)
    m_hat = m / (1.0 - ADAM_B1 ** ADAM_STEP)
    v_hat = v / (1.0 - ADAM_B2 ** ADAM_STEP)
    delta = -ADAM_LR * (m_hat / (jnp.sqrt(v_hat) + ADAM_EPS) + ADAM_WD * w)
    return delta, m, v


def adam_big(p_own, p_sib, w, m, v, layer, stacks, name):
    r, c = p_own.shape
    tr = _pick(r, ROW_TILES)
    tc = _pick(c, (512, 256, 128))

    def body(p_ref, q_ref, w_ref, m_ref, v_ref, *rest):
        g_out, d_out, m_out, v_out = rest[4:]
        g = p_ref[...] + q_ref[...]
        delta, m_new, v_new = _adamw(w_ref[...], g, m_ref[...], v_ref[...])
        g_out[...] = g
        d_out[...] = delta
        m_out[...] = m_new
        v_out[...] = v_new

    flat = pl.BlockSpec((tr, tc), lambda i, j: (i, j))
    layered = pl.BlockSpec((None, tr, tc), lambda i, j: (layer, i, j))
    anyspace = pl.BlockSpec(memory_space=pl.ANY)
    out = jax.ShapeDtypeStruct(w.shape, F32)
    return pl.pallas_call(
        body, name=name, grid=(r // tr, c // tc),
        in_specs=[flat, flat, layered, layered, layered] + [anyspace] * 4,
        out_specs=[layered] * 4, out_shape=[out] * 4, input_output_aliases={5: 0, 6: 1, 7: 2, 8: 3},
        compiler_params=_params("parallel", "parallel"),
    )(p_own, p_sib, w, m, v, *stacks)


def adam_small(g, w, m, v, name):
    rows, lanes = g.shape
    tr = _pick(rows, (512,) + ROW_TILES)

    def body(g_ref, w_ref, m_ref, v_ref, d_out, m_out, v_out):
        delta, m_new, v_new = _adamw(w_ref[...], g_ref[...], m_ref[...], v_ref[...])
        d_out[...] = delta
        m_out[...] = m_new
        v_out[...] = v_new

    out = jax.ShapeDtypeStruct((rows, lanes), F32)
    return pl.pallas_call(
        body, name=name, grid=(rows // tr,), in_specs=[_row(tr, lanes)] * 4, out_specs=[_row(tr, lanes)] * 3,
        out_shape=[out] * 3, compiler_params=_params("parallel"),
    )(g, w, m, v)


def ffn_fwd(x, g_pre, w_up, w_down, g_post, tag, deps=()):
    h = rms_fwd(x, g_pre, name=f"{tag}_rms_fwd", deps=deps)
    z = matmul(h, w_up, out_dtype=BF16, name=f"{tag}_up")
    a = swiglu_fwd(z, name=f"{tag}_swiglu_fwd")
    f = matmul(a, w_down, out_dtype=F32, name=f"{tag}_down")
    y = res_rms_fwd(x, f, g_post, MACARON_WEIGHT, name=f"{tag}_res_fwd")
    return y, (x, h, z, a, f)


def ffn_bwd(dy, saved, g_pre, w_up, w_down, g_post, tag, deps=()):
    x, h, z, a, f = saved
    df, dg_post = rms_bwd(f, g_post, dy, MACARON_WEIGHT, None, BF16, name=f"{tag}_res_bwd", deps=deps)
    da = matmul(df, w_down, tb=True, out_dtype=BF16, name=f"{tag}_down_dx")
    dw_down = matmul(a, df, ta=True, out_dtype=BF16, name=f"{tag}_down_dw")
    dz = swiglu_bwd(z, da, name=f"{tag}_swiglu_bwd")
    dh = matmul(dz, w_up, tb=True, out_dtype=F32, name=f"{tag}_up_dx")
    dw_up = matmul(h, dz, ta=True, out_dtype=BF16, name=f"{tag}_up_dw")
    dx, dg_pre = rms_bwd(x, g_pre, dh, 1.0, dy, F32, name=f"{tag}_rms_bwd")
    return dx, dw_up, dw_down, dg_pre, dg_post


def mixer_fwd(x, wt, sm, dims):
    pw, sw, d = dims
    h = rms_fwd(x, sm['g_mix_pre'], name="mix_rms_fwd")
    z = matmul(h, wt['w_in'], out_dtype=F32, name="mix_in")
    dd, e, yp = pool_fwd(z, wt['pool_group_w'], sm['pool_scale'], pw, name="mix_pool_fwd")
    ya = matmul(yp, wt['w_pool_out'], out_dtype=BF16, name="mix_pool_out")
    sg = sgu_fwd(z, sm['sgu_v_gain'], sm['sgu_w_s'], sm['sgu_b_s'], pw, sw, name="mix_sgu_fwd")
    yb = matmul(sg, wt['w_sgu_out'], out_dtype=BF16, name="mix_sgu_out")
    m = gate_fwd(z, ya, yb, pw + 2 * sw, pw + 2 * sw + d, name="mix_gate_fwd")
    o = matmul(m, wt['w_out'], out_dtype=F32, name="mix_out")
    y = res_rms_fwd(x, o, sm['g_mix_post'], 1.0, name="mix_res_fwd")
    return y, (x, h, z, dd, e, yp, sg, ya, yb, m, o)


def mixer_bwd(dy, saved, wt, sm, dims, deps=()):
    pw, sw, d = dims
    x, h, z, dd, e, yp, sg, ya, yb, m, o = saved
    grads = {}
    do, grads['g_mix_post'] = rms_bwd(o, sm['g_mix_post'], dy, 1.0, None, BF16, name="mix_res_bwd", deps=deps)
    dm = matmul(do, wt['w_out'], tb=True, out_dtype=BF16, name="mix_out_dx")
    grads['w_out'] = matmul(m, do, ta=True, out_dtype=BF16, name="mix_out_dw")
    dya, dyb, dga, dgb = gate_bwd(z, ya, yb, dm, pw + 2 * sw, pw + 2 * sw + d, name="mix_gate_bwd")
    dyp = matmul(dya, wt['w_pool_out'], tb=True, out_dtype=BF16, name="mix_pool_out_dx")
    grads['w_pool_out'] = matmul(yp, dya, ta=True, out_dtype=BF16, name="mix_pool_out_dw")
    dp, dwg, dscale = pool_bwd(dyp, e, dd, wt['pool_group_w'], sm['pool_scale'], name="mix_pool_bwd")
    grads['pool_group_w'] = dwg.astype(BF16)
    grads['pool_scale'] = dscale.reshape(pw)
    dsg = matmul(dyb, wt['w_sgu_out'], tb=True, out_dtype=BF16, name="mix_sgu_out_dx")
    grads['w_sgu_out'] = matmul(sg, dyb, ta=True, out_dtype=BF16, name="mix_sgu_out_dw")
    du, dv, grads['sgu_w_s'], grads['sgu_b_s'], grads['sgu_v_gain'] = sgu_bwd(
        z, dsg, sm['sgu_v_gain'], sm['sgu_w_s'], sm['sgu_b_s'], pw, sw, name="mix_sgu_bwd")
    dz = jnp.concatenate([dp, du, dv, dga, dgb], axis=1)
    dh = matmul(dz, wt['w_in'], tb=True, out_dtype=F32, name="mix_in_dx")
    grads['w_in'] = matmul(h, dz, ta=True, out_dtype=BF16, name="mix_in_dw")
    dx, grads['g_mix_pre'] = rms_bwd(x, sm['g_mix_pre'], dh, 1.0, dy, F32, name="mix_rms_bwd")
    return dx, grads


def _as_rows(a):
    return a.reshape(a.shape[0], -1, a.shape[-1])


def kernel(x, g_ffn1_pre, w_ffn1_up, w_ffn1_down, g_ffn1_post, g_mix_pre, w_in, pool_group_w, pool_scale, w_pool_out, sgu_v_gain, sgu_w_s, sgu_b_s, w_sgu_out, w_out, g_mix_post, g_ffn2_pre, w_ffn2_up, w_ffn2_down, g_ffn2_post, loss_target, m_g_ffn1_pre, m_w_ffn1_up, m_w_ffn1_down, m_g_ffn1_post, m_g_mix_pre, m_w_in, m_pool_group_w, m_pool_scale, m_w_pool_out, m_sgu_v_gain, m_sgu_w_s, m_sgu_b_s, m_w_sgu_out, m_w_out, m_g_mix_post, m_g_ffn2_pre, m_w_ffn2_up, m_w_ffn2_down, m_g_ffn2_post, v_g_ffn1_pre, v_w_ffn1_up, v_w_ffn1_down, v_g_ffn1_post, v_g_mix_pre, v_w_in, v_pool_group_w, v_pool_scale, v_w_pool_out, v_sgu_v_gain, v_sgu_w_s, v_sgu_b_s, v_w_sgu_out, v_w_out, v_g_mix_post, v_g_ffn2_pre, v_w_ffn2_up, v_w_ffn2_down, v_g_ffn2_post):
    w = dict(zip(WEIGHTS, (g_ffn1_pre, w_ffn1_up, w_ffn1_down, g_ffn1_post, g_mix_pre, w_in, pool_group_w, pool_scale, w_pool_out, sgu_v_gain, sgu_w_s, sgu_b_s, w_sgu_out, w_out, g_mix_post, g_ffn2_pre, w_ffn2_up, w_ffn2_down, g_ffn2_post)))
    mom = dict(zip(WEIGHTS, (m_g_ffn1_pre, m_w_ffn1_up, m_w_ffn1_down, m_g_ffn1_post, m_g_mix_pre, m_w_in, m_pool_group_w, m_pool_scale, m_w_pool_out, m_sgu_v_gain, m_sgu_w_s, m_sgu_b_s, m_w_sgu_out, m_w_out, m_g_mix_post, m_g_ffn2_pre, m_w_ffn2_up, m_w_ffn2_down, m_g_ffn2_post)))
    var = dict(zip(WEIGHTS, (v_g_ffn1_pre, v_w_ffn1_up, v_w_ffn1_down, v_g_ffn1_post, v_g_mix_pre, v_w_in, v_pool_group_w, v_pool_scale, v_w_pool_out, v_sgu_v_gain, v_sgu_w_s, v_sgu_b_s, v_w_sgu_out, v_w_out, v_g_mix_post, v_g_ffn2_pre, v_w_ffn2_up, v_w_ffn2_down, v_g_ffn2_post)))
    depth = g_ffn1_pre.shape[0]
    d = x.shape[-1]
    pw = pool_scale.shape[-1]
    sw = sgu_v_gain.shape[-1]
    dims = (pw, sw, d)
    axes = [BIG_AXIS[n] for n in BIG]

    me = 2 * lax.axis_index("x") + lax.axis_index("y")
    gather = Exchange(True, axes)
    group_ids = [[BIG.index(n) for n in group] for group in BLOCK_WEIGHTS]

    def start_gather(l, deps=()):
        lands = []
        for n, ax in zip(BIG, axes):
            shard = w[n][l].astype(BF16)
            shape, at = list(shard.shape), [0] * shard.ndim
            shape[ax] *= N_CHIPS
            at[ax] = me * shard.shape[ax]
            lands.append(lax.dynamic_update_slice(lax.empty(tuple(shape), BF16), shard, at))
        return exchange_start(lands, gather, group_ids, name=f"gather_start_l{l}", deps=deps)

    def finish_gather(started, l, gi, after):
        sems, lands, _ = started
        ids = group_ids[gi]
        got = exchange_wait([lands[t] for t in ids], ids, gather, sems[gi], (after,), name=f"gather_wait_l{l}_b{gi}")
        return dict(zip(BLOCK_WEIGHTS[gi], got))

    small = [{n: w[n][l] for n in SMALL} for l in range(depth)]

    act = x[0]
    saved, full = [], []
    started = start_gather(0)
    for l in range(depth):
        ahead = start_gather(l + 1, (started[2],)) if l + 1 < depth else None
        deps = (ahead[2],) if ahead is not None else ()
        sm, wt = small[l], {}
        wt.update(finish_gather(started, l, 0, act))
        act, s1 = ffn_fwd(act, sm['g_ffn1_pre'], wt['w_ffn1_up'], wt['w_ffn1_down'], sm['g_ffn1_post'], "ffn1", deps)
        wt.update(finish_gather(started, l, 1, act))
        act, s2 = mixer_fwd(act, wt, sm, dims)
        wt.update(finish_gather(started, l, 2, act))
        act, s3 = ffn_fwd(act, sm['g_ffn2_pre'], wt['w_ffn2_up'], wt['w_ffn2_down'], sm['g_ffn2_post'], "ffn2")
        saved.append((s1, s2, s3))
        full.append(wt)
        started = ahead
    dact, sq_sum = loss_grad(act, loss_target[0], name="loss_grad")
    loss = lax.psum(0.5 * sq_sum / d, ("x", "y", "c"))

    rows3 = {n: (_as_rows(w[n]), _as_rows(mom[n]), _as_rows(var[n])) for n in BIG}
    stacks = {n: tuple(lax.empty(rows3[n][0].shape, F32) for _ in range(4)) for n in BIG}
    small_grads = [{} for _ in range(depth)]

    me1 = me.reshape(1).astype(jnp.int32)

    def start_scatter(names, grads, l, block):
        ex = Exchange(False, [BIG_AXIS[n] for n in names])
        bufs = []
        for n in names:
            piece = list(grads[n].shape)
            piece[BIG_AXIS[n]] //= N_CHIPS
            bufs += [grads[n], lax.empty((N_CHIPS, *piece), BF16)]
        return (names, ex, l, block) + exchange_start(bufs, ex, [list(range(len(names)))],
                                                      name=f"scatter_start_l{l}_b{block}")

    def finish_scatter(pending, after):
        names, ex, l, block, sems, bufs, _ = pending
        got = exchange_wait(bufs, list(range(len(names))), ex, sems[0], after, name=f"scatter_wait_l{l}_b{block}")
        plane = []
        for ti, n in enumerate(names):
            full_g, stack = got[2 * ti], got[2 * ti + 1]
            if full_g.ndim == 2:
                full_g, stack = full_g[None], stack[:, None]
            p = sum_pieces(stack, full_g, BIG_AXIS[n] + 3 - got[2 * ti].ndim, me1, name=f"sum_{n}")
            plane.append(p.reshape(-1, p.shape[-1]))
        other = swap_with_sibling(plane, name=f"swap_plane_sums_{len(names)}")
        for n, p_own, p_sib in zip(names, plane, other):
            stacks[n] = tuple(adam_big(p_own, p_sib, *rows3[n], l, stacks[n], name=f"adam_{n}"))
        return stacks[names[-1]][0]

    pending, done = None, ()
    for l in reversed(range(depth)):
        wt, sm = full[l], small[l]
        s1, s2, s3 = saved[l]
        for block in (2, 1, 0):
            deps = (pending[-1],) if pending is not None else ()
            g = {}
            if block == 2:
                dnew, g['w_ffn2_up'], g['w_ffn2_down'], g['g_ffn2_pre'], g['g_ffn2_post'] = ffn_bwd(
                    dact, s3, sm['g_ffn2_pre'], wt['w_ffn2_up'], wt['w_ffn2_down'], sm['g_ffn2_post'], "ffn2", deps)
            elif block == 1:
                dnew, g = mixer_bwd(dact, s2, wt, sm, dims, deps)
            else:
                dnew, g['w_ffn1_up'], g['w_ffn1_down'], g['g_ffn1_pre'], g['g_ffn1_post'] = ffn_bwd(
                    dact, s1, sm['g_ffn1_pre'], wt['w_ffn1_up'], wt['w_ffn1_down'], sm['g_ffn1_post'], "ffn1", deps)
            small_grads[l].update({n: g[n] for n in g if n in SMALL})
            if pending is not None:
                done = (finish_scatter(pending, (dnew,) + done),)
            pending = start_scatter(BLOCK_WEIGHTS[block], g, l, block)
            dact = dnew
    finish_scatter(pending, (dact,) + done)

    def flat(tree):
        v = jnp.concatenate([tree[n].reshape(-1).astype(F32) for n in SMALL])
        pad = (-v.shape[0]) % (SUBLANE * LANE)
        return jnp.pad(v, (0, pad)).reshape(-1, LANE)

    g_small = all_reduce_small(flat({n: jnp.stack([small_grads[l][n] for l in range(depth)]) for n in SMALL}),
                               name="all_reduce_small")
    d_small, m_small, v_small = adam_small(g_small, flat(w), flat(mom), flat(var), name="adam_small")

    def unflat(block):
        v, out, at = block.reshape(-1), {}, 0
        for n in SMALL:
            out[n] = v[at:at + w[n].size].reshape(w[n].shape)
            at += w[n].size
        return out

    result = [{}, {}, {}, {}]
    for tree, block in zip(result, (g_small, d_small, m_small, v_small)):
        tree.update(unflat(block))
    for n in BIG:
        for tree, stack in zip(result, stacks[n]):
            tree[n] = stack.reshape(w[n].shape)
    return (loss, dact.reshape(x.shape), *[tree[n] for tree in result for n in WEIGHTS])
```

```python
import math

import jax
import jax.numpy as jnp
from jax import lax
from jax.experimental import pallas as pl
from jax.experimental.pallas import tpu as pltpu

F32 = jnp.float32
BF16 = jnp.bfloat16
MESH = pl.DeviceIdType.MESH

EPS = 1e-6
MACARON_WEIGHT = 0.5
POOL_WINDOWS = (2, 4, 8, 16)
POOL_HALO = 16
ADAM_LR = 0.001
ADAM_B1 = 0.9
ADAM_B2 = 0.999
ADAM_EPS = 1e-08
ADAM_WD = 0.01
ADAM_STEP = 10
GELU_K = math.sqrt(2.0 / math.pi)
GELU_C = 0.044715

N_CHIPS = 4
N_DEV = 8
V7X_VMEM_BYTES = 64 * 1024 * 1024
VMEM_LIMIT = (V7X_VMEM_BYTES * 3) // 4
LANE = 128
SUBLANE = 8

WEIGHTS = ['g_ffn1_pre', 'w_ffn1_up', 'w_ffn1_down', 'g_ffn1_post', 'g_mix_pre', 'w_in', 'pool_group_w',
           'pool_scale', 'w_pool_out', 'sgu_v_gain', 'sgu_w_s', 'sgu_b_s', 'w_sgu_out', 'w_out', 'g_mix_post',
           'g_ffn2_pre', 'w_ffn2_up', 'w_ffn2_down', 'g_ffn2_post']
BIG_AXIS = {'w_ffn1_up': 1, 'w_ffn1_down': 0, 'w_in': 1, 'pool_group_w': 1, 'w_pool_out': 1, 'w_sgu_out': 1,
            'w_out': 0, 'w_ffn2_up': 1, 'w_ffn2_down': 0}
BIG = list(BIG_AXIS)
BLOCK_WEIGHTS = [['w_ffn1_up', 'w_ffn1_down'], ['w_in', 'pool_group_w', 'w_pool_out', 'w_sgu_out', 'w_out'],
                 ['w_ffn2_up', 'w_ffn2_down']]
SMALL = [n for n in WEIGHTS if n not in BIG_AXIS]


def _pick(dim, cands):
    for c in cands:
        if dim % c == 0:
            return c
    return dim


STREAM_COL_TILES = (1024, 1408, 896, 512, 256, 128)
STREAM_ROW_TILES = (512, 352, 256, 128, 64, 32, 16, 8)
STREAM_BLOCK_ELEMS = 384 * 1024


def _stream_tiles(r, c):
    tc = _pick(c, STREAM_COL_TILES)
    for tr in STREAM_ROW_TILES:
        if r % tr == 0 and tr * tc <= STREAM_BLOCK_ELEMS:
            return tr, tc
    return r, tc


def _params(*sem):
    return pltpu.CompilerParams(dimension_semantics=sem if sem else None, vmem_limit_bytes=VMEM_LIMIT)


def _sigmoid(x):
    return 1.0 / (1.0 + jnp.exp(-x))


def _gelu(x):
    t = jnp.tanh(GELU_K * (x + GELU_C * (x * x * x)))
    return x * (0.5 * (1.0 + t)), t


def _gelu_grad(x, t):
    return 0.5 * (1.0 + t) + (0.5 * x) * (1.0 - t * t) * (GELU_K * (1.0 + (3.0 * GELU_C) * (x * x)))


MATMUL_MN_TILES = (1024, 1408, 512, 256, 128)
MATMUL_K_TILES = (2048, 2816, 1024, 512, 256, 128)


def matmul(a, b, *, ta=False, tb=False, out_dtype=BF16, name):
    m_dim, k_dim = (a.shape[1], a.shape[0]) if ta else a.shape
    n_dim = b.shape[0] if tb else b.shape[1]
    tm = _pick(m_dim, MATMUL_MN_TILES)
    tn = _pick(n_dim, MATMUL_MN_TILES)
    tk = _pick(k_dim, MATMUL_K_TILES)
    nk = k_dim // tk
    dims = (((0 if ta else 1,), (1 if tb else 0,)), ((), ()))

    def body(a_ref, b_ref, o_ref, acc_ref):
        k = pl.program_id(2)

        @pl.when(k == 0)
        def _():
            acc_ref[...] = jnp.zeros_like(acc_ref)

        acc_ref[...] += lax.dot_general(a_ref[...], b_ref[...], dims, preferred_element_type=F32)

        @pl.when(k == nk - 1)
        def _():
            o_ref[...] = acc_ref[...].astype(o_ref.dtype)

    a_spec = pl.BlockSpec((tk, tm), lambda i, j, k: (k, i)) if ta else pl.BlockSpec((tm, tk), lambda i, j, k: (i, k))
    b_spec = pl.BlockSpec((tn, tk), lambda i, j, k: (j, k)) if tb else pl.BlockSpec((tk, tn), lambda i, j, k: (k, j))
    return pl.pallas_call(
        body, name=name, grid=(m_dim // tm, n_dim // tn, nk),
        in_specs=[a_spec, b_spec], out_specs=pl.BlockSpec((tm, tn), lambda i, j, k: (i, j)),
        out_shape=jax.ShapeDtypeStruct((m_dim, n_dim), out_dtype),
        scratch_shapes=[pltpu.VMEM((tm, tn), F32)],
        compiler_params=_params("parallel", "parallel", "arbitrary"),
    )(a, b)


def matmul_swiglu(h, w_up, name):
    m_dim, k_dim = h.shape
    f = w_up.shape[1] // 2
    tm = _pick(m_dim, MATMUL_MN_TILES)
    tn = _pick(f, (512, 256, 128))
    tk = _pick(k_dim, MATMUL_K_TILES)
    nk, nf = k_dim // tk, f // tn

    def body(h_ref, wg_ref, wu_ref, g_ref, u_ref, a_ref, accg_ref, accu_ref):
        k = pl.program_id(2)

        @pl.when(k == 0)
        def _():
            accg_ref[...] = jnp.zeros_like(accg_ref)
            accu_ref[...] = jnp.zeros_like(accu_ref)

        hv = h_ref[...]
        accg_ref[...] += jnp.dot(hv, wg_ref[...], preferred_element_type=F32)
        accu_ref[...] += jnp.dot(hv, wu_ref[...], preferred_element_type=F32)

        @pl.when(k == nk - 1)
        def _():
            g, u = accg_ref[...], accu_ref[...]
            g_ref[...] = g.astype(BF16)
            u_ref[...] = u.astype(BF16)
            a_ref[...] = (g * _sigmoid(g) * u).astype(BF16)

    out = jax.ShapeDtypeStruct((m_dim, f), BF16)
    blk = pl.BlockSpec((tm, tn), lambda i, j, k: (i, j))
    return pl.pallas_call(
        body, name=name, grid=(m_dim // tm, nf, nk),
        in_specs=[pl.BlockSpec((tm, tk), lambda i, j, k: (i, k)), pl.BlockSpec((tk, tn), lambda i, j, k: (k, j)),
                  pl.BlockSpec((tk, tn), lambda i, j, k: (k, j + nf))],
        out_specs=[blk, blk, blk], out_shape=[out, out, out],
        scratch_shapes=[pltpu.VMEM((tm, tn), F32), pltpu.VMEM((tm, tn), F32)],
        compiler_params=_params("parallel", "parallel", "arbitrary"),
    )(h, w_up, w_up)


ROW_TILES = (256, 128, 64, 32, 16, 8)


def _row(tr, width):
    return pl.BlockSpec((tr, width), lambda i: (i, 0))


def _vec(width):
    return pl.BlockSpec((1, width), lambda i: (0, 0))


def rms_fwd(x, g, name, deps=()):
    t_dim, d = x.shape
    tr = _pick(t_dim, ROW_TILES)

    def body(x_ref, g_ref, *rest):
        o_ref = rest[-1]
        xv = x_ref[...]
        r = lax.rsqrt(jnp.mean(xv * xv, axis=-1, keepdims=True) + EPS)
        o_ref[...] = ((xv * r) * g_ref[...]).astype(o_ref.dtype)

    return pl.pallas_call(
        body, name=name, grid=(t_dim // tr,), in_specs=[_row(tr, d), _vec(d)] + [ANY_SPEC] * len(deps),
        out_specs=_row(tr, d), out_shape=jax.ShapeDtypeStruct((t_dim, d), BF16), compiler_params=_params("parallel"),
    )(x, g.reshape(1, d), *deps)


def res_rms_fwd(x, f, g, weight, name):
    t_dim, d = x.shape
    tr = _pick(t_dim, ROW_TILES)

    def body(x_ref, f_ref, g_ref, o_ref):
        fv = f_ref[...]
        r = lax.rsqrt(jnp.mean(fv * fv, axis=-1, keepdims=True) + EPS)
        o_ref[...] = x_ref[...] + weight * ((fv * r) * g_ref[...])

    return pl.pallas_call(
        body, name=name, grid=(t_dim // tr,), in_specs=[_row(tr, d), _row(tr, d), _vec(d)], out_specs=_row(tr, d),
        out_shape=jax.ShapeDtypeStruct((t_dim, d), F32), compiler_params=_params("parallel"),
    )(x, f, g.reshape(1, d))


def rms_bwd(f, g, dy, weight, resid, out_dtype, name, deps=()):
    t_dim, d = f.shape
    tr = _pick(t_dim, ROW_TILES)
    has_resid = resid is not None

    def body(*refs):
        o_ref, dg_ref = refs[-2:]
        if has_resid:
            f_ref, g_ref, dy_ref, res_ref = refs[:4]
        else:
            f_ref, g_ref, dy_ref = refs[:3]

        @pl.when(pl.program_id(0) == 0)
        def _():
            dg_ref[...] = jnp.zeros_like(dg_ref)

        fv = f_ref[...]
        r = lax.rsqrt(jnp.mean(fv * fv, axis=-1, keepdims=True) + EPS)
        n = fv * r
        dyw = dy_ref[...] * weight
        dn = dyw * g_ref[...]
        df = r * (dn - n * jnp.mean(dn * n, axis=-1, keepdims=True))
        if has_resid:
            df = df + res_ref[...]
        o_ref[...] = df.astype(o_ref.dtype)
        dg_ref[...] += jnp.sum(dyw * n, axis=0, keepdims=True)

    ins = [f, g.reshape(1, d), dy] + ([resid] if has_resid else []) + list(deps)
    in_specs = [_row(tr, d), _vec(d), _row(tr, d)] + ([_row(tr, d)] if has_resid else []) + [ANY_SPEC] * len(deps)
    out, dg = pl.pallas_call(
        body, name=name, grid=(t_dim // tr,), in_specs=in_specs, out_specs=[_row(tr, d), _vec(d)],
        out_shape=[jax.ShapeDtypeStruct((t_dim, d), out_dtype), jax.ShapeDtypeStruct((1, d), F32)],
        compiler_params=_params("arbitrary"),
    )(*ins)
    return out, dg.reshape(d)


def loss_grad(y, target, name):
    t_dim, d = y.shape
    tr = _pick(t_dim, ROW_TILES)
    inv_d = 1.0 / d

    def body(y_ref, t_ref, dy_ref, s_ref):
        @pl.when(pl.program_id(0) == 0)
        def _():
            s_ref[...] = jnp.zeros_like(s_ref)

        e = y_ref[...] - t_ref[...]
        dy_ref[...] = e * inv_d
        s_ref[...] += jnp.sum(e * e)

    dy, s = pl.pallas_call(
        body, name=name, grid=(t_dim // tr,), in_specs=[_row(tr, d), _row(tr, d)],
        out_specs=[_row(tr, d), pl.BlockSpec((SUBLANE, LANE), lambda i: (0, 0))],
        out_shape=[jax.ShapeDtypeStruct((t_dim, d), F32), jax.ShapeDtypeStruct((SUBLANE, LANE), F32)],
        compiler_params=_params("arbitrary"),
    )(y, target)
    return dy, s[0, 0]


def swiglu_bwd(gate, up, da, name):
    t_dim, f = gate.shape
    tr = _pick(t_dim, ROW_TILES)
    tc = _pick(f, (512, 256, 128))

    def body(g_ref, u_ref, da_ref, o_ref):
        for c in range(f // tc):
            lo = c * tc
            g = g_ref[:, lo:lo + tc].astype(F32)
            u = u_ref[:, lo:lo + tc].astype(F32)
            da = da_ref[:, lo:lo + tc].astype(F32)
            s = _sigmoid(g)
            o_ref[:, lo:lo + tc] = (da * u * (s * (1.0 + g * (1.0 - s)))).astype(o_ref.dtype)
            o_ref[:, f + lo:f + lo + tc] = (da * (g * s)).astype(o_ref.dtype)

    return pl.pallas_call(
        body, name=name, grid=(t_dim // tr,), in_specs=[_row(tr, f)] * 3, out_specs=_row(tr, 2 * f),
        out_shape=jax.ShapeDtypeStruct((t_dim, 2 * f), BF16), compiler_params=_params("parallel"),
    )(gate, up, da)


def gate_fwd(z, ya, yb, off_a, off_b, name):
    t_dim, d = ya.shape
    tr = _pick(t_dim, (512,) + ROW_TILES)
    tc = math.gcd(math.gcd(off_a, off_b), _pick(d, (512, 256, 128)))
    ja, jb = off_a // tc, off_b // tc

    def body(ga_ref, gb_ref, ya_ref, yb_ref, m_ref):
        m = _sigmoid(ga_ref[...]) * ya_ref[...].astype(F32) + _sigmoid(gb_ref[...]) * yb_ref[...].astype(F32)
        m_ref[...] = m.astype(m_ref.dtype)

    blk = pl.BlockSpec((tr, tc), lambda i, j: (i, j))
    return pl.pallas_call(
        body, name=name, grid=(t_dim // tr, d // tc),
        in_specs=[pl.BlockSpec((tr, tc), lambda i, j: (i, j + ja)), pl.BlockSpec((tr, tc), lambda i, j: (i, j + jb)),
                  blk, blk],
        out_specs=blk, out_shape=jax.ShapeDtypeStruct((t_dim, d), BF16),
        compiler_params=_params("parallel", "parallel"),
    )(z, z, ya, yb)


def gate_bwd(z, ya, yb, dm, off_a, off_b, name):
    t_dim, d = ya.shape
    tr = _pick(t_dim, (512,) + ROW_TILES)
    tc = math.gcd(math.gcd(off_a, off_b), _pick(d, (512, 256, 128)))
    ja, jb = off_a // tc, off_b // tc

    def body(ga_ref, gb_ref, ya_ref, yb_ref, dm_ref, dya_ref, dyb_ref, dga_ref, dgb_ref):
        dm = dm_ref[...].astype(F32)
        sa = _sigmoid(ga_ref[...])
        sb = _sigmoid(gb_ref[...])
        dya_ref[...] = (dm * sa).astype(BF16)
        dyb_ref[...] = (dm * sb).astype(BF16)
        dga_ref[...] = (dm * ya_ref[...].astype(F32) * (sa * (1.0 - sa))).astype(BF16)
        dgb_ref[...] = (dm * yb_ref[...].astype(F32) * (sb * (1.0 - sb))).astype(BF16)

    blk = pl.BlockSpec((tr, tc), lambda i, j: (i, j))
    out = jax.ShapeDtypeStruct((t_dim, d), BF16)
    return pl.pallas_call(
        body, name=name, grid=(t_dim // tr, d // tc),
        in_specs=[pl.BlockSpec((tr, tc), lambda i, j: (i, j + ja)), pl.BlockSpec((tr, tc), lambda i, j: (i, j + jb)),
                  blk, blk, blk],
        out_specs=[blk] * 4, out_shape=[out] * 4, compiler_params=_params("parallel", "parallel"),
    )(z, z, ya, yb, dm)


def _window_sums(e, n_rows, forward):
    def shifted(v, k):
        return pltpu.roll(v, (n_rows - k) if forward else k, 0)

    s2 = e + shifted(e, 1)
    s4 = s2 + shifted(s2, 2)
    s8 = s4 + shifted(s4, 4)
    s16 = s8 + shifted(s8, 8)
    return (s2, s4, s8, s16)


def _pool_rows(t_dim):
    return _pick(t_dim, (256, 128, 64, 32, 16))


def pool_fwd(z, w_group, scale, pw, name):
    t_dim = z.shape[0]
    n_groups, c, _ = w_group.shape
    tr = _pool_rows(t_dim)
    per = tr // POOL_HALO

    def body(cur_ref, prev_ref, w_ref, scale_ref, d_ref, e_ref, yp_ref):
        i = pl.program_id(0)
        cur = cur_ref[...]
        prev = jnp.where(i > 0, prev_ref[...], 0.0)
        ext = jnp.concatenate([prev, cur], axis=0)
        sums = _window_sums(ext, tr + POOL_HALO, forward=False)
        pos = (i * tr + 1 + lax.broadcasted_iota(jnp.int32, (tr, 1), 0)).astype(F32)
        for g, w in enumerate(POOL_WINDOWS):
            cols = slice(g * c, (g + 1) * c)
            cnt = jnp.minimum(pos, float(w))
            d = (sums[g][POOL_HALO:, cols] / cnt - cur[:, cols]).astype(BF16)
            e = jnp.dot(d, w_ref[g], preferred_element_type=F32)
            d_ref[:, cols] = d
            e_ref[:, cols] = e.astype(BF16)
            yp_ref[:, cols] = (e * scale_ref[:, cols]).astype(BF16)

    out = jax.ShapeDtypeStruct((t_dim, pw), BF16)
    return pl.pallas_call(
        body, name=name, grid=(t_dim // tr,),
        in_specs=[_row(tr, pw), pl.BlockSpec((POOL_HALO, pw), lambda i: (jnp.maximum(i * per - 1, 0), 0)),
                  pl.BlockSpec((n_groups, c, c), lambda i: (0, 0, 0)), _vec(pw)],
        out_specs=[_row(tr, pw)] * 3, out_shape=[out] * 3, compiler_params=_params("parallel"),
    )(z, z, w_group, scale.reshape(1, pw))


def pool_bwd(dyp, e, d, w_group, scale, name):
    t_dim, pw = dyp.shape
    n_groups, c, _ = w_group.shape
    tr = _pool_rows(t_dim)
    per = tr // POOL_HALO
    n_tiles = t_dim // tr
    last_halo = t_dim // POOL_HALO - 1
    nt_dims = (((1,), (1,)), ((), ()))
    tn_dims = (((0,), (0,)), ((), ()))

    def body(dyp_ref, nxt_ref, e_ref, d_ref, w_ref, scale_ref, dp_ref, dw_ref, dscale_ref):
        i = pl.program_id(0)

        @pl.when(i == 0)
        def _():
            dw_ref[...] = jnp.zeros_like(dw_ref)
            dscale_ref[...] = jnp.zeros_like(dscale_ref)

        dyp_v = dyp_ref[...].astype(F32)
        dscale_ref[...] += jnp.sum(dyp_v * e_ref[...].astype(F32), axis=0, keepdims=True)
        de_cur = dyp_v * scale_ref[...]
        de_nxt = jnp.where(i < n_tiles - 1, nxt_ref[...].astype(F32) * scale_ref[...], 0.0)
        de = jnp.concatenate([de_cur, de_nxt], axis=0).astype(BF16)
        pos = (i * tr + 1 + lax.broadcasted_iota(jnp.int32, (tr + POOL_HALO, 1), 0)).astype(F32)
        for g, w in enumerate(POOL_WINDOWS):
            cols = slice(g * c, (g + 1) * c)
            de_g = de[:, cols]
            dd = lax.dot_general(de_g, w_ref[g], nt_dims, preferred_element_type=F32)
            dw_ref[g] += lax.dot_general(d_ref[:, cols], de_g[:tr], tn_dims, preferred_element_type=F32)
            q = dd / jnp.minimum(pos, float(w))
            win = _window_sums(q, tr + POOL_HALO, forward=True)[g]
            dp_ref[:, cols] = (win[:tr] - dd[:tr]).astype(BF16)

    return pl.pallas_call(
        body, name=name, grid=(n_tiles,),
        in_specs=[_row(tr, pw), pl.BlockSpec((POOL_HALO, pw), lambda i: (jnp.minimum((i + 1) * per, last_halo), 0)),
                  _row(tr, pw), _row(tr, pw), pl.BlockSpec((n_groups, c, c), lambda i: (0, 0, 0)), _vec(pw)],
        out_specs=[_row(tr, pw), pl.BlockSpec((n_groups, c, c), lambda i: (0, 0, 0)), _vec(pw)],
        out_shape=[jax.ShapeDtypeStruct((t_dim, pw), BF16), jax.ShapeDtypeStruct((n_groups, c, c), F32),
                   jax.ShapeDtypeStruct((1, pw), F32)],
        compiler_params=_params("arbitrary"),
    )(dyp, dyp, e, d, w_group, scale.reshape(1, pw))


def _sgu_rows(t_dim, chunk):
    return chunk * _pick(t_dim // chunk, (2, 1))


def _tril(chunk):
    return lax.broadcasted_iota(jnp.int32, (chunk, chunk), 0) >= lax.broadcasted_iota(jnp.int32, (chunk, chunk), 1)


def sgu_fwd(z, gain, w_s, b_s, pw, sw, name):
    t_dim = z.shape[0]
    n_heads, chunk, _ = w_s.shape
    hd = sw // n_heads
    tr = _sgu_rows(t_dim, chunk)
    ju = pw // sw

    def body(u_ref, v_ref, gain_ref, w_ref, bt_ref, sg_ref):
        ug, _ = _gelu(u_ref[...])
        vg, _ = _gelu(v_ref[...])
        r = lax.rsqrt(jnp.mean(vg * vg, axis=-1, keepdims=True) + EPS)
        vn = ((vg * r) * gain_ref[...]).astype(BF16)
        tri = _tril(chunk)
        for h in range(n_heads):
            wm = jnp.where(tri, w_ref[h], 0.0).astype(BF16)
            cols = slice(h * hd, (h + 1) * hd)
            for ch in range(tr // chunk):
                rows = slice(ch * chunk, (ch + 1) * chunk)
                s = jnp.dot(wm, vn[rows, cols], preferred_element_type=F32) + bt_ref[:, h:h + 1]
                sg_ref[rows, cols] = (ug[rows, cols] * s).astype(BF16)

    return pl.pallas_call(
        body, name=name, grid=(t_dim // tr,),
        in_specs=[pl.BlockSpec((tr, sw), lambda i: (i, ju)), pl.BlockSpec((tr, sw), lambda i: (i, ju + 1)), _vec(sw),
                  pl.BlockSpec((n_heads, chunk, chunk), lambda i: (0, 0, 0)),
                  pl.BlockSpec((chunk, n_heads), lambda i: (0, 0))],
        out_specs=_row(tr, sw), out_shape=jax.ShapeDtypeStruct((t_dim, sw), BF16),
        compiler_params=_params("parallel"),
    )(z, z, gain.reshape(1, sw), w_s, b_s.T)


def sgu_bwd(z, dsg, gain, w_s, b_s, pw, sw, name):
    t_dim = z.shape[0]
    n_heads, chunk, _ = w_s.shape
    hd = sw // n_heads
    tr = _sgu_rows(t_dim, chunk)
    ju = pw // sw
    nt_dims = (((1,), (1,)), ((), ()))
    tn_dims = (((0,), (0,)), ((), ()))

    def body(u_ref, v_ref, dsg_ref, gain_ref, w_ref, bt_ref, du_ref, dv_ref, dw_ref, dbt_ref, dgain_ref,
             dvn_ref, dug_ref):
        @pl.when(pl.program_id(0) == 0)
        def _():
            dw_ref[...] = jnp.zeros_like(dw_ref)
            dbt_ref[...] = jnp.zeros_like(dbt_ref)
            dgain_ref[...] = jnp.zeros_like(dgain_ref)

        u = u_ref[...]
        v = v_ref[...]
        ug, tu = _gelu(u)
        vg, tv = _gelu(v)
        r = lax.rsqrt(jnp.mean(vg * vg, axis=-1, keepdims=True) + EPS)
        n = vg * r
        gain_v = gain_ref[...]
        vn = (n * gain_v).astype(BF16)
        dsg_v = dsg_ref[...].astype(F32)
        tri = _tril(chunk)
        for h in range(n_heads):
            wm = jnp.where(tri, w_ref[h], 0.0).astype(BF16)
            cols = slice(h * hd, (h + 1) * hd)
            for ch in range(tr // chunk):
                rows = slice(ch * chunk, (ch + 1) * chunk)
                vn_b = vn[rows, cols]
                s = jnp.dot(wm, vn_b, preferred_element_type=F32) + bt_ref[:, h:h + 1]
                dsg_b = dsg_v[rows, cols]
                dug_ref[rows, cols] = dsg_b * s
                ds = dsg_b * ug[rows, cols]
                ds_b = ds.astype(BF16)
                dw_ref[h] += jnp.where(tri, lax.dot_general(ds_b, vn_b, nt_dims, preferred_element_type=F32), 0.0)
                dbt_ref[:, h:h + 1] += jnp.sum(ds, axis=1, keepdims=True)
                dvn_ref[rows, cols] = lax.dot_general(wm, ds_b, tn_dims, preferred_element_type=F32)
        dvn = dvn_ref[...]
        dgain_ref[...] += jnp.sum(dvn * n, axis=0, keepdims=True)
        dn = dvn * gain_v
        dvg = r * (dn - n * jnp.mean(dn * n, axis=-1, keepdims=True))
        dv_ref[...] = (dvg * _gelu_grad(v, tv)).astype(BF16)
        du_ref[...] = (dug_ref[...] * _gelu_grad(u, tu)).astype(BF16)

    full_w = pl.BlockSpec((n_heads, chunk, chunk), lambda i: (0, 0, 0))
    full_b = pl.BlockSpec((chunk, n_heads), lambda i: (0, 0))
    du, dv, dw, dbt, dgain = pl.pallas_call(
        body, name=name, grid=(t_dim // tr,),
        in_specs=[pl.BlockSpec((tr, sw), lambda i: (i, ju)), pl.BlockSpec((tr, sw), lambda i: (i, ju + 1)),
                  _row(tr, sw), _vec(sw), full_w, full_b],
        out_specs=[_row(tr, sw), _row(tr, sw), full_w, full_b, _vec(sw)],
        out_shape=[jax.ShapeDtypeStruct((t_dim, sw), BF16), jax.ShapeDtypeStruct((t_dim, sw), BF16),
                   jax.ShapeDtypeStruct((n_heads, chunk, chunk), F32), jax.ShapeDtypeStruct((chunk, n_heads), F32),
                   jax.ShapeDtypeStruct((1, sw), F32)],
        scratch_shapes=[pltpu.VMEM((tr, sw), F32), pltpu.VMEM((tr, sw), F32)],
        compiler_params=_params("arbitrary"),
    )(z, z, dsg, gain.reshape(1, sw), w_s, b_s.T)
    return du, dv, dw, dbt.T, dgain.reshape(sw)


HBM_SPEC = pl.BlockSpec(memory_space=pltpu.HBM)
SEM_SPEC = pl.BlockSpec(memory_space=pltpu.SEMAPHORE)
ANY_SPEC = pl.BlockSpec(memory_space=pl.ANY)
DATAFLOW = pltpu.SideEffectType.DATAFLOW_SIDE_EFFECTING


def _hbm(a):
    return pltpu.with_memory_space_constraint(a, pltpu.HBM)


def _slot(ref, axis, k, n):
    idx = [slice(None)] * len(ref.shape)
    idx[axis] = pl.ds(k * n, n)
    return ref.at[tuple(idx)]


def _chip_of(k, core):
    return (k // 2, k % 2, core)


def _my_chip():
    return 2 * lax.axis_index("x") + lax.axis_index("y")


class Exchange:
    def __init__(self, gather, axes):
        self.gather, self.axes = gather, axes
        self.per = 1 if gather else 2

    def bufs(self, t, refs):
        return refs[t * self.per:(t + 1) * self.per]

    def views(self, t, bufs, src_chip, dst_chip):
        ax = self.axes[t]
        if self.gather:
            slot = _slot(bufs[0], ax, src_chip, bufs[0].shape[ax] // N_CHIPS)
            return slot, slot
        return _slot(bufs[0], ax, dst_chip, bufs[0].shape[ax] // N_CHIPS), bufs[1].at[src_chip]


def exchange_start(bufs, ex, groups, name, deps=()):
    nb, ng = len(bufs), len(groups)

    def body(*refs):
        ins, outs = refs[:nb], refs[nb + len(deps):]
        sems, token = outs[:2 * ng], outs[-1]
        core = lax.axis_index("c")
        me = _my_chip()
        for k in range(N_CHIPS):
            @pl.when(me == k)
            def _(k=k):
                for gi, group in enumerate(groups):
                    for ti, t in enumerate(group):
                        for j in range(N_CHIPS):
                            if j != k:
                                src, dst = ex.views(t, ex.bufs(t, ins), k, j)
                                pltpu.make_async_remote_copy(
                                    src_ref=src, dst_ref=dst, send_sem=sems[2 * gi].at[ti * N_CHIPS + j],
                                    recv_sem=sems[2 * gi + 1].at[ti * N_CHIPS + k], device_id=_chip_of(j, core),
                                    device_id_type=MESH).start()
        token[...] = jnp.zeros_like(token)

    sem_shapes = []
    for group in groups:
        sem_shapes += [pltpu.SemaphoreType.DMA((len(group) * N_CHIPS,))] * 2
    thru = [pltpu.HBM(a.shape, a.dtype) for a in bufs]
    outs = pl.pallas_call(
        body, name=name, out_shape=sem_shapes + thru + [jax.ShapeDtypeStruct((SUBLANE, LANE), F32)],
        in_specs=[HBM_SPEC] * nb + [ANY_SPEC] * len(deps),
        out_specs=[SEM_SPEC] * (2 * ng) + [HBM_SPEC] * nb + [pl.BlockSpec(memory_space=pltpu.VMEM)],
        input_output_aliases={i: 2 * ng + i for i in range(nb)},
        compiler_params=pltpu.CompilerParams(has_side_effects=DATAFLOW),
    )(*[_hbm(a) for a in bufs], *deps)
    sems = [(outs[2 * gi], outs[2 * gi + 1]) for gi in range(ng)]
    return sems, outs[2 * ng:2 * ng + nb], outs[-1]


def exchange_wait(bufs, tensors, ex, sems, after, name):
    nb = len(bufs)
    send_sems, recv_sems = sems

    def body(*refs):
        ins, send_ref, recv_ref = refs[:nb], refs[nb], refs[nb + 1]
        core = lax.axis_index("c")
        me = _my_chip()
        for k in range(N_CHIPS):
            @pl.when(me == k)
            def _(k=k):
                for ti, t in enumerate(tensors):
                    for j in range(N_CHIPS):
                        if j != k:
                            src, _ = ex.views(t, ex.bufs(ti, ins), k, j)
                            _, dst = ex.views(t, ex.bufs(ti, ins), j, k)
                            copy = pltpu.make_async_remote_copy(
                                src_ref=src, dst_ref=dst, send_sem=send_ref.at[ti * N_CHIPS + j],
                                recv_sem=recv_ref.at[ti * N_CHIPS + j],
                                device_id=_chip_of(j, core), device_id_type=MESH)
                            copy.wait_send()
                            copy.wait_recv()

    return pl.pallas_call(
        body, name=name, out_shape=[pltpu.HBM(a.shape, a.dtype) for a in bufs],
        in_specs=[HBM_SPEC] * nb + [SEM_SPEC, SEM_SPEC] + [ANY_SPEC] * len(after),
        out_specs=[HBM_SPEC] * nb, input_output_aliases={i: i for i in range(nb)},
        compiler_params=pltpu.CompilerParams(has_side_effects=DATAFLOW),
    )(*bufs, send_sems, recv_sems, *after)


def swap_with_sibling(arrs, name):
    nt = len(arrs)

    def body(*refs):
        ins, outs = refs[:nt], refs[nt:2 * nt]
        send_sems, recv_sems = refs[2 * nt:]
        sibling = (lax.axis_index("x"), lax.axis_index("y"), 1 - lax.axis_index("c"))
        copies = [pltpu.make_async_remote_copy(src_ref=ins[t], dst_ref=outs[t], send_sem=send_sems.at[t],
                                               recv_sem=recv_sems.at[t], device_id=sibling, device_id_type=MESH)
                  for t in range(nt)]
        for cp in copies:
            cp.start()
        for cp in copies:
            cp.wait()

    return pl.pallas_call(
        body, name=name, out_shape=[jax.ShapeDtypeStruct(a.shape, a.dtype) for a in arrs],
        in_specs=[HBM_SPEC] * nt, out_specs=[HBM_SPEC] * nt,
        scratch_shapes=[pltpu.SemaphoreType.DMA((nt,)), pltpu.SemaphoreType.DMA((nt,))],
        compiler_params=pltpu.CompilerParams(has_side_effects=True),
    )(*arrs)


def place_shard(w_stack, layer, axis, me, name):
    _, a_dim, r, c = w_stack.shape
    tr, tc = _stream_tiles(r, c)
    nr, nc = r // tr, c // tc
    full = (a_dim, r * N_CHIPS, c) if axis == 1 else (a_dim, r, c * N_CHIPS)

    def body(me_ref, w_ref, o_ref):
        o_ref[...] = w_ref[...].astype(BF16)

    def own_map(a, i, j, me_ref):
        return (a, me_ref[0] * nr + i, j) if axis == 1 else (a, i, me_ref[0] * nc + j)

    return pl.pallas_call(
        body, name=name, out_shape=jax.ShapeDtypeStruct(full, BF16),
        grid_spec=pltpu.PrefetchScalarGridSpec(
            num_scalar_prefetch=1, grid=(a_dim, nr, nc),
            in_specs=[pl.BlockSpec((None, None, tr, tc), lambda a, i, j, me_ref: (layer, a, i, j))],
            out_specs=pl.BlockSpec((None, tr, tc), own_map)),
        compiler_params=_params("parallel", "parallel", "parallel"),
    )(me, w_stack)


def sum_pieces(stack, full, axis, me, name):
    _, a_dim, r, c = stack.shape
    tr, tc = _stream_tiles(r, c)
    nr, nc = r // tr, c // tc

    def body(me_ref, own_ref, s1_ref, s2_ref, s3_ref, o_ref):
        acc = own_ref[...].astype(F32)
        for ref in (s1_ref, s2_ref, s3_ref):
            acc = acc + ref[...].astype(F32)
        o_ref[...] = acc

    def own_map(a, i, j, me_ref):
        return (a, me_ref[0] * nr + i, j) if axis == 1 else (a, i, me_ref[0] * nc + j)

    def from_chip(step):
        return pl.BlockSpec((None, None, tr, tc), lambda a, i, j, me_ref: ((me_ref[0] + step) % N_CHIPS, a, i, j))

    return pl.pallas_call(
        body, name=name, out_shape=jax.ShapeDtypeStruct((a_dim, r, c), F32),
        grid_spec=pltpu.PrefetchScalarGridSpec(
            num_scalar_prefetch=1, grid=(a_dim, nr, nc),
            in_specs=[pl.BlockSpec((None, tr, tc), own_map), from_chip(1), from_chip(2), from_chip(3)],
            out_specs=pl.BlockSpec((None, tr, tc), lambda a, i, j, me_ref: (a, i, j))),
        compiler_params=_params("parallel", "parallel", "parallel"),
    )(me, full, stack, stack, stack)


def all_reduce_small(x, name):
    rows, lanes = x.shape

    def body(x_ref, sum_ref, gath_ref, send_sems, recv_sems, local_sem):
        cx, cy, cc = lax.axis_index("x"), lax.axis_index("y"), lax.axis_index("c")
        me, sibling = (cx, cy, cc), (cx, cy, 1 - cc)
        chips = [(1 - cx, cy), (cx, 1 - cy), (1 - cx, 1 - cy)]

        def block(px, py, pc):
            return gath_ref.at[pl.ds(pl.multiple_of((4 * px + 2 * py + pc) * rows, SUBLANE), rows), :]

        def copy(k, blk, to, src=None):
            return pltpu.make_async_remote_copy(
                src_ref=block(*blk) if src is None else src, dst_ref=block(*blk),
                send_sem=send_sems.at[k], recv_sem=recv_sems.at[k], device_id=to, device_id_type=MESH)

        mine = pltpu.make_async_copy(x_ref, block(*me), local_sem)
        mine.start()
        first = [copy(0, me, sibling, src=x_ref)]
        first += [copy(1 + j, me, (*chip, cc), src=x_ref) for j, chip in enumerate(chips)]
        for cp in first:
            cp.start()
        passed = [copy(4 + j, (*chip, cc), sibling) for j, chip in enumerate(chips)]
        for j, chip in enumerate(chips):
            copy(1 + j, (*chip, cc), me).wait_recv()
            passed[j].start()
        copy(0, sibling, me).wait_recv()
        for j, chip in enumerate(chips):
            copy(4 + j, (*chip, 1 - cc), me).wait_recv()
        for cp in first + passed:
            cp.wait_send()
        mine.wait()
        acc = gath_ref[pl.ds(0, rows), :]
        for k in range(1, N_DEV):
            acc = acc + gath_ref[pl.ds(k * rows, rows), :]
        sum_ref[...] = acc

    return pl.pallas_call(
        body, name=name, out_shape=jax.ShapeDtypeStruct((rows, lanes), F32),
        in_specs=[pl.BlockSpec(memory_space=pltpu.VMEM)], out_specs=pl.BlockSpec(memory_space=pltpu.VMEM),
        scratch_shapes=[pltpu.VMEM((N_DEV * rows, lanes), F32), pltpu.SemaphoreType.DMA((7,)),
                        pltpu.SemaphoreType.DMA((7,)), pltpu.SemaphoreType.DMA],
        compiler_params=pltpu.CompilerParams(has_side_effects=True, vmem_limit_bytes=VMEM_LIMIT),
    )(x)


def _adamw(w, g, m, v):
    m = ADAM_B1 * m + (1.0 - ADAM_B1) * g
    v = ADAM_B2 * v + (1.0 - ADAM_B2) * (g * g)
    m_hat = m / (1.0 - ADAM_B1 ** ADAM_STEP)
    v_hat = v / (1.0 - ADAM_B2 ** ADAM_STEP)
    delta = -ADAM_LR * (m_hat / (jnp.sqrt(v_hat) + ADAM_EPS) + ADAM_WD * w)
    return delta, m, v


def adam_big(p_own, p_sib, w, m, v, layer, stacks, name):
    r, c = p_own.shape
    tr, tc = _stream_tiles(r, c)

    def body(p_ref, q_ref, w_ref, m_ref, v_ref, *rest):
        g_out, d_out, m_out, v_out = rest[4:]
        g = p_ref[...] + q_ref[...]
        delta, m_new, v_new = _adamw(w_ref[...], g, m_ref[...], v_ref[...])
        g_out[...] = g
        d_out[...] = delta
        m_out[...] = m_new
        v_out[...] = v_new

    flat = pl.BlockSpec((tr, tc), lambda i, j: (i, j))
    layered = pl.BlockSpec((None, tr, tc), lambda i, j: (layer, i, j))
    anyspace = pl.BlockSpec(memory_space=pl.ANY)
    out = jax.ShapeDtypeStruct(w.shape, F32)
    return pl.pallas_call(
        body, name=name, grid=(r // tr, c // tc),
        in_specs=[flat, flat, layered, layered, layered] + [anyspace] * 4,
        out_specs=[layered] * 4, out_shape=[out] * 4, input_output_aliases={5: 0, 6: 1, 7: 2, 8: 3},
        compiler_params=_params("parallel", "parallel"),
    )(p_own, p_sib, w, m, v, *stacks)


def adam_small(g, w, m, v, name):
    rows, lanes = g.shape
    tr = _pick(rows, (512,) + ROW_TILES)

    def body(g_ref, w_ref, m_ref, v_ref, d_out, m_out, v_out):
        delta, m_new, v_new = _adamw(w_ref[...], g_ref[...], m_ref[...], v_ref[...])
        d_out[...] = delta
        m_out[...] = m_new
        v_out[...] = v_new

    out = jax.ShapeDtypeStruct((rows, lanes), F32)
    return pl.pallas_call(
        body, name=name, grid=(rows // tr,), in_specs=[_row(tr, lanes)] * 4, out_specs=[_row(tr, lanes)] * 3,
        out_shape=[out] * 3, compiler_params=_params("parallel"),
    )(g, w, m, v)


def ffn_fwd(x, g_pre, w_up, w_down, g_post, tag, deps=()):
    h = rms_fwd(x, g_pre, name=f"{tag}_rms_fwd", deps=deps)
    gate, up, a = matmul_swiglu(h, w_up, name=f"{tag}_up")
    f = matmul(a, w_down, out_dtype=F32, name=f"{tag}_down")
    y = res_rms_fwd(x, f, g_post, MACARON_WEIGHT, name=f"{tag}_res_fwd")
    return y, (x, h, gate, up, a, f)


def ffn_bwd(dy, saved, g_pre, w_up, w_down, g_post, tag, deps=()):
    x, h, gate, up, a, f = saved
    df, dg_post = rms_bwd(f, g_post, dy, MACARON_WEIGHT, None, BF16, name=f"{tag}_res_bwd", deps=deps)
    da = matmul(df, w_down, tb=True, out_dtype=BF16, name=f"{tag}_down_dx")
    dw_down = matmul(a, df, ta=True, out_dtype=BF16, name=f"{tag}_down_dw")
    dz = swiglu_bwd(gate, up, da, name=f"{tag}_swiglu_bwd")
    dh = matmul(dz, w_up, tb=True, out_dtype=F32, name=f"{tag}_up_dx")
    dw_up = matmul(h, dz, ta=True, out_dtype=BF16, name=f"{tag}_up_dw")
    dx, dg_pre = rms_bwd(x, g_pre, dh, 1.0, dy, F32, name=f"{tag}_rms_bwd")
    return dx, dw_up, dw_down, dg_pre, dg_post


def mixer_fwd(x, wt, sm, dims):
    pw, sw, d = dims
    h = rms_fwd(x, sm['g_mix_pre'], name="mix_rms_fwd")
    z = matmul(h, wt['w_in'], out_dtype=F32, name="mix_in")
    dd, e, yp = pool_fwd(z, wt['pool_group_w'], sm['pool_scale'], pw, name="mix_pool_fwd")
    ya = matmul(yp, wt['w_pool_out'], out_dtype=BF16, name="mix_pool_out")
    sg = sgu_fwd(z, sm['sgu_v_gain'], sm['sgu_w_s'], sm['sgu_b_s'], pw, sw, name="mix_sgu_fwd")
    yb = matmul(sg, wt['w_sgu_out'], out_dtype=BF16, name="mix_sgu_out")
    m = gate_fwd(z, ya, yb, pw + 2 * sw, pw + 2 * sw + d, name="mix_gate_fwd")
    o = matmul(m, wt['w_out'], out_dtype=F32, name="mix_out")
    y = res_rms_fwd(x, o, sm['g_mix_post'], 1.0, name="mix_res_fwd")
    return y, (x, h, z, dd, e, yp, sg, ya, yb, m, o)


def mixer_bwd(dy, saved, wt, sm, dims, deps=()):
    pw, sw, d = dims
    x, h, z, dd, e, yp, sg, ya, yb, m, o = saved
    grads = {}
    do, grads['g_mix_post'] = rms_bwd(o, sm['g_mix_post'], dy, 1.0, None, BF16, name="mix_res_bwd", deps=deps)
    dm = matmul(do, wt['w_out'], tb=True, out_dtype=BF16, name="mix_out_dx")
    grads['w_out'] = matmul(m, do, ta=True, out_dtype=BF16, name="mix_out_dw")
    dya, dyb, dga, dgb = gate_bwd(z, ya, yb, dm, pw + 2 * sw, pw + 2 * sw + d, name="mix_gate_bwd")
    dyp = matmul(dya, wt['w_pool_out'], tb=True, out_dtype=BF16, name="mix_pool_out_dx")
    grads['w_pool_out'] = matmul(yp, dya, ta=True, out_dtype=BF16, name="mix_pool_out_dw")
    dp, dwg, dscale = pool_bwd(dyp, e, dd, wt['pool_group_w'], sm['pool_scale'], name="mix_pool_bwd")
    grads['pool_group_w'] = dwg.astype(BF16)
    grads['pool_scale'] = dscale.reshape(pw)
    dsg = matmul(dyb, wt['w_sgu_out'], tb=True, out_dtype=BF16, name="mix_sgu_out_dx")
    grads['w_sgu_out'] = matmul(sg, dyb, ta=True, out_dtype=BF16, name="mix_sgu_out_dw")
    du, dv, grads['sgu_w_s'], grads['sgu_b_s'], grads['sgu_v_gain'] = sgu_bwd(
        z, dsg, sm['sgu_v_gain'], sm['sgu_w_s'], sm['sgu_b_s'], pw, sw, name="mix_sgu_bwd")
    dz = jnp.concatenate([dp, du, dv, dga, dgb], axis=1)
    dh = matmul(dz, wt['w_in'], tb=True, out_dtype=F32, name="mix_in_dx")
    grads['w_in'] = matmul(h, dz, ta=True, out_dtype=BF16, name="mix_in_dw")
    dx, grads['g_mix_pre'] = rms_bwd(x, sm['g_mix_pre'], dh, 1.0, dy, F32, name="mix_rms_bwd")
    return dx, grads


def _as_rows(a):
    return a.reshape(a.shape[0], -1, a.shape[-1])


def kernel(x, g_ffn1_pre, w_ffn1_up, w_ffn1_down, g_ffn1_post, g_mix_pre, w_in, pool_group_w, pool_scale, w_pool_out, sgu_v_gain, sgu_w_s, sgu_b_s, w_sgu_out, w_out, g_mix_post, g_ffn2_pre, w_ffn2_up, w_ffn2_down, g_ffn2_post, loss_target, m_g_ffn1_pre, m_w_ffn1_up, m_w_ffn1_down, m_g_ffn1_post, m_g_mix_pre, m_w_in, m_pool_group_w, m_pool_scale, m_w_pool_out, m_sgu_v_gain, m_sgu_w_s, m_sgu_b_s, m_w_sgu_out, m_w_out, m_g_mix_post, m_g_ffn2_pre, m_w_ffn2_up, m_w_ffn2_down, m_g_ffn2_post, v_g_ffn1_pre, v_w_ffn1_up, v_w_ffn1_down, v_g_ffn1_post, v_g_mix_pre, v_w_in, v_pool_group_w, v_pool_scale, v_w_pool_out, v_sgu_v_gain, v_sgu_w_s, v_sgu_b_s, v_w_sgu_out, v_w_out, v_g_mix_post, v_g_ffn2_pre, v_w_ffn2_up, v_w_ffn2_down, v_g_ffn2_post):
    w = dict(zip(WEIGHTS, (g_ffn1_pre, w_ffn1_up, w_ffn1_down, g_ffn1_post, g_mix_pre, w_in, pool_group_w, pool_scale, w_pool_out, sgu_v_gain, sgu_w_s, sgu_b_s, w_sgu_out, w_out, g_mix_post, g_ffn2_pre, w_ffn2_up, w_ffn2_down, g_ffn2_post)))
    mom = dict(zip(WEIGHTS, (m_g_ffn1_pre, m_w_ffn1_up, m_w_ffn1_down, m_g_ffn1_post, m_g_mix_pre, m_w_in, m_pool_group_w, m_pool_scale, m_w_pool_out, m_sgu_v_gain, m_sgu_w_s, m_sgu_b_s, m_w_sgu_out, m_w_out, m_g_mix_post, m_g_ffn2_pre, m_w_ffn2_up, m_w_ffn2_down, m_g_ffn2_post)))
    var = dict(zip(WEIGHTS, (v_g_ffn1_pre, v_w_ffn1_up, v_w_ffn1_down, v_g_ffn1_post, v_g_mix_pre, v_w_in, v_pool_group_w, v_pool_scale, v_w_pool_out, v_sgu_v_gain, v_sgu_w_s, v_sgu_b_s, v_w_sgu_out, v_w_out, v_g_mix_post, v_g_ffn2_pre, v_w_ffn2_up, v_w_ffn2_down, v_g_ffn2_post)))
    depth = g_ffn1_pre.shape[0]
    d = x.shape[-1]
    pw = pool_scale.shape[-1]
    sw = sgu_v_gain.shape[-1]
    dims = (pw, sw, d)
    axes = [BIG_AXIS[n] for n in BIG]

    me = 2 * lax.axis_index("x") + lax.axis_index("y")
    gather = Exchange(True, axes)
    group_ids = [[BIG.index(n) for n in group] for group in BLOCK_WEIGHTS]

    me1 = me.reshape(1).astype(jnp.int32)

    def start_gather(l, deps=()):
        lands = []
        for n, ax in zip(BIG, axes):
            shards = w[n] if w[n].ndim == 4 else w[n][:, None]
            full = place_shard(shards, l, ax + 4 - w[n].ndim, me1, name=f"place_{n}")
            lands.append(full if w[n].ndim == 4 else full[0])
        return exchange_start(lands, gather, group_ids, name=f"gather_start_l{l}", deps=deps)

    def finish_gather(started, l, gi, after):
        sems, lands, _ = started
        ids = group_ids[gi]
        got = exchange_wait([lands[t] for t in ids], ids, gather, sems[gi], (after,), name=f"gather_wait_l{l}_b{gi}")
        return dict(zip(BLOCK_WEIGHTS[gi], got))

    small = [{n: w[n][l] for n in SMALL} for l in range(depth)]

    act = x[0]
    saved, full = [], []
    gathers = []
    for l in range(depth):
        gathers.append(start_gather(l, (gathers[-1][2],) if gathers else ()))
    for l in range(depth):
        started = gathers[l]
        deps = (gathers[-1][2],) if l == 0 else ()
        sm, wt = small[l], {}
        wt.update(finish_gather(started, l, 0, act))
        act, s1 = ffn_fwd(act, sm['g_ffn1_pre'], wt['w_ffn1_up'], wt['w_ffn1_down'], sm['g_ffn1_post'], "ffn1", deps)
        wt.update(finish_gather(started, l, 1, act))
        act, s2 = mixer_fwd(act, wt, sm, dims)
        wt.update(finish_gather(started, l, 2, act))
        act, s3 = ffn_fwd(act, sm['g_ffn2_pre'], wt['w_ffn2_up'], wt['w_ffn2_down'], sm['g_ffn2_post'], "ffn2")
        saved.append((s1, s2, s3))
        full.append(wt)
    dact, sq_sum = loss_grad(act, loss_target[0], name="loss_grad")
    loss = lax.psum(0.5 * sq_sum / d, ("x", "y", "c"))

    rows3 = {n: (_as_rows(w[n]), _as_rows(mom[n]), _as_rows(var[n])) for n in BIG}
    stacks = {n: tuple(lax.empty(rows3[n][0].shape, F32) for _ in range(4)) for n in BIG}
    small_grads = [{} for _ in range(depth)]

    def start_scatter(names, grads, l, block):
        ex = Exchange(False, [BIG_AXIS[n] for n in names])
        bufs = []
        for n in names:
            piece = list(grads[n].shape)
            piece[BIG_AXIS[n]] //= N_CHIPS
            bufs += [grads[n], lax.empty((N_CHIPS, *piece), BF16)]
        return (names, ex, l, block) + exchange_start(bufs, ex, [list(range(len(names)))],
                                                      name=f"scatter_start_l{l}_b{block}")

    def finish_scatter(pending, after):
        names, ex, l, block, sems, bufs, _ = pending
        got = exchange_wait(bufs, list(range(len(names))), ex, sems[0], after, name=f"scatter_wait_l{l}_b{block}")
        plane = []
        for ti, n in enumerate(names):
            full_g, stack = got[2 * ti], got[2 * ti + 1]
            if full_g.ndim == 2:
                full_g, stack = full_g[None], stack[:, None]
            p = sum_pieces(stack, full_g, BIG_AXIS[n] + 3 - got[2 * ti].ndim, me1, name=f"sum_{n}")
            plane.append(p.reshape(-1, p.shape[-1]))
        other = swap_with_sibling(plane, name=f"swap_plane_sums_{len(names)}")
        for n, p_own, p_sib in zip(names, plane, other):
            stacks[n] = tuple(adam_big(p_own, p_sib, *rows3[n], l, stacks[n], name=f"adam_{n}"))
        return stacks[names[-1]][0]

    pending, done = None, ()
    for l in reversed(range(depth)):
        wt, sm = full[l], small[l]
        s1, s2, s3 = saved[l]
        for block in (2, 1, 0):
            deps = (pending[-1],) if pending is not None else ()
            g = {}
            if block == 2:
                dnew, g['w_ffn2_up'], g['w_ffn2_down'], g['g_ffn2_pre'], g['g_ffn2_post'] = ffn_bwd(
                    dact, s3, sm['g_ffn2_pre'], wt['w_ffn2_up'], wt['w_ffn2_down'], sm['g_ffn2_post'], "ffn2", deps)
            elif block == 1:
                dnew, g = mixer_bwd(dact, s2, wt, sm, dims, deps)
            else:
                dnew, g['w_ffn1_up'], g['w_ffn1_down'], g['g_ffn1_pre'], g['g_ffn1_post'] = ffn_bwd(
                    dact, s1, sm['g_ffn1_pre'], wt['w_ffn1_up'], wt['w_ffn1_down'], sm['g_ffn1_post'], "ffn1", deps)
            small_grads[l].update({n: g[n] for n in g if n in SMALL})
            if pending is not None:
                done = (finish_scatter(pending, (dnew,) + done),)
            pending = start_scatter(BLOCK_WEIGHTS[block], g, l, block)
            dact = dnew
    finish_scatter(pending, (dact,) + done)

    def flat(tree):
        v = jnp.concatenate([tree[n].reshape(-1).astype(F32) for n in SMALL])
        pad = (-v.shape[0]) % (SUBLANE * LANE)
        return jnp.pad(v, (0, pad)).reshape(-1, LANE)

    g_small = all_reduce_small(flat({n: jnp.stack([small_grads[l][n] for l in range(depth)]) for n in SMALL}),
                               name="all_reduce_small")
    d_small, m_small, v_small = adam_small(g_small, flat(w), flat(mom), flat(var), name="adam_small")

    def unflat(block):
        v, out, at = block.reshape(-1), {}, 0
        for n in SMALL:
            out[n] = v[at:at + w[n].size].reshape(w[n].shape)
            at += w[n].size
        return out

    result = [{}, {}, {}, {}]
    for tree, block in zip(result, (g_small, d_small, m_small, v_small)):
        tree.update(unflat(block))
    for n in BIG:
        for tree, stack in zip(result, stacks[n]):
            tree[n] = stack.reshape(w[n].shape)
    return (loss, dact.reshape(x.shape), *[tree[n] for tree in result for n in WEIGHTS])
```

```python
import math

import jax
import jax.numpy as jnp
from jax import lax
from jax.experimental import pallas as pl
from jax.experimental.pallas import tpu as pltpu

F32 = jnp.float32
BF16 = jnp.bfloat16
MESH = pl.DeviceIdType.MESH

EPS = 1e-6
MACARON_WEIGHT = 0.5
POOL_WINDOWS = (2, 4, 8, 16)
POOL_HALO = 16
ADAM_LR = 0.001
ADAM_B1 = 0.9
ADAM_B2 = 0.999
ADAM_EPS = 1e-08
ADAM_WD = 0.01
ADAM_STEP = 10
GELU_K = math.sqrt(2.0 / math.pi)
GELU_C = 0.044715

N_CHIPS = 4
N_DEV = 8
V7X_VMEM_BYTES = 64 * 1024 * 1024
VMEM_LIMIT = (V7X_VMEM_BYTES * 3) // 4
LANE = 128
SUBLANE = 8

WEIGHTS = ['g_ffn1_pre', 'w_ffn1_up', 'w_ffn1_down', 'g_ffn1_post', 'g_mix_pre', 'w_in', 'pool_group_w',
           'pool_scale', 'w_pool_out', 'sgu_v_gain', 'sgu_w_s', 'sgu_b_s', 'w_sgu_out', 'w_out', 'g_mix_post',
           'g_ffn2_pre', 'w_ffn2_up', 'w_ffn2_down', 'g_ffn2_post']
BIG_AXIS = {'w_ffn1_up': 1, 'w_ffn1_down': 0, 'w_in': 1, 'pool_group_w': 1, 'w_pool_out': 1, 'w_sgu_out': 1,
            'w_out': 0, 'w_ffn2_up': 1, 'w_ffn2_down': 0}
BIG = list(BIG_AXIS)
BLOCK_WEIGHTS = [['w_ffn1_up', 'w_ffn1_down'], ['w_in', 'pool_group_w', 'w_pool_out', 'w_sgu_out', 'w_out'],
                 ['w_ffn2_up', 'w_ffn2_down']]
SMALL = [n for n in WEIGHTS if n not in BIG_AXIS]


def _pick(dim, cands):
    for c in cands:
        if dim % c == 0:
            return c
    return dim


STREAM_COL_TILES = (1024, 1408, 896, 512, 256, 128)
STREAM_ROW_TILES = (512, 352, 256, 128, 64, 32, 16, 8)
STREAM_BLOCK_ELEMS = 384 * 1024


def _stream_tiles(r, c):
    tc = _pick(c, STREAM_COL_TILES)
    for tr in STREAM_ROW_TILES:
        if r % tr == 0 and tr * tc <= STREAM_BLOCK_ELEMS:
            return tr, tc
    return r, tc


def _params(*sem):
    return pltpu.CompilerParams(dimension_semantics=sem if sem else None, vmem_limit_bytes=VMEM_LIMIT)


def _sigmoid(x):
    return 1.0 / (1.0 + jnp.exp(-x))


def _gelu(x):
    t = jnp.tanh(GELU_K * (x + GELU_C * (x * x * x)))
    return x * (0.5 * (1.0 + t)), t


def _gelu_grad(x, t):
    return 0.5 * (1.0 + t) + (0.5 * x) * (1.0 - t * t) * (GELU_K * (1.0 + (3.0 * GELU_C) * (x * x)))


MATMUL_MN_TILES = (1024, 1408, 512, 256, 128)
MATMUL_K_TILES = (2048, 2816, 1024, 512, 256, 128)


def matmul(a, b, *, ta=False, tb=False, out_dtype=BF16, name):
    m_dim, k_dim = (a.shape[1], a.shape[0]) if ta else a.shape
    n_dim = b.shape[0] if tb else b.shape[1]
    tm = _pick(m_dim, MATMUL_MN_TILES)
    tn = _pick(n_dim, MATMUL_MN_TILES)
    tk = _pick(k_dim, MATMUL_K_TILES)
    nk = k_dim // tk
    dims = (((0 if ta else 1,), (1 if tb else 0,)), ((), ()))

    def body(a_ref, b_ref, o_ref, acc_ref):
        k = pl.program_id(2)

        @pl.when(k == 0)
        def _():
            acc_ref[...] = jnp.zeros_like(acc_ref)

        acc_ref[...] += lax.dot_general(a_ref[...], b_ref[...], dims, preferred_element_type=F32)

        @pl.when(k == nk - 1)
        def _():
            o_ref[...] = acc_ref[...].astype(o_ref.dtype)

    a_spec = pl.BlockSpec((tk, tm), lambda i, j, k: (k, i)) if ta else pl.BlockSpec((tm, tk), lambda i, j, k: (i, k))
    b_spec = pl.BlockSpec((tn, tk), lambda i, j, k: (j, k)) if tb else pl.BlockSpec((tk, tn), lambda i, j, k: (k, j))
    return pl.pallas_call(
        body, name=name, grid=(m_dim // tm, n_dim // tn, nk),
        in_specs=[a_spec, b_spec], out_specs=pl.BlockSpec((tm, tn), lambda i, j, k: (i, j)),
        out_shape=jax.ShapeDtypeStruct((m_dim, n_dim), out_dtype),
        scratch_shapes=[pltpu.VMEM((tm, tn), F32)],
        compiler_params=_params("parallel", "parallel", "arbitrary"),
    )(a, b)


def matmul_swiglu(h, w_up, name):
    m_dim, k_dim = h.shape
    f = w_up.shape[1] // 2
    tm = _pick(m_dim, MATMUL_MN_TILES)
    tn = _pick(f, (512, 256, 128))
    tk = _pick(k_dim, MATMUL_K_TILES)
    nk, nf = k_dim // tk, f // tn

    def body(h_ref, wg_ref, wu_ref, g_ref, u_ref, a_ref, accg_ref, accu_ref):
        k = pl.program_id(2)

        @pl.when(k == 0)
        def _():
            accg_ref[...] = jnp.zeros_like(accg_ref)
            accu_ref[...] = jnp.zeros_like(accu_ref)

        hv = h_ref[...]
        accg_ref[...] += jnp.dot(hv, wg_ref[...], preferred_element_type=F32)
        accu_ref[...] += jnp.dot(hv, wu_ref[...], preferred_element_type=F32)

        @pl.when(k == nk - 1)
        def _():
            g, u = accg_ref[...], accu_ref[...]
            g_ref[...] = g.astype(BF16)
            u_ref[...] = u.astype(BF16)
            a_ref[...] = (g * _sigmoid(g) * u).astype(BF16)

    out = jax.ShapeDtypeStruct((m_dim, f), BF16)
    blk = pl.BlockSpec((tm, tn), lambda i, j, k: (i, j))
    return pl.pallas_call(
        body, name=name, grid=(m_dim // tm, nf, nk),
        in_specs=[pl.BlockSpec((tm, tk), lambda i, j, k: (i, k)), pl.BlockSpec((tk, tn), lambda i, j, k: (k, j)),
                  pl.BlockSpec((tk, tn), lambda i, j, k: (k, j + nf))],
        out_specs=[blk, blk, blk], out_shape=[out, out, out],
        scratch_shapes=[pltpu.VMEM((tm, tn), F32), pltpu.VMEM((tm, tn), F32)],
        compiler_params=_params("parallel", "parallel", "arbitrary"),
    )(h, w_up, w_up)


ROW_TILES = (256, 128, 64, 32, 16, 8)


def _row(tr, width):
    return pl.BlockSpec((tr, width), lambda i: (i, 0))


def _vec(width):
    return pl.BlockSpec((1, width), lambda i: (0, 0))


def rms_fwd(x, g, name, deps=()):
    t_dim, d = x.shape
    tr = _pick(t_dim, ROW_TILES)

    def body(x_ref, g_ref, *rest):
        o_ref = rest[-1]
        xv = x_ref[...]
        r = lax.rsqrt(jnp.mean(xv * xv, axis=-1, keepdims=True) + EPS)
        o_ref[...] = ((xv * r) * g_ref[...]).astype(o_ref.dtype)

    return pl.pallas_call(
        body, name=name, grid=(t_dim // tr,), in_specs=[_row(tr, d), _vec(d)] + [ANY_SPEC] * len(deps),
        out_specs=_row(tr, d), out_shape=jax.ShapeDtypeStruct((t_dim, d), BF16), compiler_params=_params("parallel"),
    )(x, g.reshape(1, d), *deps)


def res_rms_fwd(x, f, g, weight, name):
    t_dim, d = x.shape
    tr = _pick(t_dim, ROW_TILES)

    def body(x_ref, f_ref, g_ref, o_ref):
        fv = f_ref[...]
        r = lax.rsqrt(jnp.mean(fv * fv, axis=-1, keepdims=True) + EPS)
        o_ref[...] = x_ref[...] + weight * ((fv * r) * g_ref[...])

    return pl.pallas_call(
        body, name=name, grid=(t_dim // tr,), in_specs=[_row(tr, d), _row(tr, d), _vec(d)], out_specs=_row(tr, d),
        out_shape=jax.ShapeDtypeStruct((t_dim, d), F32), compiler_params=_params("parallel"),
    )(x, f, g.reshape(1, d))


def rms_bwd(f, g, dy, weight, resid, out_dtype, name, deps=()):
    t_dim, d = f.shape
    tr = _pick(t_dim, ROW_TILES)
    has_resid = resid is not None

    def body(*refs):
        o_ref, dg_ref = refs[-2:]
        if has_resid:
            f_ref, g_ref, dy_ref, res_ref = refs[:4]
        else:
            f_ref, g_ref, dy_ref = refs[:3]

        @pl.when(pl.program_id(0) == 0)
        def _():
            dg_ref[...] = jnp.zeros_like(dg_ref)

        fv = f_ref[...]
        r = lax.rsqrt(jnp.mean(fv * fv, axis=-1, keepdims=True) + EPS)
        n = fv * r
        dyw = dy_ref[...] * weight
        dn = dyw * g_ref[...]
        df = r * (dn - n * jnp.mean(dn * n, axis=-1, keepdims=True))
        if has_resid:
            df = df + res_ref[...]
        o_ref[...] = df.astype(o_ref.dtype)
        dg_ref[...] += jnp.sum(dyw * n, axis=0, keepdims=True)

    ins = [f, g.reshape(1, d), dy] + ([resid] if has_resid else []) + list(deps)
    in_specs = [_row(tr, d), _vec(d), _row(tr, d)] + ([_row(tr, d)] if has_resid else []) + [ANY_SPEC] * len(deps)
    out, dg = pl.pallas_call(
        body, name=name, grid=(t_dim // tr,), in_specs=in_specs, out_specs=[_row(tr, d), _vec(d)],
        out_shape=[jax.ShapeDtypeStruct((t_dim, d), out_dtype), jax.ShapeDtypeStruct((1, d), F32)],
        compiler_params=_params("arbitrary"),
    )(*ins)
    return out, dg.reshape(d)


def loss_grad(y, target, name):
    t_dim, d = y.shape
    tr = _pick(t_dim, ROW_TILES)
    inv_d = 1.0 / d

    def body(y_ref, t_ref, dy_ref, s_ref):
        @pl.when(pl.program_id(0) == 0)
        def _():
            s_ref[...] = jnp.zeros_like(s_ref)

        e = y_ref[...] - t_ref[...]
        dy_ref[...] = e * inv_d
        s_ref[...] += jnp.sum(e * e)

    dy, s = pl.pallas_call(
        body, name=name, grid=(t_dim // tr,), in_specs=[_row(tr, d), _row(tr, d)],
        out_specs=[_row(tr, d), pl.BlockSpec((SUBLANE, LANE), lambda i: (0, 0))],
        out_shape=[jax.ShapeDtypeStruct((t_dim, d), F32), jax.ShapeDtypeStruct((SUBLANE, LANE), F32)],
        compiler_params=_params("arbitrary"),
    )(y, target)
    return dy, s[0, 0]


def swiglu_bwd(gate, up, da, name):
    t_dim, f = gate.shape
    tr = _pick(t_dim, ROW_TILES)
    tc = _pick(f, (512, 256, 128))

    def body(g_ref, u_ref, da_ref, o_ref):
        for c in range(f // tc):
            lo = c * tc
            g = g_ref[:, lo:lo + tc].astype(F32)
            u = u_ref[:, lo:lo + tc].astype(F32)
            da = da_ref[:, lo:lo + tc].astype(F32)
            s = _sigmoid(g)
            o_ref[:, lo:lo + tc] = (da * u * (s * (1.0 + g * (1.0 - s)))).astype(o_ref.dtype)
            o_ref[:, f + lo:f + lo + tc] = (da * (g * s)).astype(o_ref.dtype)

    return pl.pallas_call(
        body, name=name, grid=(t_dim // tr,), in_specs=[_row(tr, f)] * 3, out_specs=_row(tr, 2 * f),
        out_shape=jax.ShapeDtypeStruct((t_dim, 2 * f), BF16), compiler_params=_params("parallel"),
    )(gate, up, da)


def gate_fwd(z, ya, yb, off_a, off_b, name):
    t_dim, d = ya.shape
    tr = _pick(t_dim, (512,) + ROW_TILES)
    tc = math.gcd(math.gcd(off_a, off_b), _pick(d, (512, 256, 128)))
    ja, jb = off_a // tc, off_b // tc

    def body(ga_ref, gb_ref, ya_ref, yb_ref, m_ref):
        m = _sigmoid(ga_ref[...]) * ya_ref[...].astype(F32) + _sigmoid(gb_ref[...]) * yb_ref[...].astype(F32)
        m_ref[...] = m.astype(m_ref.dtype)

    blk = pl.BlockSpec((tr, tc), lambda i, j: (i, j))
    return pl.pallas_call(
        body, name=name, grid=(t_dim // tr, d // tc),
        in_specs=[pl.BlockSpec((tr, tc), lambda i, j: (i, j + ja)), pl.BlockSpec((tr, tc), lambda i, j: (i, j + jb)),
                  blk, blk],
        out_specs=blk, out_shape=jax.ShapeDtypeStruct((t_dim, d), BF16),
        compiler_params=_params("parallel", "parallel"),
    )(z, z, ya, yb)


def gate_bwd(z, ya, yb, dm, off_a, off_b, name):
    t_dim, d = ya.shape
    tr = _pick(t_dim, (512,) + ROW_TILES)
    tc = math.gcd(math.gcd(off_a, off_b), _pick(d, (512, 256, 128)))
    ja, jb = off_a // tc, off_b // tc

    def body(ga_ref, gb_ref, ya_ref, yb_ref, dm_ref, dya_ref, dyb_ref, dga_ref, dgb_ref):
        dm = dm_ref[...].astype(F32)
        sa = _sigmoid(ga_ref[...])
        sb = _sigmoid(gb_ref[...])
        dya_ref[...] = (dm * sa).astype(BF16)
        dyb_ref[...] = (dm * sb).astype(BF16)
        dga_ref[...] = (dm * ya_ref[...].astype(F32) * (sa * (1.0 - sa))).astype(BF16)
        dgb_ref[...] = (dm * yb_ref[...].astype(F32) * (sb * (1.0 - sb))).astype(BF16)

    blk = pl.BlockSpec((tr, tc), lambda i, j: (i, j))
    out = jax.ShapeDtypeStruct((t_dim, d), BF16)
    return pl.pallas_call(
        body, name=name, grid=(t_dim // tr, d // tc),
        in_specs=[pl.BlockSpec((tr, tc), lambda i, j: (i, j + ja)), pl.BlockSpec((tr, tc), lambda i, j: (i, j + jb)),
                  blk, blk, blk],
        out_specs=[blk] * 4, out_shape=[out] * 4, compiler_params=_params("parallel", "parallel"),
    )(z, z, ya, yb, dm)


def _window_sums(e, n_rows, forward):
    def shifted(v, k):
        return pltpu.roll(v, (n_rows - k) if forward else k, 0)

    s2 = e + shifted(e, 1)
    s4 = s2 + shifted(s2, 2)
    s8 = s4 + shifted(s4, 4)
    s16 = s8 + shifted(s8, 8)
    return (s2, s4, s8, s16)


def _pool_rows(t_dim):
    return _pick(t_dim, (256, 128, 64, 32, 16))


def pool_fwd(z, w_group, scale, pw, name):
    t_dim = z.shape[0]
    n_groups, c, _ = w_group.shape
    tr = _pool_rows(t_dim)
    per = tr // POOL_HALO

    def body(cur_ref, prev_ref, w_ref, scale_ref, d_ref, e_ref, yp_ref):
        i = pl.program_id(0)
        cur = cur_ref[...]
        prev = jnp.where(i > 0, prev_ref[...], 0.0)
        ext = jnp.concatenate([prev, cur], axis=0)
        sums = _window_sums(ext, tr + POOL_HALO, forward=False)
        pos = (i * tr + 1 + lax.broadcasted_iota(jnp.int32, (tr, 1), 0)).astype(F32)
        for g, w in enumerate(POOL_WINDOWS):
            cols = slice(g * c, (g + 1) * c)
            cnt = jnp.minimum(pos, float(w))
            d = (sums[g][POOL_HALO:, cols] / cnt - cur[:, cols]).astype(BF16)
            e = jnp.dot(d, w_ref[g], preferred_element_type=F32)
            d_ref[:, cols] = d
            e_ref[:, cols] = e.astype(BF16)
            yp_ref[:, cols] = (e * scale_ref[:, cols]).astype(BF16)

    out = jax.ShapeDtypeStruct((t_dim, pw), BF16)
    return pl.pallas_call(
        body, name=name, grid=(t_dim // tr,),
        in_specs=[_row(tr, pw), pl.BlockSpec((POOL_HALO, pw), lambda i: (jnp.maximum(i * per - 1, 0), 0)),
                  pl.BlockSpec((n_groups, c, c), lambda i: (0, 0, 0)), _vec(pw)],
        out_specs=[_row(tr, pw)] * 3, out_shape=[out] * 3, compiler_params=_params("parallel"),
    )(z, z, w_group, scale.reshape(1, pw))


def pool_bwd(dyp, e, d, w_group, scale, name):
    t_dim, pw = dyp.shape
    n_groups, c, _ = w_group.shape
    tr = _pool_rows(t_dim)
    per = tr // POOL_HALO
    n_tiles = t_dim // tr
    last_halo = t_dim // POOL_HALO - 1
    nt_dims = (((1,), (1,)), ((), ()))
    tn_dims = (((0,), (0,)), ((), ()))

    def body(dyp_ref, nxt_ref, e_ref, d_ref, w_ref, scale_ref, dp_ref, dw_ref, dscale_ref):
        i = pl.program_id(0)

        @pl.when(i == 0)
        def _():
            dw_ref[...] = jnp.zeros_like(dw_ref)
            dscale_ref[...] = jnp.zeros_like(dscale_ref)

        dyp_v = dyp_ref[...].astype(F32)
        dscale_ref[...] += jnp.sum(dyp_v * e_ref[...].astype(F32), axis=0, keepdims=True)
        de_cur = dyp_v * scale_ref[...]
        de_nxt = jnp.where(i < n_tiles - 1, nxt_ref[...].astype(F32) * scale_ref[...], 0.0)
        de = jnp.concatenate([de_cur, de_nxt], axis=0).astype(BF16)
        pos = (i * tr + 1 + lax.broadcasted_iota(jnp.int32, (tr + POOL_HALO, 1), 0)).astype(F32)
        for g, w in enumerate(POOL_WINDOWS):
            cols = slice(g * c, (g + 1) * c)
            de_g = de[:, cols]
            dd = lax.dot_general(de_g, w_ref[g], nt_dims, preferred_element_type=F32)
            dw_ref[g] += lax.dot_general(d_ref[:, cols], de_g[:tr], tn_dims, preferred_element_type=F32)
            q = dd / jnp.minimum(pos, float(w))
            win = _window_sums(q, tr + POOL_HALO, forward=True)[g]
            dp_ref[:, cols] = (win[:tr] - dd[:tr]).astype(BF16)

    return pl.pallas_call(
        body, name=name, grid=(n_tiles,),
        in_specs=[_row(tr, pw), pl.BlockSpec((POOL_HALO, pw), lambda i: (jnp.minimum((i + 1) * per, last_halo), 0)),
                  _row(tr, pw), _row(tr, pw), pl.BlockSpec((n_groups, c, c), lambda i: (0, 0, 0)), _vec(pw)],
        out_specs=[_row(tr, pw), pl.BlockSpec((n_groups, c, c), lambda i: (0, 0, 0)), _vec(pw)],
        out_shape=[jax.ShapeDtypeStruct((t_dim, pw), BF16), jax.ShapeDtypeStruct((n_groups, c, c), F32),
                   jax.ShapeDtypeStruct((1, pw), F32)],
        compiler_params=_params("arbitrary"),
    )(dyp, dyp, e, d, w_group, scale.reshape(1, pw))


def _sgu_rows(t_dim, chunk):
    return chunk * _pick(t_dim // chunk, (2, 1))


def _tril(chunk):
    return lax.broadcasted_iota(jnp.int32, (chunk, chunk), 0) >= lax.broadcasted_iota(jnp.int32, (chunk, chunk), 1)


def sgu_fwd(z, gain, w_s, b_s, pw, sw, name):
    t_dim = z.shape[0]
    n_heads, chunk, _ = w_s.shape
    hd = sw // n_heads
    tr = _sgu_rows(t_dim, chunk)
    ju = pw // sw

    def body(u_ref, v_ref, gain_ref, w_ref, bt_ref, sg_ref):
        ug, _ = _gelu(u_ref[...])
        vg, _ = _gelu(v_ref[...])
        r = lax.rsqrt(jnp.mean(vg * vg, axis=-1, keepdims=True) + EPS)
        vn = ((vg * r) * gain_ref[...]).astype(BF16)
        tri = _tril(chunk)
        for h in range(n_heads):
            wm = jnp.where(tri, w_ref[h], 0.0).astype(BF16)
            cols = slice(h * hd, (h + 1) * hd)
            for ch in range(tr // chunk):
                rows = slice(ch * chunk, (ch + 1) * chunk)
                s = jnp.dot(wm, vn[rows, cols], preferred_element_type=F32) + bt_ref[:, h:h + 1]
                sg_ref[rows, cols] = (ug[rows, cols] * s).astype(BF16)

    return pl.pallas_call(
        body, name=name, grid=(t_dim // tr,),
        in_specs=[pl.BlockSpec((tr, sw), lambda i: (i, ju)), pl.BlockSpec((tr, sw), lambda i: (i, ju + 1)), _vec(sw),
                  pl.BlockSpec((n_heads, chunk, chunk), lambda i: (0, 0, 0)),
                  pl.BlockSpec((chunk, n_heads), lambda i: (0, 0))],
        out_specs=_row(tr, sw), out_shape=jax.ShapeDtypeStruct((t_dim, sw), BF16),
        compiler_params=_params("parallel"),
    )(z, z, gain.reshape(1, sw), w_s, b_s.T)


def sgu_bwd(z, dsg, gain, w_s, b_s, pw, sw, name):
    t_dim = z.shape[0]
    n_heads, chunk, _ = w_s.shape
    hd = sw // n_heads
    tr = _sgu_rows(t_dim, chunk)
    ju = pw // sw
    nt_dims = (((1,), (1,)), ((), ()))
    tn_dims = (((0,), (0,)), ((), ()))

    def body(u_ref, v_ref, dsg_ref, gain_ref, w_ref, bt_ref, du_ref, dv_ref, dw_ref, dbt_ref, dgain_ref,
             dvn_ref, dug_ref):
        @pl.when(pl.program_id(0) == 0)
        def _():
            dw_ref[...] = jnp.zeros_like(dw_ref)
            dbt_ref[...] = jnp.zeros_like(dbt_ref)
            dgain_ref[...] = jnp.zeros_like(dgain_ref)

        u = u_ref[...]
        v = v_ref[...]
        ug, tu = _gelu(u)
        vg, tv = _gelu(v)
        r = lax.rsqrt(jnp.mean(vg * vg, axis=-1, keepdims=True) + EPS)
        n = vg * r
        gain_v = gain_ref[...]
        vn = (n * gain_v).astype(BF16)
        dsg_v = dsg_ref[...].astype(F32)
        tri = _tril(chunk)
        for h in range(n_heads):
            wm = jnp.where(tri, w_ref[h], 0.0).astype(BF16)
            cols = slice(h * hd, (h + 1) * hd)
            for ch in range(tr // chunk):
                rows = slice(ch * chunk, (ch + 1) * chunk)
                vn_b = vn[rows, cols]
                s = jnp.dot(wm, vn_b, preferred_element_type=F32) + bt_ref[:, h:h + 1]
                dsg_b = dsg_v[rows, cols]
                dug_ref[rows, cols] = dsg_b * s
                ds = dsg_b * ug[rows, cols]
                ds_b = ds.astype(BF16)
                dw_ref[h] += jnp.where(tri, lax.dot_general(ds_b, vn_b, nt_dims, preferred_element_type=F32), 0.0)
                dbt_ref[:, h:h + 1] += jnp.sum(ds, axis=1, keepdims=True)
                dvn_ref[rows, cols] = lax.dot_general(wm, ds_b, tn_dims, preferred_element_type=F32)
        dvn = dvn_ref[...]
        dgain_ref[...] += jnp.sum(dvn * n, axis=0, keepdims=True)
        dn = dvn * gain_v
        dvg = r * (dn - n * jnp.mean(dn * n, axis=-1, keepdims=True))
        dv_ref[...] = (dvg * _gelu_grad(v, tv)).astype(BF16)
        du_ref[...] = (dug_ref[...] * _gelu_grad(u, tu)).astype(BF16)

    full_w = pl.BlockSpec((n_heads, chunk, chunk), lambda i: (0, 0, 0))
    full_b = pl.BlockSpec((chunk, n_heads), lambda i: (0, 0))
    du, dv, dw, dbt, dgain = pl.pallas_call(
        body, name=name, grid=(t_dim // tr,),
        in_specs=[pl.BlockSpec((tr, sw), lambda i: (i, ju)), pl.BlockSpec((tr, sw), lambda i: (i, ju + 1)),
                  _row(tr, sw), _vec(sw), full_w, full_b],
        out_specs=[_row(tr, sw), _row(tr, sw), full_w, full_b, _vec(sw)],
        out_shape=[jax.ShapeDtypeStruct((t_dim, sw), BF16), jax.ShapeDtypeStruct((t_dim, sw), BF16),
                   jax.ShapeDtypeStruct((n_heads, chunk, chunk), F32), jax.ShapeDtypeStruct((chunk, n_heads), F32),
                   jax.ShapeDtypeStruct((1, sw), F32)],
        scratch_shapes=[pltpu.VMEM((tr, sw), F32), pltpu.VMEM((tr, sw), F32)],
        compiler_params=_params("arbitrary"),
    )(z, z, dsg, gain.reshape(1, sw), w_s, b_s.T)
    return du, dv, dw, dbt.T, dgain.reshape(sw)


HBM_SPEC = pl.BlockSpec(memory_space=pltpu.HBM)
SEM_SPEC = pl.BlockSpec(memory_space=pltpu.SEMAPHORE)
ANY_SPEC = pl.BlockSpec(memory_space=pl.ANY)
DATAFLOW = pltpu.SideEffectType.DATAFLOW_SIDE_EFFECTING


def _hbm(a):
    return pltpu.with_memory_space_constraint(a, pltpu.HBM)


def _slot(ref, axis, k, n):
    idx = [slice(None)] * len(ref.shape)
    idx[axis] = pl.ds(k * n, n)
    return ref.at[tuple(idx)]


def _slot_half(ref, axis, k, half):
    idx = [slice(None)] * len(ref.shape)
    n = ref.shape[axis] // N_CHIPS
    if axis == 0:
        idx[0] = pl.ds(k * n + half * (n // 2), n // 2)
    else:
        idx[axis] = pl.ds(k * n, n)
        idx[0] = pl.ds(half * (ref.shape[0] // 2), ref.shape[0] // 2)
    return ref.at[tuple(idx)]


def _chip_of(k, core):
    return (k // 2, k % 2, core)


def _my_chip():
    return 2 * lax.axis_index("x") + lax.axis_index("y")


class Exchange:
    def __init__(self, gather, axes):
        self.gather, self.axes = gather, axes
        self.per = 1 if gather else 2

    def bufs(self, t, refs):
        return refs[t * self.per:(t + 1) * self.per]

    def branches(self, me, core):
        if self.gather:
            return [(k, h, (me == k) & (core == h)) for k in range(N_CHIPS) for h in range(2)]
        return [(k, None, me == k) for k in range(N_CHIPS)]

    def views(self, t, bufs, src_chip, dst_chip, half):
        ax = self.axes[t]
        if self.gather:
            slot = _slot_half(bufs[0], ax, src_chip, half)
            return slot, slot
        return _slot(bufs[0], ax, dst_chip, bufs[0].shape[ax] // N_CHIPS), bufs[1].at[src_chip]


def exchange_start(bufs, ex, groups, name, deps=()):
    nb, ng = len(bufs), len(groups)

    def body(*refs):
        ins, outs = refs[:nb], refs[nb + len(deps):]
        sems, token = outs[:2 * ng], outs[-1]
        core = lax.axis_index("c")
        me = _my_chip()
        for k, half, mine in ex.branches(me, core):
            @pl.when(mine)
            def _(k=k, half=half):
                for gi, group in enumerate(groups):
                    for ti, t in enumerate(group):
                        for j in range(N_CHIPS):
                            if j != k:
                                src, dst = ex.views(t, ex.bufs(t, ins), k, j, half)
                                pltpu.make_async_remote_copy(
                                    src_ref=src, dst_ref=dst, send_sem=sems[2 * gi].at[ti * N_CHIPS + j],
                                    recv_sem=sems[2 * gi + 1].at[ti * N_CHIPS + k], device_id=_chip_of(j, core),
                                    device_id_type=MESH).start()
        token[...] = jnp.zeros_like(token)

    sem_shapes = []
    for group in groups:
        sem_shapes += [pltpu.SemaphoreType.DMA((len(group) * N_CHIPS,))] * 2
    thru = [pltpu.HBM(a.shape, a.dtype) for a in bufs]
    outs = pl.pallas_call(
        body, name=name, out_shape=sem_shapes + thru + [jax.ShapeDtypeStruct((SUBLANE, LANE), F32)],
        in_specs=[HBM_SPEC] * nb + [ANY_SPEC] * len(deps),
        out_specs=[SEM_SPEC] * (2 * ng) + [HBM_SPEC] * nb + [pl.BlockSpec(memory_space=pltpu.VMEM)],
        input_output_aliases={i: 2 * ng + i for i in range(nb)},
        compiler_params=pltpu.CompilerParams(has_side_effects=DATAFLOW),
    )(*[_hbm(a) for a in bufs], *deps)
    sems = [(outs[2 * gi], outs[2 * gi + 1]) for gi in range(ng)]
    return sems, outs[2 * ng:2 * ng + nb], outs[-1]


def exchange_wait(bufs, tensors, ex, sems, after, name):
    nb = len(bufs)
    send_sems, recv_sems = sems

    def body(*refs):
        ins, send_ref, recv_ref = refs[:nb], refs[nb], refs[nb + 1]
        core = lax.axis_index("c")
        me = _my_chip()
        for k, half, mine in ex.branches(me, core):
            @pl.when(mine)
            def _(k=k, half=half):
                for ti, t in enumerate(tensors):
                    for j in range(N_CHIPS):
                        if j != k:
                            src, _ = ex.views(t, ex.bufs(ti, ins), k, j, half)
                            _, dst = ex.views(t, ex.bufs(ti, ins), j, k, half)
                            copy = pltpu.make_async_remote_copy(
                                src_ref=src, dst_ref=dst, send_sem=send_ref.at[ti * N_CHIPS + j],
                                recv_sem=recv_ref.at[ti * N_CHIPS + j],
                                device_id=_chip_of(j, core), device_id_type=MESH)
                            copy.wait_send()
                            copy.wait_recv()

    return pl.pallas_call(
        body, name=name, out_shape=[pltpu.HBM(a.shape, a.dtype) for a in bufs],
        in_specs=[HBM_SPEC] * nb + [SEM_SPEC, SEM_SPEC] + [ANY_SPEC] * len(after),
        out_specs=[HBM_SPEC] * nb, input_output_aliases={i: i for i in range(nb)},
        compiler_params=pltpu.CompilerParams(has_side_effects=DATAFLOW),
    )(*bufs, send_sems, recv_sems, *after)


def share_halves(bufs, axes, name):
    nb = len(bufs)

    def body(*refs):
        outs = refs[nb:2 * nb]
        send_sems, recv_sems = refs[2 * nb:]
        core = lax.axis_index("c")
        me = _my_chip()
        sibling = (lax.axis_index("x"), lax.axis_index("y"), 1 - core)
        for k in range(N_CHIPS):
            for half in range(2):
                @pl.when((me == k) & (core == half))
                def _(k=k, half=half):
                    def copy(t, j, h):
                        part = _slot_half(outs[t], axes[t], j, h)
                        return pltpu.make_async_remote_copy(
                            src_ref=part, dst_ref=part, send_sem=send_sems.at[t * N_CHIPS + j],
                            recv_sem=recv_sems.at[t * N_CHIPS + j], device_id=sibling, device_id_type=MESH)

                    pairs = [(t, j) for t in range(nb) for j in range(N_CHIPS) if j != k]
                    for t, j in pairs:
                        copy(t, j, half).start()
                    for t, j in pairs:
                        copy(t, j, 1 - half).wait_recv()
                    for t, j in pairs:
                        copy(t, j, half).wait_send()

    return pl.pallas_call(
        body, name=name, out_shape=[jax.ShapeDtypeStruct(a.shape, a.dtype) for a in bufs],
        in_specs=[HBM_SPEC] * nb, out_specs=[HBM_SPEC] * nb, input_output_aliases={i: i for i in range(nb)},
        scratch_shapes=[pltpu.SemaphoreType.DMA((nb * N_CHIPS,)), pltpu.SemaphoreType.DMA((nb * N_CHIPS,))],
        compiler_params=pltpu.CompilerParams(has_side_effects=True),
    )(*bufs)


def swap_with_sibling(arrs, name):
    nt = len(arrs)

    def body(*refs):
        ins, outs = refs[:nt], refs[nt:2 * nt]
        send_sems, recv_sems = refs[2 * nt:]
        sibling = (lax.axis_index("x"), lax.axis_index("y"), 1 - lax.axis_index("c"))
        copies = [pltpu.make_async_remote_copy(src_ref=ins[t], dst_ref=outs[t], send_sem=send_sems.at[t],
                                               recv_sem=recv_sems.at[t], device_id=sibling, device_id_type=MESH)
                  for t in range(nt)]
        for cp in copies:
            cp.start()
        for cp in copies:
            cp.wait()

    return pl.pallas_call(
        body, name=name, out_shape=[jax.ShapeDtypeStruct(a.shape, a.dtype) for a in arrs],
        in_specs=[HBM_SPEC] * nt, out_specs=[HBM_SPEC] * nt,
        scratch_shapes=[pltpu.SemaphoreType.DMA((nt,)), pltpu.SemaphoreType.DMA((nt,))],
        compiler_params=pltpu.CompilerParams(has_side_effects=True),
    )(*arrs)


def place_shard(w_stack, layer, axis, me, name):
    _, a_dim, r, c = w_stack.shape
    tr, tc = _stream_tiles(r, c)
    nr, nc = r // tr, c // tc
    full = (a_dim, r * N_CHIPS, c) if axis == 1 else (a_dim, r, c * N_CHIPS)

    def body(me_ref, w_ref, o_ref):
        o_ref[...] = w_ref[...].astype(BF16)

    def own_map(a, i, j, me_ref):
        return (a, me_ref[0] * nr + i, j) if axis == 1 else (a, i, me_ref[0] * nc + j)

    return pl.pallas_call(
        body, name=name, out_shape=jax.ShapeDtypeStruct(full, BF16),
        grid_spec=pltpu.PrefetchScalarGridSpec(
            num_scalar_prefetch=1, grid=(a_dim, nr, nc),
            in_specs=[pl.BlockSpec((None, None, tr, tc), lambda a, i, j, me_ref: (layer, a, i, j))],
            out_specs=pl.BlockSpec((None, tr, tc), own_map)),
        compiler_params=_params("parallel", "parallel", "parallel"),
    )(me, w_stack)


def sum_pieces(stack, full, axis, me, name):
    _, a_dim, r, c = stack.shape
    tr, tc = _stream_tiles(r, c)
    nr, nc = r // tr, c // tc

    def body(me_ref, own_ref, s1_ref, s2_ref, s3_ref, o_ref):
        acc = own_ref[...].astype(F32)
        for ref in (s1_ref, s2_ref, s3_ref):
            acc = acc + ref[...].astype(F32)
        o_ref[...] = acc

    def own_map(a, i, j, me_ref):
        return (a, me_ref[0] * nr + i, j) if axis == 1 else (a, i, me_ref[0] * nc + j)

    def from_chip(step):
        return pl.BlockSpec((None, None, tr, tc), lambda a, i, j, me_ref: ((me_ref[0] + step) % N_CHIPS, a, i, j))

    return pl.pallas_call(
        body, name=name, out_shape=jax.ShapeDtypeStruct((a_dim, r, c), F32),
        grid_spec=pltpu.PrefetchScalarGridSpec(
            num_scalar_prefetch=1, grid=(a_dim, nr, nc),
            in_specs=[pl.BlockSpec((None, tr, tc), own_map), from_chip(1), from_chip(2), from_chip(3)],
            out_specs=pl.BlockSpec((None, tr, tc), lambda a, i, j, me_ref: (a, i, j))),
        compiler_params=_params("parallel", "parallel", "parallel"),
    )(me, full, stack, stack, stack)


def all_reduce_small(x, name):
    rows, lanes = x.shape

    def body(x_ref, sum_ref, gath_ref, send_sems, recv_sems, local_sem):
        cx, cy, cc = lax.axis_index("x"), lax.axis_index("y"), lax.axis_index("c")
        me, sibling = (cx, cy, cc), (cx, cy, 1 - cc)
        chips = [(1 - cx, cy), (cx, 1 - cy), (1 - cx, 1 - cy)]

        def block(px, py, pc):
            return gath_ref.at[pl.ds(pl.multiple_of((4 * px + 2 * py + pc) * rows, SUBLANE), rows), :]

        def copy(k, blk, to, src=None):
            return pltpu.make_async_remote_copy(
                src_ref=block(*blk) if src is None else src, dst_ref=block(*blk),
                send_sem=send_sems.at[k], recv_sem=recv_sems.at[k], device_id=to, device_id_type=MESH)

        mine = pltpu.make_async_copy(x_ref, block(*me), local_sem)
        mine.start()
        first = [copy(0, me, sibling, src=x_ref)]
        first += [copy(1 + j, me, (*chip, cc), src=x_ref) for j, chip in enumerate(chips)]
        for cp in first:
            cp.start()
        passed = [copy(4 + j, (*chip, cc), sibling) for j, chip in enumerate(chips)]
        for j, chip in enumerate(chips):
            copy(1 + j, (*chip, cc), me).wait_recv()
            passed[j].start()
        copy(0, sibling, me).wait_recv()
        for j, chip in enumerate(chips):
            copy(4 + j, (*chip, 1 - cc), me).wait_recv()
        for cp in first + passed:
            cp.wait_send()
        mine.wait()
        acc = gath_ref[pl.ds(0, rows), :]
        for k in range(1, N_DEV):
            acc = acc + gath_ref[pl.ds(k * rows, rows), :]
        sum_ref[...] = acc

    return pl.pallas_call(
        body, name=name, out_shape=jax.ShapeDtypeStruct((rows, lanes), F32),
        in_specs=[pl.BlockSpec(memory_space=pltpu.VMEM)], out_specs=pl.BlockSpec(memory_space=pltpu.VMEM),
        scratch_shapes=[pltpu.VMEM((N_DEV * rows, lanes), F32), pltpu.SemaphoreType.DMA((7,)),
                        pltpu.SemaphoreType.DMA((7,)), pltpu.SemaphoreType.DMA],
        compiler_params=pltpu.CompilerParams(has_side_effects=True, vmem_limit_bytes=VMEM_LIMIT),
    )(x)


def _adamw(w, g, m, v):
    m = ADAM_B1 * m + (1.0 - ADAM_B1) * g
    v = ADAM_B2 * v + (1.0 - ADAM_B2) * (g * g)
    m_hat = m / (1.0 - ADAM_B1 ** ADAM_STEP)
    v_hat = v / (1.0 - ADAM_B2 ** ADAM_STEP)
    delta = -ADAM_LR * (m_hat / (jnp.sqrt(v_hat) + ADAM_EPS) + ADAM_WD * w)
    return delta, m, v


def adam_big(p_own, p_sib, w, m, v, layer, stacks, name):
    r, c = p_own.shape
    tr, tc = _stream_tiles(r, c)

    def body(p_ref, q_ref, w_ref, m_ref, v_ref, *rest):
        g_out, d_out, m_out, v_out = rest[4:]
        g = p_ref[...] + q_ref[...]
        delta, m_new, v_new = _adamw(w_ref[...], g, m_ref[...], v_ref[...])
        g_out[...] = g
        d_out[...] = delta
        m_out[...] = m_new
        v_out[...] = v_new

    flat = pl.BlockSpec((tr, tc), lambda i, j: (i, j))
    layered = pl.BlockSpec((None, tr, tc), lambda i, j: (layer, i, j))
    anyspace = pl.BlockSpec(memory_space=pl.ANY)
    out = jax.ShapeDtypeStruct(w.shape, F32)
    return pl.pallas_call(
        body, name=name, grid=(r // tr, c // tc),
        in_specs=[flat, flat, layered, layered, layered] + [anyspace] * 4,
        out_specs=[layered] * 4, out_shape=[out] * 4, input_output_aliases={5: 0, 6: 1, 7: 2, 8: 3},
        compiler_params=_params("parallel", "parallel"),
    )(p_own, p_sib, w, m, v, *stacks)


def adam_small(g, w, m, v, name):
    rows, lanes = g.shape
    tr = _pick(rows, (512,) + ROW_TILES)

    def body(g_ref, w_ref, m_ref, v_ref, d_out, m_out, v_out):
        delta, m_new, v_new = _adamw(w_ref[...], g_ref[...], m_ref[...], v_ref[...])
        d_out[...] = delta
        m_out[...] = m_new
        v_out[...] = v_new

    out = jax.ShapeDtypeStruct((rows, lanes), F32)
    return pl.pallas_call(
        body, name=name, grid=(rows // tr,), in_specs=[_row(tr, lanes)] * 4, out_specs=[_row(tr, lanes)] * 3,
        out_shape=[out] * 3, compiler_params=_params("parallel"),
    )(g, w, m, v)


def ffn_fwd(x, g_pre, w_up, w_down, g_post, tag, deps=()):
    h = rms_fwd(x, g_pre, name=f"{tag}_rms_fwd", deps=deps)
    gate, up, a = matmul_swiglu(h, w_up, name=f"{tag}_up")
    f = matmul(a, w_down, out_dtype=F32, name=f"{tag}_down")
    y = res_rms_fwd(x, f, g_post, MACARON_WEIGHT, name=f"{tag}_res_fwd")
    return y, (x, h, gate, up, a, f)


def ffn_bwd(dy, saved, g_pre, w_up, w_down, g_post, tag, deps=()):
    x, h, gate, up, a, f = saved
    df, dg_post = rms_bwd(f, g_post, dy, MACARON_WEIGHT, None, BF16, name=f"{tag}_res_bwd", deps=deps)
    da = matmul(df, w_down, tb=True, out_dtype=BF16, name=f"{tag}_down_dx")
    dw_down = matmul(a, df, ta=True, out_dtype=BF16, name=f"{tag}_down_dw")
    dz = swiglu_bwd(gate, up, da, name=f"{tag}_swiglu_bwd")
    dh = matmul(dz, w_up, tb=True, out_dtype=F32, name=f"{tag}_up_dx")
    dw_up = matmul(h, dz, ta=True, out_dtype=BF16, name=f"{tag}_up_dw")
    dx, dg_pre = rms_bwd(x, g_pre, dh, 1.0, dy, F32, name=f"{tag}_rms_bwd")
    return dx, dw_up, dw_down, dg_pre, dg_post


def mixer_fwd(x, wt, sm, dims):
    pw, sw, d = dims
    h = rms_fwd(x, sm['g_mix_pre'], name="mix_rms_fwd")
    z = matmul(h, wt['w_in'], out_dtype=F32, name="mix_in")
    dd, e, yp = pool_fwd(z, wt['pool_group_w'], sm['pool_scale'], pw, name="mix_pool_fwd")
    ya = matmul(yp, wt['w_pool_out'], out_dtype=BF16, name="mix_pool_out")
    sg = sgu_fwd(z, sm['sgu_v_gain'], sm['sgu_w_s'], sm['sgu_b_s'], pw, sw, name="mix_sgu_fwd")
    yb = matmul(sg, wt['w_sgu_out'], out_dtype=BF16, name="mix_sgu_out")
    m = gate_fwd(z, ya, yb, pw + 2 * sw, pw + 2 * sw + d, name="mix_gate_fwd")
    o = matmul(m, wt['w_out'], out_dtype=F32, name="mix_out")
    y = res_rms_fwd(x, o, sm['g_mix_post'], 1.0, name="mix_res_fwd")
    return y, (x, h, z, dd, e, yp, sg, ya, yb, m, o)


def mixer_bwd(dy, saved, wt, sm, dims, deps=()):
    pw, sw, d = dims
    x, h, z, dd, e, yp, sg, ya, yb, m, o = saved
    grads = {}
    do, grads['g_mix_post'] = rms_bwd(o, sm['g_mix_post'], dy, 1.0, None, BF16, name="mix_res_bwd", deps=deps)
    dm = matmul(do, wt['w_out'], tb=True, out_dtype=BF16, name="mix_out_dx")
    grads['w_out'] = matmul(m, do, ta=True, out_dtype=BF16, name="mix_out_dw")
    dya, dyb, dga, dgb = gate_bwd(z, ya, yb, dm, pw + 2 * sw, pw + 2 * sw + d, name="mix_gate_bwd")
    dyp = matmul(dya, wt['w_pool_out'], tb=True, out_dtype=BF16, name="mix_pool_out_dx")
    grads['w_pool_out'] = matmul(yp, dya, ta=True, out_dtype=BF16, name="mix_pool_out_dw")
    dp, dwg, dscale = pool_bwd(dyp, e, dd, wt['pool_group_w'], sm['pool_scale'], name="mix_pool_bwd")
    grads['pool_group_w'] = dwg.astype(BF16)
    grads['pool_scale'] = dscale.reshape(pw)
    dsg = matmul(dyb, wt['w_sgu_out'], tb=True, out_dtype=BF16, name="mix_sgu_out_dx")
    grads['w_sgu_out'] = matmul(sg, dyb, ta=True, out_dtype=BF16, name="mix_sgu_out_dw")
    du, dv, grads['sgu_w_s'], grads['sgu_b_s'], grads['sgu_v_gain'] = sgu_bwd(
        z, dsg, sm['sgu_v_gain'], sm['sgu_w_s'], sm['sgu_b_s'], pw, sw, name="mix_sgu_bwd")
    dz = jnp.concatenate([dp, du, dv, dga, dgb], axis=1)
    dh = matmul(dz, wt['w_in'], tb=True, out_dtype=F32, name="mix_in_dx")
    grads['w_in'] = matmul(h, dz, ta=True, out_dtype=BF16, name="mix_in_dw")
    dx, grads['g_mix_pre'] = rms_bwd(x, sm['g_mix_pre'], dh, 1.0, dy, F32, name="mix_rms_bwd")
    return dx, grads


def _as_rows(a):
    return a.reshape(a.shape[0], -1, a.shape[-1])


def kernel(x, g_ffn1_pre, w_ffn1_up, w_ffn1_down, g_ffn1_post, g_mix_pre, w_in, pool_group_w, pool_scale, w_pool_out, sgu_v_gain, sgu_w_s, sgu_b_s, w_sgu_out, w_out, g_mix_post, g_ffn2_pre, w_ffn2_up, w_ffn2_down, g_ffn2_post, loss_target, m_g_ffn1_pre, m_w_ffn1_up, m_w_ffn1_down, m_g_ffn1_post, m_g_mix_pre, m_w_in, m_pool_group_w, m_pool_scale, m_w_pool_out, m_sgu_v_gain, m_sgu_w_s, m_sgu_b_s, m_w_sgu_out, m_w_out, m_g_mix_post, m_g_ffn2_pre, m_w_ffn2_up, m_w_ffn2_down, m_g_ffn2_post, v_g_ffn1_pre, v_w_ffn1_up, v_w_ffn1_down, v_g_ffn1_post, v_g_mix_pre, v_w_in, v_pool_group_w, v_pool_scale, v_w_pool_out, v_sgu_v_gain, v_sgu_w_s, v_sgu_b_s, v_w_sgu_out, v_w_out, v_g_mix_post, v_g_ffn2_pre, v_w_ffn2_up, v_w_ffn2_down, v_g_ffn2_post):
    w = dict(zip(WEIGHTS, (g_ffn1_pre, w_ffn1_up, w_ffn1_down, g_ffn1_post, g_mix_pre, w_in, pool_group_w, pool_scale, w_pool_out, sgu_v_gain, sgu_w_s, sgu_b_s, w_sgu_out, w_out, g_mix_post, g_ffn2_pre, w_ffn2_up, w_ffn2_down, g_ffn2_post)))
    mom = dict(zip(WEIGHTS, (m_g_ffn1_pre, m_w_ffn1_up, m_w_ffn1_down, m_g_ffn1_post, m_g_mix_pre, m_w_in, m_pool_group_w, m_pool_scale, m_w_pool_out, m_sgu_v_gain, m_sgu_w_s, m_sgu_b_s, m_w_sgu_out, m_w_out, m_g_mix_post, m_g_ffn2_pre, m_w_ffn2_up, m_w_ffn2_down, m_g_ffn2_post)))
    var = dict(zip(WEIGHTS, (v_g_ffn1_pre, v_w_ffn1_up, v_w_ffn1_down, v_g_ffn1_post, v_g_mix_pre, v_w_in, v_pool_group_w, v_pool_scale, v_w_pool_out, v_sgu_v_gain, v_sgu_w_s, v_sgu_b_s, v_w_sgu_out, v_w_out, v_g_mix_post, v_g_ffn2_pre, v_w_ffn2_up, v_w_ffn2_down, v_g_ffn2_post)))
    depth = g_ffn1_pre.shape[0]
    d = x.shape[-1]
    pw = pool_scale.shape[-1]
    sw = sgu_v_gain.shape[-1]
    dims = (pw, sw, d)
    axes = [BIG_AXIS[n] for n in BIG]

    me = 2 * lax.axis_index("x") + lax.axis_index("y")
    gather = Exchange(True, axes)
    group_ids = [[BIG.index(n) for n in group] for group in BLOCK_WEIGHTS]

    me1 = me.reshape(1).astype(jnp.int32)

    def start_gather(l, deps=()):
        lands = []
        for n, ax in zip(BIG, axes):
            shards = w[n] if w[n].ndim == 4 else w[n][:, None]
            full = place_shard(shards, l, ax + 4 - w[n].ndim, me1, name=f"place_{n}")
            lands.append(full if w[n].ndim == 4 else full[0])
        return exchange_start(lands, gather, group_ids, name=f"gather_start_l{l}", deps=deps)

    def finish_gather(started, l, gi, after):
        sems, lands, _ = started
        ids = group_ids[gi]
        got = exchange_wait([lands[t] for t in ids], ids, gather, sems[gi], (after,), name=f"gather_wait_l{l}_b{gi}")
        got = share_halves(got, [axes[t] for t in ids], name=f"share_halves_b{gi}")
        return dict(zip(BLOCK_WEIGHTS[gi], got))

    small = [{n: w[n][l] for n in SMALL} for l in range(depth)]

    act = x[0]
    saved, full = [], []
    gathers = []
    for l in range(depth):
        gathers.append(start_gather(l, (gathers[-1][2],) if gathers else ()))
    for l in range(depth):
        started = gathers[l]
        deps = (gathers[-1][2],) if l == 0 else ()
        sm, wt = small[l], {}
        wt.update(finish_gather(started, l, 0, act))
        act, s1 = ffn_fwd(act, sm['g_ffn1_pre'], wt['w_ffn1_up'], wt['w_ffn1_down'], sm['g_ffn1_post'], "ffn1", deps)
        wt.update(finish_gather(started, l, 1, act))
        act, s2 = mixer_fwd(act, wt, sm, dims)
        wt.update(finish_gather(started, l, 2, act))
        act, s3 = ffn_fwd(act, sm['g_ffn2_pre'], wt['w_ffn2_up'], wt['w_ffn2_down'], sm['g_ffn2_post'], "ffn2")
        saved.append((s1, s2, s3))
        full.append(wt)
    dact, sq_sum = loss_grad(act, loss_target[0], name="loss_grad")
    loss = lax.psum(0.5 * sq_sum / d, ("x", "y", "c"))

    rows3 = {n: (_as_rows(w[n]), _as_rows(mom[n]), _as_rows(var[n])) for n in BIG}
    stacks = {n: tuple(lax.empty(rows3[n][0].shape, F32) for _ in range(4)) for n in BIG}
    small_grads = [{} for _ in range(depth)]

    def start_scatter(names, grads, l, block):
        ex = Exchange(False, [BIG_AXIS[n] for n in names])
        bufs = []
        for n in names:
            piece = list(grads[n].shape)
            piece[BIG_AXIS[n]] //= N_CHIPS
            bufs += [grads[n], lax.empty((N_CHIPS, *piece), BF16)]
        return (names, ex, l, block) + exchange_start(bufs, ex, [list(range(len(names)))],
                                                      name=f"scatter_start_l{l}_b{block}")

    def finish_scatter(pending, after):
        names, ex, l, block, sems, bufs, _ = pending
        got = exchange_wait(bufs, list(range(len(names))), ex, sems[0], after, name=f"scatter_wait_l{l}_b{block}")
        plane = []
        for ti, n in enumerate(names):
            full_g, stack = got[2 * ti], got[2 * ti + 1]
            if full_g.ndim == 2:
                full_g, stack = full_g[None], stack[:, None]
            p = sum_pieces(stack, full_g, BIG_AXIS[n] + 3 - got[2 * ti].ndim, me1, name=f"sum_{n}")
            plane.append(p.reshape(-1, p.shape[-1]))
        other = swap_with_sibling(plane, name=f"swap_plane_sums_{len(names)}")
        for n, p_own, p_sib in zip(names, plane, other):
            stacks[n] = tuple(adam_big(p_own, p_sib, *rows3[n], l, stacks[n], name=f"adam_{n}"))
        return stacks[names[-1]][0]

    pending, done = None, ()
    for l in reversed(range(depth)):
        wt, sm = full[l], small[l]
        s1, s2, s3 = saved[l]
        for block in (2, 1, 0):
            deps = (pending[-1],) if pending is not None else ()
            g = {}
            if block == 2:
                dnew, g['w_ffn2_up'], g['w_ffn2_down'], g['g_ffn2_pre'], g['g_ffn2_post'] = ffn_bwd(
                    dact, s3, sm['g_ffn2_pre'], wt['w_ffn2_up'], wt['w_ffn2_down'], sm['g_ffn2_post'], "ffn2", deps)
            elif block == 1:
                dnew, g = mixer_bwd(dact, s2, wt, sm, dims, deps)
            else:
                dnew, g['w_ffn1_up'], g['w_ffn1_down'], g['g_ffn1_pre'], g['g_ffn1_post'] = ffn_bwd(
                    dact, s1, sm['g_ffn1_pre'], wt['w_ffn1_up'], wt['w_ffn1_down'], sm['g_ffn1_post'], "ffn1", deps)
            small_grads[l].update({n: g[n] for n in g if n in SMALL})
            if pending is not None:
                done = (finish_scatter(pending, (dnew,) + done),)
            pending = start_scatter(BLOCK_WEIGHTS[block], g, l, block)
            dact = dnew
    finish_scatter(pending, (dact,) + done)

    def flat(tree):
        v = jnp.concatenate([tree[n].reshape(-1).astype(F32) for n in SMALL])
        pad = (-v.shape[0]) % (SUBLANE * LANE)
        return jnp.pad(v, (0, pad)).reshape(-1, LANE)

    g_small = all_reduce_small(flat({n: jnp.stack([small_grads[l][n] for l in range(depth)]) for n in SMALL}),
                               name="all_reduce_small")
    d_small, m_small, v_small = adam_small(g_small, flat(w), flat(mom), flat(var), name="adam_small")

    def unflat(block):
        v, out, at = block.reshape(-1), {}, 0
        for n in SMALL:
            out[n] = v[at:at + w[n].size].reshape(w[n].shape)
            at += w[n].size
        return out

    result = [{}, {}, {}, {}]
    for tree, block in zip(result, (g_small, d_small, m_small, v_small)):
        tree.update(unflat(block))
    for n in BIG:
        for tree, stack in zip(result, stacks[n]):
            tree[n] = stack.reshape(w[n].shape)
    return (loss, dact.reshape(x.shape), *[tree[n] for tree in result for n in WEIGHTS])
```

```python
import math

import jax
import jax.numpy as jnp
from jax import lax
from jax.experimental import pallas as pl
from jax.experimental.pallas import tpu as pltpu

F32 = jnp.float32
BF16 = jnp.bfloat16
MESH = pl.DeviceIdType.MESH

EPS = 1e-6
MACARON_WEIGHT = 0.5
POOL_WINDOWS = (2, 4, 8, 16)
POOL_HALO = 16
ADAM_LR = 0.001
ADAM_B1 = 0.9
ADAM_B2 = 0.999
ADAM_EPS = 1e-08
ADAM_WD = 0.01
ADAM_STEP = 10
GELU_K = math.sqrt(2.0 / math.pi)
GELU_C = 0.044715

N_CHIPS = 4
N_DEV = 8
V7X_VMEM_BYTES = 64 * 1024 * 1024
VMEM_LIMIT = (V7X_VMEM_BYTES * 3) // 4
LANE = 128
SUBLANE = 8

WEIGHTS = ['g_ffn1_pre', 'w_ffn1_up', 'w_ffn1_down', 'g_ffn1_post', 'g_mix_pre', 'w_in', 'pool_group_w',
           'pool_scale', 'w_pool_out', 'sgu_v_gain', 'sgu_w_s', 'sgu_b_s', 'w_sgu_out', 'w_out', 'g_mix_post',
           'g_ffn2_pre', 'w_ffn2_up', 'w_ffn2_down', 'g_ffn2_post']
BIG_AXIS = {'w_ffn1_up': 1, 'w_ffn1_down': 0, 'w_in': 1, 'pool_group_w': 1, 'w_pool_out': 1, 'w_sgu_out': 1,
            'w_out': 0, 'w_ffn2_up': 1, 'w_ffn2_down': 0}
BIG = list(BIG_AXIS)
BLOCK_WEIGHTS = [['w_ffn1_up', 'w_ffn1_down'], ['w_in', 'pool_group_w', 'w_pool_out', 'w_sgu_out', 'w_out'],
                 ['w_ffn2_up', 'w_ffn2_down']]
SMALL = [n for n in WEIGHTS if n not in BIG_AXIS]


def _pick(dim, cands):
    for c in cands:
        if dim % c == 0:
            return c
    return dim


STREAM_COL_TILES = (1024, 1408, 896, 512, 256, 128)
STREAM_ROW_TILES = (512, 352, 256, 128, 64, 32, 16, 8)
STREAM_BLOCK_ELEMS = 384 * 1024


def _stream_tiles(r, c):
    tc = _pick(c, STREAM_COL_TILES)
    for tr in STREAM_ROW_TILES:
        if r % tr == 0 and tr * tc <= STREAM_BLOCK_ELEMS:
            return tr, tc
    return r, tc


def _params(*sem):
    return pltpu.CompilerParams(dimension_semantics=sem if sem else None, vmem_limit_bytes=VMEM_LIMIT)


def _sigmoid(x):
    return 1.0 / (1.0 + jnp.exp(-x))


def _gelu(x):
    t = jnp.tanh(GELU_K * (x + GELU_C * (x * x * x)))
    return x * (0.5 * (1.0 + t)), t


def _gelu_grad(x, t):
    return 0.5 * (1.0 + t) + (0.5 * x) * (1.0 - t * t) * (GELU_K * (1.0 + (3.0 * GELU_C) * (x * x)))


MATMUL_MN_TILES = (1024, 1408, 512, 256, 128)
MATMUL_K_TILES = (2048, 2816, 1024, 512, 256, 128)


def matmul(a, b, *, ta=False, tb=False, out_dtype=BF16, name):
    m_dim, k_dim = (a.shape[1], a.shape[0]) if ta else a.shape
    n_dim = b.shape[0] if tb else b.shape[1]
    tm = _pick(m_dim, MATMUL_MN_TILES)
    tn = _pick(n_dim, MATMUL_MN_TILES)
    tk = _pick(k_dim, MATMUL_K_TILES)
    nk = k_dim // tk
    dims = (((0 if ta else 1,), (1 if tb else 0,)), ((), ()))

    def body(a_ref, b_ref, o_ref, acc_ref):
        k = pl.program_id(2)

        @pl.when(k == 0)
        def _():
            acc_ref[...] = jnp.zeros_like(acc_ref)

        acc_ref[...] += lax.dot_general(a_ref[...], b_ref[...], dims, preferred_element_type=F32)

        @pl.when(k == nk - 1)
        def _():
            o_ref[...] = acc_ref[...].astype(o_ref.dtype)

    a_spec = pl.BlockSpec((tk, tm), lambda i, j, k: (k, i)) if ta else pl.BlockSpec((tm, tk), lambda i, j, k: (i, k))
    b_spec = pl.BlockSpec((tn, tk), lambda i, j, k: (j, k)) if tb else pl.BlockSpec((tk, tn), lambda i, j, k: (k, j))
    return pl.pallas_call(
        body, name=name, grid=(m_dim // tm, n_dim // tn, nk),
        in_specs=[a_spec, b_spec], out_specs=pl.BlockSpec((tm, tn), lambda i, j, k: (i, j)),
        out_shape=jax.ShapeDtypeStruct((m_dim, n_dim), out_dtype),
        scratch_shapes=[pltpu.VMEM((tm, tn), F32)],
        compiler_params=_params("parallel", "parallel", "arbitrary"),
    )(a, b)


def matmul_swiglu(h, w_up, name):
    m_dim, k_dim = h.shape
    f = w_up.shape[1] // 2
    tm = _pick(m_dim, MATMUL_MN_TILES)
    tn = _pick(f, (512, 256, 128))
    tk = _pick(k_dim, MATMUL_K_TILES)
    nk, nf = k_dim // tk, f // tn

    def body(h_ref, wg_ref, wu_ref, g_ref, u_ref, a_ref, accg_ref, accu_ref):
        k = pl.program_id(2)

        @pl.when(k == 0)
        def _():
            accg_ref[...] = jnp.zeros_like(accg_ref)
            accu_ref[...] = jnp.zeros_like(accu_ref)

        hv = h_ref[...]
        accg_ref[...] += jnp.dot(hv, wg_ref[...], preferred_element_type=F32)
        accu_ref[...] += jnp.dot(hv, wu_ref[...], preferred_element_type=F32)

        @pl.when(k == nk - 1)
        def _():
            g, u = accg_ref[...], accu_ref[...]
            g_ref[...] = g.astype(BF16)
            u_ref[...] = u.astype(BF16)
            a_ref[...] = (g * _sigmoid(g) * u).astype(BF16)

    out = jax.ShapeDtypeStruct((m_dim, f), BF16)
    blk = pl.BlockSpec((tm, tn), lambda i, j, k: (i, j))
    return pl.pallas_call(
        body, name=name, grid=(m_dim // tm, nf, nk),
        in_specs=[pl.BlockSpec((tm, tk), lambda i, j, k: (i, k)), pl.BlockSpec((tk, tn), lambda i, j, k: (k, j)),
                  pl.BlockSpec((tk, tn), lambda i, j, k: (k, j + nf))],
        out_specs=[blk, blk, blk], out_shape=[out, out, out],
        scratch_shapes=[pltpu.VMEM((tm, tn), F32), pltpu.VMEM((tm, tn), F32)],
        compiler_params=_params("parallel", "parallel", "arbitrary"),
    )(h, w_up, w_up)


ROW_TILES = (256, 128, 64, 32, 16, 8)


def _row(tr, width):
    return pl.BlockSpec((tr, width), lambda i: (i, 0))


def _vec(width):
    return pl.BlockSpec((1, width), lambda i: (0, 0))


def rms_fwd(x, g, name, deps=()):
    t_dim, d = x.shape
    tr = _pick(t_dim, ROW_TILES)

    def body(x_ref, g_ref, *rest):
        o_ref = rest[-1]
        xv = x_ref[...]
        r = lax.rsqrt(jnp.mean(xv * xv, axis=-1, keepdims=True) + EPS)
        o_ref[...] = ((xv * r) * g_ref[...]).astype(o_ref.dtype)

    return pl.pallas_call(
        body, name=name, grid=(t_dim // tr,), in_specs=[_row(tr, d), _vec(d)] + [ANY_SPEC] * len(deps),
        out_specs=_row(tr, d), out_shape=jax.ShapeDtypeStruct((t_dim, d), BF16), compiler_params=_params("parallel"),
    )(x, g.reshape(1, d), *deps)


def res_rms_fwd(x, f, g, weight, g_next, name):
    t_dim, d = x.shape
    tr = _pick(t_dim, ROW_TILES)
    chained = g_next is not None

    def body(x_ref, f_ref, g_ref, *rest):
        fv = f_ref[...]
        r = lax.rsqrt(jnp.mean(fv * fv, axis=-1, keepdims=True) + EPS)
        y = x_ref[...] + weight * ((fv * r) * g_ref[...])
        if chained:
            gn_ref, o_ref, h_ref = rest
            rn = lax.rsqrt(jnp.mean(y * y, axis=-1, keepdims=True) + EPS)
            h_ref[...] = ((y * rn) * gn_ref[...]).astype(BF16)
        else:
            (o_ref,) = rest
        o_ref[...] = y

    outs = pl.pallas_call(
        body, name=name, grid=(t_dim // tr,),
        in_specs=[_row(tr, d), _row(tr, d), _vec(d)] + ([_vec(d)] if chained else []),
        out_specs=[_row(tr, d)] + ([_row(tr, d)] if chained else []),
        out_shape=[jax.ShapeDtypeStruct((t_dim, d), F32)] + ([jax.ShapeDtypeStruct((t_dim, d), BF16)] if chained else []),
        compiler_params=_params("parallel"),
    )(x, f, g.reshape(1, d), *([g_next.reshape(1, d)] if chained else []))
    return (outs[0], outs[1]) if chained else (outs[0], None)


def rms_bwd(f, g, dy, weight, resid, out_dtype, name, deps=()):
    t_dim, d = f.shape
    tr = _pick(t_dim, ROW_TILES)
    has_resid = resid is not None

    def body(*refs):
        o_ref, dg_ref = refs[-2:]
        if has_resid:
            f_ref, g_ref, dy_ref, res_ref = refs[:4]
        else:
            f_ref, g_ref, dy_ref = refs[:3]

        @pl.when(pl.program_id(0) == 0)
        def _():
            dg_ref[...] = jnp.zeros_like(dg_ref)

        fv = f_ref[...]
        r = lax.rsqrt(jnp.mean(fv * fv, axis=-1, keepdims=True) + EPS)
        n = fv * r
        dyw = dy_ref[...] * weight
        dn = dyw * g_ref[...]
        df = r * (dn - n * jnp.mean(dn * n, axis=-1, keepdims=True))
        if has_resid:
            df = df + res_ref[...]
        o_ref[...] = df.astype(o_ref.dtype)
        dg_ref[...] += jnp.sum(dyw * n, axis=0, keepdims=True)

    ins = [f, g.reshape(1, d), dy] + ([resid] if has_resid else []) + list(deps)
    in_specs = [_row(tr, d), _vec(d), _row(tr, d)] + ([_row(tr, d)] if has_resid else []) + [ANY_SPEC] * len(deps)
    out, dg = pl.pallas_call(
        body, name=name, grid=(t_dim // tr,), in_specs=in_specs, out_specs=[_row(tr, d), _vec(d)],
        out_shape=[jax.ShapeDtypeStruct((t_dim, d), out_dtype), jax.ShapeDtypeStruct((1, d), F32)],
        compiler_params=_params("arbitrary"),
    )(*ins)
    return out, dg.reshape(d)


def loss_grad(y, target, name):
    t_dim, d = y.shape
    tr = _pick(t_dim, ROW_TILES)
    inv_d = 1.0 / d

    def body(y_ref, t_ref, dy_ref, s_ref):
        @pl.when(pl.program_id(0) == 0)
        def _():
            s_ref[...] = jnp.zeros_like(s_ref)

        e = y_ref[...] - t_ref[...]
        dy_ref[...] = e * inv_d
        s_ref[...] += jnp.sum(e * e)

    dy, s = pl.pallas_call(
        body, name=name, grid=(t_dim // tr,), in_specs=[_row(tr, d), _row(tr, d)],
        out_specs=[_row(tr, d), pl.BlockSpec((SUBLANE, LANE), lambda i: (0, 0))],
        out_shape=[jax.ShapeDtypeStruct((t_dim, d), F32), jax.ShapeDtypeStruct((SUBLANE, LANE), F32)],
        compiler_params=_params("arbitrary"),
    )(y, target)
    return dy, s[0, 0]


def swiglu_bwd(gate, up, da, name):
    t_dim, f = gate.shape
    tr = _pick(t_dim, ROW_TILES)
    tc = _pick(f, (512, 256, 128))

    def body(g_ref, u_ref, da_ref, o_ref):
        for c in range(f // tc):
            lo = c * tc
            g = g_ref[:, lo:lo + tc].astype(F32)
            u = u_ref[:, lo:lo + tc].astype(F32)
            da = da_ref[:, lo:lo + tc].astype(F32)
            s = _sigmoid(g)
            o_ref[:, lo:lo + tc] = (da * u * (s * (1.0 + g * (1.0 - s)))).astype(o_ref.dtype)
            o_ref[:, f + lo:f + lo + tc] = (da * (g * s)).astype(o_ref.dtype)

    return pl.pallas_call(
        body, name=name, grid=(t_dim // tr,), in_specs=[_row(tr, f)] * 3, out_specs=_row(tr, 2 * f),
        out_shape=jax.ShapeDtypeStruct((t_dim, 2 * f), BF16), compiler_params=_params("parallel"),
    )(gate, up, da)


def gate_fwd(z, ya, yb, off_a, off_b, name):
    t_dim, d = ya.shape
    tr = _pick(t_dim, (512,) + ROW_TILES)
    tc = math.gcd(math.gcd(off_a, off_b), _pick(d, (512, 256, 128)))
    ja, jb = off_a // tc, off_b // tc

    def body(ga_ref, gb_ref, ya_ref, yb_ref, m_ref):
        m = _sigmoid(ga_ref[...]) * ya_ref[...].astype(F32) + _sigmoid(gb_ref[...]) * yb_ref[...].astype(F32)
        m_ref[...] = m.astype(m_ref.dtype)

    blk = pl.BlockSpec((tr, tc), lambda i, j: (i, j))
    return pl.pallas_call(
        body, name=name, grid=(t_dim // tr, d // tc),
        in_specs=[pl.BlockSpec((tr, tc), lambda i, j: (i, j + ja)), pl.BlockSpec((tr, tc), lambda i, j: (i, j + jb)),
                  blk, blk],
        out_specs=blk, out_shape=jax.ShapeDtypeStruct((t_dim, d), BF16),
        compiler_params=_params("parallel", "parallel"),
    )(z, z, ya, yb)


def gate_bwd(z, ya, yb, dm, off_a, off_b, name):
    t_dim, d = ya.shape
    tr = _pick(t_dim, (512,) + ROW_TILES)
    tc = math.gcd(math.gcd(off_a, off_b), _pick(d, (512, 256, 128)))
    ja, jb = off_a // tc, off_b // tc

    def body(ga_ref, gb_ref, ya_ref, yb_ref, dm_ref, dya_ref, dyb_ref, dga_ref, dgb_ref):
        dm = dm_ref[...].astype(F32)
        sa = _sigmoid(ga_ref[...])
        sb = _sigmoid(gb_ref[...])
        dya_ref[...] = (dm * sa).astype(BF16)
        dyb_ref[...] = (dm * sb).astype(BF16)
        dga_ref[...] = (dm * ya_ref[...].astype(F32) * (sa * (1.0 - sa))).astype(BF16)
        dgb_ref[...] = (dm * yb_ref[...].astype(F32) * (sb * (1.0 - sb))).astype(BF16)

    blk = pl.BlockSpec((tr, tc), lambda i, j: (i, j))
    out = jax.ShapeDtypeStruct((t_dim, d), BF16)
    return pl.pallas_call(
        body, name=name, grid=(t_dim // tr, d // tc),
        in_specs=[pl.BlockSpec((tr, tc), lambda i, j: (i, j + ja)), pl.BlockSpec((tr, tc), lambda i, j: (i, j + jb)),
                  blk, blk, blk],
        out_specs=[blk] * 4, out_shape=[out] * 4, compiler_params=_params("parallel", "parallel"),
    )(z, z, ya, yb, dm)


def _window_sums(e, n_rows, forward):
    def shifted(v, k):
        return pltpu.roll(v, (n_rows - k) if forward else k, 0)

    s2 = e + shifted(e, 1)
    s4 = s2 + shifted(s2, 2)
    s8 = s4 + shifted(s4, 4)
    s16 = s8 + shifted(s8, 8)
    return (s2, s4, s8, s16)


def _pool_rows(t_dim):
    return _pick(t_dim, (256, 128, 64, 32, 16))


def pool_fwd(z, w_group, scale, pw, name):
    t_dim = z.shape[0]
    n_groups, c, _ = w_group.shape
    tr = _pool_rows(t_dim)
    per = tr // POOL_HALO

    def body(cur_ref, prev_ref, w_ref, scale_ref, d_ref, e_ref, yp_ref):
        i = pl.program_id(0)
        cur = cur_ref[...]
        prev = jnp.where(i > 0, prev_ref[...], 0.0)
        ext = jnp.concatenate([prev, cur], axis=0)
        sums = _window_sums(ext, tr + POOL_HALO, forward=False)
        pos = (i * tr + 1 + lax.broadcasted_iota(jnp.int32, (tr, 1), 0)).astype(F32)
        for g, w in enumerate(POOL_WINDOWS):
            cols = slice(g * c, (g + 1) * c)
            cnt = jnp.minimum(pos, float(w))
            d = (sums[g][POOL_HALO:, cols] / cnt - cur[:, cols]).astype(BF16)
            e = jnp.dot(d, w_ref[g], preferred_element_type=F32)
            d_ref[:, cols] = d
            e_ref[:, cols] = e.astype(BF16)
            yp_ref[:, cols] = (e * scale_ref[:, cols]).astype(BF16)

    out = jax.ShapeDtypeStruct((t_dim, pw), BF16)
    return pl.pallas_call(
        body, name=name, grid=(t_dim // tr,),
        in_specs=[_row(tr, pw), pl.BlockSpec((POOL_HALO, pw), lambda i: (jnp.maximum(i * per - 1, 0), 0)),
                  pl.BlockSpec((n_groups, c, c), lambda i: (0, 0, 0)), _vec(pw)],
        out_specs=[_row(tr, pw)] * 3, out_shape=[out] * 3, compiler_params=_params("parallel"),
    )(z, z, w_group, scale.reshape(1, pw))


def pool_bwd(dyp, e, d, w_group, scale, name):
    t_dim, pw = dyp.shape
    n_groups, c, _ = w_group.shape
    tr = _pool_rows(t_dim)
    per = tr // POOL_HALO
    n_tiles = t_dim // tr
    last_halo = t_dim // POOL_HALO - 1
    nt_dims = (((1,), (1,)), ((), ()))
    tn_dims = (((0,), (0,)), ((), ()))

    def body(dyp_ref, nxt_ref, e_ref, d_ref, w_ref, scale_ref, dp_ref, dw_ref, dscale_ref):
        i = pl.program_id(0)

        @pl.when(i == 0)
        def _():
            dw_ref[...] = jnp.zeros_like(dw_ref)
            dscale_ref[...] = jnp.zeros_like(dscale_ref)

        dyp_v = dyp_ref[...].astype(F32)
        dscale_ref[...] += jnp.sum(dyp_v * e_ref[...].astype(F32), axis=0, keepdims=True)
        de_cur = dyp_v * scale_ref[...]
        de_nxt = jnp.where(i < n_tiles - 1, nxt_ref[...].astype(F32) * scale_ref[...], 0.0)
        de = jnp.concatenate([de_cur, de_nxt], axis=0).astype(BF16)
        pos = (i * tr + 1 + lax.broadcasted_iota(jnp.int32, (tr + POOL_HALO, 1), 0)).astype(F32)
        for g, w in enumerate(POOL_WINDOWS):
            cols = slice(g * c, (g + 1) * c)
            de_g = de[:, cols]
            dd = lax.dot_general(de_g, w_ref[g], nt_dims, preferred_element_type=F32)
            dw_ref[g] += lax.dot_general(d_ref[:, cols], de_g[:tr], tn_dims, preferred_element_type=F32)
            q = dd / jnp.minimum(pos, float(w))
            win = _window_sums(q, tr + POOL_HALO, forward=True)[g]
            dp_ref[:, cols] = (win[:tr] - dd[:tr]).astype(BF16)

    return pl.pallas_call(
        body, name=name, grid=(n_tiles,),
        in_specs=[_row(tr, pw), pl.BlockSpec((POOL_HALO, pw), lambda i: (jnp.minimum((i + 1) * per, last_halo), 0)),
                  _row(tr, pw), _row(tr, pw), pl.BlockSpec((n_groups, c, c), lambda i: (0, 0, 0)), _vec(pw)],
        out_specs=[_row(tr, pw), pl.BlockSpec((n_groups, c, c), lambda i: (0, 0, 0)), _vec(pw)],
        out_shape=[jax.ShapeDtypeStruct((t_dim, pw), BF16), jax.ShapeDtypeStruct((n_groups, c, c), F32),
                   jax.ShapeDtypeStruct((1, pw), F32)],
        compiler_params=_params("arbitrary"),
    )(dyp, dyp, e, d, w_group, scale.reshape(1, pw))


def _sgu_rows(t_dim, chunk):
    return chunk * _pick(t_dim // chunk, (2, 1))


def _tril(chunk):
    return lax.broadcasted_iota(jnp.int32, (chunk, chunk), 0) >= lax.broadcasted_iota(jnp.int32, (chunk, chunk), 1)


def sgu_fwd(z, gain, w_s, b_s, pw, sw, name):
    t_dim = z.shape[0]
    n_heads, chunk, _ = w_s.shape
    hd = sw // n_heads
    tr = _sgu_rows(t_dim, chunk)
    ju = pw // sw

    def body(u_ref, v_ref, gain_ref, w_ref, bt_ref, sg_ref):
        ug, _ = _gelu(u_ref[...])
        vg, _ = _gelu(v_ref[...])
        r = lax.rsqrt(jnp.mean(vg * vg, axis=-1, keepdims=True) + EPS)
        vn = ((vg * r) * gain_ref[...]).astype(BF16)
        tri = _tril(chunk)
        for h in range(n_heads):
            wm = jnp.where(tri, w_ref[h], 0.0).astype(BF16)
            cols = slice(h * hd, (h + 1) * hd)
            for ch in range(tr // chunk):
                rows = slice(ch * chunk, (ch + 1) * chunk)
                s = jnp.dot(wm, vn[rows, cols], preferred_element_type=F32) + bt_ref[:, h:h + 1]
                sg_ref[rows, cols] = (ug[rows, cols] * s).astype(BF16)

    return pl.pallas_call(
        body, name=name, grid=(t_dim // tr,),
        in_specs=[pl.BlockSpec((tr, sw), lambda i: (i, ju)), pl.BlockSpec((tr, sw), lambda i: (i, ju + 1)), _vec(sw),
                  pl.BlockSpec((n_heads, chunk, chunk), lambda i: (0, 0, 0)),
                  pl.BlockSpec((chunk, n_heads), lambda i: (0, 0))],
        out_specs=_row(tr, sw), out_shape=jax.ShapeDtypeStruct((t_dim, sw), BF16),
        compiler_params=_params("parallel"),
    )(z, z, gain.reshape(1, sw), w_s, b_s.T)


def sgu_bwd(z, dsg, gain, w_s, b_s, pw, sw, name):
    t_dim = z.shape[0]
    n_heads, chunk, _ = w_s.shape
    hd = sw // n_heads
    tr = _sgu_rows(t_dim, chunk)
    ju = pw // sw
    nt_dims = (((1,), (1,)), ((), ()))
    tn_dims = (((0,), (0,)), ((), ()))

    def body(u_ref, v_ref, dsg_ref, gain_ref, w_ref, bt_ref, du_ref, dv_ref, dw_ref, dbt_ref, dgain_ref,
             dvn_ref, dug_ref):
        @pl.when(pl.program_id(0) == 0)
        def _():
            dw_ref[...] = jnp.zeros_like(dw_ref)
            dbt_ref[...] = jnp.zeros_like(dbt_ref)
            dgain_ref[...] = jnp.zeros_like(dgain_ref)

        u = u_ref[...]
        v = v_ref[...]
        ug, tu = _gelu(u)
        vg, tv = _gelu(v)
        r = lax.rsqrt(jnp.mean(vg * vg, axis=-1, keepdims=True) + EPS)
        n = vg * r
        gain_v = gain_ref[...]
        vn = (n * gain_v).astype(BF16)
        dsg_v = dsg_ref[...].astype(F32)
        tri = _tril(chunk)
        for h in range(n_heads):
            wm = jnp.where(tri, w_ref[h], 0.0).astype(BF16)
            cols = slice(h * hd, (h + 1) * hd)
            for ch in range(tr // chunk):
                rows = slice(ch * chunk, (ch + 1) * chunk)
                vn_b = vn[rows, cols]
                s = jnp.dot(wm, vn_b, preferred_element_type=F32) + bt_ref[:, h:h + 1]
                dsg_b = dsg_v[rows, cols]
                dug_ref[rows, cols] = dsg_b * s
                ds = dsg_b * ug[rows, cols]
                ds_b = ds.astype(BF16)
                dw_ref[h] += jnp.where(tri, lax.dot_general(ds_b, vn_b, nt_dims, preferred_element_type=F32), 0.0)
                dbt_ref[:, h:h + 1] += jnp.sum(ds, axis=1, keepdims=True)
                dvn_ref[rows, cols] = lax.dot_general(wm, ds_b, tn_dims, preferred_element_type=F32)
        dvn = dvn_ref[...]
        dgain_ref[...] += jnp.sum(dvn * n, axis=0, keepdims=True)
        dn = dvn * gain_v
        dvg = r * (dn - n * jnp.mean(dn * n, axis=-1, keepdims=True))
        dv_ref[...] = (dvg * _gelu_grad(v, tv)).astype(BF16)
        du_ref[...] = (dug_ref[...] * _gelu_grad(u, tu)).astype(BF16)

    full_w = pl.BlockSpec((n_heads, chunk, chunk), lambda i: (0, 0, 0))
    full_b = pl.BlockSpec((chunk, n_heads), lambda i: (0, 0))
    du, dv, dw, dbt, dgain = pl.pallas_call(
        body, name=name, grid=(t_dim // tr,),
        in_specs=[pl.BlockSpec((tr, sw), lambda i: (i, ju)), pl.BlockSpec((tr, sw), lambda i: (i, ju + 1)),
                  _row(tr, sw), _vec(sw), full_w, full_b],
        out_specs=[_row(tr, sw), _row(tr, sw), full_w, full_b, _vec(sw)],
        out_shape=[jax.ShapeDtypeStruct((t_dim, sw), BF16), jax.ShapeDtypeStruct((t_dim, sw), BF16),
                   jax.ShapeDtypeStruct((n_heads, chunk, chunk), F32), jax.ShapeDtypeStruct((chunk, n_heads), F32),
                   jax.ShapeDtypeStruct((1, sw), F32)],
        scratch_shapes=[pltpu.VMEM((tr, sw), F32), pltpu.VMEM((tr, sw), F32)],
        compiler_params=_params("arbitrary"),
    )(z, z, dsg, gain.reshape(1, sw), w_s, b_s.T)
    return du, dv, dw, dbt.T, dgain.reshape(sw)


HBM_SPEC = pl.BlockSpec(memory_space=pltpu.HBM)
SEM_SPEC = pl.BlockSpec(memory_space=pltpu.SEMAPHORE)
ANY_SPEC = pl.BlockSpec(memory_space=pl.ANY)
DATAFLOW = pltpu.SideEffectType.DATAFLOW_SIDE_EFFECTING


def _hbm(a):
    return pltpu.with_memory_space_constraint(a, pltpu.HBM)


def _slot(ref, axis, k, n):
    idx = [slice(None)] * len(ref.shape)
    idx[axis] = pl.ds(k * n, n)
    return ref.at[tuple(idx)]


def _slot_half(ref, axis, k, half):
    idx = [slice(None)] * len(ref.shape)
    n = ref.shape[axis] // N_CHIPS
    if axis == 0:
        idx[0] = pl.ds(k * n + half * (n // 2), n // 2)
    else:
        idx[axis] = pl.ds(k * n, n)
        idx[0] = pl.ds(half * (ref.shape[0] // 2), ref.shape[0] // 2)
    return ref.at[tuple(idx)]


def _chip_of(k, core):
    return (k // 2, k % 2, core)


def _my_chip():
    return 2 * lax.axis_index("x") + lax.axis_index("y")


class Exchange:
    def __init__(self, gather, axes):
        self.gather, self.axes = gather, axes
        self.per = 1 if gather else 2

    def bufs(self, t, refs):
        return refs[t * self.per:(t + 1) * self.per]

    def branches(self, me, core):
        if self.gather:
            return [(k, h, (me == k) & (core == h)) for k in range(N_CHIPS) for h in range(2)]
        return [(k, None, me == k) for k in range(N_CHIPS)]

    def views(self, t, bufs, src_chip, dst_chip, half):
        ax = self.axes[t]
        if self.gather:
            slot = _slot_half(bufs[0], ax, src_chip, half)
            return slot, slot
        return _slot(bufs[0], ax, dst_chip, bufs[0].shape[ax] // N_CHIPS), bufs[1].at[src_chip]


def exchange_start(bufs, ex, groups, name, deps=()):
    nb, ng = len(bufs), len(groups)

    def body(*refs):
        ins, outs = refs[:nb], refs[nb + len(deps):]
        sems, token = outs[:2 * ng], outs[-1]
        core = lax.axis_index("c")
        me = _my_chip()
        for k, half, mine in ex.branches(me, core):
            @pl.when(mine)
            def _(k=k, half=half):
                for gi, group in enumerate(groups):
                    for ti, t in enumerate(group):
                        for j in range(N_CHIPS):
                            if j != k:
                                src, dst = ex.views(t, ex.bufs(t, ins), k, j, half)
                                pltpu.make_async_remote_copy(
                                    src_ref=src, dst_ref=dst, send_sem=sems[2 * gi].at[ti * N_CHIPS + j],
                                    recv_sem=sems[2 * gi + 1].at[ti * N_CHIPS + k], device_id=_chip_of(j, core),
                                    device_id_type=MESH).start()
        token[...] = jnp.zeros_like(token)

    sem_shapes = []
    for group in groups:
        sem_shapes += [pltpu.SemaphoreType.DMA((len(group) * N_CHIPS,))] * 2
    thru = [pltpu.HBM(a.shape, a.dtype) for a in bufs]
    outs = pl.pallas_call(
        body, name=name, out_shape=sem_shapes + thru + [jax.ShapeDtypeStruct((SUBLANE, LANE), F32)],
        in_specs=[HBM_SPEC] * nb + [ANY_SPEC] * len(deps),
        out_specs=[SEM_SPEC] * (2 * ng) + [HBM_SPEC] * nb + [pl.BlockSpec(memory_space=pltpu.VMEM)],
        input_output_aliases={i: 2 * ng + i for i in range(nb)},
        compiler_params=pltpu.CompilerParams(has_side_effects=DATAFLOW),
    )(*[_hbm(a) for a in bufs], *deps)
    sems = [(outs[2 * gi], outs[2 * gi + 1]) for gi in range(ng)]
    return sems, outs[2 * ng:2 * ng + nb], outs[-1]


def exchange_wait(bufs, tensors, ex, sems, after, name):
    nb = len(bufs)
    send_sems, recv_sems = sems

    def body(*refs):
        ins, send_ref, recv_ref = refs[:nb], refs[nb], refs[nb + 1]
        core = lax.axis_index("c")
        me = _my_chip()
        for k, half, mine in ex.branches(me, core):
            @pl.when(mine)
            def _(k=k, half=half):
                for ti, t in enumerate(tensors):
                    for j in range(N_CHIPS):
                        if j != k:
                            src, _ = ex.views(t, ex.bufs(ti, ins), k, j, half)
                            _, dst = ex.views(t, ex.bufs(ti, ins), j, k, half)
                            copy = pltpu.make_async_remote_copy(
                                src_ref=src, dst_ref=dst, send_sem=send_ref.at[ti * N_CHIPS + j],
                                recv_sem=recv_ref.at[ti * N_CHIPS + j],
                                device_id=_chip_of(j, core), device_id_type=MESH)
                            copy.wait_send()
                            copy.wait_recv()

    return pl.pallas_call(
        body, name=name, out_shape=[pltpu.HBM(a.shape, a.dtype) for a in bufs],
        in_specs=[HBM_SPEC] * nb + [SEM_SPEC, SEM_SPEC] + [ANY_SPEC] * len(after),
        out_specs=[HBM_SPEC] * nb, input_output_aliases={i: i for i in range(nb)},
        compiler_params=pltpu.CompilerParams(has_side_effects=DATAFLOW),
    )(*bufs, send_sems, recv_sems, *after)


def share_halves(bufs, axes, name):
    nb = len(bufs)

    def body(*refs):
        outs = refs[nb:2 * nb]
        send_sems, recv_sems = refs[2 * nb:]
        core = lax.axis_index("c")
        me = _my_chip()
        sibling = (lax.axis_index("x"), lax.axis_index("y"), 1 - core)
        for k in range(N_CHIPS):
            for half in range(2):
                @pl.when((me == k) & (core == half))
                def _(k=k, half=half):
                    def copy(t, j, h):
                        part = _slot_half(outs[t], axes[t], j, h)
                        return pltpu.make_async_remote_copy(
                            src_ref=part, dst_ref=part, send_sem=send_sems.at[t * N_CHIPS + j],
                            recv_sem=recv_sems.at[t * N_CHIPS + j], device_id=sibling, device_id_type=MESH)

                    pairs = [(t, j) for t in range(nb) for j in range(N_CHIPS) if j != k]
                    for t, j in pairs:
                        copy(t, j, half).start()
                    for t, j in pairs:
                        copy(t, j, 1 - half).wait_recv()
                    for t, j in pairs:
                        copy(t, j, half).wait_send()

    return pl.pallas_call(
        body, name=name, out_shape=[jax.ShapeDtypeStruct(a.shape, a.dtype) for a in bufs],
        in_specs=[HBM_SPEC] * nb, out_specs=[HBM_SPEC] * nb, input_output_aliases={i: i for i in range(nb)},
        scratch_shapes=[pltpu.SemaphoreType.DMA((nb * N_CHIPS,)), pltpu.SemaphoreType.DMA((nb * N_CHIPS,))],
        compiler_params=pltpu.CompilerParams(has_side_effects=True),
    )(*bufs)


def swap_start(arrs, name):
    nt = len(arrs)

    def body(*refs):
        ins, lands, outs = refs[:nt], refs[nt:2 * nt], refs[2 * nt:]
        send_sems, recv_sems, token = outs[0], outs[1], outs[-1]
        sibling = (lax.axis_index("x"), lax.axis_index("y"), 1 - lax.axis_index("c"))
        for t in range(nt):
            pltpu.make_async_remote_copy(src_ref=ins[t], dst_ref=lands[t], send_sem=send_sems.at[t],
                                         recv_sem=recv_sems.at[t], device_id=sibling, device_id_type=MESH).start()
        token[...] = jnp.zeros_like(token)

    thru = [pltpu.HBM(a.shape, a.dtype) for a in arrs] * 2
    outs = pl.pallas_call(
        body, name=name,
        out_shape=[pltpu.SemaphoreType.DMA((nt,))] * 2 + thru + [jax.ShapeDtypeStruct((SUBLANE, LANE), F32)],
        in_specs=[HBM_SPEC] * (2 * nt),
        out_specs=[SEM_SPEC, SEM_SPEC] + [HBM_SPEC] * (2 * nt) + [pl.BlockSpec(memory_space=pltpu.VMEM)],
        input_output_aliases={i: 2 + i for i in range(2 * nt)},
        compiler_params=pltpu.CompilerParams(has_side_effects=DATAFLOW),
    )(*[_hbm(a) for a in arrs], *[_hbm(lax.empty(a.shape, a.dtype)) for a in arrs])
    return outs[0], outs[1], outs[2:2 + nt], outs[2 + nt:2 + 2 * nt], outs[-1]


def swap_wait(send_sems, recv_sems, arrs, lands, after, name):
    nt = len(arrs)

    def body(*refs):
        ins, land_refs, send_ref, recv_ref = refs[:nt], refs[nt:2 * nt], refs[2 * nt], refs[2 * nt + 1]
        sibling = (lax.axis_index("x"), lax.axis_index("y"), 1 - lax.axis_index("c"))
        for t in range(nt):
            copy = pltpu.make_async_remote_copy(src_ref=ins[t], dst_ref=land_refs[t], send_sem=send_ref.at[t],
                                                recv_sem=recv_ref.at[t], device_id=sibling,
                                                device_id_type=MESH)
            copy.wait_send()
            copy.wait_recv()

    outs = pl.pallas_call(
        body, name=name, out_shape=[pltpu.HBM(a.shape, a.dtype) for a in list(arrs) + list(lands)],
        in_specs=[HBM_SPEC] * (2 * nt) + [SEM_SPEC, SEM_SPEC] + [ANY_SPEC] * len(after),
        out_specs=[HBM_SPEC] * (2 * nt), input_output_aliases={i: i for i in range(2 * nt)},
        compiler_params=pltpu.CompilerParams(has_side_effects=DATAFLOW),
    )(*arrs, *lands, send_sems, recv_sems, *after)
    return outs[:nt], outs[nt:]


def place_shard(w_stack, layer, axis, me, name):
    _, a_dim, r, c = w_stack.shape
    tr, tc = _stream_tiles(r, c)
    nr, nc = r // tr, c // tc
    full = (a_dim, r * N_CHIPS, c) if axis == 1 else (a_dim, r, c * N_CHIPS)

    def body(me_ref, w_ref, o_ref):
        o_ref[...] = w_ref[...].astype(BF16)

    def own_map(a, i, j, me_ref):
        return (a, me_ref[0] * nr + i, j) if axis == 1 else (a, i, me_ref[0] * nc + j)

    return pl.pallas_call(
        body, name=name, out_shape=jax.ShapeDtypeStruct(full, BF16),
        grid_spec=pltpu.PrefetchScalarGridSpec(
            num_scalar_prefetch=1, grid=(a_dim, nr, nc),
            in_specs=[pl.BlockSpec((None, None, tr, tc), lambda a, i, j, me_ref: (layer, a, i, j))],
            out_specs=pl.BlockSpec((None, tr, tc), own_map)),
        compiler_params=_params("parallel", "parallel", "parallel"),
    )(me, w_stack)


def sum_pieces(stack, full, axis, me, name):
    _, a_dim, r, c = stack.shape
    tr, tc = _stream_tiles(r, c)
    nr, nc = r // tr, c // tc

    def body(me_ref, own_ref, s1_ref, s2_ref, s3_ref, o_ref):
        acc = own_ref[...].astype(F32)
        for ref in (s1_ref, s2_ref, s3_ref):
            acc = acc + ref[...].astype(F32)
        o_ref[...] = acc.astype(BF16)

    def own_map(a, i, j, me_ref):
        return (a, me_ref[0] * nr + i, j) if axis == 1 else (a, i, me_ref[0] * nc + j)

    def from_chip(step):
        return pl.BlockSpec((None, None, tr, tc), lambda a, i, j, me_ref: ((me_ref[0] + step) % N_CHIPS, a, i, j))

    return pl.pallas_call(
        body, name=name, out_shape=jax.ShapeDtypeStruct((a_dim, r, c), BF16),
        grid_spec=pltpu.PrefetchScalarGridSpec(
            num_scalar_prefetch=1, grid=(a_dim, nr, nc),
            in_specs=[pl.BlockSpec((None, tr, tc), own_map), from_chip(1), from_chip(2), from_chip(3)],
            out_specs=pl.BlockSpec((None, tr, tc), lambda a, i, j, me_ref: (a, i, j))),
        compiler_params=_params("parallel", "parallel", "parallel"),
    )(me, full, stack, stack, stack)


def all_reduce_small(x, name):
    rows, lanes = x.shape

    def body(x_ref, sum_ref, gath_ref, send_sems, recv_sems, local_sem):
        cx, cy, cc = lax.axis_index("x"), lax.axis_index("y"), lax.axis_index("c")
        me, sibling = (cx, cy, cc), (cx, cy, 1 - cc)
        chips = [(1 - cx, cy), (cx, 1 - cy), (1 - cx, 1 - cy)]

        def block(px, py, pc):
            return gath_ref.at[pl.ds(pl.multiple_of((4 * px + 2 * py + pc) * rows, SUBLANE), rows), :]

        def copy(k, blk, to, src=None):
            return pltpu.make_async_remote_copy(
                src_ref=block(*blk) if src is None else src, dst_ref=block(*blk),
                send_sem=send_sems.at[k], recv_sem=recv_sems.at[k], device_id=to, device_id_type=MESH)

        mine = pltpu.make_async_copy(x_ref, block(*me), local_sem)
        mine.start()
        first = [copy(0, me, sibling, src=x_ref)]
        first += [copy(1 + j, me, (*chip, cc), src=x_ref) for j, chip in enumerate(chips)]
        for cp in first:
            cp.start()
        passed = [copy(4 + j, (*chip, cc), sibling) for j, chip in enumerate(chips)]
        for j, chip in enumerate(chips):
            copy(1 + j, (*chip, cc), me).wait_recv()
            passed[j].start()
        copy(0, sibling, me).wait_recv()
        for j, chip in enumerate(chips):
            copy(4 + j, (*chip, 1 - cc), me).wait_recv()
        for cp in first + passed:
            cp.wait_send()
        mine.wait()
        acc = gath_ref[pl.ds(0, rows), :]
        for k in range(1, N_DEV):
            acc = acc + gath_ref[pl.ds(k * rows, rows), :]
        sum_ref[...] = acc

    return pl.pallas_call(
        body, name=name, out_shape=jax.ShapeDtypeStruct((rows, lanes), F32),
        in_specs=[pl.BlockSpec(memory_space=pltpu.VMEM)], out_specs=pl.BlockSpec(memory_space=pltpu.VMEM),
        scratch_shapes=[pltpu.VMEM((N_DEV * rows, lanes), F32), pltpu.SemaphoreType.DMA((7,)),
                        pltpu.SemaphoreType.DMA((7,)), pltpu.SemaphoreType.DMA],
        compiler_params=pltpu.CompilerParams(has_side_effects=True, vmem_limit_bytes=VMEM_LIMIT),
    )(x)


def _adamw(w, g, m, v):
    m = ADAM_B1 * m + (1.0 - ADAM_B1) * g
    v = ADAM_B2 * v + (1.0 - ADAM_B2) * (g * g)
    m_hat = m / (1.0 - ADAM_B1 ** ADAM_STEP)
    v_hat = v / (1.0 - ADAM_B2 ** ADAM_STEP)
    delta = -ADAM_LR * (m_hat / (jnp.sqrt(v_hat) + ADAM_EPS) + ADAM_WD * w)
    return delta, m, v


def adam_big(p_own, p_sib, w, m, v, layer, stacks, name):
    r, c = p_own.shape
    tr, tc = _stream_tiles(r, c)

    def body(p_ref, q_ref, w_ref, m_ref, v_ref, *rest):
        g_out, d_out, m_out, v_out = rest[4:]
        g = p_ref[...].astype(F32) + q_ref[...].astype(F32)
        delta, m_new, v_new = _adamw(w_ref[...], g, m_ref[...], v_ref[...])
        g_out[...] = g
        d_out[...] = delta
        m_out[...] = m_new
        v_out[...] = v_new

    flat = pl.BlockSpec((tr, tc), lambda i, j: (i, j))
    layered = pl.BlockSpec((None, tr, tc), lambda i, j: (layer, i, j))
    anyspace = pl.BlockSpec(memory_space=pl.ANY)
    out = jax.ShapeDtypeStruct(w.shape, F32)
    return pl.pallas_call(
        body, name=name, grid=(r // tr, c // tc),
        in_specs=[flat, flat, layered, layered, layered] + [anyspace] * 4,
        out_specs=[layered] * 4, out_shape=[out] * 4, input_output_aliases={5: 0, 6: 1, 7: 2, 8: 3},
        compiler_params=_params("parallel", "parallel"),
    )(p_own, p_sib, w, m, v, *stacks)


def adam_small(g, w, m, v, name):
    rows, lanes = g.shape
    tr = _pick(rows, (512,) + ROW_TILES)

    def body(g_ref, w_ref, m_ref, v_ref, d_out, m_out, v_out):
        delta, m_new, v_new = _adamw(w_ref[...], g_ref[...], m_ref[...], v_ref[...])
        d_out[...] = delta
        m_out[...] = m_new
        v_out[...] = v_new

    out = jax.ShapeDtypeStruct((rows, lanes), F32)
    return pl.pallas_call(
        body, name=name, grid=(rows // tr,), in_specs=[_row(tr, lanes)] * 4, out_specs=[_row(tr, lanes)] * 3,
        out_shape=[out] * 3, compiler_params=_params("parallel"),
    )(g, w, m, v)


def ffn_fwd(x, h, w_up, w_down, g_post, g_next, tag):
    gate, up, a = matmul_swiglu(h, w_up, name=f"{tag}_up")
    f = matmul(a, w_down, out_dtype=F32, name=f"{tag}_down")
    y, h_next = res_rms_fwd(x, f, g_post, MACARON_WEIGHT, g_next, name=f"{tag}_res_fwd")
    return y, h_next, (x, h, gate, up, a, f)


def ffn_bwd(dy, saved, g_pre, w_up, w_down, g_post, tag, deps=()):
    x, h, gate, up, a, f = saved
    df, dg_post = rms_bwd(f, g_post, dy, MACARON_WEIGHT, None, BF16, name=f"{tag}_res_bwd", deps=deps)
    da = matmul(df, w_down, tb=True, out_dtype=BF16, name=f"{tag}_down_dx")
    dw_down = matmul(a, df, ta=True, out_dtype=BF16, name=f"{tag}_down_dw")
    dz = swiglu_bwd(gate, up, da, name=f"{tag}_swiglu_bwd")
    dh = matmul(dz, w_up, tb=True, out_dtype=F32, name=f"{tag}_up_dx")
    dw_up = matmul(h, dz, ta=True, out_dtype=BF16, name=f"{tag}_up_dw")
    dx, dg_pre = rms_bwd(x, g_pre, dh, 1.0, dy, F32, name=f"{tag}_rms_bwd")
    return dx, dw_up, dw_down, dg_pre, dg_post


def mixer_fwd(x, h, wt, sm, g_next, dims):
    pw, sw, d = dims
    z = matmul(h, wt['w_in'], out_dtype=F32, name="mix_in")
    dd, e, yp = pool_fwd(z, wt['pool_group_w'], sm['pool_scale'], pw, name="mix_pool_fwd")
    ya = matmul(yp, wt['w_pool_out'], out_dtype=BF16, name="mix_pool_out")
    sg = sgu_fwd(z, sm['sgu_v_gain'], sm['sgu_w_s'], sm['sgu_b_s'], pw, sw, name="mix_sgu_fwd")
    yb = matmul(sg, wt['w_sgu_out'], out_dtype=BF16, name="mix_sgu_out")
    m = gate_fwd(z, ya, yb, pw + 2 * sw, pw + 2 * sw + d, name="mix_gate_fwd")
    o = matmul(m, wt['w_out'], out_dtype=F32, name="mix_out")
    y, h_next = res_rms_fwd(x, o, sm['g_mix_post'], 1.0, g_next, name="mix_res_fwd")
    return y, h_next, (x, h, z, dd, e, yp, sg, ya, yb, m, o)


def mixer_bwd(dy, saved, wt, sm, dims, deps=()):
    pw, sw, d = dims
    x, h, z, dd, e, yp, sg, ya, yb, m, o = saved
    grads = {}
    do, grads['g_mix_post'] = rms_bwd(o, sm['g_mix_post'], dy, 1.0, None, BF16, name="mix_res_bwd", deps=deps)
    dm = matmul(do, wt['w_out'], tb=True, out_dtype=BF16, name="mix_out_dx")
    grads['w_out'] = matmul(m, do, ta=True, out_dtype=BF16, name="mix_out_dw")
    dya, dyb, dga, dgb = gate_bwd(z, ya, yb, dm, pw + 2 * sw, pw + 2 * sw + d, name="mix_gate_bwd")
    dyp = matmul(dya, wt['w_pool_out'], tb=True, out_dtype=BF16, name="mix_pool_out_dx")
    grads['w_pool_out'] = matmul(yp, dya, ta=True, out_dtype=BF16, name="mix_pool_out_dw")
    dp, dwg, dscale = pool_bwd(dyp, e, dd, wt['pool_group_w'], sm['pool_scale'], name="mix_pool_bwd")
    grads['pool_group_w'] = dwg.astype(BF16)
    grads['pool_scale'] = dscale.reshape(pw)
    dsg = matmul(dyb, wt['w_sgu_out'], tb=True, out_dtype=BF16, name="mix_sgu_out_dx")
    grads['w_sgu_out'] = matmul(sg, dyb, ta=True, out_dtype=BF16, name="mix_sgu_out_dw")
    du, dv, grads['sgu_w_s'], grads['sgu_b_s'], grads['sgu_v_gain'] = sgu_bwd(
        z, dsg, sm['sgu_v_gain'], sm['sgu_w_s'], sm['sgu_b_s'], pw, sw, name="mix_sgu_bwd")
    dz = jnp.concatenate([dp, du, dv, dga, dgb], axis=1)
    dh = matmul(dz, wt['w_in'], tb=True, out_dtype=F32, name="mix_in_dx")
    grads['w_in'] = matmul(h, dz, ta=True, out_dtype=BF16, name="mix_in_dw")
    dx, grads['g_mix_pre'] = rms_bwd(x, sm['g_mix_pre'], dh, 1.0, dy, F32, name="mix_rms_bwd")
    return dx, grads


def _as_rows(a):
    return a.reshape(a.shape[0], -1, a.shape[-1])


def kernel(x, g_ffn1_pre, w_ffn1_up, w_ffn1_down, g_ffn1_post, g_mix_pre, w_in, pool_group_w, pool_scale, w_pool_out, sgu_v_gain, sgu_w_s, sgu_b_s, w_sgu_out, w_out, g_mix_post, g_ffn2_pre, w_ffn2_up, w_ffn2_down, g_ffn2_post, loss_target, m_g_ffn1_pre, m_w_ffn1_up, m_w_ffn1_down, m_g_ffn1_post, m_g_mix_pre, m_w_in, m_pool_group_w, m_pool_scale, m_w_pool_out, m_sgu_v_gain, m_sgu_w_s, m_sgu_b_s, m_w_sgu_out, m_w_out, m_g_mix_post, m_g_ffn2_pre, m_w_ffn2_up, m_w_ffn2_down, m_g_ffn2_post, v_g_ffn1_pre, v_w_ffn1_up, v_w_ffn1_down, v_g_ffn1_post, v_g_mix_pre, v_w_in, v_pool_group_w, v_pool_scale, v_w_pool_out, v_sgu_v_gain, v_sgu_w_s, v_sgu_b_s, v_w_sgu_out, v_w_out, v_g_mix_post, v_g_ffn2_pre, v_w_ffn2_up, v_w_ffn2_down, v_g_ffn2_post):
    w = dict(zip(WEIGHTS, (g_ffn1_pre, w_ffn1_up, w_ffn1_down, g_ffn1_post, g_mix_pre, w_in, pool_group_w, pool_scale, w_pool_out, sgu_v_gain, sgu_w_s, sgu_b_s, w_sgu_out, w_out, g_mix_post, g_ffn2_pre, w_ffn2_up, w_ffn2_down, g_ffn2_post)))
    mom = dict(zip(WEIGHTS, (m_g_ffn1_pre, m_w_ffn1_up, m_w_ffn1_down, m_g_ffn1_post, m_g_mix_pre, m_w_in, m_pool_group_w, m_pool_scale, m_w_pool_out, m_sgu_v_gain, m_sgu_w_s, m_sgu_b_s, m_w_sgu_out, m_w_out, m_g_mix_post, m_g_ffn2_pre, m_w_ffn2_up, m_w_ffn2_down, m_g_ffn2_post)))
    var = dict(zip(WEIGHTS, (v_g_ffn1_pre, v_w_ffn1_up, v_w_ffn1_down, v_g_ffn1_post, v_g_mix_pre, v_w_in, v_pool_group_w, v_pool_scale, v_w_pool_out, v_sgu_v_gain, v_sgu_w_s, v_sgu_b_s, v_w_sgu_out, v_w_out, v_g_mix_post, v_g_ffn2_pre, v_w_ffn2_up, v_w_ffn2_down, v_g_ffn2_post)))
    depth = g_ffn1_pre.shape[0]
    d = x.shape[-1]
    pw = pool_scale.shape[-1]
    sw = sgu_v_gain.shape[-1]
    dims = (pw, sw, d)
    axes = [BIG_AXIS[n] for n in BIG]

    me = 2 * lax.axis_index("x") + lax.axis_index("y")
    gather = Exchange(True, axes)
    group_ids = [[BIG.index(n) for n in group] for group in BLOCK_WEIGHTS]

    me1 = me.reshape(1).astype(jnp.int32)

    def start_gather(l, deps=()):
        lands = []
        for n, ax in zip(BIG, axes):
            shards = w[n] if w[n].ndim == 4 else w[n][:, None]
            full = place_shard(shards, l, ax + 4 - w[n].ndim, me1, name=f"place_{n}")
            lands.append(full if w[n].ndim == 4 else full[0])
        return exchange_start(lands, gather, group_ids, name=f"gather_start_l{l}", deps=deps)

    def finish_gather(started, l, gi, after):
        sems, lands, _ = started
        ids = group_ids[gi]
        got = exchange_wait([lands[t] for t in ids], ids, gather, sems[gi], (after,), name=f"gather_wait_l{l}_b{gi}")
        got = share_halves(got, [axes[t] for t in ids], name=f"share_halves_b{gi}")
        return dict(zip(BLOCK_WEIGHTS[gi], got))

    small = [{n: w[n][l] for n in SMALL} for l in range(depth)]

    act = x[0]
    saved, full = [], []
    gathers = []
    for l in range(depth):
        gathers.append(start_gather(l, (gathers[-1][2],) if gathers else ()))
    h = rms_fwd(act, small[0]['g_ffn1_pre'], name="first_rms_fwd", deps=(gathers[-1][2],))
    for l in range(depth):
        started = gathers[l]
        sm, wt = small[l], {}
        g_after = small[l + 1]['g_ffn1_pre'] if l + 1 < depth else None
        wt.update(finish_gather(started, l, 0, act))
        act, h, s1 = ffn_fwd(act, h, wt['w_ffn1_up'], wt['w_ffn1_down'], sm['g_ffn1_post'], sm['g_mix_pre'], "ffn1")
        wt.update(finish_gather(started, l, 1, act))
        act, h, s2 = mixer_fwd(act, h, wt, sm, sm['g_ffn2_pre'], dims)
        wt.update(finish_gather(started, l, 2, act))
        act, h, s3 = ffn_fwd(act, h, wt['w_ffn2_up'], wt['w_ffn2_down'], sm['g_ffn2_post'], g_after, "ffn2")
        saved.append((s1, s2, s3))
        full.append(wt)
    dact, sq_sum = loss_grad(act, loss_target[0], name="loss_grad")
    loss = lax.psum(0.5 * sq_sum / d, ("x", "y", "c"))

    rows3 = {n: (_as_rows(w[n]), _as_rows(mom[n]), _as_rows(var[n])) for n in BIG}
    stacks = {n: tuple(lax.empty(rows3[n][0].shape, F32) for _ in range(4)) for n in BIG}
    small_grads = [{} for _ in range(depth)]

    def start_scatter(names, grads, l, block):
        ex = Exchange(False, [BIG_AXIS[n] for n in names])
        bufs = []
        for n in names:
            piece = list(grads[n].shape)
            piece[BIG_AXIS[n]] //= N_CHIPS
            bufs += [grads[n], lax.empty((N_CHIPS, *piece), BF16)]
        return (names, ex, l, block) + exchange_start(bufs, ex, [list(range(len(names)))],
                                                      name=f"scatter_start_l{l}_b{block}")

    def finish_scatter(pending, after):
        names, ex, l, block, sems, bufs, _ = pending
        got = exchange_wait(bufs, list(range(len(names))), ex, sems[0], after, name=f"scatter_wait_l{l}_b{block}")
        plane = []
        for ti, n in enumerate(names):
            full_g, stack = got[2 * ti], got[2 * ti + 1]
            if full_g.ndim == 2:
                full_g, stack = full_g[None], stack[:, None]
            p = sum_pieces(stack, full_g, BIG_AXIS[n] + 3 - got[2 * ti].ndim, me1, name=f"sum_{n}")
            plane.append(p.reshape(-1, p.shape[-1]))
        return (names, l, block) + swap_start(plane, name=f"swap_start_l{l}_b{block}")

    def finish_swap(swapping, after):
        names, l, block, send_sems, recv_sems, plane, lands, _ = swapping
        plane, other = swap_wait(send_sems, recv_sems, plane, lands, after, name=f"swap_wait_l{l}_b{block}")
        for n, p_own, p_sib in zip(names, plane, other):
            stacks[n] = tuple(adam_big(p_own, p_sib, *rows3[n], l, stacks[n], name=f"adam_{n}"))
        return stacks[names[-1]][0]

    def advance(pending, swapping, done, after):
        started = finish_scatter(pending, (after,) + done)
        if swapping is not None:
            done = (finish_swap(swapping, (started[-1],)),)
        return started, done

    pending, swapping, done = None, None, ()
    for l in reversed(range(depth)):
        wt, sm = full[l], small[l]
        s1, s2, s3 = saved[l]
        for block in (2, 1, 0):
            deps = (pending[-1],) if pending is not None else ()
            g = {}
            if block == 2:
                dnew, g['w_ffn2_up'], g['w_ffn2_down'], g['g_ffn2_pre'], g['g_ffn2_post'] = ffn_bwd(
                    dact, s3, sm['g_ffn2_pre'], wt['w_ffn2_up'], wt['w_ffn2_down'], sm['g_ffn2_post'], "ffn2", deps)
            elif block == 1:
                dnew, g = mixer_bwd(dact, s2, wt, sm, dims, deps)
            else:
                dnew, g['w_ffn1_up'], g['w_ffn1_down'], g['g_ffn1_pre'], g['g_ffn1_post'] = ffn_bwd(
                    dact, s1, sm['g_ffn1_pre'], wt['w_ffn1_up'], wt['w_ffn1_down'], sm['g_ffn1_post'], "ffn1", deps)
            small_grads[l].update({n: g[n] for n in g if n in SMALL})
            if pending is not None:
                swapping, done = advance(pending, swapping, done, dnew)
            pending = start_scatter(BLOCK_WEIGHTS[block], g, l, block)
            dact = dnew
    swapping, done = advance(pending, swapping, done, dact)
    finish_swap(swapping, done)

    def flat(tree):
        v = jnp.concatenate([tree[n].reshape(-1).astype(F32) for n in SMALL])
        pad = (-v.shape[0]) % (SUBLANE * LANE)
        return jnp.pad(v, (0, pad)).reshape(-1, LANE)

    g_small = all_reduce_small(flat({n: jnp.stack([small_grads[l][n] for l in range(depth)]) for n in SMALL}),
                               name="all_reduce_small")
    d_small, m_small, v_small = adam_small(g_small, flat(w), flat(mom), flat(var), name="adam_small")

    def unflat(block):
        v, out, at = block.reshape(-1), {}, 0
        for n in SMALL:
            out[n] = v[at:at + w[n].size].reshape(w[n].shape)
            at += w[n].size
        return out

    result = [{}, {}, {}, {}]
    for tree, block in zip(result, (g_small, d_small, m_small, v_small)):
        tree.update(unflat(block))
    for n in BIG:
        for tree, stack in zip(result, stacks[n]):
            tree[n] = stack.reshape(w[n].shape)
    return (loss, dact.reshape(x.shape), *[tree[n] for tree in result for n in WEIGHTS])
```

```python
import math

import jax
import jax.numpy as jnp
from jax import lax
from jax.experimental import pallas as pl
from jax.experimental.pallas import tpu as pltpu

F32 = jnp.float32
BF16 = jnp.bfloat16
MESH = pl.DeviceIdType.MESH

EPS = 1e-6
MACARON_WEIGHT = 0.5
POOL_WINDOWS = (2, 4, 8, 16)
POOL_HALO = 16
ADAM_LR = 0.001
ADAM_B1 = 0.9
ADAM_B2 = 0.999
ADAM_EPS = 1e-08
ADAM_WD = 0.01
ADAM_STEP = 10
GELU_K = math.sqrt(2.0 / math.pi)
GELU_C = 0.044715

N_CHIPS = 4
N_DEV = 8
V7X_VMEM_BYTES = 64 * 1024 * 1024
VMEM_LIMIT = (V7X_VMEM_BYTES * 3) // 4
LANE = 128
SUBLANE = 8

WEIGHTS = ['g_ffn1_pre', 'w_ffn1_up', 'w_ffn1_down', 'g_ffn1_post', 'g_mix_pre', 'w_in', 'pool_group_w',
           'pool_scale', 'w_pool_out', 'sgu_v_gain', 'sgu_w_s', 'sgu_b_s', 'w_sgu_out', 'w_out', 'g_mix_post',
           'g_ffn2_pre', 'w_ffn2_up', 'w_ffn2_down', 'g_ffn2_post']
BIG_AXIS = {'w_ffn1_up': 1, 'w_ffn1_down': 0, 'w_in': 1, 'pool_group_w': 1, 'w_pool_out': 1, 'w_sgu_out': 1,
            'w_out': 0, 'w_ffn2_up': 1, 'w_ffn2_down': 0}
BIG = list(BIG_AXIS)
BLOCK_WEIGHTS = [['w_ffn1_up', 'w_ffn1_down'], ['w_in', 'pool_group_w', 'w_pool_out', 'w_sgu_out', 'w_out'],
                 ['w_ffn2_up', 'w_ffn2_down']]
SMALL = [n for n in WEIGHTS if n not in BIG_AXIS]
SHARE_AHEAD_FROM = 4


def _pick(dim, cands):
    for c in cands:
        if dim % c == 0:
            return c
    return dim


STREAM_COL_TILES = (1024, 1408, 896, 512, 256, 128)
STREAM_ROW_TILES = (512, 352, 256, 128, 64, 32, 16, 8)
STREAM_BLOCK_ELEMS = 384 * 1024


def _stream_tiles(r, c):
    tc = _pick(c, STREAM_COL_TILES)
    for tr in STREAM_ROW_TILES:
        if r % tr == 0 and tr * tc <= STREAM_BLOCK_ELEMS:
            return tr, tc
    return r, tc


def _params(*sem):
    return pltpu.CompilerParams(dimension_semantics=sem if sem else None, vmem_limit_bytes=VMEM_LIMIT)


def _sigmoid(x):
    return 1.0 / (1.0 + jnp.exp(-x))


def _gelu(x):
    t = jnp.tanh(GELU_K * (x + GELU_C * (x * x * x)))
    return x * (0.5 * (1.0 + t)), t


def _gelu_grad(x, t):
    return 0.5 * (1.0 + t) + (0.5 * x) * (1.0 - t * t) * (GELU_K * (1.0 + (3.0 * GELU_C) * (x * x)))


MATMUL_MN_TILES = (1024, 1408, 512, 256, 128)
MATMUL_K_TILES = (2048, 2816, 1024, 512, 256, 128)


def matmul(a, b, *, ta=False, tb=False, out_dtype=BF16, name, deps=()):
    m_dim, k_dim = (a.shape[1], a.shape[0]) if ta else a.shape
    n_dim = b.shape[0] if tb else b.shape[1]
    tm = _pick(m_dim, MATMUL_MN_TILES)
    tn = _pick(n_dim, MATMUL_MN_TILES)
    tk = _pick(k_dim, MATMUL_K_TILES)
    nk = k_dim // tk
    dims = (((0 if ta else 1,), (1 if tb else 0,)), ((), ()))

    def body(a_ref, b_ref, *rest):
        o_ref, acc_ref = rest[-2:]
        k = pl.program_id(2)

        @pl.when(k == 0)
        def _():
            acc_ref[...] = jnp.zeros_like(acc_ref)

        acc_ref[...] += lax.dot_general(a_ref[...], b_ref[...], dims, preferred_element_type=F32)

        @pl.when(k == nk - 1)
        def _():
            o_ref[...] = acc_ref[...].astype(o_ref.dtype)

    a_spec = pl.BlockSpec((tk, tm), lambda i, j, k: (k, i)) if ta else pl.BlockSpec((tm, tk), lambda i, j, k: (i, k))
    b_spec = pl.BlockSpec((tn, tk), lambda i, j, k: (j, k)) if tb else pl.BlockSpec((tk, tn), lambda i, j, k: (k, j))
    return pl.pallas_call(
        body, name=name, grid=(m_dim // tm, n_dim // tn, nk),
        in_specs=[a_spec, b_spec] + [ANY_SPEC] * len(deps), out_specs=pl.BlockSpec((tm, tn), lambda i, j, k: (i, j)),
        out_shape=jax.ShapeDtypeStruct((m_dim, n_dim), out_dtype),
        scratch_shapes=[pltpu.VMEM((tm, tn), F32)],
        compiler_params=_params("parallel", "parallel", "arbitrary"),
    )(a, b, *deps)


def matmul_swiglu(h, w_up, name, deps=()):
    m_dim, k_dim = h.shape
    f = w_up.shape[1] // 2
    tm = _pick(m_dim, MATMUL_MN_TILES)
    tn = _pick(f, (512, 256, 128))
    tk = _pick(k_dim, MATMUL_K_TILES)
    nk, nf = k_dim // tk, f // tn

    def body(h_ref, wg_ref, wu_ref, *rest):
        g_ref, u_ref, a_ref, accg_ref, accu_ref = rest[-5:]
        k = pl.program_id(2)

        @pl.when(k == 0)
        def _():
            accg_ref[...] = jnp.zeros_like(accg_ref)
            accu_ref[...] = jnp.zeros_like(accu_ref)

        hv = h_ref[...]
        accg_ref[...] += jnp.dot(hv, wg_ref[...], preferred_element_type=F32)
        accu_ref[...] += jnp.dot(hv, wu_ref[...], preferred_element_type=F32)

        @pl.when(k == nk - 1)
        def _():
            g, u = accg_ref[...], accu_ref[...]
            g_ref[...] = g.astype(BF16)
            u_ref[...] = u.astype(BF16)
            a_ref[...] = (g * _sigmoid(g) * u).astype(BF16)

    out = jax.ShapeDtypeStruct((m_dim, f), BF16)
    blk = pl.BlockSpec((tm, tn), lambda i, j, k: (i, j))
    return pl.pallas_call(
        body, name=name, grid=(m_dim // tm, nf, nk),
        in_specs=[pl.BlockSpec((tm, tk), lambda i, j, k: (i, k)), pl.BlockSpec((tk, tn), lambda i, j, k: (k, j)),
                  pl.BlockSpec((tk, tn), lambda i, j, k: (k, j + nf))] + [ANY_SPEC] * len(deps),
        out_specs=[blk, blk, blk], out_shape=[out, out, out],
        scratch_shapes=[pltpu.VMEM((tm, tn), F32), pltpu.VMEM((tm, tn), F32)],
        compiler_params=_params("parallel", "parallel", "arbitrary"),
    )(h, w_up, w_up, *deps)


ROW_TILES = (256, 128, 64, 32, 16, 8)


def _row(tr, width):
    return pl.BlockSpec((tr, width), lambda i: (i, 0))


def _vec(width):
    return pl.BlockSpec((1, width), lambda i: (0, 0))


def rms_fwd(x, g, name, deps=()):
    t_dim, d = x.shape
    tr = _pick(t_dim, ROW_TILES)

    def body(x_ref, g_ref, *rest):
        o_ref = rest[-1]
        xv = x_ref[...]
        r = lax.rsqrt(jnp.mean(xv * xv, axis=-1, keepdims=True) + EPS)
        o_ref[...] = ((xv * r) * g_ref[...]).astype(o_ref.dtype)

    return pl.pallas_call(
        body, name=name, grid=(t_dim // tr,), in_specs=[_row(tr, d), _vec(d)] + [ANY_SPEC] * len(deps),
        out_specs=_row(tr, d), out_shape=jax.ShapeDtypeStruct((t_dim, d), BF16), compiler_params=_params("parallel"),
    )(x, g.reshape(1, d), *deps)


def res_rms_fwd(x, f, g, weight, g_next, name):
    t_dim, d = x.shape
    tr = _pick(t_dim, ROW_TILES)
    chained = g_next is not None

    def body(x_ref, f_ref, g_ref, *rest):
        fv = f_ref[...]
        r = lax.rsqrt(jnp.mean(fv * fv, axis=-1, keepdims=True) + EPS)
        y = x_ref[...] + weight * ((fv * r) * g_ref[...])
        if chained:
            gn_ref, o_ref, h_ref = rest
            rn = lax.rsqrt(jnp.mean(y * y, axis=-1, keepdims=True) + EPS)
            h_ref[...] = ((y * rn) * gn_ref[...]).astype(BF16)
        else:
            (o_ref,) = rest
        o_ref[...] = y

    outs = pl.pallas_call(
        body, name=name, grid=(t_dim // tr,),
        in_specs=[_row(tr, d), _row(tr, d), _vec(d)] + ([_vec(d)] if chained else []),
        out_specs=[_row(tr, d)] + ([_row(tr, d)] if chained else []),
        out_shape=[jax.ShapeDtypeStruct((t_dim, d), F32)] + ([jax.ShapeDtypeStruct((t_dim, d), BF16)] if chained else []),
        compiler_params=_params("parallel"),
    )(x, f, g.reshape(1, d), *([g_next.reshape(1, d)] if chained else []))
    return (outs[0], outs[1]) if chained else (outs[0], None)


def rms_bwd(f, g, dy, weight, resid, out_dtype, name, deps=()):
    t_dim, d = f.shape
    tr = _pick(t_dim, ROW_TILES)
    has_resid = resid is not None

    def body(*refs):
        o_ref, dg_ref = refs[-2:]
        if has_resid:
            f_ref, g_ref, dy_ref, res_ref = refs[:4]
        else:
            f_ref, g_ref, dy_ref = refs[:3]

        @pl.when(pl.program_id(0) == 0)
        def _():
            dg_ref[...] = jnp.zeros_like(dg_ref)

        fv = f_ref[...]
        r = lax.rsqrt(jnp.mean(fv * fv, axis=-1, keepdims=True) + EPS)
        n = fv * r
        dyw = dy_ref[...] * weight
        dn = dyw * g_ref[...]
        df = r * (dn - n * jnp.mean(dn * n, axis=-1, keepdims=True))
        if has_resid:
            df = df + res_ref[...]
        o_ref[...] = df.astype(o_ref.dtype)
        dg_ref[...] += jnp.sum(dyw * n, axis=0, keepdims=True)

    ins = [f, g.reshape(1, d), dy] + ([resid] if has_resid else []) + list(deps)
    in_specs = [_row(tr, d), _vec(d), _row(tr, d)] + ([_row(tr, d)] if has_resid else []) + [ANY_SPEC] * len(deps)
    out, dg = pl.pallas_call(
        body, name=name, grid=(t_dim // tr,), in_specs=in_specs, out_specs=[_row(tr, d), _vec(d)],
        out_shape=[jax.ShapeDtypeStruct((t_dim, d), out_dtype), jax.ShapeDtypeStruct((1, d), F32)],
        compiler_params=_params("arbitrary"),
    )(*ins)
    return out, dg.reshape(d)


def loss_grad(y, target, name):
    t_dim, d = y.shape
    tr = _pick(t_dim, ROW_TILES)
    inv_d = 1.0 / d

    def body(y_ref, t_ref, dy_ref, s_ref):
        @pl.when(pl.program_id(0) == 0)
        def _():
            s_ref[...] = jnp.zeros_like(s_ref)

        e = y_ref[...] - t_ref[...]
        dy_ref[...] = e * inv_d
        s_ref[...] += jnp.sum(e * e)

    dy, s = pl.pallas_call(
        body, name=name, grid=(t_dim // tr,), in_specs=[_row(tr, d), _row(tr, d)],
        out_specs=[_row(tr, d), pl.BlockSpec((SUBLANE, LANE), lambda i: (0, 0))],
        out_shape=[jax.ShapeDtypeStruct((t_dim, d), F32), jax.ShapeDtypeStruct((SUBLANE, LANE), F32)],
        compiler_params=_params("arbitrary"),
    )(y, target)
    return dy, s[0, 0]


def swiglu_bwd(gate, up, da, name):
    t_dim, f = gate.shape
    tr = _pick(t_dim, ROW_TILES)
    tc = _pick(f, (512, 256, 128))

    def body(g_ref, u_ref, da_ref, o_ref):
        for c in range(f // tc):
            lo = c * tc
            g = g_ref[:, lo:lo + tc].astype(F32)
            u = u_ref[:, lo:lo + tc].astype(F32)
            da = da_ref[:, lo:lo + tc].astype(F32)
            s = _sigmoid(g)
            o_ref[:, lo:lo + tc] = (da * u * (s * (1.0 + g * (1.0 - s)))).astype(o_ref.dtype)
            o_ref[:, f + lo:f + lo + tc] = (da * (g * s)).astype(o_ref.dtype)

    return pl.pallas_call(
        body, name=name, grid=(t_dim // tr,), in_specs=[_row(tr, f)] * 3, out_specs=_row(tr, 2 * f),
        out_shape=jax.ShapeDtypeStruct((t_dim, 2 * f), BF16), compiler_params=_params("parallel"),
    )(gate, up, da)


def gate_fwd(z, ya, yb, off_a, off_b, name):
    t_dim, d = ya.shape
    tr = _pick(t_dim, (512,) + ROW_TILES)
    tc = math.gcd(math.gcd(off_a, off_b), _pick(d, (512, 256, 128)))
    ja, jb = off_a // tc, off_b // tc

    def body(ga_ref, gb_ref, ya_ref, yb_ref, m_ref):
        m = _sigmoid(ga_ref[...]) * ya_ref[...].astype(F32) + _sigmoid(gb_ref[...]) * yb_ref[...].astype(F32)
        m_ref[...] = m.astype(m_ref.dtype)

    blk = pl.BlockSpec((tr, tc), lambda i, j: (i, j))
    return pl.pallas_call(
        body, name=name, grid=(t_dim // tr, d // tc),
        in_specs=[pl.BlockSpec((tr, tc), lambda i, j: (i, j + ja)), pl.BlockSpec((tr, tc), lambda i, j: (i, j + jb)),
                  blk, blk],
        out_specs=blk, out_shape=jax.ShapeDtypeStruct((t_dim, d), BF16),
        compiler_params=_params("parallel", "parallel"),
    )(z, z, ya, yb)


def gate_bwd(z, ya, yb, dm, off_a, off_b, name):
    t_dim, d = ya.shape
    tr = _pick(t_dim, (512,) + ROW_TILES)
    tc = math.gcd(math.gcd(off_a, off_b), _pick(d, (512, 256, 128)))
    ja, jb = off_a // tc, off_b // tc

    def body(ga_ref, gb_ref, ya_ref, yb_ref, dm_ref, dya_ref, dyb_ref, dga_ref, dgb_ref):
        dm = dm_ref[...].astype(F32)
        sa = _sigmoid(ga_ref[...])
        sb = _sigmoid(gb_ref[...])
        dya_ref[...] = (dm * sa).astype(BF16)
        dyb_ref[...] = (dm * sb).astype(BF16)
        dga_ref[...] = (dm * ya_ref[...].astype(F32) * (sa * (1.0 - sa))).astype(BF16)
        dgb_ref[...] = (dm * yb_ref[...].astype(F32) * (sb * (1.0 - sb))).astype(BF16)

    blk = pl.BlockSpec((tr, tc), lambda i, j: (i, j))
    out = jax.ShapeDtypeStruct((t_dim, d), BF16)
    return pl.pallas_call(
        body, name=name, grid=(t_dim // tr, d // tc),
        in_specs=[pl.BlockSpec((tr, tc), lambda i, j: (i, j + ja)), pl.BlockSpec((tr, tc), lambda i, j: (i, j + jb)),
                  blk, blk, blk],
        out_specs=[blk] * 4, out_shape=[out] * 4, compiler_params=_params("parallel", "parallel"),
    )(z, z, ya, yb, dm)


def _window_sums(e, n_rows, forward):
    def shifted(v, k):
        return pltpu.roll(v, (n_rows - k) if forward else k, 0)

    s2 = e + shifted(e, 1)
    s4 = s2 + shifted(s2, 2)
    s8 = s4 + shifted(s4, 4)
    s16 = s8 + shifted(s8, 8)
    return (s2, s4, s8, s16)


def _pool_rows(t_dim):
    return _pick(t_dim, (256, 128, 64, 32, 16))


def pool_fwd(z, w_group, scale, pw, name):
    t_dim = z.shape[0]
    n_groups, c, _ = w_group.shape
    tr = _pool_rows(t_dim)
    per = tr // POOL_HALO

    def body(cur_ref, prev_ref, w_ref, scale_ref, d_ref, e_ref, yp_ref):
        i = pl.program_id(0)
        cur = cur_ref[...]
        prev = jnp.where(i > 0, prev_ref[...], 0.0)
        ext = jnp.concatenate([prev, cur], axis=0)
        sums = _window_sums(ext, tr + POOL_HALO, forward=False)
        pos = (i * tr + 1 + lax.broadcasted_iota(jnp.int32, (tr, 1), 0)).astype(F32)
        for g, w in enumerate(POOL_WINDOWS):
            cols = slice(g * c, (g + 1) * c)
            cnt = jnp.minimum(pos, float(w))
            d = (sums[g][POOL_HALO:, cols] / cnt - cur[:, cols]).astype(BF16)
            e = jnp.dot(d, w_ref[g], preferred_element_type=F32)
            d_ref[:, cols] = d
            e_ref[:, cols] = e.astype(BF16)
            yp_ref[:, cols] = (e * scale_ref[:, cols]).astype(BF16)

    out = jax.ShapeDtypeStruct((t_dim, pw), BF16)
    return pl.pallas_call(
        body, name=name, grid=(t_dim // tr,),
        in_specs=[_row(tr, pw), pl.BlockSpec((POOL_HALO, pw), lambda i: (jnp.maximum(i * per - 1, 0), 0)),
                  pl.BlockSpec((n_groups, c, c), lambda i: (0, 0, 0)), _vec(pw)],
        out_specs=[_row(tr, pw)] * 3, out_shape=[out] * 3, compiler_params=_params("parallel"),
    )(z, z, w_group, scale.reshape(1, pw))


def pool_bwd(dyp, e, d, w_group, scale, name):
    t_dim, pw = dyp.shape
    n_groups, c, _ = w_group.shape
    tr = _pool_rows(t_dim)
    per = tr // POOL_HALO
    n_tiles = t_dim // tr
    last_halo = t_dim // POOL_HALO - 1
    nt_dims = (((1,), (1,)), ((), ()))
    tn_dims = (((0,), (0,)), ((), ()))

    def body(dyp_ref, nxt_ref, e_ref, d_ref, w_ref, scale_ref, dp_ref, dw_ref, dscale_ref):
        i = pl.program_id(0)

        @pl.when(i == 0)
        def _():
            dw_ref[...] = jnp.zeros_like(dw_ref)
            dscale_ref[...] = jnp.zeros_like(dscale_ref)

        dyp_v = dyp_ref[...].astype(F32)
        dscale_ref[...] += jnp.sum(dyp_v * e_ref[...].astype(F32), axis=0, keepdims=True)
        de_cur = dyp_v * scale_ref[...]
        de_nxt = jnp.where(i < n_tiles - 1, nxt_ref[...].astype(F32) * scale_ref[...], 0.0)
        de = jnp.concatenate([de_cur, de_nxt], axis=0).astype(BF16)
        pos = (i * tr + 1 + lax.broadcasted_iota(jnp.int32, (tr + POOL_HALO, 1), 0)).astype(F32)
        for g, w in enumerate(POOL_WINDOWS):
            cols = slice(g * c, (g + 1) * c)
            de_g = de[:, cols]
            dd = lax.dot_general(de_g, w_ref[g], nt_dims, preferred_element_type=F32)
            dw_ref[g] += lax.dot_general(d_ref[:, cols], de_g[:tr], tn_dims, preferred_element_type=F32)
            q = dd / jnp.minimum(pos, float(w))
            win = _window_sums(q, tr + POOL_HALO, forward=True)[g]
            dp_ref[:, cols] = (win[:tr] - dd[:tr]).astype(BF16)

    return pl.pallas_call(
        body, name=name, grid=(n_tiles,),
        in_specs=[_row(tr, pw), pl.BlockSpec((POOL_HALO, pw), lambda i: (jnp.minimum((i + 1) * per, last_halo), 0)),
                  _row(tr, pw), _row(tr, pw), pl.BlockSpec((n_groups, c, c), lambda i: (0, 0, 0)), _vec(pw)],
        out_specs=[_row(tr, pw), pl.BlockSpec((n_groups, c, c), lambda i: (0, 0, 0)), _vec(pw)],
        out_shape=[jax.ShapeDtypeStruct((t_dim, pw), BF16), jax.ShapeDtypeStruct((n_groups, c, c), F32),
                   jax.ShapeDtypeStruct((1, pw), F32)],
        compiler_params=_params("arbitrary"),
    )(dyp, dyp, e, d, w_group, scale.reshape(1, pw))


def _sgu_rows(t_dim, chunk):
    return chunk * _pick(t_dim // chunk, (2, 1))


def _tril(chunk):
    return lax.broadcasted_iota(jnp.int32, (chunk, chunk), 0) >= lax.broadcasted_iota(jnp.int32, (chunk, chunk), 1)


def sgu_fwd(z, gain, w_s, b_s, pw, sw, name):
    t_dim = z.shape[0]
    n_heads, chunk, _ = w_s.shape
    hd = sw // n_heads
    tr = _sgu_rows(t_dim, chunk)
    ju = pw // sw

    def body(u_ref, v_ref, gain_ref, w_ref, bt_ref, sg_ref):
        ug, _ = _gelu(u_ref[...])
        vg, _ = _gelu(v_ref[...])
        r = lax.rsqrt(jnp.mean(vg * vg, axis=-1, keepdims=True) + EPS)
        vn = ((vg * r) * gain_ref[...]).astype(BF16)
        tri = _tril(chunk)
        for h in range(n_heads):
            wm = jnp.where(tri, w_ref[h], 0.0).astype(BF16)
            cols = slice(h * hd, (h + 1) * hd)
            for ch in range(tr // chunk):
                rows = slice(ch * chunk, (ch + 1) * chunk)
                s = jnp.dot(wm, vn[rows, cols], preferred_element_type=F32) + bt_ref[:, h:h + 1]
                sg_ref[rows, cols] = (ug[rows, cols] * s).astype(BF16)

    return pl.pallas_call(
        body, name=name, grid=(t_dim // tr,),
        in_specs=[pl.BlockSpec((tr, sw), lambda i: (i, ju)), pl.BlockSpec((tr, sw), lambda i: (i, ju + 1)), _vec(sw),
                  pl.BlockSpec((n_heads, chunk, chunk), lambda i: (0, 0, 0)),
                  pl.BlockSpec((chunk, n_heads), lambda i: (0, 0))],
        out_specs=_row(tr, sw), out_shape=jax.ShapeDtypeStruct((t_dim, sw), BF16),
        compiler_params=_params("parallel"),
    )(z, z, gain.reshape(1, sw), w_s, b_s.T)


def sgu_bwd(z, dsg, gain, w_s, b_s, pw, sw, name):
    t_dim = z.shape[0]
    n_heads, chunk, _ = w_s.shape
    hd = sw // n_heads
    tr = _sgu_rows(t_dim, chunk)
    ju = pw // sw
    nt_dims = (((1,), (1,)), ((), ()))
    tn_dims = (((0,), (0,)), ((), ()))

    def body(u_ref, v_ref, dsg_ref, gain_ref, w_ref, bt_ref, du_ref, dv_ref, dw_ref, dbt_ref, dgain_ref,
             dvn_ref, dug_ref):
        @pl.when(pl.program_id(0) == 0)
        def _():
            dw_ref[...] = jnp.zeros_like(dw_ref)
            dbt_ref[...] = jnp.zeros_like(dbt_ref)
            dgain_ref[...] = jnp.zeros_like(dgain_ref)

        u = u_ref[...]
        v = v_ref[...]
        ug, tu = _gelu(u)
        vg, tv = _gelu(v)
        r = lax.rsqrt(jnp.mean(vg * vg, axis=-1, keepdims=True) + EPS)
        n = vg * r
        gain_v = gain_ref[...]
        vn = (n * gain_v).astype(BF16)
        dsg_v = dsg_ref[...].astype(F32)
        tri = _tril(chunk)
        for h in range(n_heads):
            wm = jnp.where(tri, w_ref[h], 0.0).astype(BF16)
            cols = slice(h * hd, (h + 1) * hd)
            for ch in range(tr // chunk):
                rows = slice(ch * chunk, (ch + 1) * chunk)
                vn_b = vn[rows, cols]
                s = jnp.dot(wm, vn_b, preferred_element_type=F32) + bt_ref[:, h:h + 1]
                dsg_b = dsg_v[rows, cols]
                dug_ref[rows, cols] = dsg_b * s
                ds = dsg_b * ug[rows, cols]
                ds_b = ds.astype(BF16)
                dw_ref[h] += jnp.where(tri, lax.dot_general(ds_b, vn_b, nt_dims, preferred_element_type=F32), 0.0)
                dbt_ref[:, h:h + 1] += jnp.sum(ds, axis=1, keepdims=True)
                dvn_ref[rows, cols] = lax.dot_general(wm, ds_b, tn_dims, preferred_element_type=F32)
        dvn = dvn_ref[...]
        dgain_ref[...] += jnp.sum(dvn * n, axis=0, keepdims=True)
        dn = dvn * gain_v
        dvg = r * (dn - n * jnp.mean(dn * n, axis=-1, keepdims=True))
        dv_ref[...] = (dvg * _gelu_grad(v, tv)).astype(BF16)
        du_ref[...] = (dug_ref[...] * _gelu_grad(u, tu)).astype(BF16)

    full_w = pl.BlockSpec((n_heads, chunk, chunk), lambda i: (0, 0, 0))
    full_b = pl.BlockSpec((chunk, n_heads), lambda i: (0, 0))
    du, dv, dw, dbt, dgain = pl.pallas_call(
        body, name=name, grid=(t_dim // tr,),
        in_specs=[pl.BlockSpec((tr, sw), lambda i: (i, ju)), pl.BlockSpec((tr, sw), lambda i: (i, ju + 1)),
                  _row(tr, sw), _vec(sw), full_w, full_b],
        out_specs=[_row(tr, sw), _row(tr, sw), full_w, full_b, _vec(sw)],
        out_shape=[jax.ShapeDtypeStruct((t_dim, sw), BF16), jax.ShapeDtypeStruct((t_dim, sw), BF16),
                   jax.ShapeDtypeStruct((n_heads, chunk, chunk), F32), jax.ShapeDtypeStruct((chunk, n_heads), F32),
                   jax.ShapeDtypeStruct((1, sw), F32)],
        scratch_shapes=[pltpu.VMEM((tr, sw), F32), pltpu.VMEM((tr, sw), F32)],
        compiler_params=_params("arbitrary"),
    )(z, z, dsg, gain.reshape(1, sw), w_s, b_s.T)
    return du, dv, dw, dbt.T, dgain.reshape(sw)


HBM_SPEC = pl.BlockSpec(memory_space=pltpu.HBM)
SEM_SPEC = pl.BlockSpec(memory_space=pltpu.SEMAPHORE)
ANY_SPEC = pl.BlockSpec(memory_space=pl.ANY)
DATAFLOW = pltpu.SideEffectType.DATAFLOW_SIDE_EFFECTING


def _hbm(a):
    return pltpu.with_memory_space_constraint(a, pltpu.HBM)


def _slot(ref, axis, k, n):
    idx = [slice(None)] * len(ref.shape)
    idx[axis] = pl.ds(k * n, n)
    return ref.at[tuple(idx)]


def _slot_half(ref, axis, k, half):
    idx = [slice(None)] * len(ref.shape)
    n = ref.shape[axis] // N_CHIPS
    if axis == 0:
        idx[0] = pl.ds(k * n + half * (n // 2), n // 2)
    else:
        idx[axis] = pl.ds(k * n, n)
        idx[0] = pl.ds(half * (ref.shape[0] // 2), ref.shape[0] // 2)
    return ref.at[tuple(idx)]


def _chip_of(k, core):
    return (k // 2, k % 2, core)


def _my_chip():
    return 2 * lax.axis_index("x") + lax.axis_index("y")


class Exchange:
    def __init__(self, gather, axes):
        self.gather, self.axes = gather, axes
        self.per = 1 if gather else 2

    def bufs(self, t, refs):
        return refs[t * self.per:(t + 1) * self.per]

    def branches(self, me, core):
        if self.gather:
            return [(k, h, (me == k) & (core == h)) for k in range(N_CHIPS) for h in range(2)]
        return [(k, None, me == k) for k in range(N_CHIPS)]

    def views(self, t, bufs, src_chip, dst_chip, half):
        ax = self.axes[t]
        if self.gather:
            slot = _slot_half(bufs[0], ax, src_chip, half)
            return slot, slot
        return _slot(bufs[0], ax, dst_chip, bufs[0].shape[ax] // N_CHIPS), bufs[1].at[src_chip]


def exchange_start(bufs, ex, groups, name, deps=()):
    nb, ng = len(bufs), len(groups)

    def body(*refs):
        ins, outs = refs[:nb], refs[nb + len(deps):]
        sems, token = outs[:2 * ng], outs[-1]
        core = lax.axis_index("c")
        me = _my_chip()
        for k, half, mine in ex.branches(me, core):
            @pl.when(mine)
            def _(k=k, half=half):
                for gi, group in enumerate(groups):
                    for ti, t in enumerate(group):
                        for j in range(N_CHIPS):
                            if j != k:
                                src, dst = ex.views(t, ex.bufs(t, ins), k, j, half)
                                pltpu.make_async_remote_copy(
                                    src_ref=src, dst_ref=dst, send_sem=sems[2 * gi].at[ti * N_CHIPS + j],
                                    recv_sem=sems[2 * gi + 1].at[ti * N_CHIPS + k], device_id=_chip_of(j, core),
                                    device_id_type=MESH).start()
        token[...] = jnp.zeros_like(token)

    sem_shapes = []
    for group in groups:
        sem_shapes += [pltpu.SemaphoreType.DMA((len(group) * N_CHIPS,))] * 2
    thru = [pltpu.HBM(a.shape, a.dtype) for a in bufs]
    outs = pl.pallas_call(
        body, name=name, out_shape=sem_shapes + thru + [jax.ShapeDtypeStruct((SUBLANE, LANE), F32)],
        in_specs=[HBM_SPEC] * nb + [ANY_SPEC] * len(deps),
        out_specs=[SEM_SPEC] * (2 * ng) + [HBM_SPEC] * nb + [pl.BlockSpec(memory_space=pltpu.VMEM)],
        input_output_aliases={i: 2 * ng + i for i in range(nb)},
        compiler_params=pltpu.CompilerParams(has_side_effects=DATAFLOW),
    )(*[_hbm(a) for a in bufs], *deps)
    sems = [(outs[2 * gi], outs[2 * gi + 1]) for gi in range(ng)]
    return sems, outs[2 * ng:2 * ng + nb], outs[-1]


def exchange_wait(bufs, tensors, ex, sems, after, name):
    nb = len(bufs)
    send_sems, recv_sems = sems

    def body(*refs):
        ins, send_ref, recv_ref = refs[:nb], refs[nb], refs[nb + 1]
        core = lax.axis_index("c")
        me = _my_chip()
        for k, half, mine in ex.branches(me, core):
            @pl.when(mine)
            def _(k=k, half=half):
                for ti, t in enumerate(tensors):
                    for j in range(N_CHIPS):
                        if j != k:
                            src, _ = ex.views(t, ex.bufs(ti, ins), k, j, half)
                            _, dst = ex.views(t, ex.bufs(ti, ins), j, k, half)
                            copy = pltpu.make_async_remote_copy(
                                src_ref=src, dst_ref=dst, send_sem=send_ref.at[ti * N_CHIPS + j],
                                recv_sem=recv_ref.at[ti * N_CHIPS + j],
                                device_id=_chip_of(j, core), device_id_type=MESH)
                            copy.wait_send()
                            copy.wait_recv()

    return pl.pallas_call(
        body, name=name, out_shape=[pltpu.HBM(a.shape, a.dtype) for a in bufs],
        in_specs=[HBM_SPEC] * nb + [SEM_SPEC, SEM_SPEC] + [ANY_SPEC] * len(after),
        out_specs=[HBM_SPEC] * nb, input_output_aliases={i: i for i in range(nb)},
        compiler_params=pltpu.CompilerParams(has_side_effects=DATAFLOW),
    )(*bufs, send_sems, recv_sems, *after)


def share_halves(bufs, axes, name):
    nb = len(bufs)

    def body(*refs):
        outs = refs[nb:2 * nb]
        send_sems, recv_sems = refs[2 * nb:]
        core = lax.axis_index("c")
        me = _my_chip()
        sibling = (lax.axis_index("x"), lax.axis_index("y"), 1 - core)
        for k in range(N_CHIPS):
            for half in range(2):
                @pl.when((me == k) & (core == half))
                def _(k=k, half=half):
                    def copy(t, j, h):
                        part = _slot_half(outs[t], axes[t], j, h)
                        return pltpu.make_async_remote_copy(
                            src_ref=part, dst_ref=part, send_sem=send_sems.at[t * N_CHIPS + j],
                            recv_sem=recv_sems.at[t * N_CHIPS + j], device_id=sibling, device_id_type=MESH)

                    pairs = [(t, j) for t in range(nb) for j in range(N_CHIPS) if j != k]
                    for t, j in pairs:
                        copy(t, j, half).start()
                    for t, j in pairs:
                        copy(t, j, 1 - half).wait_recv()
                    for t, j in pairs:
                        copy(t, j, half).wait_send()

    return pl.pallas_call(
        body, name=name, out_shape=[jax.ShapeDtypeStruct(a.shape, a.dtype) for a in bufs],
        in_specs=[HBM_SPEC] * nb, out_specs=[HBM_SPEC] * nb, input_output_aliases={i: i for i in range(nb)},
        scratch_shapes=[pltpu.SemaphoreType.DMA((nb * N_CHIPS,)), pltpu.SemaphoreType.DMA((nb * N_CHIPS,))],
        compiler_params=pltpu.CompilerParams(has_side_effects=True),
    )(*bufs)


def _share_copy(bufs, axes, send_sems, recv_sems, sibling, t, j, half):
    part = _slot_half(bufs[t], axes[t], j, half)
    return pltpu.make_async_remote_copy(src_ref=part, dst_ref=part, send_sem=send_sems.at[t * N_CHIPS + j],
                                        recv_sem=recv_sems.at[t * N_CHIPS + j], device_id=sibling, device_id_type=MESH)


def share_start(bufs, axes, name):
    nb = len(bufs)

    def body(*refs):
        ins, outs = refs[:nb], refs[nb:]
        send_sems, recv_sems, token = outs[0], outs[1], outs[-1]
        core = lax.axis_index("c")
        me = _my_chip()
        sibling = (lax.axis_index("x"), lax.axis_index("y"), 1 - core)
        for k in range(N_CHIPS):
            for half in range(2):
                @pl.when((me == k) & (core == half))
                def _(k=k, half=half):
                    for t in range(nb):
                        for j in range(N_CHIPS):
                            if j != k:
                                _share_copy(ins, axes, send_sems, recv_sems, sibling, t, j, half).start()
        token[...] = jnp.zeros_like(token)

    outs = pl.pallas_call(
        body, name=name,
        out_shape=[pltpu.SemaphoreType.DMA((nb * N_CHIPS,))] * 2 + [pltpu.HBM(a.shape, a.dtype) for a in bufs]
        + [jax.ShapeDtypeStruct((SUBLANE, LANE), F32)],
        in_specs=[HBM_SPEC] * nb,
        out_specs=[SEM_SPEC, SEM_SPEC] + [HBM_SPEC] * nb + [pl.BlockSpec(memory_space=pltpu.VMEM)],
        input_output_aliases={i: 2 + i for i in range(nb)},
        compiler_params=pltpu.CompilerParams(has_side_effects=DATAFLOW),
    )(*[_hbm(a) for a in bufs])
    return outs[0], outs[1], outs[2:2 + nb], outs[-1]


def share_wait(send_sems, recv_sems, bufs, axes, after, name):
    nb = len(bufs)

    def body(*refs):
        ins, send_ref, recv_ref = refs[:nb], refs[nb], refs[nb + 1]
        core = lax.axis_index("c")
        me = _my_chip()
        sibling = (lax.axis_index("x"), lax.axis_index("y"), 1 - core)
        for k in range(N_CHIPS):
            for half in range(2):
                @pl.when((me == k) & (core == half))
                def _(k=k, half=half):
                    for t in range(nb):
                        for j in range(N_CHIPS):
                            if j != k:
                                _share_copy(ins, axes, send_ref, recv_ref, sibling, t, j, 1 - half).wait_recv()
                                _share_copy(ins, axes, send_ref, recv_ref, sibling, t, j, half).wait_send()

    return pl.pallas_call(
        body, name=name, out_shape=[pltpu.HBM(a.shape, a.dtype) for a in bufs],
        in_specs=[HBM_SPEC] * nb + [SEM_SPEC, SEM_SPEC] + [ANY_SPEC] * len(after),
        out_specs=[HBM_SPEC] * nb, input_output_aliases={i: i for i in range(nb)},
        compiler_params=pltpu.CompilerParams(has_side_effects=DATAFLOW),
    )(*bufs, send_sems, recv_sems, *after)


def swap_start(arrs, name):
    nt = len(arrs)

    def body(*refs):
        ins, lands, outs = refs[:nt], refs[nt:2 * nt], refs[2 * nt:]
        send_sems, recv_sems, token = outs[0], outs[1], outs[-1]
        sibling = (lax.axis_index("x"), lax.axis_index("y"), 1 - lax.axis_index("c"))
        for t in range(nt):
            pltpu.make_async_remote_copy(src_ref=ins[t], dst_ref=lands[t], send_sem=send_sems.at[t],
                                         recv_sem=recv_sems.at[t], device_id=sibling, device_id_type=MESH).start()
        token[...] = jnp.zeros_like(token)

    thru = [pltpu.HBM(a.shape, a.dtype) for a in arrs] * 2
    outs = pl.pallas_call(
        body, name=name,
        out_shape=[pltpu.SemaphoreType.DMA((nt,))] * 2 + thru + [jax.ShapeDtypeStruct((SUBLANE, LANE), F32)],
        in_specs=[HBM_SPEC] * (2 * nt),
        out_specs=[SEM_SPEC, SEM_SPEC] + [HBM_SPEC] * (2 * nt) + [pl.BlockSpec(memory_space=pltpu.VMEM)],
        input_output_aliases={i: 2 + i for i in range(2 * nt)},
        compiler_params=pltpu.CompilerParams(has_side_effects=DATAFLOW),
    )(*[_hbm(a) for a in arrs], *[_hbm(lax.empty(a.shape, a.dtype)) for a in arrs])
    return outs[0], outs[1], outs[2:2 + nt], outs[2 + nt:2 + 2 * nt], outs[-1]


def swap_wait(send_sems, recv_sems, arrs, lands, after, name):
    nt = len(arrs)

    def body(*refs):
        ins, land_refs, send_ref, recv_ref = refs[:nt], refs[nt:2 * nt], refs[2 * nt], refs[2 * nt + 1]
        sibling = (lax.axis_index("x"), lax.axis_index("y"), 1 - lax.axis_index("c"))
        for t in range(nt):
            copy = pltpu.make_async_remote_copy(src_ref=ins[t], dst_ref=land_refs[t], send_sem=send_ref.at[t],
                                                recv_sem=recv_ref.at[t], device_id=sibling,
                                                device_id_type=MESH)
            copy.wait_send()
            copy.wait_recv()

    outs = pl.pallas_call(
        body, name=name, out_shape=[pltpu.HBM(a.shape, a.dtype) for a in list(arrs) + list(lands)],
        in_specs=[HBM_SPEC] * (2 * nt) + [SEM_SPEC, SEM_SPEC] + [ANY_SPEC] * len(after),
        out_specs=[HBM_SPEC] * (2 * nt), input_output_aliases={i: i for i in range(2 * nt)},
        compiler_params=pltpu.CompilerParams(has_side_effects=DATAFLOW),
    )(*arrs, *lands, send_sems, recv_sems, *after)
    return outs[:nt], outs[nt:]


def place_shard(w_stack, layer, axis, me, name):
    _, a_dim, r, c = w_stack.shape
    tr, tc = _stream_tiles(r, c)
    nr, nc = r // tr, c // tc
    full = (a_dim, r * N_CHIPS, c) if axis == 1 else (a_dim, r, c * N_CHIPS)

    def body(me_ref, w_ref, o_ref):
        o_ref[...] = w_ref[...].astype(BF16)

    def own_map(a, i, j, me_ref):
        return (a, me_ref[0] * nr + i, j) if axis == 1 else (a, i, me_ref[0] * nc + j)

    return pl.pallas_call(
        body, name=name, out_shape=jax.ShapeDtypeStruct(full, BF16),
        grid_spec=pltpu.PrefetchScalarGridSpec(
            num_scalar_prefetch=1, grid=(a_dim, nr, nc),
            in_specs=[pl.BlockSpec((None, None, tr, tc), lambda a, i, j, me_ref: (layer, a, i, j))],
            out_specs=pl.BlockSpec((None, tr, tc), own_map)),
        compiler_params=_params("parallel", "parallel", "parallel"),
    )(me, w_stack)


def sum_pieces(stack, full, axis, me, name):
    _, a_dim, r, c = stack.shape
    tr, tc = _stream_tiles(r, c)
    nr, nc = r // tr, c // tc

    def body(me_ref, own_ref, s1_ref, s2_ref, s3_ref, o_ref):
        acc = own_ref[...].astype(F32)
        for ref in (s1_ref, s2_ref, s3_ref):
            acc = acc + ref[...].astype(F32)
        o_ref[...] = acc.astype(BF16)

    def own_map(a, i, j, me_ref):
        return (a, me_ref[0] * nr + i, j) if axis == 1 else (a, i, me_ref[0] * nc + j)

    def from_chip(step):
        return pl.BlockSpec((None, None, tr, tc), lambda a, i, j, me_ref: ((me_ref[0] + step) % N_CHIPS, a, i, j))

    return pl.pallas_call(
        body, name=name, out_shape=jax.ShapeDtypeStruct((a_dim, r, c), BF16),
        grid_spec=pltpu.PrefetchScalarGridSpec(
            num_scalar_prefetch=1, grid=(a_dim, nr, nc),
            in_specs=[pl.BlockSpec((None, tr, tc), own_map), from_chip(1), from_chip(2), from_chip(3)],
            out_specs=pl.BlockSpec((None, tr, tc), lambda a, i, j, me_ref: (a, i, j))),
        compiler_params=_params("parallel", "parallel", "parallel"),
    )(me, full, stack, stack, stack)


def all_reduce_small(x, name):
    rows, lanes = x.shape

    def body(x_ref, sum_ref, gath_ref, send_sems, recv_sems, local_sem):
        cx, cy, cc = lax.axis_index("x"), lax.axis_index("y"), lax.axis_index("c")
        me, sibling = (cx, cy, cc), (cx, cy, 1 - cc)
        chips = [(1 - cx, cy), (cx, 1 - cy), (1 - cx, 1 - cy)]

        def block(px, py, pc):
            return gath_ref.at[pl.ds(pl.multiple_of((4 * px + 2 * py + pc) * rows, SUBLANE), rows), :]

        def copy(k, blk, to, src=None):
            return pltpu.make_async_remote_copy(
                src_ref=block(*blk) if src is None else src, dst_ref=block(*blk),
                send_sem=send_sems.at[k], recv_sem=recv_sems.at[k], device_id=to, device_id_type=MESH)

        mine = pltpu.make_async_copy(x_ref, block(*me), local_sem)
        mine.start()
        first = [copy(0, me, sibling, src=x_ref)]
        first += [copy(1 + j, me, (*chip, cc), src=x_ref) for j, chip in enumerate(chips)]
        for cp in first:
            cp.start()
        passed = [copy(4 + j, (*chip, cc), sibling) for j, chip in enumerate(chips)]
        for j, chip in enumerate(chips):
            copy(1 + j, (*chip, cc), me).wait_recv()
            passed[j].start()
        copy(0, sibling, me).wait_recv()
        for j, chip in enumerate(chips):
            copy(4 + j, (*chip, 1 - cc), me).wait_recv()
        for cp in first + passed:
            cp.wait_send()
        mine.wait()
        acc = gath_ref[pl.ds(0, rows), :]
        for k in range(1, N_DEV):
            acc = acc + gath_ref[pl.ds(k * rows, rows), :]
        sum_ref[...] = acc

    return pl.pallas_call(
        body, name=name, out_shape=jax.ShapeDtypeStruct((rows, lanes), F32),
        in_specs=[pl.BlockSpec(memory_space=pltpu.VMEM)], out_specs=pl.BlockSpec(memory_space=pltpu.VMEM),
        scratch_shapes=[pltpu.VMEM((N_DEV * rows, lanes), F32), pltpu.SemaphoreType.DMA((7,)),
                        pltpu.SemaphoreType.DMA((7,)), pltpu.SemaphoreType.DMA],
        compiler_params=pltpu.CompilerParams(has_side_effects=True, vmem_limit_bytes=VMEM_LIMIT),
    )(x)


def _adamw(w, g, m, v):
    m = ADAM_B1 * m + (1.0 - ADAM_B1) * g
    v = ADAM_B2 * v + (1.0 - ADAM_B2) * (g * g)
    m_hat = m / (1.0 - ADAM_B1 ** ADAM_STEP)
    v_hat = v / (1.0 - ADAM_B2 ** ADAM_STEP)
    delta = -ADAM_LR * (m_hat / (jnp.sqrt(v_hat) + ADAM_EPS) + ADAM_WD * w)
    return delta, m, v


def adam_big(p_own, p_sib, w, m, v, layer, stacks, name):
    r, c = p_own.shape
    tr, tc = _stream_tiles(r, c)

    def body(p_ref, q_ref, w_ref, m_ref, v_ref, *rest):
        g_out, d_out, m_out, v_out = rest[4:]
        g = p_ref[...].astype(F32) + q_ref[...].astype(F32)
        delta, m_new, v_new = _adamw(w_ref[...], g, m_ref[...], v_ref[...])
        g_out[...] = g
        d_out[...] = delta
        m_out[...] = m_new
        v_out[...] = v_new

    flat = pl.BlockSpec((tr, tc), lambda i, j: (i, j))
    layered = pl.BlockSpec((None, tr, tc), lambda i, j: (layer, i, j))
    anyspace = pl.BlockSpec(memory_space=pl.ANY)
    out = jax.ShapeDtypeStruct(w.shape, F32)
    return pl.pallas_call(
        body, name=name, grid=(r // tr, c // tc),
        in_specs=[flat, flat, layered, layered, layered] + [anyspace] * 4,
        out_specs=[layered] * 4, out_shape=[out] * 4, input_output_aliases={5: 0, 6: 1, 7: 2, 8: 3},
        compiler_params=_params("parallel", "parallel"),
    )(p_own, p_sib, w, m, v, *stacks)


def adam_small(g, w, m, v, name):
    rows, lanes = g.shape
    tr = _pick(rows, (512,) + ROW_TILES)

    def body(g_ref, w_ref, m_ref, v_ref, d_out, m_out, v_out):
        delta, m_new, v_new = _adamw(w_ref[...], g_ref[...], m_ref[...], v_ref[...])
        d_out[...] = delta
        m_out[...] = m_new
        v_out[...] = v_new

    out = jax.ShapeDtypeStruct((rows, lanes), F32)
    return pl.pallas_call(
        body, name=name, grid=(rows // tr,), in_specs=[_row(tr, lanes)] * 4, out_specs=[_row(tr, lanes)] * 3,
        out_shape=[out] * 3, compiler_params=_params("parallel"),
    )(g, w, m, v)


def ffn_fwd(x, h, w_up, w_down, g_post, g_next, tag, deps=()):
    gate, up, a = matmul_swiglu(h, w_up, name=f"{tag}_up", deps=deps)
    f = matmul(a, w_down, out_dtype=F32, name=f"{tag}_down")
    y, h_next = res_rms_fwd(x, f, g_post, MACARON_WEIGHT, g_next, name=f"{tag}_res_fwd")
    return y, h_next, (x, h, gate, up, a, f)


def ffn_bwd(dy, saved, g_pre, w_up, w_down, g_post, tag, deps=()):
    x, h, gate, up, a, f = saved
    df, dg_post = rms_bwd(f, g_post, dy, MACARON_WEIGHT, None, BF16, name=f"{tag}_res_bwd", deps=deps)
    da = matmul(df, w_down, tb=True, out_dtype=BF16, name=f"{tag}_down_dx")
    dw_down = matmul(a, df, ta=True, out_dtype=BF16, name=f"{tag}_down_dw")
    dz = swiglu_bwd(gate, up, da, name=f"{tag}_swiglu_bwd")
    dh = matmul(dz, w_up, tb=True, out_dtype=F32, name=f"{tag}_up_dx")
    dw_up = matmul(h, dz, ta=True, out_dtype=BF16, name=f"{tag}_up_dw")
    dx, dg_pre = rms_bwd(x, g_pre, dh, 1.0, dy, F32, name=f"{tag}_rms_bwd")
    return dx, dw_up, dw_down, dg_pre, dg_post


def mixer_fwd(x, h, wt, sm, g_next, dims, deps=()):
    pw, sw, d = dims
    z = matmul(h, wt['w_in'], out_dtype=F32, name="mix_in", deps=deps)
    dd, e, yp = pool_fwd(z, wt['pool_group_w'], sm['pool_scale'], pw, name="mix_pool_fwd")
    ya = matmul(yp, wt['w_pool_out'], out_dtype=BF16, name="mix_pool_out")
    sg = sgu_fwd(z, sm['sgu_v_gain'], sm['sgu_w_s'], sm['sgu_b_s'], pw, sw, name="mix_sgu_fwd")
    yb = matmul(sg, wt['w_sgu_out'], out_dtype=BF16, name="mix_sgu_out")
    m = gate_fwd(z, ya, yb, pw + 2 * sw, pw + 2 * sw + d, name="mix_gate_fwd")
    o = matmul(m, wt['w_out'], out_dtype=F32, name="mix_out")
    y, h_next = res_rms_fwd(x, o, sm['g_mix_post'], 1.0, g_next, name="mix_res_fwd")
    return y, h_next, (x, h, z, dd, e, yp, sg, ya, yb, m, o)


def mixer_bwd(dy, saved, wt, sm, dims, deps=()):
    pw, sw, d = dims
    x, h, z, dd, e, yp, sg, ya, yb, m, o = saved
    grads = {}
    do, grads['g_mix_post'] = rms_bwd(o, sm['g_mix_post'], dy, 1.0, None, BF16, name="mix_res_bwd", deps=deps)
    dm = matmul(do, wt['w_out'], tb=True, out_dtype=BF16, name="mix_out_dx")
    grads['w_out'] = matmul(m, do, ta=True, out_dtype=BF16, name="mix_out_dw")
    dya, dyb, dga, dgb = gate_bwd(z, ya, yb, dm, pw + 2 * sw, pw + 2 * sw + d, name="mix_gate_bwd")
    dyp = matmul(dya, wt['w_pool_out'], tb=True, out_dtype=BF16, name="mix_pool_out_dx")
    grads['w_pool_out'] = matmul(yp, dya, ta=True, out_dtype=BF16, name="mix_pool_out_dw")
    dp, dwg, dscale = pool_bwd(dyp, e, dd, wt['pool_group_w'], sm['pool_scale'], name="mix_pool_bwd")
    grads['pool_group_w'] = dwg.astype(BF16)
    grads['pool_scale'] = dscale.reshape(pw)
    dsg = matmul(dyb, wt['w_sgu_out'], tb=True, out_dtype=BF16, name="mix_sgu_out_dx")
    grads['w_sgu_out'] = matmul(sg, dyb, ta=True, out_dtype=BF16, name="mix_sgu_out_dw")
    du, dv, grads['sgu_w_s'], grads['sgu_b_s'], grads['sgu_v_gain'] = sgu_bwd(
        z, dsg, sm['sgu_v_gain'], sm['sgu_w_s'], sm['sgu_b_s'], pw, sw, name="mix_sgu_bwd")
    dz = jnp.concatenate([dp, du, dv, dga, dgb], axis=1)
    dh = matmul(dz, wt['w_in'], tb=True, out_dtype=F32, name="mix_in_dx")
    grads['w_in'] = matmul(h, dz, ta=True, out_dtype=BF16, name="mix_in_dw")
    dx, grads['g_mix_pre'] = rms_bwd(x, sm['g_mix_pre'], dh, 1.0, dy, F32, name="mix_rms_bwd")
    return dx, grads


def _as_rows(a):
    return a.reshape(a.shape[0], -1, a.shape[-1])


def kernel(x, g_ffn1_pre, w_ffn1_up, w_ffn1_down, g_ffn1_post, g_mix_pre, w_in, pool_group_w, pool_scale, w_pool_out, sgu_v_gain, sgu_w_s, sgu_b_s, w_sgu_out, w_out, g_mix_post, g_ffn2_pre, w_ffn2_up, w_ffn2_down, g_ffn2_post, loss_target, m_g_ffn1_pre, m_w_ffn1_up, m_w_ffn1_down, m_g_ffn1_post, m_g_mix_pre, m_w_in, m_pool_group_w, m_pool_scale, m_w_pool_out, m_sgu_v_gain, m_sgu_w_s, m_sgu_b_s, m_w_sgu_out, m_w_out, m_g_mix_post, m_g_ffn2_pre, m_w_ffn2_up, m_w_ffn2_down, m_g_ffn2_post, v_g_ffn1_pre, v_w_ffn1_up, v_w_ffn1_down, v_g_ffn1_post, v_g_mix_pre, v_w_in, v_pool_group_w, v_pool_scale, v_w_pool_out, v_sgu_v_gain, v_sgu_w_s, v_sgu_b_s, v_w_sgu_out, v_w_out, v_g_mix_post, v_g_ffn2_pre, v_w_ffn2_up, v_w_ffn2_down, v_g_ffn2_post):
    w = dict(zip(WEIGHTS, (g_ffn1_pre, w_ffn1_up, w_ffn1_down, g_ffn1_post, g_mix_pre, w_in, pool_group_w, pool_scale, w_pool_out, sgu_v_gain, sgu_w_s, sgu_b_s, w_sgu_out, w_out, g_mix_post, g_ffn2_pre, w_ffn2_up, w_ffn2_down, g_ffn2_post)))
    mom = dict(zip(WEIGHTS, (m_g_ffn1_pre, m_w_ffn1_up, m_w_ffn1_down, m_g_ffn1_post, m_g_mix_pre, m_w_in, m_pool_group_w, m_pool_scale, m_w_pool_out, m_sgu_v_gain, m_sgu_w_s, m_sgu_b_s, m_w_sgu_out, m_w_out, m_g_mix_post, m_g_ffn2_pre, m_w_ffn2_up, m_w_ffn2_down, m_g_ffn2_post)))
    var = dict(zip(WEIGHTS, (v_g_ffn1_pre, v_w_ffn1_up, v_w_ffn1_down, v_g_ffn1_post, v_g_mix_pre, v_w_in, v_pool_group_w, v_pool_scale, v_w_pool_out, v_sgu_v_gain, v_sgu_w_s, v_sgu_b_s, v_w_sgu_out, v_w_out, v_g_mix_post, v_g_ffn2_pre, v_w_ffn2_up, v_w_ffn2_down, v_g_ffn2_post)))
    depth = g_ffn1_pre.shape[0]
    d = x.shape[-1]
    pw = pool_scale.shape[-1]
    sw = sgu_v_gain.shape[-1]
    dims = (pw, sw, d)
    axes = [BIG_AXIS[n] for n in BIG]

    me = 2 * lax.axis_index("x") + lax.axis_index("y")
    gather = Exchange(True, axes)
    group_ids = [[BIG.index(n) for n in group] for group in BLOCK_WEIGHTS]

    me1 = me.reshape(1).astype(jnp.int32)

    def start_gather(l, deps=()):
        lands = []
        for n, ax in zip(BIG, axes):
            shards = w[n] if w[n].ndim == 4 else w[n][:, None]
            full = place_shard(shards, l, ax + 4 - w[n].ndim, me1, name=f"place_{n}")
            lands.append(full if w[n].ndim == 4 else full[0])
        return exchange_start(lands, gather, group_ids, name=f"gather_start_l{l}", deps=deps)

    def arrived(l, gi, after):
        sems, lands, _ = gathers[l]
        ids = group_ids[gi]
        return exchange_wait([lands[t] for t in ids], ids, gather, sems[gi], (after,), name=f"gather_wait_l{l}_b{gi}")

    def block_axes(gi):
        return [axes[t] for t in group_ids[gi]]

    small = [{n: w[n][l] for n in SMALL} for l in range(depth)]

    act = x[0]
    saved, full = [], []
    gathers = []
    for l in range(depth):
        gathers.append(start_gather(l, (gathers[-1][2],) if gathers else ()))
    h = rms_fwd(act, small[0]['g_ffn1_pre'], name="first_rms_fwd", deps=(gathers[-1][2],))
    blocks = [(l, gi) for l in range(depth) for gi in range(len(BLOCK_WEIGHTS))]
    sharing = None
    for n, (l, gi) in enumerate(blocks):
        sm = small[l]
        if sharing is None:
            got = share_halves(arrived(l, gi, act), block_axes(gi), name=f"share_halves_b{gi}")
        else:
            got = share_wait(*sharing, block_axes(gi), (act,), name=f"share_wait_l{l}_b{gi}")
        wt = dict(zip(BLOCK_WEIGHTS[gi], got))
        sharing, deps = None, ()
        if SHARE_AHEAD_FROM <= n + 1 < len(blocks):
            l2, gi2 = blocks[n + 1]
            started = share_start(arrived(l2, gi2, act), block_axes(gi2), name=f"share_start_l{l2}_b{gi2}")
            sharing, deps = started[:3], (started[3],)
        if gi == 0:
            full.append({})
            saved.append([])
            act, h, s = ffn_fwd(act, h, wt['w_ffn1_up'], wt['w_ffn1_down'], sm['g_ffn1_post'], sm['g_mix_pre'],
                                "ffn1", deps)
        elif gi == 1:
            act, h, s = mixer_fwd(act, h, wt, sm, sm['g_ffn2_pre'], dims, deps)
        else:
            g_after = small[l + 1]['g_ffn1_pre'] if l + 1 < depth else None
            act, h, s = ffn_fwd(act, h, wt['w_ffn2_up'], wt['w_ffn2_down'], sm['g_ffn2_post'], g_after, "ffn2", deps)
        full[l].update(wt)
        saved[l].append(s)
    dact, sq_sum = loss_grad(act, loss_target[0], name="loss_grad")
    loss = lax.psum(0.5 * sq_sum / d, ("x", "y", "c"))

    rows3 = {n: (_as_rows(w[n]), _as_rows(mom[n]), _as_rows(var[n])) for n in BIG}
    stacks = {n: tuple(lax.empty(rows3[n][0].shape, F32) for _ in range(4)) for n in BIG}
    small_grads = [{} for _ in range(depth)]

    def start_scatter(names, grads, l, block):
        ex = Exchange(False, [BIG_AXIS[n] for n in names])
        bufs = []
        for n in names:
            piece = list(grads[n].shape)
            piece[BIG_AXIS[n]] //= N_CHIPS
            bufs += [grads[n], lax.empty((N_CHIPS, *piece), BF16)]
        return (names, ex, l, block) + exchange_start(bufs, ex, [list(range(len(names)))],
                                                      name=f"scatter_start_l{l}_b{block}")

    def finish_scatter(pending, after):
        names, ex, l, block, sems, bufs, _ = pending
        got = exchange_wait(bufs, list(range(len(names))), ex, sems[0], after, name=f"scatter_wait_l{l}_b{block}")
        plane = []
        for ti, n in enumerate(names):
            full_g, stack = got[2 * ti], got[2 * ti + 1]
            if full_g.ndim == 2:
                full_g, stack = full_g[None], stack[:, None]
            p = sum_pieces(stack, full_g, BIG_AXIS[n] + 3 - got[2 * ti].ndim, me1, name=f"sum_{n}")
            plane.append(p.reshape(-1, p.shape[-1]))
        return (names, l, block) + swap_start(plane, name=f"swap_start_l{l}_b{block}")

    def finish_swap(swapping, after):
        names, l, block, send_sems, recv_sems, plane, lands, _ = swapping
        plane, other = swap_wait(send_sems, recv_sems, plane, lands, after, name=f"swap_wait_l{l}_b{block}")
        for n, p_own, p_sib in zip(names, plane, other):
            stacks[n] = tuple(adam_big(p_own, p_sib, *rows3[n], l, stacks[n], name=f"adam_{n}"))
        return stacks[names[-1]][0]

    def advance(pending, swapping, done, after):
        started = finish_scatter(pending, (after,) + done)
        if swapping is not None:
            done = (finish_swap(swapping, (started[-1],)),)
        return started, done

    pending, swapping, done = None, None, ()
    for l in reversed(range(depth)):
        wt, sm = full[l], small[l]
        s1, s2, s3 = saved[l]
        for block in (2, 1, 0):
            deps = (pending[-1],) if pending is not None else ()
            g = {}
            if block == 2:
                dnew, g['w_ffn2_up'], g['w_ffn2_down'], g['g_ffn2_pre'], g['g_ffn2_post'] = ffn_bwd(
                    dact, s3, sm['g_ffn2_pre'], wt['w_ffn2_up'], wt['w_ffn2_down'], sm['g_ffn2_post'], "ffn2", deps)
            elif block == 1:
                dnew, g = mixer_bwd(dact, s2, wt, sm, dims, deps)
            else:
                dnew, g['w_ffn1_up'], g['w_ffn1_down'], g['g_ffn1_pre'], g['g_ffn1_post'] = ffn_bwd(
                    dact, s1, sm['g_ffn1_pre'], wt['w_ffn1_up'], wt['w_ffn1_down'], sm['g_ffn1_post'], "ffn1", deps)
            small_grads[l].update({n: g[n] for n in g if n in SMALL})
            if pending is not None:
                swapping, done = advance(pending, swapping, done, dnew)
            pending = start_scatter(BLOCK_WEIGHTS[block], g, l, block)
            dact = dnew
    swapping, done = advance(pending, swapping, done, dact)
    finish_swap(swapping, done)

    def flat(tree):
        v = jnp.concatenate([tree[n].reshape(-1).astype(F32) for n in SMALL])
        pad = (-v.shape[0]) % (SUBLANE * LANE)
        return jnp.pad(v, (0, pad)).reshape(-1, LANE)

    g_small = all_reduce_small(flat({n: jnp.stack([small_grads[l][n] for l in range(depth)]) for n in SMALL}),
                               name="all_reduce_small")
    d_small, m_small, v_small = adam_small(g_small, flat(w), flat(mom), flat(var), name="adam_small")

    def unflat(block):
        v, out, at = block.reshape(-1), {}, 0
        for n in SMALL:
            out[n] = v[at:at + w[n].size].reshape(w[n].shape)
            at += w[n].size
        return out

    result = [{}, {}, {}, {}]
    for tree, block in zip(result, (g_small, d_small, m_small, v_small)):
        tree.update(unflat(block))
    for n in BIG:
        for tree, stack in zip(result, stacks[n]):
            tree[n] = stack.reshape(w[n].shape)
    return (loss, dact.reshape(x.shape), *[tree[n] for tree in result for n in WEIGHTS])
```

```python
import math

import jax
import jax.numpy as jnp
from jax import lax
from jax.experimental import pallas as pl
from jax.experimental.pallas import tpu as pltpu

F32 = jnp.float32
BF16 = jnp.bfloat16
MESH = pl.DeviceIdType.MESH

EPS = 1e-6
MACARON_WEIGHT = 0.5
POOL_WINDOWS = (2, 4, 8, 16)
POOL_HALO = 16
ADAM_LR = 0.001
ADAM_B1 = 0.9
ADAM_B2 = 0.999
ADAM_EPS = 1e-08
ADAM_WD = 0.01
ADAM_STEP = 10
GELU_K = math.sqrt(2.0 / math.pi)
GELU_C = 0.044715

N_CHIPS = 4
N_DEV = 8
V7X_VMEM_BYTES = 64 * 1024 * 1024
VMEM_LIMIT = (V7X_VMEM_BYTES * 3) // 4
LANE = 128
SUBLANE = 8

WEIGHTS = ['g_ffn1_pre', 'w_ffn1_up', 'w_ffn1_down', 'g_ffn1_post', 'g_mix_pre', 'w_in', 'pool_group_w',
           'pool_scale', 'w_pool_out', 'sgu_v_gain', 'sgu_w_s', 'sgu_b_s', 'w_sgu_out', 'w_out', 'g_mix_post',
           'g_ffn2_pre', 'w_ffn2_up', 'w_ffn2_down', 'g_ffn2_post']
BIG_AXIS = {'w_ffn1_up': 1, 'w_ffn1_down': 0, 'w_in': 1, 'pool_group_w': 1, 'w_pool_out': 1, 'w_sgu_out': 1,
            'w_out': 0, 'w_ffn2_up': 1, 'w_ffn2_down': 0}
BIG = list(BIG_AXIS)
BLOCK_WEIGHTS = [['w_ffn1_up', 'w_ffn1_down'], ['w_in', 'pool_group_w', 'w_pool_out', 'w_sgu_out', 'w_out'],
                 ['w_ffn2_up', 'w_ffn2_down']]
SMALL = [n for n in WEIGHTS if n not in BIG_AXIS]
SHARE_AHEAD_FROM = 4


def _pick(dim, cands):
    for c in cands:
        if dim % c == 0:
            return c
    return dim


STREAM_COL_TILES = (1024, 1408, 896, 512, 256, 128)
STREAM_ROW_TILES = (512, 352, 256, 128, 64, 32, 16, 8)
STREAM_BLOCK_ELEMS = 384 * 1024


def _stream_tiles(r, c):
    tc = _pick(c, STREAM_COL_TILES)
    for tr in STREAM_ROW_TILES:
        if r % tr == 0 and tr * tc <= STREAM_BLOCK_ELEMS:
            return tr, tc
    return r, tc


def _params(*sem):
    return pltpu.CompilerParams(dimension_semantics=sem if sem else None, vmem_limit_bytes=VMEM_LIMIT)


def _sigmoid(x):
    return 1.0 / (1.0 + jnp.exp(-x))


def _gelu(x):
    t = jnp.tanh(GELU_K * (x + GELU_C * (x * x * x)))
    return x * (0.5 * (1.0 + t)), t


def _gelu_grad(x, t):
    return 0.5 * (1.0 + t) + (0.5 * x) * (1.0 - t * t) * (GELU_K * (1.0 + (3.0 * GELU_C) * (x * x)))


MATMUL_MN_TILES = (1024, 1408, 512, 256, 128)
MATMUL_K_TILES = (2048, 2816, 1024, 512, 256, 128)


def matmul(a, b, *, ta=False, tb=False, out_dtype=BF16, name, deps=()):
    m_dim, k_dim = (a.shape[1], a.shape[0]) if ta else a.shape
    n_dim = b.shape[0] if tb else b.shape[1]
    tm = _pick(m_dim, MATMUL_MN_TILES)
    tn = _pick(n_dim, MATMUL_MN_TILES)
    tk = _pick(k_dim, MATMUL_K_TILES)
    nk = k_dim // tk
    dims = (((0 if ta else 1,), (1 if tb else 0,)), ((), ()))

    def body(a_ref, b_ref, *rest):
        o_ref, acc_ref = rest[-2:]
        k = pl.program_id(2)

        @pl.when(k == 0)
        def _():
            acc_ref[...] = jnp.zeros_like(acc_ref)

        acc_ref[...] += lax.dot_general(a_ref[...], b_ref[...], dims, preferred_element_type=F32)

        @pl.when(k == nk - 1)
        def _():
            o_ref[...] = acc_ref[...].astype(o_ref.dtype)

    a_spec = pl.BlockSpec((tk, tm), lambda i, j, k: (k, i)) if ta else pl.BlockSpec((tm, tk), lambda i, j, k: (i, k))
    b_spec = pl.BlockSpec((tn, tk), lambda i, j, k: (j, k)) if tb else pl.BlockSpec((tk, tn), lambda i, j, k: (k, j))
    return pl.pallas_call(
        body, name=name, grid=(m_dim // tm, n_dim // tn, nk),
        in_specs=[a_spec, b_spec] + [ANY_SPEC] * len(deps), out_specs=pl.BlockSpec((tm, tn), lambda i, j, k: (i, j)),
        out_shape=jax.ShapeDtypeStruct((m_dim, n_dim), out_dtype),
        scratch_shapes=[pltpu.VMEM((tm, tn), F32)],
        compiler_params=_params("parallel", "parallel", "arbitrary"),
    )(a, b, *deps)


def matmul_swiglu(h, w_up, name, deps=()):
    m_dim, k_dim = h.shape
    f = w_up.shape[1] // 2
    tm = _pick(m_dim, MATMUL_MN_TILES)
    tn = _pick(f, (512, 256, 128))
    tk = _pick(k_dim, MATMUL_K_TILES)
    nk, nf = k_dim // tk, f // tn

    def body(h_ref, wg_ref, wu_ref, *rest):
        g_ref, u_ref, a_ref, accg_ref, accu_ref = rest[-5:]
        k = pl.program_id(2)

        @pl.when(k == 0)
        def _():
            accg_ref[...] = jnp.zeros_like(accg_ref)
            accu_ref[...] = jnp.zeros_like(accu_ref)

        hv = h_ref[...]
        accg_ref[...] += jnp.dot(hv, wg_ref[...], preferred_element_type=F32)
        accu_ref[...] += jnp.dot(hv, wu_ref[...], preferred_element_type=F32)

        @pl.when(k == nk - 1)
        def _():
            g, u = accg_ref[...], accu_ref[...]
            g_ref[...] = g.astype(BF16)
            u_ref[...] = u.astype(BF16)
            a_ref[...] = (g * _sigmoid(g) * u).astype(BF16)

    out = jax.ShapeDtypeStruct((m_dim, f), BF16)
    blk = pl.BlockSpec((tm, tn), lambda i, j, k: (i, j))
    return pl.pallas_call(
        body, name=name, grid=(m_dim // tm, nf, nk),
        in_specs=[pl.BlockSpec((tm, tk), lambda i, j, k: (i, k)), pl.BlockSpec((tk, tn), lambda i, j, k: (k, j)),
                  pl.BlockSpec((tk, tn), lambda i, j, k: (k, j + nf))] + [ANY_SPEC] * len(deps),
        out_specs=[blk, blk, blk], out_shape=[out, out, out],
        scratch_shapes=[pltpu.VMEM((tm, tn), F32), pltpu.VMEM((tm, tn), F32)],
        compiler_params=_params("parallel", "parallel", "arbitrary"),
    )(h, w_up, w_up, *deps)


ROW_TILES = (256, 128, 64, 32, 16, 8)


def _row(tr, width):
    return pl.BlockSpec((tr, width), lambda i: (i, 0))


def _vec(width):
    return pl.BlockSpec((1, width), lambda i: (0, 0))


def rms_fwd(x, g, name, deps=()):
    t_dim, d = x.shape
    tr = _pick(t_dim, ROW_TILES)

    def body(x_ref, g_ref, *rest):
        o_ref = rest[-1]
        xv = x_ref[...]
        r = lax.rsqrt(jnp.mean(xv * xv, axis=-1, keepdims=True) + EPS)
        o_ref[...] = ((xv * r) * g_ref[...]).astype(o_ref.dtype)

    return pl.pallas_call(
        body, name=name, grid=(t_dim // tr,), in_specs=[_row(tr, d), _vec(d)] + [ANY_SPEC] * len(deps),
        out_specs=_row(tr, d), out_shape=jax.ShapeDtypeStruct((t_dim, d), BF16), compiler_params=_params("parallel"),
    )(x, g.reshape(1, d), *deps)


def res_rms_fwd(x, f, g, weight, g_next, name):
    t_dim, d = x.shape
    tr = _pick(t_dim, ROW_TILES)
    chained = g_next is not None

    def body(x_ref, f_ref, g_ref, *rest):
        fv = f_ref[...]
        r = lax.rsqrt(jnp.mean(fv * fv, axis=-1, keepdims=True) + EPS)
        y = x_ref[...] + weight * ((fv * r) * g_ref[...])
        if chained:
            gn_ref, o_ref, h_ref = rest
            rn = lax.rsqrt(jnp.mean(y * y, axis=-1, keepdims=True) + EPS)
            h_ref[...] = ((y * rn) * gn_ref[...]).astype(BF16)
        else:
            (o_ref,) = rest
        o_ref[...] = y

    outs = pl.pallas_call(
        body, name=name, grid=(t_dim // tr,),
        in_specs=[_row(tr, d), _row(tr, d), _vec(d)] + ([_vec(d)] if chained else []),
        out_specs=[_row(tr, d)] + ([_row(tr, d)] if chained else []),
        out_shape=[jax.ShapeDtypeStruct((t_dim, d), F32)] + ([jax.ShapeDtypeStruct((t_dim, d), BF16)] if chained else []),
        compiler_params=_params("parallel"),
    )(x, f, g.reshape(1, d), *([g_next.reshape(1, d)] if chained else []))
    return (outs[0], outs[1]) if chained else (outs[0], None)


def rms_bwd(f, g, dy, weight, resid, out_dtype, name, deps=()):
    t_dim, d = f.shape
    tr = _pick(t_dim, ROW_TILES)
    has_resid = resid is not None

    def body(*refs):
        o_ref, dg_ref = refs[-2:]
        if has_resid:
            f_ref, g_ref, dy_ref, res_ref = refs[:4]
        else:
            f_ref, g_ref, dy_ref = refs[:3]

        @pl.when(pl.program_id(0) == 0)
        def _():
            dg_ref[...] = jnp.zeros_like(dg_ref)

        fv = f_ref[...]
        r = lax.rsqrt(jnp.mean(fv * fv, axis=-1, keepdims=True) + EPS)
        n = fv * r
        dyw = dy_ref[...] * weight
        dn = dyw * g_ref[...]
        df = r * (dn - n * jnp.mean(dn * n, axis=-1, keepdims=True))
        if has_resid:
            df = df + res_ref[...]
        o_ref[...] = df.astype(o_ref.dtype)
        dg_ref[...] += jnp.sum(dyw * n, axis=0, keepdims=True)

    ins = [f, g.reshape(1, d), dy] + ([resid] if has_resid else []) + list(deps)
    in_specs = [_row(tr, d), _vec(d), _row(tr, d)] + ([_row(tr, d)] if has_resid else []) + [ANY_SPEC] * len(deps)
    out, dg = pl.pallas_call(
        body, name=name, grid=(t_dim // tr,), in_specs=in_specs, out_specs=[_row(tr, d), _vec(d)],
        out_shape=[jax.ShapeDtypeStruct((t_dim, d), out_dtype), jax.ShapeDtypeStruct((1, d), F32)],
        compiler_params=_params("arbitrary"),
    )(*ins)
    return out, dg.reshape(d)


def loss_grad(y, target, name):
    t_dim, d = y.shape
    tr = _pick(t_dim, ROW_TILES)
    inv_d = 1.0 / d

    def body(y_ref, t_ref, dy_ref, s_ref):
        @pl.when(pl.program_id(0) == 0)
        def _():
            s_ref[...] = jnp.zeros_like(s_ref)

        e = y_ref[...] - t_ref[...]
        dy_ref[...] = e * inv_d
        s_ref[...] += jnp.sum(e * e)

    dy, s = pl.pallas_call(
        body, name=name, grid=(t_dim // tr,), in_specs=[_row(tr, d), _row(tr, d)],
        out_specs=[_row(tr, d), pl.BlockSpec((SUBLANE, LANE), lambda i: (0, 0))],
        out_shape=[jax.ShapeDtypeStruct((t_dim, d), F32), jax.ShapeDtypeStruct((SUBLANE, LANE), F32)],
        compiler_params=_params("arbitrary"),
    )(y, target)
    return dy, s[0, 0]


def swiglu_bwd(gate, up, da, name):
    t_dim, f = gate.shape
    tr = _pick(t_dim, ROW_TILES)
    tc = _pick(f, (512, 256, 128))

    def body(g_ref, u_ref, da_ref, o_ref):
        for c in range(f // tc):
            lo = c * tc
            g = g_ref[:, lo:lo + tc].astype(F32)
            u = u_ref[:, lo:lo + tc].astype(F32)
            da = da_ref[:, lo:lo + tc].astype(F32)
            s = _sigmoid(g)
            o_ref[:, lo:lo + tc] = (da * u * (s * (1.0 + g * (1.0 - s)))).astype(o_ref.dtype)
            o_ref[:, f + lo:f + lo + tc] = (da * (g * s)).astype(o_ref.dtype)

    return pl.pallas_call(
        body, name=name, grid=(t_dim // tr,), in_specs=[_row(tr, f)] * 3, out_specs=_row(tr, 2 * f),
        out_shape=jax.ShapeDtypeStruct((t_dim, 2 * f), BF16), compiler_params=_params("parallel"),
    )(gate, up, da)


def gate_fwd(z, ya, yb, off_a, off_b, name):
    t_dim, d = ya.shape
    tr = _pick(t_dim, (512,) + ROW_TILES)
    tc = math.gcd(math.gcd(off_a, off_b), _pick(d, (512, 256, 128)))
    ja, jb = off_a // tc, off_b // tc

    def body(ga_ref, gb_ref, ya_ref, yb_ref, m_ref):
        m = _sigmoid(ga_ref[...]) * ya_ref[...].astype(F32) + _sigmoid(gb_ref[...]) * yb_ref[...].astype(F32)
        m_ref[...] = m.astype(m_ref.dtype)

    blk = pl.BlockSpec((tr, tc), lambda i, j: (i, j))
    return pl.pallas_call(
        body, name=name, grid=(t_dim // tr, d // tc),
        in_specs=[pl.BlockSpec((tr, tc), lambda i, j: (i, j + ja)), pl.BlockSpec((tr, tc), lambda i, j: (i, j + jb)),
                  blk, blk],
        out_specs=blk, out_shape=jax.ShapeDtypeStruct((t_dim, d), BF16),
        compiler_params=_params("parallel", "parallel"),
    )(z, z, ya, yb)


def gate_bwd(z, ya, yb, dm, off_a, off_b, name):
    t_dim, d = ya.shape
    tr = _pick(t_dim, (512,) + ROW_TILES)
    tc = math.gcd(math.gcd(off_a, off_b), _pick(d, (512, 256, 128)))
    ja, jb = off_a // tc, off_b // tc

    def body(ga_ref, gb_ref, ya_ref, yb_ref, dm_ref, dya_ref, dyb_ref, dga_ref, dgb_ref):
        dm = dm_ref[...].astype(F32)
        sa = _sigmoid(ga_ref[...])
        sb = _sigmoid(gb_ref[...])
        dya_ref[...] = (dm * sa).astype(BF16)
        dyb_ref[...] = (dm * sb).astype(BF16)
        dga_ref[...] = (dm * ya_ref[...].astype(F32) * (sa * (1.0 - sa))).astype(BF16)
        dgb_ref[...] = (dm * yb_ref[...].astype(F32) * (sb * (1.0 - sb))).astype(BF16)

    blk = pl.BlockSpec((tr, tc), lambda i, j: (i, j))
    out = jax.ShapeDtypeStruct((t_dim, d), BF16)
    return pl.pallas_call(
        body, name=name, grid=(t_dim // tr, d // tc),
        in_specs=[pl.BlockSpec((tr, tc), lambda i, j: (i, j + ja)), pl.BlockSpec((tr, tc), lambda i, j: (i, j + jb)),
                  blk, blk, blk],
        out_specs=[blk] * 4, out_shape=[out] * 4, compiler_params=_params("parallel", "parallel"),
    )(z, z, ya, yb, dm)


def _window_sums(e, n_rows, forward):
    def shifted(v, k):
        return pltpu.roll(v, (n_rows - k) if forward else k, 0)

    s2 = e + shifted(e, 1)
    s4 = s2 + shifted(s2, 2)
    s8 = s4 + shifted(s4, 4)
    s16 = s8 + shifted(s8, 8)
    return (s2, s4, s8, s16)


def _pool_rows(t_dim):
    return _pick(t_dim, (256, 128, 64, 32, 16))


def pool_fwd(z, w_group, scale, pw, name):
    t_dim = z.shape[0]
    n_groups, c, _ = w_group.shape
    tr = _pool_rows(t_dim)
    per = tr // POOL_HALO

    def body(cur_ref, prev_ref, w_ref, scale_ref, d_ref, e_ref, yp_ref):
        i = pl.program_id(0)
        cur = cur_ref[...]
        prev = jnp.where(i > 0, prev_ref[...], 0.0)
        ext = jnp.concatenate([prev, cur], axis=0)
        sums = _window_sums(ext, tr + POOL_HALO, forward=False)
        pos = (i * tr + 1 + lax.broadcasted_iota(jnp.int32, (tr, 1), 0)).astype(F32)
        for g, w in enumerate(POOL_WINDOWS):
            cols = slice(g * c, (g + 1) * c)
            cnt = jnp.minimum(pos, float(w))
            d = (sums[g][POOL_HALO:, cols] / cnt - cur[:, cols]).astype(BF16)
            e = jnp.dot(d, w_ref[g], preferred_element_type=F32)
            d_ref[:, cols] = d
            e_ref[:, cols] = e.astype(BF16)
            yp_ref[:, cols] = (e * scale_ref[:, cols]).astype(BF16)

    out = jax.ShapeDtypeStruct((t_dim, pw), BF16)
    return pl.pallas_call(
        body, name=name, grid=(t_dim // tr,),
        in_specs=[_row(tr, pw), pl.BlockSpec((POOL_HALO, pw), lambda i: (jnp.maximum(i * per - 1, 0), 0)),
                  pl.BlockSpec((n_groups, c, c), lambda i: (0, 0, 0)), _vec(pw)],
        out_specs=[_row(tr, pw)] * 3, out_shape=[out] * 3, compiler_params=_params("parallel"),
    )(z, z, w_group, scale.reshape(1, pw))


def pool_bwd(dyp, e, d, w_group, scale, name):
    t_dim, pw = dyp.shape
    n_groups, c, _ = w_group.shape
    tr = _pool_rows(t_dim)
    per = tr // POOL_HALO
    n_tiles = t_dim // tr
    last_halo = t_dim // POOL_HALO - 1
    nt_dims = (((1,), (1,)), ((), ()))
    tn_dims = (((0,), (0,)), ((), ()))

    def body(dyp_ref, nxt_ref, e_ref, d_ref, w_ref, scale_ref, dp_ref, dw_ref, dscale_ref):
        i = pl.program_id(0)

        @pl.when(i == 0)
        def _():
            dw_ref[...] = jnp.zeros_like(dw_ref)
            dscale_ref[...] = jnp.zeros_like(dscale_ref)

        dyp_v = dyp_ref[...].astype(F32)
        dscale_ref[...] += jnp.sum(dyp_v * e_ref[...].astype(F32), axis=0, keepdims=True)
        de_cur = dyp_v * scale_ref[...]
        de_nxt = jnp.where(i < n_tiles - 1, nxt_ref[...].astype(F32) * scale_ref[...], 0.0)
        de = jnp.concatenate([de_cur, de_nxt], axis=0).astype(BF16)
        pos = (i * tr + 1 + lax.broadcasted_iota(jnp.int32, (tr + POOL_HALO, 1), 0)).astype(F32)
        for g, w in enumerate(POOL_WINDOWS):
            cols = slice(g * c, (g + 1) * c)
            de_g = de[:, cols]
            dd = lax.dot_general(de_g, w_ref[g], nt_dims, preferred_element_type=F32)
            dw_ref[g] += lax.dot_general(d_ref[:, cols], de_g[:tr], tn_dims, preferred_element_type=F32)
            q = dd / jnp.minimum(pos, float(w))
            win = _window_sums(q, tr + POOL_HALO, forward=True)[g]
            dp_ref[:, cols] = (win[:tr] - dd[:tr]).astype(BF16)

    return pl.pallas_call(
        body, name=name, grid=(n_tiles,),
        in_specs=[_row(tr, pw), pl.BlockSpec((POOL_HALO, pw), lambda i: (jnp.minimum((i + 1) * per, last_halo), 0)),
                  _row(tr, pw), _row(tr, pw), pl.BlockSpec((n_groups, c, c), lambda i: (0, 0, 0)), _vec(pw)],
        out_specs=[_row(tr, pw), pl.BlockSpec((n_groups, c, c), lambda i: (0, 0, 0)), _vec(pw)],
        out_shape=[jax.ShapeDtypeStruct((t_dim, pw), BF16), jax.ShapeDtypeStruct((n_groups, c, c), F32),
                   jax.ShapeDtypeStruct((1, pw), F32)],
        compiler_params=_params("arbitrary"),
    )(dyp, dyp, e, d, w_group, scale.reshape(1, pw))


def _sgu_rows(t_dim, chunk):
    return chunk * _pick(t_dim // chunk, (2, 1))


def _tril(chunk):
    return lax.broadcasted_iota(jnp.int32, (chunk, chunk), 0) >= lax.broadcasted_iota(jnp.int32, (chunk, chunk), 1)


def sgu_fwd(z, gain, w_s, b_s, pw, sw, name):
    t_dim = z.shape[0]
    n_heads, chunk, _ = w_s.shape
    hd = sw // n_heads
    tr = _sgu_rows(t_dim, chunk)
    ju = pw // sw

    def body(u_ref, v_ref, gain_ref, w_ref, bt_ref, sg_ref):
        ug, _ = _gelu(u_ref[...])
        vg, _ = _gelu(v_ref[...])
        r = lax.rsqrt(jnp.mean(vg * vg, axis=-1, keepdims=True) + EPS)
        vn = ((vg * r) * gain_ref[...]).astype(BF16)
        tri = _tril(chunk)
        for h in range(n_heads):
            wm = jnp.where(tri, w_ref[h], 0.0).astype(BF16)
            cols = slice(h * hd, (h + 1) * hd)
            for ch in range(tr // chunk):
                rows = slice(ch * chunk, (ch + 1) * chunk)
                s = jnp.dot(wm, vn[rows, cols], preferred_element_type=F32) + bt_ref[:, h:h + 1]
                sg_ref[rows, cols] = (ug[rows, cols] * s).astype(BF16)

    return pl.pallas_call(
        body, name=name, grid=(t_dim // tr,),
        in_specs=[pl.BlockSpec((tr, sw), lambda i: (i, ju)), pl.BlockSpec((tr, sw), lambda i: (i, ju + 1)), _vec(sw),
                  pl.BlockSpec((n_heads, chunk, chunk), lambda i: (0, 0, 0)),
                  pl.BlockSpec((chunk, n_heads), lambda i: (0, 0))],
        out_specs=_row(tr, sw), out_shape=jax.ShapeDtypeStruct((t_dim, sw), BF16),
        compiler_params=_params("parallel"),
    )(z, z, gain.reshape(1, sw), w_s, b_s.T)


def sgu_bwd(z, dsg, gain, w_s, b_s, pw, sw, name):
    t_dim = z.shape[0]
    n_heads, chunk, _ = w_s.shape
    hd = sw // n_heads
    tr = _sgu_rows(t_dim, chunk)
    ju = pw // sw
    nt_dims = (((1,), (1,)), ((), ()))
    tn_dims = (((0,), (0,)), ((), ()))

    def body(u_ref, v_ref, dsg_ref, gain_ref, w_ref, bt_ref, du_ref, dv_ref, dw_ref, dbt_ref, dgain_ref,
             dvn_ref, dug_ref):
        @pl.when(pl.program_id(0) == 0)
        def _():
            dw_ref[...] = jnp.zeros_like(dw_ref)
            dbt_ref[...] = jnp.zeros_like(dbt_ref)
            dgain_ref[...] = jnp.zeros_like(dgain_ref)

        u = u_ref[...]
        v = v_ref[...]
        ug, tu = _gelu(u)
        vg, tv = _gelu(v)
        r = lax.rsqrt(jnp.mean(vg * vg, axis=-1, keepdims=True) + EPS)
        n = vg * r
        gain_v = gain_ref[...]
        vn = (n * gain_v).astype(BF16)
        dsg_v = dsg_ref[...].astype(F32)
        tri = _tril(chunk)
        for h in range(n_heads):
            wm = jnp.where(tri, w_ref[h], 0.0).astype(BF16)
            cols = slice(h * hd, (h + 1) * hd)
            for ch in range(tr // chunk):
                rows = slice(ch * chunk, (ch + 1) * chunk)
                vn_b = vn[rows, cols]
                s = jnp.dot(wm, vn_b, preferred_element_type=F32) + bt_ref[:, h:h + 1]
                dsg_b = dsg_v[rows, cols]
                dug_ref[rows, cols] = dsg_b * s
                ds = dsg_b * ug[rows, cols]
                ds_b = ds.astype(BF16)
                dw_ref[h] += jnp.where(tri, lax.dot_general(ds_b, vn_b, nt_dims, preferred_element_type=F32), 0.0)
                dbt_ref[:, h:h + 1] += jnp.sum(ds, axis=1, keepdims=True)
                dvn_ref[rows, cols] = lax.dot_general(wm, ds_b, tn_dims, preferred_element_type=F32)
        dvn = dvn_ref[...]
        dgain_ref[...] += jnp.sum(dvn * n, axis=0, keepdims=True)
        dn = dvn * gain_v
        dvg = r * (dn - n * jnp.mean(dn * n, axis=-1, keepdims=True))
        dv_ref[...] = (dvg * _gelu_grad(v, tv)).astype(BF16)
        du_ref[...] = (dug_ref[...] * _gelu_grad(u, tu)).astype(BF16)

    full_w = pl.BlockSpec((n_heads, chunk, chunk), lambda i: (0, 0, 0))
    full_b = pl.BlockSpec((chunk, n_heads), lambda i: (0, 0))
    du, dv, dw, dbt, dgain = pl.pallas_call(
        body, name=name, grid=(t_dim // tr,),
        in_specs=[pl.BlockSpec((tr, sw), lambda i: (i, ju)), pl.BlockSpec((tr, sw), lambda i: (i, ju + 1)),
                  _row(tr, sw), _vec(sw), full_w, full_b],
        out_specs=[_row(tr, sw), _row(tr, sw), full_w, full_b, _vec(sw)],
        out_shape=[jax.ShapeDtypeStruct((t_dim, sw), BF16), jax.ShapeDtypeStruct((t_dim, sw), BF16),
                   jax.ShapeDtypeStruct((n_heads, chunk, chunk), F32), jax.ShapeDtypeStruct((chunk, n_heads), F32),
                   jax.ShapeDtypeStruct((1, sw), F32)],
        scratch_shapes=[pltpu.VMEM((tr, sw), F32), pltpu.VMEM((tr, sw), F32)],
        compiler_params=_params("arbitrary"),
    )(z, z, dsg, gain.reshape(1, sw), w_s, b_s.T)
    return du, dv, dw, dbt.T, dgain.reshape(sw)


HBM_SPEC = pl.BlockSpec(memory_space=pltpu.HBM)
SEM_SPEC = pl.BlockSpec(memory_space=pltpu.SEMAPHORE)
ANY_SPEC = pl.BlockSpec(memory_space=pl.ANY)
DATAFLOW = pltpu.SideEffectType.DATAFLOW_SIDE_EFFECTING


def _hbm(a):
    return pltpu.with_memory_space_constraint(a, pltpu.HBM)


def _slot(ref, axis, k, n):
    idx = [slice(None)] * len(ref.shape)
    idx[axis] = pl.ds(k * n, n)
    return ref.at[tuple(idx)]


def _slot_half(ref, axis, k, half):
    idx = [slice(None)] * len(ref.shape)
    n = ref.shape[axis] // N_CHIPS
    if axis == 0:
        idx[0] = pl.ds(k * n + half * (n // 2), n // 2)
    else:
        idx[axis] = pl.ds(k * n, n)
        idx[0] = pl.ds(half * (ref.shape[0] // 2), ref.shape[0] // 2)
    return ref.at[tuple(idx)]


def _chip_of(k, core):
    return (k // 2, k % 2, core)


def _my_chip():
    return 2 * lax.axis_index("x") + lax.axis_index("y")


class Exchange:
    def __init__(self, gather, axes):
        self.gather, self.axes = gather, axes
        self.per = 1 if gather else 2

    def bufs(self, t, refs):
        return refs[t * self.per:(t + 1) * self.per]

    def branches(self, me, core):
        if self.gather:
            return [(k, h, (me == k) & (core == h)) for k in range(N_CHIPS) for h in range(2)]
        return [(k, None, me == k) for k in range(N_CHIPS)]

    def views(self, t, bufs, src_chip, dst_chip, half):
        ax = self.axes[t]
        if self.gather:
            slot = _slot_half(bufs[0], ax, src_chip, half)
            return slot, slot
        return _slot(bufs[0], ax, dst_chip, bufs[0].shape[ax] // N_CHIPS), bufs[1].at[src_chip]


def exchange_start(bufs, ex, groups, name, deps=()):
    nb, ng = len(bufs), len(groups)

    def body(*refs):
        ins, outs = refs[:nb], refs[nb + len(deps):]
        sems, token = outs[:2 * ng], outs[-1]
        core = lax.axis_index("c")
        me = _my_chip()
        for k, half, mine in ex.branches(me, core):
            @pl.when(mine)
            def _(k=k, half=half):
                for gi, group in enumerate(groups):
                    for ti, t in enumerate(group):
                        for j in range(N_CHIPS):
                            if j != k:
                                src, dst = ex.views(t, ex.bufs(t, ins), k, j, half)
                                pltpu.make_async_remote_copy(
                                    src_ref=src, dst_ref=dst, send_sem=sems[2 * gi].at[ti * N_CHIPS + j],
                                    recv_sem=sems[2 * gi + 1].at[ti * N_CHIPS + k], device_id=_chip_of(j, core),
                                    device_id_type=MESH).start()
        token[...] = jnp.zeros_like(token)

    sem_shapes = []
    for group in groups:
        sem_shapes += [pltpu.SemaphoreType.DMA((len(group) * N_CHIPS,))] * 2
    thru = [pltpu.HBM(a.shape, a.dtype) for a in bufs]
    outs = pl.pallas_call(
        body, name=name, out_shape=sem_shapes + thru + [jax.ShapeDtypeStruct((SUBLANE, LANE), F32)],
        in_specs=[HBM_SPEC] * nb + [ANY_SPEC] * len(deps),
        out_specs=[SEM_SPEC] * (2 * ng) + [HBM_SPEC] * nb + [pl.BlockSpec(memory_space=pltpu.VMEM)],
        input_output_aliases={i: 2 * ng + i for i in range(nb)},
        compiler_params=pltpu.CompilerParams(has_side_effects=DATAFLOW),
    )(*[_hbm(a) for a in bufs], *deps)
    sems = [(outs[2 * gi], outs[2 * gi + 1]) for gi in range(ng)]
    return sems, outs[2 * ng:2 * ng + nb], outs[-1]


def exchange_wait(bufs, tensors, ex, sems, after, name):
    nb = len(bufs)
    send_sems, recv_sems = sems

    def body(*refs):
        ins, send_ref, recv_ref = refs[:nb], refs[nb], refs[nb + 1]
        core = lax.axis_index("c")
        me = _my_chip()
        for k, half, mine in ex.branches(me, core):
            @pl.when(mine)
            def _(k=k, half=half):
                for ti, t in enumerate(tensors):
                    for j in range(N_CHIPS):
                        if j != k:
                            src, _ = ex.views(t, ex.bufs(ti, ins), k, j, half)
                            _, dst = ex.views(t, ex.bufs(ti, ins), j, k, half)
                            copy = pltpu.make_async_remote_copy(
                                src_ref=src, dst_ref=dst, send_sem=send_ref.at[ti * N_CHIPS + j],
                                recv_sem=recv_ref.at[ti * N_CHIPS + j],
                                device_id=_chip_of(j, core), device_id_type=MESH)
                            copy.wait_send()
                            copy.wait_recv()

    return pl.pallas_call(
        body, name=name, out_shape=[pltpu.HBM(a.shape, a.dtype) for a in bufs],
        in_specs=[HBM_SPEC] * nb + [SEM_SPEC, SEM_SPEC] + [ANY_SPEC] * len(after),
        out_specs=[HBM_SPEC] * nb, input_output_aliases={i: i for i in range(nb)},
        compiler_params=pltpu.CompilerParams(has_side_effects=DATAFLOW),
    )(*bufs, send_sems, recv_sems, *after)


def share_halves(bufs, axes, name):
    nb = len(bufs)

    def body(*refs):
        outs = refs[nb:2 * nb]
        send_sems, recv_sems = refs[2 * nb:]
        core = lax.axis_index("c")
        me = _my_chip()
        sibling = (lax.axis_index("x"), lax.axis_index("y"), 1 - core)
        for k in range(N_CHIPS):
            for half in range(2):
                @pl.when((me == k) & (core == half))
                def _(k=k, half=half):
                    def copy(t, j, h):
                        part = _slot_half(outs[t], axes[t], j, h)
                        return pltpu.make_async_remote_copy(
                            src_ref=part, dst_ref=part, send_sem=send_sems.at[t * N_CHIPS + j],
                            recv_sem=recv_sems.at[t * N_CHIPS + j], device_id=sibling, device_id_type=MESH)

                    pairs = [(t, j) for t in range(nb) for j in range(N_CHIPS) if j != k]
                    for t, j in pairs:
                        copy(t, j, half).start()
                    for t, j in pairs:
                        copy(t, j, 1 - half).wait_recv()
                    for t, j in pairs:
                        copy(t, j, half).wait_send()

    return pl.pallas_call(
        body, name=name, out_shape=[jax.ShapeDtypeStruct(a.shape, a.dtype) for a in bufs],
        in_specs=[HBM_SPEC] * nb, out_specs=[HBM_SPEC] * nb, input_output_aliases={i: i for i in range(nb)},
        scratch_shapes=[pltpu.SemaphoreType.DMA((nb * N_CHIPS,)), pltpu.SemaphoreType.DMA((nb * N_CHIPS,))],
        compiler_params=pltpu.CompilerParams(has_side_effects=True),
    )(*bufs)


def _share_copy(bufs, axes, send_sems, recv_sems, sibling, t, j, half):
    part = _slot_half(bufs[t], axes[t], j, half)
    return pltpu.make_async_remote_copy(src_ref=part, dst_ref=part, send_sem=send_sems.at[t * N_CHIPS + j],
                                        recv_sem=recv_sems.at[t * N_CHIPS + j], device_id=sibling, device_id_type=MESH)


def share_start(bufs, axes, name):
    nb = len(bufs)

    def body(*refs):
        ins, outs = refs[:nb], refs[nb:]
        send_sems, recv_sems, token = outs[0], outs[1], outs[-1]
        core = lax.axis_index("c")
        me = _my_chip()
        sibling = (lax.axis_index("x"), lax.axis_index("y"), 1 - core)
        for k in range(N_CHIPS):
            for half in range(2):
                @pl.when((me == k) & (core == half))
                def _(k=k, half=half):
                    for t in range(nb):
                        for j in range(N_CHIPS):
                            if j != k:
                                _share_copy(ins, axes, send_sems, recv_sems, sibling, t, j, half).start()
        token[...] = jnp.zeros_like(token)

    outs = pl.pallas_call(
        body, name=name,
        out_shape=[pltpu.SemaphoreType.DMA((nb * N_CHIPS,))] * 2 + [pltpu.HBM(a.shape, a.dtype) for a in bufs]
        + [jax.ShapeDtypeStruct((SUBLANE, LANE), F32)],
        in_specs=[HBM_SPEC] * nb,
        out_specs=[SEM_SPEC, SEM_SPEC] + [HBM_SPEC] * nb + [pl.BlockSpec(memory_space=pltpu.VMEM)],
        input_output_aliases={i: 2 + i for i in range(nb)},
        compiler_params=pltpu.CompilerParams(has_side_effects=DATAFLOW),
    )(*[_hbm(a) for a in bufs])
    return outs[0], outs[1], outs[2:2 + nb], outs[-1]


def share_wait(send_sems, recv_sems, bufs, axes, after, name):
    nb = len(bufs)

    def body(*refs):
        ins, send_ref, recv_ref = refs[:nb], refs[nb], refs[nb + 1]
        core = lax.axis_index("c")
        me = _my_chip()
        sibling = (lax.axis_index("x"), lax.axis_index("y"), 1 - core)
        for k in range(N_CHIPS):
            for half in range(2):
                @pl.when((me == k) & (core == half))
                def _(k=k, half=half):
                    for t in range(nb):
                        for j in range(N_CHIPS):
                            if j != k:
                                _share_copy(ins, axes, send_ref, recv_ref, sibling, t, j, 1 - half).wait_recv()
                                _share_copy(ins, axes, send_ref, recv_ref, sibling, t, j, half).wait_send()

    return pl.pallas_call(
        body, name=name, out_shape=[pltpu.HBM(a.shape, a.dtype) for a in bufs],
        in_specs=[HBM_SPEC] * nb + [SEM_SPEC, SEM_SPEC] + [ANY_SPEC] * len(after),
        out_specs=[HBM_SPEC] * nb, input_output_aliases={i: i for i in range(nb)},
        compiler_params=pltpu.CompilerParams(has_side_effects=DATAFLOW),
    )(*bufs, send_sems, recv_sems, *after)


def swap_start(arrs, name):
    nt = len(arrs)

    def body(*refs):
        ins, lands, outs = refs[:nt], refs[nt:2 * nt], refs[2 * nt:]
        send_sems, recv_sems, token = outs[0], outs[1], outs[-1]
        sibling = (lax.axis_index("x"), lax.axis_index("y"), 1 - lax.axis_index("c"))
        for t in range(nt):
            pltpu.make_async_remote_copy(src_ref=ins[t], dst_ref=lands[t], send_sem=send_sems.at[t],
                                         recv_sem=recv_sems.at[t], device_id=sibling, device_id_type=MESH).start()
        token[...] = jnp.zeros_like(token)

    thru = [pltpu.HBM(a.shape, a.dtype) for a in arrs] * 2
    outs = pl.pallas_call(
        body, name=name,
        out_shape=[pltpu.SemaphoreType.DMA((nt,))] * 2 + thru + [jax.ShapeDtypeStruct((SUBLANE, LANE), F32)],
        in_specs=[HBM_SPEC] * (2 * nt),
        out_specs=[SEM_SPEC, SEM_SPEC] + [HBM_SPEC] * (2 * nt) + [pl.BlockSpec(memory_space=pltpu.VMEM)],
        input_output_aliases={i: 2 + i for i in range(2 * nt)},
        compiler_params=pltpu.CompilerParams(has_side_effects=DATAFLOW),
    )(*[_hbm(a) for a in arrs], *[_hbm(lax.empty(a.shape, a.dtype)) for a in arrs])
    return outs[0], outs[1], outs[2:2 + nt], outs[2 + nt:2 + 2 * nt], outs[-1]


def swap_wait(send_sems, recv_sems, arrs, lands, after, name):
    nt = len(arrs)

    def body(*refs):
        ins, land_refs, send_ref, recv_ref = refs[:nt], refs[nt:2 * nt], refs[2 * nt], refs[2 * nt + 1]
        sibling = (lax.axis_index("x"), lax.axis_index("y"), 1 - lax.axis_index("c"))
        for t in range(nt):
            copy = pltpu.make_async_remote_copy(src_ref=ins[t], dst_ref=land_refs[t], send_sem=send_ref.at[t],
                                                recv_sem=recv_ref.at[t], device_id=sibling,
                                                device_id_type=MESH)
            copy.wait_send()
            copy.wait_recv()

    outs = pl.pallas_call(
        body, name=name, out_shape=[pltpu.HBM(a.shape, a.dtype) for a in list(arrs) + list(lands)],
        in_specs=[HBM_SPEC] * (2 * nt) + [SEM_SPEC, SEM_SPEC] + [ANY_SPEC] * len(after),
        out_specs=[HBM_SPEC] * (2 * nt), input_output_aliases={i: i for i in range(2 * nt)},
        compiler_params=pltpu.CompilerParams(has_side_effects=DATAFLOW),
    )(*arrs, *lands, send_sems, recv_sems, *after)
    return outs[:nt], outs[nt:]


def place_shard(w_stack, layer, axis, me, name, deps=()):
    _, a_dim, r, c = w_stack.shape
    tr, tc = _stream_tiles(r, c)
    nr, nc = r // tr, c // tc
    full = (a_dim, r * N_CHIPS, c) if axis == 1 else (a_dim, r, c * N_CHIPS)

    def body(me_ref, w_ref, *rest):
        o_ref = rest[-1]
        o_ref[...] = w_ref[...].astype(BF16)

    def own_map(a, i, j, me_ref):
        return (a, me_ref[0] * nr + i, j) if axis == 1 else (a, i, me_ref[0] * nc + j)

    return pl.pallas_call(
        body, name=name, out_shape=jax.ShapeDtypeStruct(full, BF16),
        grid_spec=pltpu.PrefetchScalarGridSpec(
            num_scalar_prefetch=1, grid=(a_dim, nr, nc),
            in_specs=[pl.BlockSpec((None, None, tr, tc), lambda a, i, j, me_ref: (layer, a, i, j))]
            + [ANY_SPEC] * len(deps),
            out_specs=pl.BlockSpec((None, tr, tc), own_map)),
        compiler_params=_params("parallel", "parallel", "parallel"),
    )(me, w_stack, *deps)


def sum_pieces(stack, full, axis, me, name):
    _, a_dim, r, c = stack.shape
    tr, tc = _stream_tiles(r, c)
    nr, nc = r // tr, c // tc

    def body(me_ref, own_ref, s1_ref, s2_ref, s3_ref, o_ref):
        acc = own_ref[...].astype(F32)
        for ref in (s1_ref, s2_ref, s3_ref):
            acc = acc + ref[...].astype(F32)
        o_ref[...] = acc.astype(BF16)

    def own_map(a, i, j, me_ref):
        return (a, me_ref[0] * nr + i, j) if axis == 1 else (a, i, me_ref[0] * nc + j)

    def from_chip(step):
        return pl.BlockSpec((None, None, tr, tc), lambda a, i, j, me_ref: ((me_ref[0] + step) % N_CHIPS, a, i, j))

    return pl.pallas_call(
        body, name=name, out_shape=jax.ShapeDtypeStruct((a_dim, r, c), BF16),
        grid_spec=pltpu.PrefetchScalarGridSpec(
            num_scalar_prefetch=1, grid=(a_dim, nr, nc),
            in_specs=[pl.BlockSpec((None, tr, tc), own_map), from_chip(1), from_chip(2), from_chip(3)],
            out_specs=pl.BlockSpec((None, tr, tc), lambda a, i, j, me_ref: (a, i, j))),
        compiler_params=_params("parallel", "parallel", "parallel"),
    )(me, full, stack, stack, stack)


def all_reduce_small(x, name):
    rows, lanes = x.shape

    def body(x_ref, sum_ref, gath_ref, send_sems, recv_sems, local_sem):
        cx, cy, cc = lax.axis_index("x"), lax.axis_index("y"), lax.axis_index("c")
        me, sibling = (cx, cy, cc), (cx, cy, 1 - cc)
        chips = [(1 - cx, cy), (cx, 1 - cy), (1 - cx, 1 - cy)]

        def block(px, py, pc):
            return gath_ref.at[pl.ds(pl.multiple_of((4 * px + 2 * py + pc) * rows, SUBLANE), rows), :]

        def copy(k, blk, to, src=None):
            return pltpu.make_async_remote_copy(
                src_ref=block(*blk) if src is None else src, dst_ref=block(*blk),
                send_sem=send_sems.at[k], recv_sem=recv_sems.at[k], device_id=to, device_id_type=MESH)

        mine = pltpu.make_async_copy(x_ref, block(*me), local_sem)
        mine.start()
        first = [copy(0, me, sibling, src=x_ref)]
        first += [copy(1 + j, me, (*chip, cc), src=x_ref) for j, chip in enumerate(chips)]
        for cp in first:
            cp.start()
        passed = [copy(4 + j, (*chip, cc), sibling) for j, chip in enumerate(chips)]
        for j, chip in enumerate(chips):
            copy(1 + j, (*chip, cc), me).wait_recv()
            passed[j].start()
        copy(0, sibling, me).wait_recv()
        for j, chip in enumerate(chips):
            copy(4 + j, (*chip, 1 - cc), me).wait_recv()
        for cp in first + passed:
            cp.wait_send()
        mine.wait()
        acc = gath_ref[pl.ds(0, rows), :]
        for k in range(1, N_DEV):
            acc = acc + gath_ref[pl.ds(k * rows, rows), :]
        sum_ref[...] = acc

    return pl.pallas_call(
        body, name=name, out_shape=jax.ShapeDtypeStruct((rows, lanes), F32),
        in_specs=[pl.BlockSpec(memory_space=pltpu.VMEM)], out_specs=pl.BlockSpec(memory_space=pltpu.VMEM),
        scratch_shapes=[pltpu.VMEM((N_DEV * rows, lanes), F32), pltpu.SemaphoreType.DMA((7,)),
                        pltpu.SemaphoreType.DMA((7,)), pltpu.SemaphoreType.DMA],
        compiler_params=pltpu.CompilerParams(has_side_effects=True, vmem_limit_bytes=VMEM_LIMIT),
    )(x)


def _adamw(w, g, m, v):
    m = ADAM_B1 * m + (1.0 - ADAM_B1) * g
    v = ADAM_B2 * v + (1.0 - ADAM_B2) * (g * g)
    m_hat = m / (1.0 - ADAM_B1 ** ADAM_STEP)
    v_hat = v / (1.0 - ADAM_B2 ** ADAM_STEP)
    delta = -ADAM_LR * (m_hat / (jnp.sqrt(v_hat) + ADAM_EPS) + ADAM_WD * w)
    return delta, m, v


def adam_big(p_own, p_sib, w, m, v, layer, stacks, name):
    r, c = p_own.shape
    tr, tc = _stream_tiles(r, c)

    def body(p_ref, q_ref, w_ref, m_ref, v_ref, *rest):
        g_out, d_out, m_out, v_out = rest[4:]
        g = p_ref[...].astype(F32) + q_ref[...].astype(F32)
        delta, m_new, v_new = _adamw(w_ref[...], g, m_ref[...], v_ref[...])
        g_out[...] = g
        d_out[...] = delta
        m_out[...] = m_new
        v_out[...] = v_new

    flat = pl.BlockSpec((tr, tc), lambda i, j: (i, j))
    layered = pl.BlockSpec((None, tr, tc), lambda i, j: (layer, i, j))
    anyspace = pl.BlockSpec(memory_space=pl.ANY)
    out = jax.ShapeDtypeStruct(w.shape, F32)
    return pl.pallas_call(
        body, name=name, grid=(r // tr, c // tc),
        in_specs=[flat, flat, layered, layered, layered] + [anyspace] * 4,
        out_specs=[layered] * 4, out_shape=[out] * 4, input_output_aliases={5: 0, 6: 1, 7: 2, 8: 3},
        compiler_params=_params("parallel", "parallel"),
    )(p_own, p_sib, w, m, v, *stacks)


def adam_small(g, w, m, v, name):
    rows, lanes = g.shape
    tr = _pick(rows, (512,) + ROW_TILES)

    def body(g_ref, w_ref, m_ref, v_ref, d_out, m_out, v_out):
        delta, m_new, v_new = _adamw(w_ref[...], g_ref[...], m_ref[...], v_ref[...])
        d_out[...] = delta
        m_out[...] = m_new
        v_out[...] = v_new

    out = jax.ShapeDtypeStruct((rows, lanes), F32)
    return pl.pallas_call(
        body, name=name, grid=(rows // tr,), in_specs=[_row(tr, lanes)] * 4, out_specs=[_row(tr, lanes)] * 3,
        out_shape=[out] * 3, compiler_params=_params("parallel"),
    )(g, w, m, v)


def ffn_fwd(x, h, w_up, w_down, g_post, g_next, tag, deps=()):
    gate, up, a = matmul_swiglu(h, w_up, name=f"{tag}_up", deps=deps)
    f = matmul(a, w_down, out_dtype=F32, name=f"{tag}_down")
    y, h_next = res_rms_fwd(x, f, g_post, MACARON_WEIGHT, g_next, name=f"{tag}_res_fwd")
    return y, h_next, (x, h, gate, up, a, f)


def ffn_bwd(dy, saved, g_pre, w_up, w_down, g_post, tag, deps=()):
    x, h, gate, up, a, f = saved
    df, dg_post = rms_bwd(f, g_post, dy, MACARON_WEIGHT, None, BF16, name=f"{tag}_res_bwd", deps=deps)
    da = matmul(df, w_down, tb=True, out_dtype=BF16, name=f"{tag}_down_dx")
    dw_down = matmul(a, df, ta=True, out_dtype=BF16, name=f"{tag}_down_dw")
    dz = swiglu_bwd(gate, up, da, name=f"{tag}_swiglu_bwd")
    dh = matmul(dz, w_up, tb=True, out_dtype=F32, name=f"{tag}_up_dx")
    dw_up = matmul(h, dz, ta=True, out_dtype=BF16, name=f"{tag}_up_dw")
    dx, dg_pre = rms_bwd(x, g_pre, dh, 1.0, dy, F32, name=f"{tag}_rms_bwd")
    return dx, dw_up, dw_down, dg_pre, dg_post


def mixer_fwd(x, h, wt, sm, g_next, dims, deps=()):
    pw, sw, d = dims
    z = matmul(h, wt['w_in'], out_dtype=F32, name="mix_in", deps=deps)
    dd, e, yp = pool_fwd(z, wt['pool_group_w'], sm['pool_scale'], pw, name="mix_pool_fwd")
    ya = matmul(yp, wt['w_pool_out'], out_dtype=BF16, name="mix_pool_out")
    sg = sgu_fwd(z, sm['sgu_v_gain'], sm['sgu_w_s'], sm['sgu_b_s'], pw, sw, name="mix_sgu_fwd")
    yb = matmul(sg, wt['w_sgu_out'], out_dtype=BF16, name="mix_sgu_out")
    m = gate_fwd(z, ya, yb, pw + 2 * sw, pw + 2 * sw + d, name="mix_gate_fwd")
    o = matmul(m, wt['w_out'], out_dtype=F32, name="mix_out")
    y, h_next = res_rms_fwd(x, o, sm['g_mix_post'], 1.0, g_next, name="mix_res_fwd")
    return y, h_next, (x, h, z, dd, e, yp, sg, ya, yb, m, o)


def mixer_bwd(dy, saved, wt, sm, dims, deps=()):
    pw, sw, d = dims
    x, h, z, dd, e, yp, sg, ya, yb, m, o = saved
    grads = {}
    do, grads['g_mix_post'] = rms_bwd(o, sm['g_mix_post'], dy, 1.0, None, BF16, name="mix_res_bwd", deps=deps)
    dm = matmul(do, wt['w_out'], tb=True, out_dtype=BF16, name="mix_out_dx")
    grads['w_out'] = matmul(m, do, ta=True, out_dtype=BF16, name="mix_out_dw")
    dya, dyb, dga, dgb = gate_bwd(z, ya, yb, dm, pw + 2 * sw, pw + 2 * sw + d, name="mix_gate_bwd")
    dyp = matmul(dya, wt['w_pool_out'], tb=True, out_dtype=BF16, name="mix_pool_out_dx")
    grads['w_pool_out'] = matmul(yp, dya, ta=True, out_dtype=BF16, name="mix_pool_out_dw")
    dp, dwg, dscale = pool_bwd(dyp, e, dd, wt['pool_group_w'], sm['pool_scale'], name="mix_pool_bwd")
    grads['pool_group_w'] = dwg.astype(BF16)
    grads['pool_scale'] = dscale.reshape(pw)
    dsg = matmul(dyb, wt['w_sgu_out'], tb=True, out_dtype=BF16, name="mix_sgu_out_dx")
    grads['w_sgu_out'] = matmul(sg, dyb, ta=True, out_dtype=BF16, name="mix_sgu_out_dw")
    du, dv, grads['sgu_w_s'], grads['sgu_b_s'], grads['sgu_v_gain'] = sgu_bwd(
        z, dsg, sm['sgu_v_gain'], sm['sgu_w_s'], sm['sgu_b_s'], pw, sw, name="mix_sgu_bwd")
    dz = jnp.concatenate([dp, du, dv, dga, dgb], axis=1)
    dh = matmul(dz, wt['w_in'], tb=True, out_dtype=F32, name="mix_in_dx")
    grads['w_in'] = matmul(h, dz, ta=True, out_dtype=BF16, name="mix_in_dw")
    dx, grads['g_mix_pre'] = rms_bwd(x, sm['g_mix_pre'], dh, 1.0, dy, F32, name="mix_rms_bwd")
    return dx, grads


def _as_rows(a):
    return a.reshape(a.shape[0], -1, a.shape[-1])


def kernel(x, g_ffn1_pre, w_ffn1_up, w_ffn1_down, g_ffn1_post, g_mix_pre, w_in, pool_group_w, pool_scale, w_pool_out, sgu_v_gain, sgu_w_s, sgu_b_s, w_sgu_out, w_out, g_mix_post, g_ffn2_pre, w_ffn2_up, w_ffn2_down, g_ffn2_post, loss_target, m_g_ffn1_pre, m_w_ffn1_up, m_w_ffn1_down, m_g_ffn1_post, m_g_mix_pre, m_w_in, m_pool_group_w, m_pool_scale, m_w_pool_out, m_sgu_v_gain, m_sgu_w_s, m_sgu_b_s, m_w_sgu_out, m_w_out, m_g_mix_post, m_g_ffn2_pre, m_w_ffn2_up, m_w_ffn2_down, m_g_ffn2_post, v_g_ffn1_pre, v_w_ffn1_up, v_w_ffn1_down, v_g_ffn1_post, v_g_mix_pre, v_w_in, v_pool_group_w, v_pool_scale, v_w_pool_out, v_sgu_v_gain, v_sgu_w_s, v_sgu_b_s, v_w_sgu_out, v_w_out, v_g_mix_post, v_g_ffn2_pre, v_w_ffn2_up, v_w_ffn2_down, v_g_ffn2_post):
    w = dict(zip(WEIGHTS, (g_ffn1_pre, w_ffn1_up, w_ffn1_down, g_ffn1_post, g_mix_pre, w_in, pool_group_w, pool_scale, w_pool_out, sgu_v_gain, sgu_w_s, sgu_b_s, w_sgu_out, w_out, g_mix_post, g_ffn2_pre, w_ffn2_up, w_ffn2_down, g_ffn2_post)))
    mom = dict(zip(WEIGHTS, (m_g_ffn1_pre, m_w_ffn1_up, m_w_ffn1_down, m_g_ffn1_post, m_g_mix_pre, m_w_in, m_pool_group_w, m_pool_scale, m_w_pool_out, m_sgu_v_gain, m_sgu_w_s, m_sgu_b_s, m_w_sgu_out, m_w_out, m_g_mix_post, m_g_ffn2_pre, m_w_ffn2_up, m_w_ffn2_down, m_g_ffn2_post)))
    var = dict(zip(WEIGHTS, (v_g_ffn1_pre, v_w_ffn1_up, v_w_ffn1_down, v_g_ffn1_post, v_g_mix_pre, v_w_in, v_pool_group_w, v_pool_scale, v_w_pool_out, v_sgu_v_gain, v_sgu_w_s, v_sgu_b_s, v_w_sgu_out, v_w_out, v_g_mix_post, v_g_ffn2_pre, v_w_ffn2_up, v_w_ffn2_down, v_g_ffn2_post)))
    depth = g_ffn1_pre.shape[0]
    d = x.shape[-1]
    pw = pool_scale.shape[-1]
    sw = sgu_v_gain.shape[-1]
    dims = (pw, sw, d)
    axes = [BIG_AXIS[n] for n in BIG]

    me = 2 * lax.axis_index("x") + lax.axis_index("y")
    gather = Exchange(True, axes)
    group_ids = [[BIG.index(n) for n in group] for group in BLOCK_WEIGHTS]

    me1 = me.reshape(1).astype(jnp.int32)

    def start_gather(l, deps=()):
        lands = []
        for n, ax in zip(BIG, axes):
            shards = w[n] if w[n].ndim == 4 else w[n][:, None]
            full = place_shard(shards, l, ax + 4 - w[n].ndim, me1, name=f"place_{n}", deps=deps)
            lands.append(full if w[n].ndim == 4 else full[0])
        return exchange_start(lands, gather, group_ids, name=f"gather_start_l{l}", deps=deps)

    def arrived(l, gi, after):
        sems, lands, _ = gathers[l]
        ids = group_ids[gi]
        return exchange_wait([lands[t] for t in ids], ids, gather, sems[gi], (after,), name=f"gather_wait_l{l}_b{gi}")

    def block_axes(gi):
        return [axes[t] for t in group_ids[gi]]

    small = [{n: w[n][l] for n in SMALL} for l in range(depth)]

    act = x[0]
    saved, full = [], []
    gathers = []
    for l in range(depth):
        gathers.append(start_gather(l, (gathers[-1][2],) if gathers else ()))
    h = rms_fwd(act, small[0]['g_ffn1_pre'], name="first_rms_fwd", deps=(gathers[-1][2],))
    blocks = [(l, gi) for l in range(depth) for gi in range(len(BLOCK_WEIGHTS))]
    sharing = None
    for n, (l, gi) in enumerate(blocks):
        sm = small[l]
        if sharing is None:
            got = share_halves(arrived(l, gi, act), block_axes(gi), name=f"share_halves_b{gi}")
        else:
            got = share_wait(*sharing, block_axes(gi), (act,), name=f"share_wait_l{l}_b{gi}")
        wt = dict(zip(BLOCK_WEIGHTS[gi], got))
        sharing, deps = None, ()
        if SHARE_AHEAD_FROM <= n + 1 < len(blocks):
            l2, gi2 = blocks[n + 1]
            started = share_start(arrived(l2, gi2, act), block_axes(gi2), name=f"share_start_l{l2}_b{gi2}")
            sharing, deps = started[:3], (started[3],)
        if gi == 0:
            full.append({})
            saved.append([])
            act, h, s = ffn_fwd(act, h, wt['w_ffn1_up'], wt['w_ffn1_down'], sm['g_ffn1_post'], sm['g_mix_pre'],
                                "ffn1", deps)
        elif gi == 1:
            act, h, s = mixer_fwd(act, h, wt, sm, sm['g_ffn2_pre'], dims, deps)
        else:
            g_after = small[l + 1]['g_ffn1_pre'] if l + 1 < depth else None
            act, h, s = ffn_fwd(act, h, wt['w_ffn2_up'], wt['w_ffn2_down'], sm['g_ffn2_post'], g_after, "ffn2", deps)
        full[l].update(wt)
        saved[l].append(s)
    dact, sq_sum = loss_grad(act, loss_target[0], name="loss_grad")
    loss = lax.psum(0.5 * sq_sum / d, ("x", "y", "c"))

    rows3 = {n: (_as_rows(w[n]), _as_rows(mom[n]), _as_rows(var[n])) for n in BIG}
    stacks = {n: tuple(lax.empty(rows3[n][0].shape, F32) for _ in range(4)) for n in BIG}
    small_grads = [{} for _ in range(depth)]

    def start_scatter(names, grads, l, block):
        ex = Exchange(False, [BIG_AXIS[n] for n in names])
        bufs = []
        for n in names:
            piece = list(grads[n].shape)
            piece[BIG_AXIS[n]] //= N_CHIPS
            bufs += [grads[n], lax.empty((N_CHIPS, *piece), BF16)]
        return (names, ex, l, block) + exchange_start(bufs, ex, [list(range(len(names)))],
                                                      name=f"scatter_start_l{l}_b{block}")

    def finish_scatter(pending, after):
        names, ex, l, block, sems, bufs, _ = pending
        got = exchange_wait(bufs, list(range(len(names))), ex, sems[0], after, name=f"scatter_wait_l{l}_b{block}")
        plane = []
        for ti, n in enumerate(names):
            full_g, stack = got[2 * ti], got[2 * ti + 1]
            if full_g.ndim == 2:
                full_g, stack = full_g[None], stack[:, None]
            p = sum_pieces(stack, full_g, BIG_AXIS[n] + 3 - got[2 * ti].ndim, me1, name=f"sum_{n}")
            plane.append(p.reshape(-1, p.shape[-1]))
        return (names, l, block) + swap_start(plane, name=f"swap_start_l{l}_b{block}")

    def finish_swap(swapping, after):
        names, l, block, send_sems, recv_sems, plane, lands, _ = swapping
        plane, other = swap_wait(send_sems, recv_sems, plane, lands, after, name=f"swap_wait_l{l}_b{block}")
        for n, p_own, p_sib in zip(names, plane, other):
            stacks[n] = tuple(adam_big(p_own, p_sib, *rows3[n], l, stacks[n], name=f"adam_{n}"))
        return stacks[names[-1]][0]

    def advance(pending, swapping, done, after):
        started = finish_scatter(pending, (after,) + done)
        if swapping is not None:
            done = (finish_swap(swapping, (started[-1],)),)
        return started, done

    pending, swapping, done = None, None, ()
    for l in reversed(range(depth)):
        wt, sm = full[l], small[l]
        s1, s2, s3 = saved[l]
        for block in (2, 1, 0):
            deps = (pending[-1],) if pending is not None else ()
            g = {}
            if block == 2:
                dnew, g['w_ffn2_up'], g['w_ffn2_down'], g['g_ffn2_pre'], g['g_ffn2_post'] = ffn_bwd(
                    dact, s3, sm['g_ffn2_pre'], wt['w_ffn2_up'], wt['w_ffn2_down'], sm['g_ffn2_post'], "ffn2", deps)
            elif block == 1:
                dnew, g = mixer_bwd(dact, s2, wt, sm, dims, deps)
            else:
                dnew, g['w_ffn1_up'], g['w_ffn1_down'], g['g_ffn1_pre'], g['g_ffn1_post'] = ffn_bwd(
                    dact, s1, sm['g_ffn1_pre'], wt['w_ffn1_up'], wt['w_ffn1_down'], sm['g_ffn1_post'], "ffn1", deps)
            small_grads[l].update({n: g[n] for n in g if n in SMALL})
            if pending is not None:
                swapping, done = advance(pending, swapping, done, dnew)
            pending = start_scatter(BLOCK_WEIGHTS[block], g, l, block)
            dact = dnew
    swapping, done = advance(pending, swapping, done, dact)
    finish_swap(swapping, done)

    def flat(tree):
        v = jnp.concatenate([tree[n].reshape(-1).astype(F32) for n in SMALL])
        pad = (-v.shape[0]) % (SUBLANE * LANE)
        return jnp.pad(v, (0, pad)).reshape(-1, LANE)

    g_small = all_reduce_small(flat({n: jnp.stack([small_grads[l][n] for l in range(depth)]) for n in SMALL}),
                               name="all_reduce_small")
    d_small, m_small, v_small = adam_small(g_small, flat(w), flat(mom), flat(var), name="adam_small")

    def unflat(block):
        v, out, at = block.reshape(-1), {}, 0
        for n in SMALL:
            out[n] = v[at:at + w[n].size].reshape(w[n].shape)
            at += w[n].size
        return out

    result = [{}, {}, {}, {}]
    for tree, block in zip(result, (g_small, d_small, m_small, v_small)):
        tree.update(unflat(block))
    for n in BIG:
        for tree, stack in zip(result, stacks[n]):
            tree[n] = stack.reshape(w[n].shape)
    return (loss, dact.reshape(x.shape), *[tree[n] for tree in result for n in WEIGHTS])
```

```python
import math

import jax
import jax.numpy as jnp
from jax import lax
from jax.experimental import pallas as pl
from jax.experimental.pallas import tpu as pltpu

F32 = jnp.float32
BF16 = jnp.bfloat16
MESH = pl.DeviceIdType.MESH

EPS = 1e-6
MACARON_WEIGHT = 0.5
POOL_WINDOWS = (2, 4, 8, 16)
POOL_HALO = 16
ADAM_LR = 0.001
ADAM_B1 = 0.9
ADAM_B2 = 0.999
ADAM_EPS = 1e-08
ADAM_WD = 0.01
ADAM_STEP = 10
GELU_K = math.sqrt(2.0 / math.pi)
GELU_C = 0.044715

N_CHIPS = 4
N_DEV = 8
V7X_VMEM_BYTES = 64 * 1024 * 1024
VMEM_LIMIT = (V7X_VMEM_BYTES * 3) // 4
LANE = 128
SUBLANE = 8

WEIGHTS = ['g_ffn1_pre', 'w_ffn1_up', 'w_ffn1_down', 'g_ffn1_post', 'g_mix_pre', 'w_in', 'pool_group_w',
           'pool_scale', 'w_pool_out', 'sgu_v_gain', 'sgu_w_s', 'sgu_b_s', 'w_sgu_out', 'w_out', 'g_mix_post',
           'g_ffn2_pre', 'w_ffn2_up', 'w_ffn2_down', 'g_ffn2_post']
BIG_AXIS = {'w_ffn1_up': 1, 'w_ffn1_down': 0, 'w_in': 1, 'pool_group_w': 1, 'w_pool_out': 1, 'w_sgu_out': 1,
            'w_out': 0, 'w_ffn2_up': 1, 'w_ffn2_down': 0}
BIG = list(BIG_AXIS)
BLOCK_WEIGHTS = [['w_ffn1_up', 'w_ffn1_down'], ['w_in', 'pool_group_w', 'w_pool_out', 'w_sgu_out', 'w_out'],
                 ['w_ffn2_up', 'w_ffn2_down']]
SMALL = [n for n in WEIGHTS if n not in BIG_AXIS]
SHARE_AHEAD_FROM = 4


def _pick(dim, cands):
    for c in cands:
        if dim % c == 0:
            return c
    return dim


STREAM_COL_TILES = (1024, 1408, 896, 512, 256, 128)
STREAM_ROW_TILES = (512, 352, 256, 128, 64, 32, 16, 8)
STREAM_BLOCK_ELEMS = 384 * 1024


def _stream_tiles(r, c):
    tc = _pick(c, STREAM_COL_TILES)
    for tr in STREAM_ROW_TILES:
        if r % tr == 0 and tr * tc <= STREAM_BLOCK_ELEMS:
            return tr, tc
    return r, tc


def _params(*sem):
    return pltpu.CompilerParams(dimension_semantics=sem if sem else None, vmem_limit_bytes=VMEM_LIMIT)


def _sigmoid(x):
    return 1.0 / (1.0 + jnp.exp(-x))


def _gelu(x):
    t = jnp.tanh(GELU_K * (x + GELU_C * (x * x * x)))
    return x * (0.5 * (1.0 + t)), t


def _gelu_grad(x, t):
    return 0.5 * (1.0 + t) + (0.5 * x) * (1.0 - t * t) * (GELU_K * (1.0 + (3.0 * GELU_C) * (x * x)))


MATMUL_MN_TILES = (1024, 1408, 512, 256, 128)
MATMUL_K_TILES = (2048, 2816, 1024, 512, 256, 128)


def matmul(a, b, *, ta=False, tb=False, out_dtype=BF16, name, deps=(), cols=None):
    m_dim, k_dim = (a.shape[1], a.shape[0]) if ta else a.shape
    col0, n_dim = cols if cols is not None else (0, b.shape[0] if tb else b.shape[1])
    tm = _pick(m_dim, MATMUL_MN_TILES)
    tn = math.gcd(_pick(n_dim, MATMUL_MN_TILES), col0)
    tk = _pick(k_dim, MATMUL_K_TILES)
    nk = k_dim // tk
    j0 = col0 // tn
    dims = (((0 if ta else 1,), (1 if tb else 0,)), ((), ()))

    def body(a_ref, b_ref, *rest):
        o_ref, acc_ref = rest[-2:]
        k = pl.program_id(2)

        @pl.when(k == 0)
        def _():
            acc_ref[...] = jnp.zeros_like(acc_ref)

        acc_ref[...] += lax.dot_general(a_ref[...], b_ref[...], dims, preferred_element_type=F32)

        @pl.when(k == nk - 1)
        def _():
            o_ref[...] = acc_ref[...].astype(o_ref.dtype)

    a_spec = pl.BlockSpec((tk, tm), lambda i, j, k: (k, i)) if ta else pl.BlockSpec((tm, tk), lambda i, j, k: (i, k))
    b_spec = pl.BlockSpec((tn, tk), lambda i, j, k: (j, k)) if tb else pl.BlockSpec((tk, tn), lambda i, j, k: (k, j + j0))
    return pl.pallas_call(
        body, name=name, grid=(m_dim // tm, n_dim // tn, nk),
        in_specs=[a_spec, b_spec] + [ANY_SPEC] * len(deps), out_specs=pl.BlockSpec((tm, tn), lambda i, j, k: (i, j)),
        out_shape=jax.ShapeDtypeStruct((m_dim, n_dim), out_dtype),
        scratch_shapes=[pltpu.VMEM((tm, tn), F32)],
        compiler_params=_params("parallel", "parallel", "arbitrary"),
    )(a, b, *deps)


def matmul_swiglu(h, w_up, name, deps=()):
    m_dim, k_dim = h.shape
    f = w_up.shape[1] // 2
    tm = _pick(m_dim, MATMUL_MN_TILES)
    tn = _pick(f, (512, 256, 128))
    tk = _pick(k_dim, MATMUL_K_TILES)
    nk, nf = k_dim // tk, f // tn

    def body(h_ref, wg_ref, wu_ref, *rest):
        g_ref, u_ref, a_ref, accg_ref, accu_ref = rest[-5:]
        k = pl.program_id(2)

        @pl.when(k == 0)
        def _():
            accg_ref[...] = jnp.zeros_like(accg_ref)
            accu_ref[...] = jnp.zeros_like(accu_ref)

        hv = h_ref[...]
        accg_ref[...] += jnp.dot(hv, wg_ref[...], preferred_element_type=F32)
        accu_ref[...] += jnp.dot(hv, wu_ref[...], preferred_element_type=F32)

        @pl.when(k == nk - 1)
        def _():
            g, u = accg_ref[...], accu_ref[...]
            g_ref[...] = g.astype(BF16)
            u_ref[...] = u.astype(BF16)
            a_ref[...] = (g * _sigmoid(g) * u).astype(BF16)

    out = jax.ShapeDtypeStruct((m_dim, f), BF16)
    blk = pl.BlockSpec((tm, tn), lambda i, j, k: (i, j))
    return pl.pallas_call(
        body, name=name, grid=(m_dim // tm, nf, nk),
        in_specs=[pl.BlockSpec((tm, tk), lambda i, j, k: (i, k)), pl.BlockSpec((tk, tn), lambda i, j, k: (k, j)),
                  pl.BlockSpec((tk, tn), lambda i, j, k: (k, j + nf))] + [ANY_SPEC] * len(deps),
        out_specs=[blk, blk, blk], out_shape=[out, out, out],
        scratch_shapes=[pltpu.VMEM((tm, tn), F32), pltpu.VMEM((tm, tn), F32)],
        compiler_params=_params("parallel", "parallel", "arbitrary"),
    )(h, w_up, w_up, *deps)


ROW_TILES = (256, 128, 64, 32, 16, 8)


def _row(tr, width):
    return pl.BlockSpec((tr, width), lambda i: (i, 0))


def _vec(width):
    return pl.BlockSpec((1, width), lambda i: (0, 0))


def rms_fwd(x, g, name, deps=()):
    t_dim, d = x.shape
    tr = _pick(t_dim, ROW_TILES)

    def body(x_ref, g_ref, *rest):
        o_ref = rest[-1]
        xv = x_ref[...]
        r = lax.rsqrt(jnp.mean(xv * xv, axis=-1, keepdims=True) + EPS)
        o_ref[...] = ((xv * r) * g_ref[...]).astype(o_ref.dtype)

    return pl.pallas_call(
        body, name=name, grid=(t_dim // tr,), in_specs=[_row(tr, d), _vec(d)] + [ANY_SPEC] * len(deps),
        out_specs=_row(tr, d), out_shape=jax.ShapeDtypeStruct((t_dim, d), BF16), compiler_params=_params("parallel"),
    )(x, g.reshape(1, d), *deps)


def res_rms_fwd(x, f, g, weight, g_next, name):
    t_dim, d = x.shape
    tr = _pick(t_dim, ROW_TILES)
    chained = g_next is not None

    def body(x_ref, f_ref, g_ref, *rest):
        fv = f_ref[...]
        r = lax.rsqrt(jnp.mean(fv * fv, axis=-1, keepdims=True) + EPS)
        y = x_ref[...] + weight * ((fv * r) * g_ref[...])
        if chained:
            gn_ref, o_ref, h_ref = rest
            rn = lax.rsqrt(jnp.mean(y * y, axis=-1, keepdims=True) + EPS)
            h_ref[...] = ((y * rn) * gn_ref[...]).astype(BF16)
        else:
            (o_ref,) = rest
        o_ref[...] = y

    outs = pl.pallas_call(
        body, name=name, grid=(t_dim // tr,),
        in_specs=[_row(tr, d), _row(tr, d), _vec(d)] + ([_vec(d)] if chained else []),
        out_specs=[_row(tr, d)] + ([_row(tr, d)] if chained else []),
        out_shape=[jax.ShapeDtypeStruct((t_dim, d), F32)] + ([jax.ShapeDtypeStruct((t_dim, d), BF16)] if chained else []),
        compiler_params=_params("parallel"),
    )(x, f, g.reshape(1, d), *([g_next.reshape(1, d)] if chained else []))
    return (outs[0], outs[1]) if chained else (outs[0], None)


def rms_bwd(f, g, dy, weight, resid, out_dtype, name, deps=()):
    t_dim, d = f.shape
    tr = _pick(t_dim, ROW_TILES)
    has_resid = resid is not None

    def body(*refs):
        o_ref, dg_ref = refs[-2:]
        if has_resid:
            f_ref, g_ref, dy_ref, res_ref = refs[:4]
        else:
            f_ref, g_ref, dy_ref = refs[:3]

        @pl.when(pl.program_id(0) == 0)
        def _():
            dg_ref[...] = jnp.zeros_like(dg_ref)

        fv = f_ref[...]
        r = lax.rsqrt(jnp.mean(fv * fv, axis=-1, keepdims=True) + EPS)
        n = fv * r
        dyw = dy_ref[...] * weight
        dn = dyw * g_ref[...]
        df = r * (dn - n * jnp.mean(dn * n, axis=-1, keepdims=True))
        if has_resid:
            df = df + res_ref[...]
        o_ref[...] = df.astype(o_ref.dtype)
        dg_ref[...] += jnp.sum(dyw * n, axis=0, keepdims=True)

    ins = [f, g.reshape(1, d), dy] + ([resid] if has_resid else []) + list(deps)
    in_specs = [_row(tr, d), _vec(d), _row(tr, d)] + ([_row(tr, d)] if has_resid else []) + [ANY_SPEC] * len(deps)
    out, dg = pl.pallas_call(
        body, name=name, grid=(t_dim // tr,), in_specs=in_specs, out_specs=[_row(tr, d), _vec(d)],
        out_shape=[jax.ShapeDtypeStruct((t_dim, d), out_dtype), jax.ShapeDtypeStruct((1, d), F32)],
        compiler_params=_params("arbitrary"),
    )(*ins)
    return out, dg.reshape(d)


def loss_grad(y, target, name):
    t_dim, d = y.shape
    tr = _pick(t_dim, ROW_TILES)
    inv_d = 1.0 / d

    def body(y_ref, t_ref, dy_ref, s_ref):
        @pl.when(pl.program_id(0) == 0)
        def _():
            s_ref[...] = jnp.zeros_like(s_ref)

        e = y_ref[...] - t_ref[...]
        dy_ref[...] = e * inv_d
        s_ref[...] += jnp.sum(e * e)

    dy, s = pl.pallas_call(
        body, name=name, grid=(t_dim // tr,), in_specs=[_row(tr, d), _row(tr, d)],
        out_specs=[_row(tr, d), pl.BlockSpec((SUBLANE, LANE), lambda i: (0, 0))],
        out_shape=[jax.ShapeDtypeStruct((t_dim, d), F32), jax.ShapeDtypeStruct((SUBLANE, LANE), F32)],
        compiler_params=_params("arbitrary"),
    )(y, target)
    return dy, s[0, 0]


def swiglu_bwd(gate, up, da, name):
    t_dim, f = gate.shape
    tr = _pick(t_dim, ROW_TILES)
    tc = _pick(f, (512, 256, 128))

    def body(g_ref, u_ref, da_ref, o_ref):
        for c in range(f // tc):
            lo = c * tc
            g = g_ref[:, lo:lo + tc].astype(F32)
            u = u_ref[:, lo:lo + tc].astype(F32)
            da = da_ref[:, lo:lo + tc].astype(F32)
            s = _sigmoid(g)
            o_ref[:, lo:lo + tc] = (da * u * (s * (1.0 + g * (1.0 - s)))).astype(o_ref.dtype)
            o_ref[:, f + lo:f + lo + tc] = (da * (g * s)).astype(o_ref.dtype)

    return pl.pallas_call(
        body, name=name, grid=(t_dim // tr,), in_specs=[_row(tr, f)] * 3, out_specs=_row(tr, 2 * f),
        out_shape=jax.ShapeDtypeStruct((t_dim, 2 * f), BF16), compiler_params=_params("parallel"),
    )(gate, up, da)


def gate_fwd(z, ya, yb, off_a, off_b, name):
    t_dim, d = ya.shape
    tr = _pick(t_dim, (512,) + ROW_TILES)
    tc = math.gcd(math.gcd(off_a, off_b), _pick(d, (512, 256, 128)))
    ja, jb = off_a // tc, off_b // tc

    def body(ga_ref, gb_ref, ya_ref, yb_ref, m_ref):
        sa, sb = _sigmoid(ga_ref[...].astype(F32)), _sigmoid(gb_ref[...].astype(F32))
        m = sa * ya_ref[...].astype(F32) + sb * yb_ref[...].astype(F32)
        m_ref[...] = m.astype(m_ref.dtype)

    blk = pl.BlockSpec((tr, tc), lambda i, j: (i, j))
    return pl.pallas_call(
        body, name=name, grid=(t_dim // tr, d // tc),
        in_specs=[pl.BlockSpec((tr, tc), lambda i, j: (i, j + ja)), pl.BlockSpec((tr, tc), lambda i, j: (i, j + jb)),
                  blk, blk],
        out_specs=blk, out_shape=jax.ShapeDtypeStruct((t_dim, d), BF16),
        compiler_params=_params("parallel", "parallel"),
    )(z, z, ya, yb)


def gate_bwd(z, ya, yb, dm, off_a, off_b, name):
    t_dim, d = ya.shape
    tr = _pick(t_dim, (512,) + ROW_TILES)
    tc = math.gcd(math.gcd(off_a, off_b), _pick(d, (512, 256, 128)))
    ja, jb = off_a // tc, off_b // tc

    def body(ga_ref, gb_ref, ya_ref, yb_ref, dm_ref, dya_ref, dyb_ref, dga_ref, dgb_ref):
        dm = dm_ref[...].astype(F32)
        sa = _sigmoid(ga_ref[...].astype(F32))
        sb = _sigmoid(gb_ref[...].astype(F32))
        dya_ref[...] = (dm * sa).astype(BF16)
        dyb_ref[...] = (dm * sb).astype(BF16)
        dga_ref[...] = (dm * ya_ref[...].astype(F32) * (sa * (1.0 - sa))).astype(BF16)
        dgb_ref[...] = (dm * yb_ref[...].astype(F32) * (sb * (1.0 - sb))).astype(BF16)

    blk = pl.BlockSpec((tr, tc), lambda i, j: (i, j))
    out = jax.ShapeDtypeStruct((t_dim, d), BF16)
    return pl.pallas_call(
        body, name=name, grid=(t_dim // tr, d // tc),
        in_specs=[pl.BlockSpec((tr, tc), lambda i, j: (i, j + ja)), pl.BlockSpec((tr, tc), lambda i, j: (i, j + jb)),
                  blk, blk, blk],
        out_specs=[blk] * 4, out_shape=[out] * 4, compiler_params=_params("parallel", "parallel"),
    )(z, z, ya, yb, dm)


def _window_sums(e, n_rows, forward):
    def shifted(v, k):
        return pltpu.roll(v, (n_rows - k) if forward else k, 0)

    s2 = e + shifted(e, 1)
    s4 = s2 + shifted(s2, 2)
    s8 = s4 + shifted(s4, 4)
    s16 = s8 + shifted(s8, 8)
    return (s2, s4, s8, s16)


def _pool_rows(t_dim):
    return _pick(t_dim, (256, 128, 64, 32, 16))


def pool_fwd(z, w_group, scale, pw, name):
    t_dim = z.shape[0]
    n_groups, c, _ = w_group.shape
    tr = _pool_rows(t_dim)
    per = tr // POOL_HALO

    def body(cur_ref, prev_ref, w_ref, scale_ref, d_ref, e_ref, yp_ref):
        i = pl.program_id(0)
        cur = cur_ref[...]
        prev = jnp.where(i > 0, prev_ref[...], 0.0)
        ext = jnp.concatenate([prev, cur], axis=0)
        sums = _window_sums(ext, tr + POOL_HALO, forward=False)
        pos = (i * tr + 1 + lax.broadcasted_iota(jnp.int32, (tr, 1), 0)).astype(F32)
        for g, w in enumerate(POOL_WINDOWS):
            cols = slice(g * c, (g + 1) * c)
            cnt = jnp.minimum(pos, float(w))
            d = (sums[g][POOL_HALO:, cols] / cnt - cur[:, cols]).astype(BF16)
            e = jnp.dot(d, w_ref[g], preferred_element_type=F32)
            d_ref[:, cols] = d
            e_ref[:, cols] = e.astype(BF16)
            yp_ref[:, cols] = (e * scale_ref[:, cols]).astype(BF16)

    out = jax.ShapeDtypeStruct((t_dim, pw), BF16)
    return pl.pallas_call(
        body, name=name, grid=(t_dim // tr,),
        in_specs=[_row(tr, pw), pl.BlockSpec((POOL_HALO, pw), lambda i: (jnp.maximum(i * per - 1, 0), 0)),
                  pl.BlockSpec((n_groups, c, c), lambda i: (0, 0, 0)), _vec(pw)],
        out_specs=[_row(tr, pw)] * 3, out_shape=[out] * 3, compiler_params=_params("parallel"),
    )(z, z, w_group, scale.reshape(1, pw))


def pool_bwd(dyp, e, d, w_group, scale, name):
    t_dim, pw = dyp.shape
    n_groups, c, _ = w_group.shape
    tr = _pool_rows(t_dim)
    per = tr // POOL_HALO
    n_tiles = t_dim // tr
    last_halo = t_dim // POOL_HALO - 1
    nt_dims = (((1,), (1,)), ((), ()))
    tn_dims = (((0,), (0,)), ((), ()))

    def body(dyp_ref, nxt_ref, e_ref, d_ref, w_ref, scale_ref, dp_ref, dw_ref, dscale_ref):
        i = pl.program_id(0)

        @pl.when(i == 0)
        def _():
            dw_ref[...] = jnp.zeros_like(dw_ref)
            dscale_ref[...] = jnp.zeros_like(dscale_ref)

        dyp_v = dyp_ref[...].astype(F32)
        dscale_ref[...] += jnp.sum(dyp_v * e_ref[...].astype(F32), axis=0, keepdims=True)
        de_cur = dyp_v * scale_ref[...]
        de_nxt = jnp.where(i < n_tiles - 1, nxt_ref[...].astype(F32) * scale_ref[...], 0.0)
        de = jnp.concatenate([de_cur, de_nxt], axis=0).astype(BF16)
        pos = (i * tr + 1 + lax.broadcasted_iota(jnp.int32, (tr + POOL_HALO, 1), 0)).astype(F32)
        for g, w in enumerate(POOL_WINDOWS):
            cols = slice(g * c, (g + 1) * c)
            de_g = de[:, cols]
            dd = lax.dot_general(de_g, w_ref[g], nt_dims, preferred_element_type=F32)
            dw_ref[g] += lax.dot_general(d_ref[:, cols], de_g[:tr], tn_dims, preferred_element_type=F32)
            q = dd / jnp.minimum(pos, float(w))
            win = _window_sums(q, tr + POOL_HALO, forward=True)[g]
            dp_ref[:, cols] = (win[:tr] - dd[:tr]).astype(BF16)

    return pl.pallas_call(
        body, name=name, grid=(n_tiles,),
        in_specs=[_row(tr, pw), pl.BlockSpec((POOL_HALO, pw), lambda i: (jnp.minimum((i + 1) * per, last_halo), 0)),
                  _row(tr, pw), _row(tr, pw), pl.BlockSpec((n_groups, c, c), lambda i: (0, 0, 0)), _vec(pw)],
        out_specs=[_row(tr, pw), pl.BlockSpec((n_groups, c, c), lambda i: (0, 0, 0)), _vec(pw)],
        out_shape=[jax.ShapeDtypeStruct((t_dim, pw), BF16), jax.ShapeDtypeStruct((n_groups, c, c), F32),
                   jax.ShapeDtypeStruct((1, pw), F32)],
        compiler_params=_params("arbitrary"),
    )(dyp, dyp, e, d, w_group, scale.reshape(1, pw))


def _sgu_rows(t_dim, chunk):
    return chunk * _pick(t_dim // chunk, (2, 1))


def _tril(chunk):
    return lax.broadcasted_iota(jnp.int32, (chunk, chunk), 0) >= lax.broadcasted_iota(jnp.int32, (chunk, chunk), 1)


def sgu_fwd(z, gain, w_s, b_s, pw, sw, name):
    t_dim = z.shape[0]
    n_heads, chunk, _ = w_s.shape
    hd = sw // n_heads
    tr = _sgu_rows(t_dim, chunk)
    ju = pw // sw

    def body(u_ref, v_ref, gain_ref, w_ref, bt_ref, sg_ref):
        ug, _ = _gelu(u_ref[...])
        vg, _ = _gelu(v_ref[...])
        r = lax.rsqrt(jnp.mean(vg * vg, axis=-1, keepdims=True) + EPS)
        vn = ((vg * r) * gain_ref[...]).astype(BF16)
        tri = _tril(chunk)
        for h in range(n_heads):
            wm = jnp.where(tri, w_ref[h], 0.0).astype(BF16)
            cols = slice(h * hd, (h + 1) * hd)
            for ch in range(tr // chunk):
                rows = slice(ch * chunk, (ch + 1) * chunk)
                s = jnp.dot(wm, vn[rows, cols], preferred_element_type=F32) + bt_ref[:, h:h + 1]
                sg_ref[rows, cols] = (ug[rows, cols] * s).astype(BF16)

    return pl.pallas_call(
        body, name=name, grid=(t_dim // tr,),
        in_specs=[pl.BlockSpec((tr, sw), lambda i: (i, ju)), pl.BlockSpec((tr, sw), lambda i: (i, ju + 1)), _vec(sw),
                  pl.BlockSpec((n_heads, chunk, chunk), lambda i: (0, 0, 0)),
                  pl.BlockSpec((chunk, n_heads), lambda i: (0, 0))],
        out_specs=_row(tr, sw), out_shape=jax.ShapeDtypeStruct((t_dim, sw), BF16),
        compiler_params=_params("parallel"),
    )(z, z, gain.reshape(1, sw), w_s, b_s.T)


def sgu_bwd(z, dsg, gain, w_s, b_s, pw, sw, name):
    t_dim = z.shape[0]
    n_heads, chunk, _ = w_s.shape
    hd = sw // n_heads
    tr = _sgu_rows(t_dim, chunk)
    ju = pw // sw
    nt_dims = (((1,), (1,)), ((), ()))
    tn_dims = (((0,), (0,)), ((), ()))

    def body(u_ref, v_ref, dsg_ref, gain_ref, w_ref, bt_ref, du_ref, dv_ref, dw_ref, dbt_ref, dgain_ref,
             dvn_ref, dug_ref):
        @pl.when(pl.program_id(0) == 0)
        def _():
            dw_ref[...] = jnp.zeros_like(dw_ref)
            dbt_ref[...] = jnp.zeros_like(dbt_ref)
            dgain_ref[...] = jnp.zeros_like(dgain_ref)

        u = u_ref[...]
        v = v_ref[...]
        ug, tu = _gelu(u)
        vg, tv = _gelu(v)
        r = lax.rsqrt(jnp.mean(vg * vg, axis=-1, keepdims=True) + EPS)
        n = vg * r
        gain_v = gain_ref[...]
        vn = (n * gain_v).astype(BF16)
        dsg_v = dsg_ref[...].astype(F32)
        tri = _tril(chunk)
        for h in range(n_heads):
            wm = jnp.where(tri, w_ref[h], 0.0).astype(BF16)
            cols = slice(h * hd, (h + 1) * hd)
            for ch in range(tr // chunk):
                rows = slice(ch * chunk, (ch + 1) * chunk)
                vn_b = vn[rows, cols]
                s = jnp.dot(wm, vn_b, preferred_element_type=F32) + bt_ref[:, h:h + 1]
                dsg_b = dsg_v[rows, cols]
                dug_ref[rows, cols] = dsg_b * s
                ds = dsg_b * ug[rows, cols]
                ds_b = ds.astype(BF16)
                dw_ref[h] += jnp.where(tri, lax.dot_general(ds_b, vn_b, nt_dims, preferred_element_type=F32), 0.0)
                dbt_ref[:, h:h + 1] += jnp.sum(ds, axis=1, keepdims=True)
                dvn_ref[rows, cols] = lax.dot_general(wm, ds_b, tn_dims, preferred_element_type=F32)
        dvn = dvn_ref[...]
        dgain_ref[...] += jnp.sum(dvn * n, axis=0, keepdims=True)
        dn = dvn * gain_v
        dvg = r * (dn - n * jnp.mean(dn * n, axis=-1, keepdims=True))
        dv_ref[...] = (dvg * _gelu_grad(v, tv)).astype(BF16)
        du_ref[...] = (dug_ref[...] * _gelu_grad(u, tu)).astype(BF16)

    full_w = pl.BlockSpec((n_heads, chunk, chunk), lambda i: (0, 0, 0))
    full_b = pl.BlockSpec((chunk, n_heads), lambda i: (0, 0))
    du, dv, dw, dbt, dgain = pl.pallas_call(
        body, name=name, grid=(t_dim // tr,),
        in_specs=[pl.BlockSpec((tr, sw), lambda i: (i, ju)), pl.BlockSpec((tr, sw), lambda i: (i, ju + 1)),
                  _row(tr, sw), _vec(sw), full_w, full_b],
        out_specs=[_row(tr, sw), _row(tr, sw), full_w, full_b, _vec(sw)],
        out_shape=[jax.ShapeDtypeStruct((t_dim, sw), BF16), jax.ShapeDtypeStruct((t_dim, sw), BF16),
                   jax.ShapeDtypeStruct((n_heads, chunk, chunk), F32), jax.ShapeDtypeStruct((chunk, n_heads), F32),
                   jax.ShapeDtypeStruct((1, sw), F32)],
        scratch_shapes=[pltpu.VMEM((tr, sw), F32), pltpu.VMEM((tr, sw), F32)],
        compiler_params=_params("arbitrary"),
    )(z, z, dsg, gain.reshape(1, sw), w_s, b_s.T)
    return du, dv, dw, dbt.T, dgain.reshape(sw)


HBM_SPEC = pl.BlockSpec(memory_space=pltpu.HBM)
SEM_SPEC = pl.BlockSpec(memory_space=pltpu.SEMAPHORE)
ANY_SPEC = pl.BlockSpec(memory_space=pl.ANY)
DATAFLOW = pltpu.SideEffectType.DATAFLOW_SIDE_EFFECTING


def _hbm(a):
    return pltpu.with_memory_space_constraint(a, pltpu.HBM)


def _slot(ref, axis, k, n):
    idx = [slice(None)] * len(ref.shape)
    idx[axis] = pl.ds(k * n, n)
    return ref.at[tuple(idx)]


def _slot_half(ref, axis, k, half):
    idx = [slice(None)] * len(ref.shape)
    n = ref.shape[axis] // N_CHIPS
    if axis == 0:
        idx[0] = pl.ds(k * n + half * (n // 2), n // 2)
    else:
        idx[axis] = pl.ds(k * n, n)
        idx[0] = pl.ds(half * (ref.shape[0] // 2), ref.shape[0] // 2)
    return ref.at[tuple(idx)]


def _chip_of(k, core):
    return (k // 2, k % 2, core)


def _my_chip():
    return 2 * lax.axis_index("x") + lax.axis_index("y")


class Exchange:
    def __init__(self, gather, axes):
        self.gather, self.axes = gather, axes
        self.per = 1 if gather else 2

    def bufs(self, t, refs):
        return refs[t * self.per:(t + 1) * self.per]

    def branches(self, me, core):
        if self.gather:
            return [(k, h, (me == k) & (core == h)) for k in range(N_CHIPS) for h in range(2)]
        return [(k, None, me == k) for k in range(N_CHIPS)]

    def views(self, t, bufs, src_chip, dst_chip, half):
        ax = self.axes[t]
        if self.gather:
            slot = _slot_half(bufs[0], ax, src_chip, half)
            return slot, slot
        return _slot(bufs[0], ax, dst_chip, bufs[0].shape[ax] // N_CHIPS), bufs[1].at[src_chip]


def exchange_start(bufs, ex, groups, name, deps=()):
    nb, ng = len(bufs), len(groups)

    def body(*refs):
        ins, outs = refs[:nb], refs[nb + len(deps):]
        sems, token = outs[:2 * ng], outs[-1]
        core = lax.axis_index("c")
        me = _my_chip()
        for k, half, mine in ex.branches(me, core):
            @pl.when(mine)
            def _(k=k, half=half):
                for gi, group in enumerate(groups):
                    for ti, t in enumerate(group):
                        for j in range(N_CHIPS):
                            if j != k:
                                src, dst = ex.views(t, ex.bufs(t, ins), k, j, half)
                                pltpu.make_async_remote_copy(
                                    src_ref=src, dst_ref=dst, send_sem=sems[2 * gi].at[ti * N_CHIPS + j],
                                    recv_sem=sems[2 * gi + 1].at[ti * N_CHIPS + k], device_id=_chip_of(j, core),
                                    device_id_type=MESH).start()
        token[...] = jnp.zeros_like(token)

    sem_shapes = []
    for group in groups:
        sem_shapes += [pltpu.SemaphoreType.DMA((len(group) * N_CHIPS,))] * 2
    thru = [pltpu.HBM(a.shape, a.dtype) for a in bufs]
    outs = pl.pallas_call(
        body, name=name, out_shape=sem_shapes + thru + [jax.ShapeDtypeStruct((SUBLANE, LANE), F32)],
        in_specs=[HBM_SPEC] * nb + [ANY_SPEC] * len(deps),
        out_specs=[SEM_SPEC] * (2 * ng) + [HBM_SPEC] * nb + [pl.BlockSpec(memory_space=pltpu.VMEM)],
        input_output_aliases={i: 2 * ng + i for i in range(nb)},
        compiler_params=pltpu.CompilerParams(has_side_effects=DATAFLOW),
    )(*[_hbm(a) for a in bufs], *deps)
    sems = [(outs[2 * gi], outs[2 * gi + 1]) for gi in range(ng)]
    return sems, outs[2 * ng:2 * ng + nb], outs[-1]


def exchange_wait(bufs, tensors, ex, sems, after, name):
    nb = len(bufs)
    send_sems, recv_sems = sems

    def body(*refs):
        ins, send_ref, recv_ref = refs[:nb], refs[nb], refs[nb + 1]
        core = lax.axis_index("c")
        me = _my_chip()
        for k, half, mine in ex.branches(me, core):
            @pl.when(mine)
            def _(k=k, half=half):
                for ti, t in enumerate(tensors):
                    for j in range(N_CHIPS):
                        if j != k:
                            src, _ = ex.views(t, ex.bufs(ti, ins), k, j, half)
                            _, dst = ex.views(t, ex.bufs(ti, ins), j, k, half)
                            copy = pltpu.make_async_remote_copy(
                                src_ref=src, dst_ref=dst, send_sem=send_ref.at[ti * N_CHIPS + j],
                                recv_sem=recv_ref.at[ti * N_CHIPS + j],
                                device_id=_chip_of(j, core), device_id_type=MESH)
                            copy.wait_send()
                            copy.wait_recv()

    return pl.pallas_call(
        body, name=name, out_shape=[pltpu.HBM(a.shape, a.dtype) for a in bufs],
        in_specs=[HBM_SPEC] * nb + [SEM_SPEC, SEM_SPEC] + [ANY_SPEC] * len(after),
        out_specs=[HBM_SPEC] * nb, input_output_aliases={i: i for i in range(nb)},
        compiler_params=pltpu.CompilerParams(has_side_effects=DATAFLOW),
    )(*bufs, send_sems, recv_sems, *after)


def share_halves(bufs, axes, name):
    nb = len(bufs)

    def body(*refs):
        outs = refs[nb:2 * nb]
        send_sems, recv_sems = refs[2 * nb:]
        core = lax.axis_index("c")
        me = _my_chip()
        sibling = (lax.axis_index("x"), lax.axis_index("y"), 1 - core)
        for k in range(N_CHIPS):
            for half in range(2):
                @pl.when((me == k) & (core == half))
                def _(k=k, half=half):
                    def copy(t, j, h):
                        part = _slot_half(outs[t], axes[t], j, h)
                        return pltpu.make_async_remote_copy(
                            src_ref=part, dst_ref=part, send_sem=send_sems.at[t * N_CHIPS + j],
                            recv_sem=recv_sems.at[t * N_CHIPS + j], device_id=sibling, device_id_type=MESH)

                    pairs = [(t, j) for t in range(nb) for j in range(N_CHIPS) if j != k]
                    for t, j in pairs:
                        copy(t, j, half).start()
                    for t, j in pairs:
                        copy(t, j, 1 - half).wait_recv()
                    for t, j in pairs:
                        copy(t, j, half).wait_send()

    return pl.pallas_call(
        body, name=name, out_shape=[jax.ShapeDtypeStruct(a.shape, a.dtype) for a in bufs],
        in_specs=[HBM_SPEC] * nb, out_specs=[HBM_SPEC] * nb, input_output_aliases={i: i for i in range(nb)},
        scratch_shapes=[pltpu.SemaphoreType.DMA((nb * N_CHIPS,)), pltpu.SemaphoreType.DMA((nb * N_CHIPS,))],
        compiler_params=pltpu.CompilerParams(has_side_effects=True),
    )(*bufs)


def _share_copy(bufs, axes, send_sems, recv_sems, sibling, t, j, half):
    part = _slot_half(bufs[t], axes[t], j, half)
    return pltpu.make_async_remote_copy(src_ref=part, dst_ref=part, send_sem=send_sems.at[t * N_CHIPS + j],
                                        recv_sem=recv_sems.at[t * N_CHIPS + j], device_id=sibling, device_id_type=MESH)


def share_start(bufs, axes, name):
    nb = len(bufs)

    def body(*refs):
        ins, outs = refs[:nb], refs[nb:]
        send_sems, recv_sems, token = outs[0], outs[1], outs[-1]
        core = lax.axis_index("c")
        me = _my_chip()
        sibling = (lax.axis_index("x"), lax.axis_index("y"), 1 - core)
        for k in range(N_CHIPS):
            for half in range(2):
                @pl.when((me == k) & (core == half))
                def _(k=k, half=half):
                    for t in range(nb):
                        for j in range(N_CHIPS):
                            if j != k:
                                _share_copy(ins, axes, send_sems, recv_sems, sibling, t, j, half).start()
        token[...] = jnp.zeros_like(token)

    outs = pl.pallas_call(
        body, name=name,
        out_shape=[pltpu.SemaphoreType.DMA((nb * N_CHIPS,))] * 2 + [pltpu.HBM(a.shape, a.dtype) for a in bufs]
        + [jax.ShapeDtypeStruct((SUBLANE, LANE), F32)],
        in_specs=[HBM_SPEC] * nb,
        out_specs=[SEM_SPEC, SEM_SPEC] + [HBM_SPEC] * nb + [pl.BlockSpec(memory_space=pltpu.VMEM)],
        input_output_aliases={i: 2 + i for i in range(nb)},
        compiler_params=pltpu.CompilerParams(has_side_effects=DATAFLOW),
    )(*[_hbm(a) for a in bufs])
    return outs[0], outs[1], outs[2:2 + nb], outs[-1]


def share_wait(send_sems, recv_sems, bufs, axes, after, name):
    nb = len(bufs)

    def body(*refs):
        ins, send_ref, recv_ref = refs[:nb], refs[nb], refs[nb + 1]
        core = lax.axis_index("c")
        me = _my_chip()
        sibling = (lax.axis_index("x"), lax.axis_index("y"), 1 - core)
        for k in range(N_CHIPS):
            for half in range(2):
                @pl.when((me == k) & (core == half))
                def _(k=k, half=half):
                    for t in range(nb):
                        for j in range(N_CHIPS):
                            if j != k:
                                _share_copy(ins, axes, send_ref, recv_ref, sibling, t, j, 1 - half).wait_recv()
                                _share_copy(ins, axes, send_ref, recv_ref, sibling, t, j, half).wait_send()

    return pl.pallas_call(
        body, name=name, out_shape=[pltpu.HBM(a.shape, a.dtype) for a in bufs],
        in_specs=[HBM_SPEC] * nb + [SEM_SPEC, SEM_SPEC] + [ANY_SPEC] * len(after),
        out_specs=[HBM_SPEC] * nb, input_output_aliases={i: i for i in range(nb)},
        compiler_params=pltpu.CompilerParams(has_side_effects=DATAFLOW),
    )(*bufs, send_sems, recv_sems, *after)


def swap_start(arrs, name):
    nt = len(arrs)

    def body(*refs):
        ins, lands, outs = refs[:nt], refs[nt:2 * nt], refs[2 * nt:]
        send_sems, recv_sems, token = outs[0], outs[1], outs[-1]
        sibling = (lax.axis_index("x"), lax.axis_index("y"), 1 - lax.axis_index("c"))
        for t in range(nt):
            pltpu.make_async_remote_copy(src_ref=ins[t], dst_ref=lands[t], send_sem=send_sems.at[t],
                                         recv_sem=recv_sems.at[t], device_id=sibling, device_id_type=MESH).start()
        token[...] = jnp.zeros_like(token)

    thru = [pltpu.HBM(a.shape, a.dtype) for a in arrs] * 2
    outs = pl.pallas_call(
        body, name=name,
        out_shape=[pltpu.SemaphoreType.DMA((nt,))] * 2 + thru + [jax.ShapeDtypeStruct((SUBLANE, LANE), F32)],
        in_specs=[HBM_SPEC] * (2 * nt),
        out_specs=[SEM_SPEC, SEM_SPEC] + [HBM_SPEC] * (2 * nt) + [pl.BlockSpec(memory_space=pltpu.VMEM)],
        input_output_aliases={i: 2 + i for i in range(2 * nt)},
        compiler_params=pltpu.CompilerParams(has_side_effects=DATAFLOW),
    )(*[_hbm(a) for a in arrs], *[_hbm(lax.empty(a.shape, a.dtype)) for a in arrs])
    return outs[0], outs[1], outs[2:2 + nt], outs[2 + nt:2 + 2 * nt], outs[-1]


def swap_wait(send_sems, recv_sems, arrs, lands, after, name):
    nt = len(arrs)

    def body(*refs):
        ins, land_refs, send_ref, recv_ref = refs[:nt], refs[nt:2 * nt], refs[2 * nt], refs[2 * nt + 1]
        sibling = (lax.axis_index("x"), lax.axis_index("y"), 1 - lax.axis_index("c"))
        for t in range(nt):
            copy = pltpu.make_async_remote_copy(src_ref=ins[t], dst_ref=land_refs[t], send_sem=send_ref.at[t],
                                                recv_sem=recv_ref.at[t], device_id=sibling,
                                                device_id_type=MESH)
            copy.wait_send()
            copy.wait_recv()

    outs = pl.pallas_call(
        body, name=name, out_shape=[pltpu.HBM(a.shape, a.dtype) for a in list(arrs) + list(lands)],
        in_specs=[HBM_SPEC] * (2 * nt) + [SEM_SPEC, SEM_SPEC] + [ANY_SPEC] * len(after),
        out_specs=[HBM_SPEC] * (2 * nt), input_output_aliases={i: i for i in range(2 * nt)},
        compiler_params=pltpu.CompilerParams(has_side_effects=DATAFLOW),
    )(*arrs, *lands, send_sems, recv_sems, *after)
    return outs[:nt], outs[nt:]


def place_shard(w_stack, layer, axis, me, name, deps=()):
    _, a_dim, r, c = w_stack.shape
    tr, tc = _stream_tiles(r, c)
    nr, nc = r // tr, c // tc
    full = (a_dim, r * N_CHIPS, c) if axis == 1 else (a_dim, r, c * N_CHIPS)

    def body(me_ref, w_ref, *rest):
        o_ref = rest[-1]
        o_ref[...] = w_ref[...].astype(BF16)

    def own_map(a, i, j, me_ref):
        return (a, me_ref[0] * nr + i, j) if axis == 1 else (a, i, me_ref[0] * nc + j)

    return pl.pallas_call(
        body, name=name, out_shape=jax.ShapeDtypeStruct(full, BF16),
        grid_spec=pltpu.PrefetchScalarGridSpec(
            num_scalar_prefetch=1, grid=(a_dim, nr, nc),
            in_specs=[pl.BlockSpec((None, None, tr, tc), lambda a, i, j, me_ref: (layer, a, i, j))]
            + [ANY_SPEC] * len(deps),
            out_specs=pl.BlockSpec((None, tr, tc), own_map)),
        compiler_params=_params("parallel", "parallel", "parallel"),
    )(me, w_stack, *deps)


def sum_pieces(stack, full, axis, me, name):
    _, a_dim, r, c = stack.shape
    tr, tc = _stream_tiles(r, c)
    nr, nc = r // tr, c // tc

    def body(me_ref, own_ref, s1_ref, s2_ref, s3_ref, o_ref):
        acc = own_ref[...].astype(F32)
        for ref in (s1_ref, s2_ref, s3_ref):
            acc = acc + ref[...].astype(F32)
        o_ref[...] = acc.astype(BF16)

    def own_map(a, i, j, me_ref):
        return (a, me_ref[0] * nr + i, j) if axis == 1 else (a, i, me_ref[0] * nc + j)

    def from_chip(step):
        return pl.BlockSpec((None, None, tr, tc), lambda a, i, j, me_ref: ((me_ref[0] + step) % N_CHIPS, a, i, j))

    return pl.pallas_call(
        body, name=name, out_shape=jax.ShapeDtypeStruct((a_dim, r, c), BF16),
        grid_spec=pltpu.PrefetchScalarGridSpec(
            num_scalar_prefetch=1, grid=(a_dim, nr, nc),
            in_specs=[pl.BlockSpec((None, tr, tc), own_map), from_chip(1), from_chip(2), from_chip(3)],
            out_specs=pl.BlockSpec((None, tr, tc), lambda a, i, j, me_ref: (a, i, j))),
        compiler_params=_params("parallel", "parallel", "parallel"),
    )(me, full, stack, stack, stack)


def all_reduce_small(x, name):
    rows, lanes = x.shape

    def body(x_ref, sum_ref, gath_ref, send_sems, recv_sems, local_sem):
        cx, cy, cc = lax.axis_index("x"), lax.axis_index("y"), lax.axis_index("c")
        me, sibling = (cx, cy, cc), (cx, cy, 1 - cc)
        chips = [(1 - cx, cy), (cx, 1 - cy), (1 - cx, 1 - cy)]

        def block(px, py, pc):
            return gath_ref.at[pl.ds(pl.multiple_of((4 * px + 2 * py + pc) * rows, SUBLANE), rows), :]

        def copy(k, blk, to, src=None):
            return pltpu.make_async_remote_copy(
                src_ref=block(*blk) if src is None else src, dst_ref=block(*blk),
                send_sem=send_sems.at[k], recv_sem=recv_sems.at[k], device_id=to, device_id_type=MESH)

        mine = pltpu.make_async_copy(x_ref, block(*me), local_sem)
        mine.start()
        first = [copy(0, me, sibling, src=x_ref)]
        first += [copy(1 + j, me, (*chip, cc), src=x_ref) for j, chip in enumerate(chips)]
        for cp in first:
            cp.start()
        passed = [copy(4 + j, (*chip, cc), sibling) for j, chip in enumerate(chips)]
        for j, chip in enumerate(chips):
            copy(1 + j, (*chip, cc), me).wait_recv()
            passed[j].start()
        copy(0, sibling, me).wait_recv()
        for j, chip in enumerate(chips):
            copy(4 + j, (*chip, 1 - cc), me).wait_recv()
        for cp in first + passed:
            cp.wait_send()
        mine.wait()
        acc = gath_ref[pl.ds(0, rows), :]
        for k in range(1, N_DEV):
            acc = acc + gath_ref[pl.ds(k * rows, rows), :]
        sum_ref[...] = acc

    return pl.pallas_call(
        body, name=name, out_shape=jax.ShapeDtypeStruct((rows, lanes), F32),
        in_specs=[pl.BlockSpec(memory_space=pltpu.VMEM)], out_specs=pl.BlockSpec(memory_space=pltpu.VMEM),
        scratch_shapes=[pltpu.VMEM((N_DEV * rows, lanes), F32), pltpu.SemaphoreType.DMA((7,)),
                        pltpu.SemaphoreType.DMA((7,)), pltpu.SemaphoreType.DMA],
        compiler_params=pltpu.CompilerParams(has_side_effects=True, vmem_limit_bytes=VMEM_LIMIT),
    )(x)


def _adamw(w, g, m, v):
    m = ADAM_B1 * m + (1.0 - ADAM_B1) * g
    v = ADAM_B2 * v + (1.0 - ADAM_B2) * (g * g)
    m_hat = m / (1.0 - ADAM_B1 ** ADAM_STEP)
    v_hat = v / (1.0 - ADAM_B2 ** ADAM_STEP)
    delta = -ADAM_LR * (m_hat / (jnp.sqrt(v_hat) + ADAM_EPS) + ADAM_WD * w)
    return delta, m, v


def adam_big(p_own, p_sib, w, m, v, layer, stacks, name):
    r, c = p_own.shape
    tr, tc = _stream_tiles(r, c)

    def body(p_ref, q_ref, w_ref, m_ref, v_ref, *rest):
        g_out, d_out, m_out, v_out = rest[4:]
        g = p_ref[...].astype(F32) + q_ref[...].astype(F32)
        delta, m_new, v_new = _adamw(w_ref[...], g, m_ref[...], v_ref[...])
        g_out[...] = g
        d_out[...] = delta
        m_out[...] = m_new
        v_out[...] = v_new

    flat = pl.BlockSpec((tr, tc), lambda i, j: (i, j))
    layered = pl.BlockSpec((None, tr, tc), lambda i, j: (layer, i, j))
    anyspace = pl.BlockSpec(memory_space=pl.ANY)
    out = jax.ShapeDtypeStruct(w.shape, F32)
    return pl.pallas_call(
        body, name=name, grid=(r // tr, c // tc),
        in_specs=[flat, flat, layered, layered, layered] + [anyspace] * 4,
        out_specs=[layered] * 4, out_shape=[out] * 4, input_output_aliases={5: 0, 6: 1, 7: 2, 8: 3},
        compiler_params=_params("parallel", "parallel"),
    )(p_own, p_sib, w, m, v, *stacks)


def adam_small(g, w, m, v, name):
    rows, lanes = g.shape
    tr = _pick(rows, (512,) + ROW_TILES)

    def body(g_ref, w_ref, m_ref, v_ref, d_out, m_out, v_out):
        delta, m_new, v_new = _adamw(w_ref[...], g_ref[...], m_ref[...], v_ref[...])
        d_out[...] = delta
        m_out[...] = m_new
        v_out[...] = v_new

    out = jax.ShapeDtypeStruct((rows, lanes), F32)
    return pl.pallas_call(
        body, name=name, grid=(rows // tr,), in_specs=[_row(tr, lanes)] * 4, out_specs=[_row(tr, lanes)] * 3,
        out_shape=[out] * 3, compiler_params=_params("parallel"),
    )(g, w, m, v)


def ffn_fwd(x, h, w_up, w_down, g_post, g_next, tag, deps=()):
    gate, up, a = matmul_swiglu(h, w_up, name=f"{tag}_up", deps=deps)
    f = matmul(a, w_down, out_dtype=F32, name=f"{tag}_down")
    y, h_next = res_rms_fwd(x, f, g_post, MACARON_WEIGHT, g_next, name=f"{tag}_res_fwd")
    return y, h_next, (x, h, gate, up, a, f)


def ffn_bwd(dy, saved, g_pre, w_up, w_down, g_post, tag, deps=()):
    x, h, gate, up, a, f = saved
    df, dg_post = rms_bwd(f, g_post, dy, MACARON_WEIGHT, None, BF16, name=f"{tag}_res_bwd", deps=deps)
    da = matmul(df, w_down, tb=True, out_dtype=BF16, name=f"{tag}_down_dx")
    dw_down = matmul(a, df, ta=True, out_dtype=BF16, name=f"{tag}_down_dw")
    dz = swiglu_bwd(gate, up, da, name=f"{tag}_swiglu_bwd")
    dh = matmul(dz, w_up, tb=True, out_dtype=F32, name=f"{tag}_up_dx")
    dw_up = matmul(h, dz, ta=True, out_dtype=BF16, name=f"{tag}_up_dw")
    dx, dg_pre = rms_bwd(x, g_pre, dh, 1.0, dy, F32, name=f"{tag}_rms_bwd")
    return dx, dw_up, dw_down, dg_pre, dg_post


def mixer_fwd(x, h, wt, sm, g_next, dims, deps=()):
    pw, sw, d = dims
    z = matmul(h, wt['w_in'], out_dtype=F32, name="mix_in", deps=deps, cols=(0, pw + 2 * sw))
    zg = matmul(h, wt['w_in'], out_dtype=BF16, name="mix_in_gates", cols=(pw + 2 * sw, 2 * d))
    dd, e, yp = pool_fwd(z, wt['pool_group_w'], sm['pool_scale'], pw, name="mix_pool_fwd")
    ya = matmul(yp, wt['w_pool_out'], out_dtype=BF16, name="mix_pool_out")
    sg = sgu_fwd(z, sm['sgu_v_gain'], sm['sgu_w_s'], sm['sgu_b_s'], pw, sw, name="mix_sgu_fwd")
    yb = matmul(sg, wt['w_sgu_out'], out_dtype=BF16, name="mix_sgu_out")
    m = gate_fwd(zg, ya, yb, 0, d, name="mix_gate_fwd")
    o = matmul(m, wt['w_out'], out_dtype=F32, name="mix_out")
    y, h_next = res_rms_fwd(x, o, sm['g_mix_post'], 1.0, g_next, name="mix_res_fwd")
    return y, h_next, (x, h, z, zg, dd, e, yp, sg, ya, yb, m, o)


def mixer_bwd(dy, saved, wt, sm, dims, deps=()):
    pw, sw, d = dims
    x, h, z, zg, dd, e, yp, sg, ya, yb, m, o = saved
    grads = {}
    do, grads['g_mix_post'] = rms_bwd(o, sm['g_mix_post'], dy, 1.0, None, BF16, name="mix_res_bwd", deps=deps)
    dm = matmul(do, wt['w_out'], tb=True, out_dtype=BF16, name="mix_out_dx")
    grads['w_out'] = matmul(m, do, ta=True, out_dtype=BF16, name="mix_out_dw")
    dya, dyb, dga, dgb = gate_bwd(zg, ya, yb, dm, 0, d, name="mix_gate_bwd")
    dyp = matmul(dya, wt['w_pool_out'], tb=True, out_dtype=BF16, name="mix_pool_out_dx")
    grads['w_pool_out'] = matmul(yp, dya, ta=True, out_dtype=BF16, name="mix_pool_out_dw")
    dp, dwg, dscale = pool_bwd(dyp, e, dd, wt['pool_group_w'], sm['pool_scale'], name="mix_pool_bwd")
    grads['pool_group_w'] = dwg.astype(BF16)
    grads['pool_scale'] = dscale.reshape(pw)
    dsg = matmul(dyb, wt['w_sgu_out'], tb=True, out_dtype=BF16, name="mix_sgu_out_dx")
    grads['w_sgu_out'] = matmul(sg, dyb, ta=True, out_dtype=BF16, name="mix_sgu_out_dw")
    du, dv, grads['sgu_w_s'], grads['sgu_b_s'], grads['sgu_v_gain'] = sgu_bwd(
        z, dsg, sm['sgu_v_gain'], sm['sgu_w_s'], sm['sgu_b_s'], pw, sw, name="mix_sgu_bwd")
    dz = jnp.concatenate([dp, du, dv, dga, dgb], axis=1)
    dh = matmul(dz, wt['w_in'], tb=True, out_dtype=F32, name="mix_in_dx")
    grads['w_in'] = matmul(h, dz, ta=True, out_dtype=BF16, name="mix_in_dw")
    dx, grads['g_mix_pre'] = rms_bwd(x, sm['g_mix_pre'], dh, 1.0, dy, F32, name="mix_rms_bwd")
    return dx, grads


def _as_rows(a):
    return a.reshape(a.shape[0], -1, a.shape[-1])


def kernel(x, g_ffn1_pre, w_ffn1_up, w_ffn1_down, g_ffn1_post, g_mix_pre, w_in, pool_group_w, pool_scale, w_pool_out, sgu_v_gain, sgu_w_s, sgu_b_s, w_sgu_out, w_out, g_mix_post, g_ffn2_pre, w_ffn2_up, w_ffn2_down, g_ffn2_post, loss_target, m_g_ffn1_pre, m_w_ffn1_up, m_w_ffn1_down, m_g_ffn1_post, m_g_mix_pre, m_w_in, m_pool_group_w, m_pool_scale, m_w_pool_out, m_sgu_v_gain, m_sgu_w_s, m_sgu_b_s, m_w_sgu_out, m_w_out, m_g_mix_post, m_g_ffn2_pre, m_w_ffn2_up, m_w_ffn2_down, m_g_ffn2_post, v_g_ffn1_pre, v_w_ffn1_up, v_w_ffn1_down, v_g_ffn1_post, v_g_mix_pre, v_w_in, v_pool_group_w, v_pool_scale, v_w_pool_out, v_sgu_v_gain, v_sgu_w_s, v_sgu_b_s, v_w_sgu_out, v_w_out, v_g_mix_post, v_g_ffn2_pre, v_w_ffn2_up, v_w_ffn2_down, v_g_ffn2_post):
    w = dict(zip(WEIGHTS, (g_ffn1_pre, w_ffn1_up, w_ffn1_down, g_ffn1_post, g_mix_pre, w_in, pool_group_w, pool_scale, w_pool_out, sgu_v_gain, sgu_w_s, sgu_b_s, w_sgu_out, w_out, g_mix_post, g_ffn2_pre, w_ffn2_up, w_ffn2_down, g_ffn2_post)))
    mom = dict(zip(WEIGHTS, (m_g_ffn1_pre, m_w_ffn1_up, m_w_ffn1_down, m_g_ffn1_post, m_g_mix_pre, m_w_in, m_pool_group_w, m_pool_scale, m_w_pool_out, m_sgu_v_gain, m_sgu_w_s, m_sgu_b_s, m_w_sgu_out, m_w_out, m_g_mix_post, m_g_ffn2_pre, m_w_ffn2_up, m_w_ffn2_down, m_g_ffn2_post)))
    var = dict(zip(WEIGHTS, (v_g_ffn1_pre, v_w_ffn1_up, v_w_ffn1_down, v_g_ffn1_post, v_g_mix_pre, v_w_in, v_pool_group_w, v_pool_scale, v_w_pool_out, v_sgu_v_gain, v_sgu_w_s, v_sgu_b_s, v_w_sgu_out, v_w_out, v_g_mix_post, v_g_ffn2_pre, v_w_ffn2_up, v_w_ffn2_down, v_g_ffn2_post)))
    depth = g_ffn1_pre.shape[0]
    d = x.shape[-1]
    pw = pool_scale.shape[-1]
    sw = sgu_v_gain.shape[-1]
    dims = (pw, sw, d)
    axes = [BIG_AXIS[n] for n in BIG]

    me = 2 * lax.axis_index("x") + lax.axis_index("y")
    gather = Exchange(True, axes)
    group_ids = [[BIG.index(n) for n in group] for group in BLOCK_WEIGHTS]

    me1 = me.reshape(1).astype(jnp.int32)

    def start_gather(l, deps=()):
        lands = []
        for n, ax in zip(BIG, axes):
            shards = w[n] if w[n].ndim == 4 else w[n][:, None]
            full = place_shard(shards, l, ax + 4 - w[n].ndim, me1, name=f"place_{n}", deps=deps)
            lands.append(full if w[n].ndim == 4 else full[0])
        return exchange_start(lands, gather, group_ids, name=f"gather_start_l{l}", deps=deps)

    def arrived(l, gi, after):
        sems, lands, _ = gathers[l]
        ids = group_ids[gi]
        return exchange_wait([lands[t] for t in ids], ids, gather, sems[gi], (after,), name=f"gather_wait_l{l}_b{gi}")

    def block_axes(gi):
        return [axes[t] for t in group_ids[gi]]

    small = [{n: w[n][l] for n in SMALL} for l in range(depth)]

    act = x[0]
    saved, full = [], []
    gathers = []
    for l in range(depth):
        gathers.append(start_gather(l, (gathers[-1][2],) if gathers else ()))
    h = rms_fwd(act, small[0]['g_ffn1_pre'], name="first_rms_fwd", deps=(gathers[-1][2],))
    blocks = [(l, gi) for l in range(depth) for gi in range(len(BLOCK_WEIGHTS))]
    sharing = None
    for n, (l, gi) in enumerate(blocks):
        sm = small[l]
        if sharing is None:
            got = share_halves(arrived(l, gi, act), block_axes(gi), name=f"share_halves_b{gi}")
        else:
            got = share_wait(*sharing, block_axes(gi), (act,), name=f"share_wait_l{l}_b{gi}")
        wt = dict(zip(BLOCK_WEIGHTS[gi], got))
        sharing, deps = None, ()
        if SHARE_AHEAD_FROM <= n + 1 < len(blocks):
            l2, gi2 = blocks[n + 1]
            started = share_start(arrived(l2, gi2, act), block_axes(gi2), name=f"share_start_l{l2}_b{gi2}")
            sharing, deps = started[:3], (started[3],)
        if gi == 0:
            full.append({})
            saved.append([])
            act, h, s = ffn_fwd(act, h, wt['w_ffn1_up'], wt['w_ffn1_down'], sm['g_ffn1_post'], sm['g_mix_pre'],
                                "ffn1", deps)
        elif gi == 1:
            act, h, s = mixer_fwd(act, h, wt, sm, sm['g_ffn2_pre'], dims, deps)
        else:
            g_after = small[l + 1]['g_ffn1_pre'] if l + 1 < depth else None
            act, h, s = ffn_fwd(act, h, wt['w_ffn2_up'], wt['w_ffn2_down'], sm['g_ffn2_post'], g_after, "ffn2", deps)
        full[l].update(wt)
        saved[l].append(s)
    dact, sq_sum = loss_grad(act, loss_target[0], name="loss_grad")
    loss = lax.psum(0.5 * sq_sum / d, ("x", "y", "c"))

    rows3 = {n: (_as_rows(w[n]), _as_rows(mom[n]), _as_rows(var[n])) for n in BIG}
    stacks = {n: tuple(lax.empty(rows3[n][0].shape, F32) for _ in range(4)) for n in BIG}
    small_grads = [{} for _ in range(depth)]

    def start_scatter(names, grads, l, block):
        ex = Exchange(False, [BIG_AXIS[n] for n in names])
        bufs = []
        for n in names:
            piece = list(grads[n].shape)
            piece[BIG_AXIS[n]] //= N_CHIPS
            bufs += [grads[n], lax.empty((N_CHIPS, *piece), BF16)]
        return (names, ex, l, block) + exchange_start(bufs, ex, [list(range(len(names)))],
                                                      name=f"scatter_start_l{l}_b{block}")

    def finish_scatter(pending, after):
        names, ex, l, block, sems, bufs, _ = pending
        got = exchange_wait(bufs, list(range(len(names))), ex, sems[0], after, name=f"scatter_wait_l{l}_b{block}")
        plane = []
        for ti, n in enumerate(names):
            full_g, stack = got[2 * ti], got[2 * ti + 1]
            if full_g.ndim == 2:
                full_g, stack = full_g[None], stack[:, None]
            p = sum_pieces(stack, full_g, BIG_AXIS[n] + 3 - got[2 * ti].ndim, me1, name=f"sum_{n}")
            plane.append(p.reshape(-1, p.shape[-1]))
        return (names, l, block) + swap_start(plane, name=f"swap_start_l{l}_b{block}")

    def finish_swap(swapping, after):
        names, l, block, send_sems, recv_sems, plane, lands, _ = swapping
        plane, other = swap_wait(send_sems, recv_sems, plane, lands, after, name=f"swap_wait_l{l}_b{block}")
        for n, p_own, p_sib in zip(names, plane, other):
            stacks[n] = tuple(adam_big(p_own, p_sib, *rows3[n], l, stacks[n], name=f"adam_{n}"))
        return stacks[names[-1]][0]

    def advance(pending, swapping, done, after):
        started = finish_scatter(pending, (after,) + done)
        if swapping is not None:
            done = (finish_swap(swapping, (started[-1],)),)
        return started, done

    pending, swapping, done = None, None, ()
    for l in reversed(range(depth)):
        wt, sm = full[l], small[l]
        s1, s2, s3 = saved[l]
        for block in (2, 1, 0):
            deps = (pending[-1],) if pending is not None else ()
            g = {}
            if block == 2:
                dnew, g['w_ffn2_up'], g['w_ffn2_down'], g['g_ffn2_pre'], g['g_ffn2_post'] = ffn_bwd(
                    dact, s3, sm['g_ffn2_pre'], wt['w_ffn2_up'], wt['w_ffn2_down'], sm['g_ffn2_post'], "ffn2", deps)
            elif block == 1:
                dnew, g = mixer_bwd(dact, s2, wt, sm, dims, deps)
            else:
                dnew, g['w_ffn1_up'], g['w_ffn1_down'], g['g_ffn1_pre'], g['g_ffn1_post'] = ffn_bwd(
                    dact, s1, sm['g_ffn1_pre'], wt['w_ffn1_up'], wt['w_ffn1_down'], sm['g_ffn1_post'], "ffn1", deps)
            small_grads[l].update({n: g[n] for n in g if n in SMALL})
            if pending is not None:
                swapping, done = advance(pending, swapping, done, dnew)
            pending = start_scatter(BLOCK_WEIGHTS[block], g, l, block)
            dact = dnew
    swapping, done = advance(pending, swapping, done, dact)
    finish_swap(swapping, done)

    def flat(tree):
        v = jnp.concatenate([tree[n].reshape(-1).astype(F32) for n in SMALL])
        pad = (-v.shape[0]) % (SUBLANE * LANE)
        return jnp.pad(v, (0, pad)).reshape(-1, LANE)

    g_small = all_reduce_small(flat({n: jnp.stack([small_grads[l][n] for l in range(depth)]) for n in SMALL}),
                               name="all_reduce_small")
    d_small, m_small, v_small = adam_small(g_small, flat(w), flat(mom), flat(var), name="adam_small")

    def unflat(block):
        v, out, at = block.reshape(-1), {}, 0
        for n in SMALL:
            out[n] = v[at:at + w[n].size].reshape(w[n].shape)
            at += w[n].size
        return out

    result = [{}, {}, {}, {}]
    for tree, block in zip(result, (g_small, d_small, m_small, v_small)):
        tree.update(unflat(block))
    for n in BIG:
        for tree, stack in zip(result, stacks[n]):
            tree[n] = stack.reshape(w[n].shape)
    return (loss, dact.reshape(x.shape), *[tree[n] for tree in result for n in WEIGHTS])
```

```python
import math

import jax
import jax.numpy as jnp
from jax import lax
from jax.experimental import pallas as pl
from jax.experimental.pallas import tpu as pltpu

F32 = jnp.float32
BF16 = jnp.bfloat16
MESH = pl.DeviceIdType.MESH

EPS = 1e-6
MACARON_WEIGHT = 0.5
POOL_WINDOWS = (2, 4, 8, 16)
POOL_HALO = 16
ADAM_LR = 0.001
ADAM_B1 = 0.9
ADAM_B2 = 0.999
ADAM_EPS = 1e-08
ADAM_WD = 0.01
ADAM_STEP = 10
GELU_K = math.sqrt(2.0 / math.pi)
GELU_C = 0.044715

N_CHIPS = 4
N_DEV = 8
V7X_VMEM_BYTES = 64 * 1024 * 1024
VMEM_LIMIT = (V7X_VMEM_BYTES * 3) // 4
LANE = 128
SUBLANE = 8

WEIGHTS = ['g_ffn1_pre', 'w_ffn1_up', 'w_ffn1_down', 'g_ffn1_post', 'g_mix_pre', 'w_in', 'pool_group_w',
           'pool_scale', 'w_pool_out', 'sgu_v_gain', 'sgu_w_s', 'sgu_b_s', 'w_sgu_out', 'w_out', 'g_mix_post',
           'g_ffn2_pre', 'w_ffn2_up', 'w_ffn2_down', 'g_ffn2_post']
BIG_AXIS = {'w_ffn1_up': 1, 'w_ffn1_down': 0, 'w_in': 1, 'pool_group_w': 1, 'w_pool_out': 1, 'w_sgu_out': 1,
            'w_out': 0, 'w_ffn2_up': 1, 'w_ffn2_down': 0}
BIG = list(BIG_AXIS)
BLOCK_WEIGHTS = [['w_ffn1_up', 'w_ffn1_down'], ['w_in', 'pool_group_w', 'w_pool_out', 'w_sgu_out', 'w_out'],
                 ['w_ffn2_up', 'w_ffn2_down']]
SMALL = [n for n in WEIGHTS if n not in BIG_AXIS]
SHARE_AHEAD_FROM = 4


def _pick(dim, cands):
    for c in cands:
        if dim % c == 0:
            return c
    return dim


STREAM_COL_TILES = (1024, 1408, 896, 512, 256, 128)
STREAM_ROW_TILES = (512, 352, 256, 128, 64, 32, 16, 8)
STREAM_BLOCK_ELEMS = 384 * 1024


def _stream_tiles(r, c):
    tc = _pick(c, STREAM_COL_TILES)
    for tr in STREAM_ROW_TILES:
        if r % tr == 0 and tr * tc <= STREAM_BLOCK_ELEMS:
            return tr, tc
    return r, tc


def _params(*sem):
    return pltpu.CompilerParams(dimension_semantics=sem if sem else None, vmem_limit_bytes=VMEM_LIMIT)


def _sigmoid(x):
    return 1.0 / (1.0 + jnp.exp(-x))


def _gelu(x):
    t = jnp.tanh(GELU_K * (x + GELU_C * (x * x * x)))
    return x * (0.5 * (1.0 + t)), t


def _gelu_grad(x, t):
    return 0.5 * (1.0 + t) + (0.5 * x) * (1.0 - t * t) * (GELU_K * (1.0 + (3.0 * GELU_C) * (x * x)))


MATMUL_MN_TILES = (1024, 1408, 512, 256, 128)
MATMUL_K_TILES = (2048, 2816, 1024, 512, 256, 128)


def matmul(a, b, *, ta=False, tb=False, out_dtype=BF16, name, deps=(), cols=None):
    m_dim, k_dim = (a.shape[1], a.shape[0]) if ta else a.shape
    col0, n_dim = cols if cols is not None else (0, b.shape[0] if tb else b.shape[1])
    tm = _pick(m_dim, MATMUL_MN_TILES)
    tn = math.gcd(_pick(n_dim, MATMUL_MN_TILES), col0)
    tk = _pick(k_dim, MATMUL_K_TILES)
    nk = k_dim // tk
    j0 = col0 // tn
    dims = (((0 if ta else 1,), (1 if tb else 0,)), ((), ()))

    def body(a_ref, b_ref, *rest):
        o_ref, acc_ref = rest[-2:]
        k = pl.program_id(2)

        @pl.when(k == 0)
        def _():
            acc_ref[...] = jnp.zeros_like(acc_ref)

        acc_ref[...] += lax.dot_general(a_ref[...], b_ref[...], dims, preferred_element_type=F32)

        @pl.when(k == nk - 1)
        def _():
            o_ref[...] = acc_ref[...].astype(o_ref.dtype)

    a_spec = pl.BlockSpec((tk, tm), lambda i, j, k: (k, i)) if ta else pl.BlockSpec((tm, tk), lambda i, j, k: (i, k))
    b_spec = pl.BlockSpec((tn, tk), lambda i, j, k: (j, k)) if tb else pl.BlockSpec((tk, tn), lambda i, j, k: (k, j + j0))
    return pl.pallas_call(
        body, name=name, grid=(m_dim // tm, n_dim // tn, nk),
        in_specs=[a_spec, b_spec] + [ANY_SPEC] * len(deps), out_specs=pl.BlockSpec((tm, tn), lambda i, j, k: (i, j)),
        out_shape=jax.ShapeDtypeStruct((m_dim, n_dim), out_dtype),
        scratch_shapes=[pltpu.VMEM((tm, tn), F32)],
        compiler_params=_params("parallel", "parallel", "arbitrary"),
    )(a, b, *deps)


def matmul_swiglu(h, w_up, name, deps=()):
    m_dim, k_dim = h.shape
    f = w_up.shape[1] // 2
    tm = _pick(m_dim, MATMUL_MN_TILES)
    tn = _pick(f, (512, 256, 128))
    tk = _pick(k_dim, MATMUL_K_TILES)
    nk, nf = k_dim // tk, f // tn

    def body(h_ref, wg_ref, wu_ref, *rest):
        g_ref, u_ref, a_ref, accg_ref, accu_ref = rest[-5:]
        k = pl.program_id(2)

        @pl.when(k == 0)
        def _():
            accg_ref[...] = jnp.zeros_like(accg_ref)
            accu_ref[...] = jnp.zeros_like(accu_ref)

        hv = h_ref[...]
        accg_ref[...] += jnp.dot(hv, wg_ref[...], preferred_element_type=F32)
        accu_ref[...] += jnp.dot(hv, wu_ref[...], preferred_element_type=F32)

        @pl.when(k == nk - 1)
        def _():
            g, u = accg_ref[...], accu_ref[...]
            g_ref[...] = g.astype(BF16)
            u_ref[...] = u.astype(BF16)
            a_ref[...] = (g * _sigmoid(g) * u).astype(BF16)

    out = jax.ShapeDtypeStruct((m_dim, f), BF16)
    blk = pl.BlockSpec((tm, tn), lambda i, j, k: (i, j))
    return pl.pallas_call(
        body, name=name, grid=(m_dim // tm, nf, nk),
        in_specs=[pl.BlockSpec((tm, tk), lambda i, j, k: (i, k)), pl.BlockSpec((tk, tn), lambda i, j, k: (k, j)),
                  pl.BlockSpec((tk, tn), lambda i, j, k: (k, j + nf))] + [ANY_SPEC] * len(deps),
        out_specs=[blk, blk, blk], out_shape=[out, out, out],
        scratch_shapes=[pltpu.VMEM((tm, tn), F32), pltpu.VMEM((tm, tn), F32)],
        compiler_params=_params("parallel", "parallel", "arbitrary"),
    )(h, w_up, w_up, *deps)


ROW_TILES = (256, 128, 64, 32, 16, 8)


def _row(tr, width):
    return pl.BlockSpec((tr, width), lambda i: (i, 0))


def _vec(width):
    return pl.BlockSpec((1, width), lambda i: (0, 0))


def rms_fwd(x, g, name, deps=()):
    t_dim, d = x.shape
    tr = _pick(t_dim, ROW_TILES)

    def body(x_ref, g_ref, *rest):
        o_ref = rest[-1]
        xv = x_ref[...]
        r = lax.rsqrt(jnp.mean(xv * xv, axis=-1, keepdims=True) + EPS)
        o_ref[...] = ((xv * r) * g_ref[...]).astype(o_ref.dtype)

    return pl.pallas_call(
        body, name=name, grid=(t_dim // tr,), in_specs=[_row(tr, d), _vec(d)] + [ANY_SPEC] * len(deps),
        out_specs=_row(tr, d), out_shape=jax.ShapeDtypeStruct((t_dim, d), BF16), compiler_params=_params("parallel"),
    )(x, g.reshape(1, d), *deps)


def res_rms_fwd(x, f, g, weight, g_next, name):
    t_dim, d = x.shape
    tr = _pick(t_dim, ROW_TILES)
    chained = g_next is not None

    def body(x_ref, f_ref, g_ref, *rest):
        fv = f_ref[...]
        r = lax.rsqrt(jnp.mean(fv * fv, axis=-1, keepdims=True) + EPS)
        y = x_ref[...] + weight * ((fv * r) * g_ref[...])
        if chained:
            gn_ref, o_ref, h_ref = rest
            rn = lax.rsqrt(jnp.mean(y * y, axis=-1, keepdims=True) + EPS)
            h_ref[...] = ((y * rn) * gn_ref[...]).astype(BF16)
        else:
            (o_ref,) = rest
        o_ref[...] = y

    outs = pl.pallas_call(
        body, name=name, grid=(t_dim // tr,),
        in_specs=[_row(tr, d), _row(tr, d), _vec(d)] + ([_vec(d)] if chained else []),
        out_specs=[_row(tr, d)] + ([_row(tr, d)] if chained else []),
        out_shape=[jax.ShapeDtypeStruct((t_dim, d), F32)] + ([jax.ShapeDtypeStruct((t_dim, d), BF16)] if chained else []),
        compiler_params=_params("parallel"),
    )(x, f, g.reshape(1, d), *([g_next.reshape(1, d)] if chained else []))
    return (outs[0], outs[1]) if chained else (outs[0], None)


def _rms_bwd_math(fv, gv, dyw):
    r = lax.rsqrt(jnp.mean(fv * fv, axis=-1, keepdims=True) + EPS)
    n = fv * r
    dn = dyw * gv
    return r * (dn - n * jnp.mean(dn * n, axis=-1, keepdims=True)), dyw * n


def rms_bwd(f, g, dy, weight, resid, out_dtype, name, deps=(), chain=None):
    t_dim, d = f.shape
    tr = _pick(t_dim, ROW_TILES)
    has_resid = resid is not None
    n_in = 3 + has_resid + (2 if chain is not None else 0)

    def body(*refs):
        outs = refs[n_in + len(deps):]
        o_ref, dg_ref = outs[:2]
        if has_resid:
            f_ref, g_ref, dy_ref, res_ref = refs[:4]
        else:
            f_ref, g_ref, dy_ref = refs[:3]

        @pl.when(pl.program_id(0) == 0)
        def _():
            for ref in outs[1::2]:
                ref[...] = jnp.zeros_like(ref)

        fv = f_ref[...]
        r = lax.rsqrt(jnp.mean(fv * fv, axis=-1, keepdims=True) + EPS)
        n = fv * r
        dyw = dy_ref[...] * weight
        dn = dyw * g_ref[...]
        df = r * (dn - n * jnp.mean(dn * n, axis=-1, keepdims=True))
        if has_resid:
            df = df + res_ref[...]
        o_ref[...] = df.astype(o_ref.dtype)
        dg_ref[...] += jnp.sum(dyw * n, axis=0, keepdims=True)
        if chain is not None:
            f2_ref, g2_ref = refs[n_in - 2:n_in]
            df2, dg2_rows = _rms_bwd_math(f2_ref[...], g2_ref[...], df * chain[2])
            outs[2][...] = df2.astype(BF16)
            outs[3][...] += jnp.sum(dg2_rows, axis=0, keepdims=True)

    ins = [f, g.reshape(1, d), dy] + ([resid] if has_resid else [])
    in_specs = [_row(tr, d), _vec(d), _row(tr, d)] + ([_row(tr, d)] if has_resid else [])
    out_specs = [_row(tr, d), _vec(d)]
    out_shape = [jax.ShapeDtypeStruct((t_dim, d), out_dtype), jax.ShapeDtypeStruct((1, d), F32)]
    if chain is not None:
        ins += [chain[0], chain[1].reshape(1, d)]
        in_specs += [_row(tr, d), _vec(d)]
        out_specs += [_row(tr, d), _vec(d)]
        out_shape += [jax.ShapeDtypeStruct((t_dim, d), BF16), jax.ShapeDtypeStruct((1, d), F32)]
    outs = pl.pallas_call(
        body, name=name, grid=(t_dim // tr,), in_specs=in_specs + [ANY_SPEC] * len(deps), out_specs=out_specs,
        out_shape=out_shape, compiler_params=_params("arbitrary"),
    )(*ins, *deps)
    if chain is None:
        return outs[0], outs[1].reshape(d)
    return outs[0], outs[1].reshape(d), (outs[2], outs[3].reshape(d))


def loss_grad(y, target, name):
    t_dim, d = y.shape
    tr = _pick(t_dim, ROW_TILES)
    inv_d = 1.0 / d

    def body(y_ref, t_ref, dy_ref, s_ref):
        @pl.when(pl.program_id(0) == 0)
        def _():
            s_ref[...] = jnp.zeros_like(s_ref)

        e = y_ref[...] - t_ref[...]
        dy_ref[...] = e * inv_d
        s_ref[...] += jnp.sum(e * e)

    dy, s = pl.pallas_call(
        body, name=name, grid=(t_dim // tr,), in_specs=[_row(tr, d), _row(tr, d)],
        out_specs=[_row(tr, d), pl.BlockSpec((SUBLANE, LANE), lambda i: (0, 0))],
        out_shape=[jax.ShapeDtypeStruct((t_dim, d), F32), jax.ShapeDtypeStruct((SUBLANE, LANE), F32)],
        compiler_params=_params("arbitrary"),
    )(y, target)
    return dy, s[0, 0]


def swiglu_bwd(gate, up, da, name):
    t_dim, f = gate.shape
    tr = _pick(t_dim, ROW_TILES)
    tc = _pick(f, (512, 256, 128))

    def body(g_ref, u_ref, da_ref, o_ref):
        for c in range(f // tc):
            lo = c * tc
            g = g_ref[:, lo:lo + tc].astype(F32)
            u = u_ref[:, lo:lo + tc].astype(F32)
            da = da_ref[:, lo:lo + tc].astype(F32)
            s = _sigmoid(g)
            o_ref[:, lo:lo + tc] = (da * u * (s * (1.0 + g * (1.0 - s)))).astype(o_ref.dtype)
            o_ref[:, f + lo:f + lo + tc] = (da * (g * s)).astype(o_ref.dtype)

    return pl.pallas_call(
        body, name=name, grid=(t_dim // tr,), in_specs=[_row(tr, f)] * 3, out_specs=_row(tr, 2 * f),
        out_shape=jax.ShapeDtypeStruct((t_dim, 2 * f), BF16), compiler_params=_params("parallel"),
    )(gate, up, da)


def gate_fwd(z, ya, yb, off_a, off_b, name):
    t_dim, d = ya.shape
    tr = _pick(t_dim, (512,) + ROW_TILES)
    tc = math.gcd(math.gcd(off_a, off_b), _pick(d, (512, 256, 128)))
    ja, jb = off_a // tc, off_b // tc

    def body(ga_ref, gb_ref, ya_ref, yb_ref, m_ref):
        sa, sb = _sigmoid(ga_ref[...].astype(F32)), _sigmoid(gb_ref[...].astype(F32))
        m = sa * ya_ref[...].astype(F32) + sb * yb_ref[...].astype(F32)
        m_ref[...] = m.astype(m_ref.dtype)

    blk = pl.BlockSpec((tr, tc), lambda i, j: (i, j))
    return pl.pallas_call(
        body, name=name, grid=(t_dim // tr, d // tc),
        in_specs=[pl.BlockSpec((tr, tc), lambda i, j: (i, j + ja)), pl.BlockSpec((tr, tc), lambda i, j: (i, j + jb)),
                  blk, blk],
        out_specs=blk, out_shape=jax.ShapeDtypeStruct((t_dim, d), BF16),
        compiler_params=_params("parallel", "parallel"),
    )(z, z, ya, yb)


def gate_bwd(z, ya, yb, dm, off_a, off_b, name):
    t_dim, d = ya.shape
    tr = _pick(t_dim, (512,) + ROW_TILES)
    tc = math.gcd(math.gcd(off_a, off_b), _pick(d, (512, 256, 128)))
    ja, jb = off_a // tc, off_b // tc

    def body(ga_ref, gb_ref, ya_ref, yb_ref, dm_ref, dya_ref, dyb_ref, dga_ref, dgb_ref):
        dm = dm_ref[...].astype(F32)
        sa = _sigmoid(ga_ref[...].astype(F32))
        sb = _sigmoid(gb_ref[...].astype(F32))
        dya_ref[...] = (dm * sa).astype(BF16)
        dyb_ref[...] = (dm * sb).astype(BF16)
        dga_ref[...] = (dm * ya_ref[...].astype(F32) * (sa * (1.0 - sa))).astype(BF16)
        dgb_ref[...] = (dm * yb_ref[...].astype(F32) * (sb * (1.0 - sb))).astype(BF16)

    blk = pl.BlockSpec((tr, tc), lambda i, j: (i, j))
    out = jax.ShapeDtypeStruct((t_dim, d), BF16)
    return pl.pallas_call(
        body, name=name, grid=(t_dim // tr, d // tc),
        in_specs=[pl.BlockSpec((tr, tc), lambda i, j: (i, j + ja)), pl.BlockSpec((tr, tc), lambda i, j: (i, j + jb)),
                  blk, blk, blk],
        out_specs=[blk] * 4, out_shape=[out] * 4, compiler_params=_params("parallel", "parallel"),
    )(z, z, ya, yb, dm)


def _window_sums(e, n_rows, forward):
    def shifted(v, k):
        return pltpu.roll(v, (n_rows - k) if forward else k, 0)

    s2 = e + shifted(e, 1)
    s4 = s2 + shifted(s2, 2)
    s8 = s4 + shifted(s4, 4)
    s16 = s8 + shifted(s8, 8)
    return (s2, s4, s8, s16)


def _pool_rows(t_dim):
    return _pick(t_dim, (256, 128, 64, 32, 16))


def pool_fwd(z, w_group, scale, pw, name):
    t_dim = z.shape[0]
    n_groups, c, _ = w_group.shape
    tr = _pool_rows(t_dim)
    per = tr // POOL_HALO

    def body(cur_ref, prev_ref, w_ref, scale_ref, d_ref, e_ref, yp_ref):
        i = pl.program_id(0)
        cur = cur_ref[...]
        prev = jnp.where(i > 0, prev_ref[...], 0.0)
        ext = jnp.concatenate([prev, cur], axis=0)
        sums = _window_sums(ext, tr + POOL_HALO, forward=False)
        pos = (i * tr + 1 + lax.broadcasted_iota(jnp.int32, (tr, 1), 0)).astype(F32)
        for g, w in enumerate(POOL_WINDOWS):
            cols = slice(g * c, (g + 1) * c)
            cnt = jnp.minimum(pos, float(w))
            d = (sums[g][POOL_HALO:, cols] / cnt - cur[:, cols]).astype(BF16)
            e = jnp.dot(d, w_ref[g], preferred_element_type=F32)
            d_ref[:, cols] = d
            e_ref[:, cols] = e.astype(BF16)
            yp_ref[:, cols] = (e * scale_ref[:, cols]).astype(BF16)

    out = jax.ShapeDtypeStruct((t_dim, pw), BF16)
    return pl.pallas_call(
        body, name=name, grid=(t_dim // tr,),
        in_specs=[_row(tr, pw), pl.BlockSpec((POOL_HALO, pw), lambda i: (jnp.maximum(i * per - 1, 0), 0)),
                  pl.BlockSpec((n_groups, c, c), lambda i: (0, 0, 0)), _vec(pw)],
        out_specs=[_row(tr, pw)] * 3, out_shape=[out] * 3, compiler_params=_params("parallel"),
    )(z, z, w_group, scale.reshape(1, pw))


def pool_bwd(dyp, e, d, w_group, scale, name):
    t_dim, pw = dyp.shape
    n_groups, c, _ = w_group.shape
    tr = _pool_rows(t_dim)
    per = tr // POOL_HALO
    n_tiles = t_dim // tr
    last_halo = t_dim // POOL_HALO - 1
    nt_dims = (((1,), (1,)), ((), ()))
    tn_dims = (((0,), (0,)), ((), ()))

    def body(dyp_ref, nxt_ref, e_ref, d_ref, w_ref, scale_ref, dp_ref, dw_ref, dscale_ref):
        i = pl.program_id(0)

        @pl.when(i == 0)
        def _():
            dw_ref[...] = jnp.zeros_like(dw_ref)
            dscale_ref[...] = jnp.zeros_like(dscale_ref)

        dyp_v = dyp_ref[...].astype(F32)
        dscale_ref[...] += jnp.sum(dyp_v * e_ref[...].astype(F32), axis=0, keepdims=True)
        de_cur = dyp_v * scale_ref[...]
        de_nxt = jnp.where(i < n_tiles - 1, nxt_ref[...].astype(F32) * scale_ref[...], 0.0)
        de = jnp.concatenate([de_cur, de_nxt], axis=0).astype(BF16)
        pos = (i * tr + 1 + lax.broadcasted_iota(jnp.int32, (tr + POOL_HALO, 1), 0)).astype(F32)
        for g, w in enumerate(POOL_WINDOWS):
            cols = slice(g * c, (g + 1) * c)
            de_g = de[:, cols]
            dd = lax.dot_general(de_g, w_ref[g], nt_dims, preferred_element_type=F32)
            dw_ref[g] += lax.dot_general(d_ref[:, cols], de_g[:tr], tn_dims, preferred_element_type=F32)
            q = dd / jnp.minimum(pos, float(w))
            win = _window_sums(q, tr + POOL_HALO, forward=True)[g]
            dp_ref[:, cols] = (win[:tr] - dd[:tr]).astype(BF16)

    return pl.pallas_call(
        body, name=name, grid=(n_tiles,),
        in_specs=[_row(tr, pw), pl.BlockSpec((POOL_HALO, pw), lambda i: (jnp.minimum((i + 1) * per, last_halo), 0)),
                  _row(tr, pw), _row(tr, pw), pl.BlockSpec((n_groups, c, c), lambda i: (0, 0, 0)), _vec(pw)],
        out_specs=[_row(tr, pw), pl.BlockSpec((n_groups, c, c), lambda i: (0, 0, 0)), _vec(pw)],
        out_shape=[jax.ShapeDtypeStruct((t_dim, pw), BF16), jax.ShapeDtypeStruct((n_groups, c, c), F32),
                   jax.ShapeDtypeStruct((1, pw), F32)],
        compiler_params=_params("arbitrary"),
    )(dyp, dyp, e, d, w_group, scale.reshape(1, pw))


def _sgu_rows(t_dim, chunk):
    return chunk * _pick(t_dim // chunk, (2, 1))


def _tril(chunk):
    return lax.broadcasted_iota(jnp.int32, (chunk, chunk), 0) >= lax.broadcasted_iota(jnp.int32, (chunk, chunk), 1)


def sgu_fwd(z, gain, w_s, b_s, pw, sw, name):
    t_dim = z.shape[0]
    n_heads, chunk, _ = w_s.shape
    hd = sw // n_heads
    tr = _sgu_rows(t_dim, chunk)
    ju = pw // sw

    def body(u_ref, v_ref, gain_ref, w_ref, bt_ref, sg_ref):
        ug, _ = _gelu(u_ref[...])
        vg, _ = _gelu(v_ref[...])
        r = lax.rsqrt(jnp.mean(vg * vg, axis=-1, keepdims=True) + EPS)
        vn = ((vg * r) * gain_ref[...]).astype(BF16)
        tri = _tril(chunk)
        for h in range(n_heads):
            wm = jnp.where(tri, w_ref[h], 0.0).astype(BF16)
            cols = slice(h * hd, (h + 1) * hd)
            for ch in range(tr // chunk):
                rows = slice(ch * chunk, (ch + 1) * chunk)
                s = jnp.dot(wm, vn[rows, cols], preferred_element_type=F32) + bt_ref[:, h:h + 1]
                sg_ref[rows, cols] = (ug[rows, cols] * s).astype(BF16)

    return pl.pallas_call(
        body, name=name, grid=(t_dim // tr,),
        in_specs=[pl.BlockSpec((tr, sw), lambda i: (i, ju)), pl.BlockSpec((tr, sw), lambda i: (i, ju + 1)), _vec(sw),
                  pl.BlockSpec((n_heads, chunk, chunk), lambda i: (0, 0, 0)),
                  pl.BlockSpec((chunk, n_heads), lambda i: (0, 0))],
        out_specs=_row(tr, sw), out_shape=jax.ShapeDtypeStruct((t_dim, sw), BF16),
        compiler_params=_params("parallel"),
    )(z, z, gain.reshape(1, sw), w_s, b_s.T)


def sgu_bwd(z, dsg, gain, w_s, b_s, pw, sw, name):
    t_dim = z.shape[0]
    n_heads, chunk, _ = w_s.shape
    hd = sw // n_heads
    tr = _sgu_rows(t_dim, chunk)
    ju = pw // sw
    nt_dims = (((1,), (1,)), ((), ()))
    tn_dims = (((0,), (0,)), ((), ()))

    def body(u_ref, v_ref, dsg_ref, gain_ref, w_ref, bt_ref, du_ref, dv_ref, dw_ref, dbt_ref, dgain_ref,
             dvn_ref, dug_ref):
        @pl.when(pl.program_id(0) == 0)
        def _():
            dw_ref[...] = jnp.zeros_like(dw_ref)
            dbt_ref[...] = jnp.zeros_like(dbt_ref)
            dgain_ref[...] = jnp.zeros_like(dgain_ref)

        u = u_ref[...]
        v = v_ref[...]
        ug, tu = _gelu(u)
        vg, tv = _gelu(v)
        r = lax.rsqrt(jnp.mean(vg * vg, axis=-1, keepdims=True) + EPS)
        n = vg * r
        gain_v = gain_ref[...]
        vn = (n * gain_v).astype(BF16)
        dsg_v = dsg_ref[...].astype(F32)
        tri = _tril(chunk)
        for h in range(n_heads):
            wm = jnp.where(tri, w_ref[h], 0.0).astype(BF16)
            cols = slice(h * hd, (h + 1) * hd)
            for ch in range(tr // chunk):
                rows = slice(ch * chunk, (ch + 1) * chunk)
                vn_b = vn[rows, cols]
                s = jnp.dot(wm, vn_b, preferred_element_type=F32) + bt_ref[:, h:h + 1]
                dsg_b = dsg_v[rows, cols]
                dug_ref[rows, cols] = dsg_b * s
                ds = dsg_b * ug[rows, cols]
                ds_b = ds.astype(BF16)
                dw_ref[h] += jnp.where(tri, lax.dot_general(ds_b, vn_b, nt_dims, preferred_element_type=F32), 0.0)
                dbt_ref[:, h:h + 1] += jnp.sum(ds, axis=1, keepdims=True)
                dvn_ref[rows, cols] = lax.dot_general(wm, ds_b, tn_dims, preferred_element_type=F32)
        dvn = dvn_ref[...]
        dgain_ref[...] += jnp.sum(dvn * n, axis=0, keepdims=True)
        dn = dvn * gain_v
        dvg = r * (dn - n * jnp.mean(dn * n, axis=-1, keepdims=True))
        dv_ref[...] = (dvg * _gelu_grad(v, tv)).astype(BF16)
        du_ref[...] = (dug_ref[...] * _gelu_grad(u, tu)).astype(BF16)

    full_w = pl.BlockSpec((n_heads, chunk, chunk), lambda i: (0, 0, 0))
    full_b = pl.BlockSpec((chunk, n_heads), lambda i: (0, 0))
    du, dv, dw, dbt, dgain = pl.pallas_call(
        body, name=name, grid=(t_dim // tr,),
        in_specs=[pl.BlockSpec((tr, sw), lambda i: (i, ju)), pl.BlockSpec((tr, sw), lambda i: (i, ju + 1)),
                  _row(tr, sw), _vec(sw), full_w, full_b],
        out_specs=[_row(tr, sw), _row(tr, sw), full_w, full_b, _vec(sw)],
        out_shape=[jax.ShapeDtypeStruct((t_dim, sw), BF16), jax.ShapeDtypeStruct((t_dim, sw), BF16),
                   jax.ShapeDtypeStruct((n_heads, chunk, chunk), F32), jax.ShapeDtypeStruct((chunk, n_heads), F32),
                   jax.ShapeDtypeStruct((1, sw), F32)],
        scratch_shapes=[pltpu.VMEM((tr, sw), F32), pltpu.VMEM((tr, sw), F32)],
        compiler_params=_params("arbitrary"),
    )(z, z, dsg, gain.reshape(1, sw), w_s, b_s.T)
    return du, dv, dw, dbt.T, dgain.reshape(sw)


HBM_SPEC = pl.BlockSpec(memory_space=pltpu.HBM)
SEM_SPEC = pl.BlockSpec(memory_space=pltpu.SEMAPHORE)
ANY_SPEC = pl.BlockSpec(memory_space=pl.ANY)
DATAFLOW = pltpu.SideEffectType.DATAFLOW_SIDE_EFFECTING


def _hbm(a):
    return pltpu.with_memory_space_constraint(a, pltpu.HBM)


def _slot(ref, axis, k, n):
    idx = [slice(None)] * len(ref.shape)
    idx[axis] = pl.ds(k * n, n)
    return ref.at[tuple(idx)]


def _slot_half(ref, axis, k, half):
    idx = [slice(None)] * len(ref.shape)
    n = ref.shape[axis] // N_CHIPS
    if axis == 0:
        idx[0] = pl.ds(k * n + half * (n // 2), n // 2)
    else:
        idx[axis] = pl.ds(k * n, n)
        idx[0] = pl.ds(half * (ref.shape[0] // 2), ref.shape[0] // 2)
    return ref.at[tuple(idx)]


def _chip_of(k, core):
    return (k // 2, k % 2, core)


def _my_chip():
    return 2 * lax.axis_index("x") + lax.axis_index("y")


class Exchange:
    def __init__(self, gather, axes):
        self.gather, self.axes = gather, axes
        self.per = 1 if gather else 2

    def bufs(self, t, refs):
        return refs[t * self.per:(t + 1) * self.per]

    def branches(self, me, core):
        if self.gather:
            return [(k, h, (me == k) & (core == h)) for k in range(N_CHIPS) for h in range(2)]
        return [(k, None, me == k) for k in range(N_CHIPS)]

    def views(self, t, bufs, src_chip, dst_chip, half):
        ax = self.axes[t]
        if self.gather:
            slot = _slot_half(bufs[0], ax, src_chip, half)
            return slot, slot
        return _slot(bufs[0], ax, dst_chip, bufs[0].shape[ax] // N_CHIPS), bufs[1].at[src_chip]


def exchange_start(bufs, ex, groups, name, deps=()):
    nb, ng = len(bufs), len(groups)

    def body(*refs):
        ins, outs = refs[:nb], refs[nb + len(deps):]
        sems, token = outs[:2 * ng], outs[-1]
        core = lax.axis_index("c")
        me = _my_chip()
        for k, half, mine in ex.branches(me, core):
            @pl.when(mine)
            def _(k=k, half=half):
                for gi, group in enumerate(groups):
                    for ti, t in enumerate(group):
                        for j in range(N_CHIPS):
                            if j != k:
                                src, dst = ex.views(t, ex.bufs(t, ins), k, j, half)
                                pltpu.make_async_remote_copy(
                                    src_ref=src, dst_ref=dst, send_sem=sems[2 * gi].at[ti * N_CHIPS + j],
                                    recv_sem=sems[2 * gi + 1].at[ti * N_CHIPS + k], device_id=_chip_of(j, core),
                                    device_id_type=MESH).start()
        token[...] = jnp.zeros_like(token)

    sem_shapes = []
    for group in groups:
        sem_shapes += [pltpu.SemaphoreType.DMA((len(group) * N_CHIPS,))] * 2
    thru = [pltpu.HBM(a.shape, a.dtype) for a in bufs]
    outs = pl.pallas_call(
        body, name=name, out_shape=sem_shapes + thru + [jax.ShapeDtypeStruct((SUBLANE, LANE), F32)],
        in_specs=[HBM_SPEC] * nb + [ANY_SPEC] * len(deps),
        out_specs=[SEM_SPEC] * (2 * ng) + [HBM_SPEC] * nb + [pl.BlockSpec(memory_space=pltpu.VMEM)],
        input_output_aliases={i: 2 * ng + i for i in range(nb)},
        compiler_params=pltpu.CompilerParams(has_side_effects=DATAFLOW),
    )(*[_hbm(a) for a in bufs], *deps)
    sems = [(outs[2 * gi], outs[2 * gi + 1]) for gi in range(ng)]
    return sems, outs[2 * ng:2 * ng + nb], outs[-1]


def exchange_wait(bufs, tensors, ex, sems, after, name):
    nb = len(bufs)
    send_sems, recv_sems = sems

    def body(*refs):
        ins, send_ref, recv_ref = refs[:nb], refs[nb], refs[nb + 1]
        core = lax.axis_index("c")
        me = _my_chip()
        for k, half, mine in ex.branches(me, core):
            @pl.when(mine)
            def _(k=k, half=half):
                for ti, t in enumerate(tensors):
                    for j in range(N_CHIPS):
                        if j != k:
                            src, _ = ex.views(t, ex.bufs(ti, ins), k, j, half)
                            _, dst = ex.views(t, ex.bufs(ti, ins), j, k, half)
                            copy = pltpu.make_async_remote_copy(
                                src_ref=src, dst_ref=dst, send_sem=send_ref.at[ti * N_CHIPS + j],
                                recv_sem=recv_ref.at[ti * N_CHIPS + j],
                                device_id=_chip_of(j, core), device_id_type=MESH)
                            copy.wait_send()
                            copy.wait_recv()

    return pl.pallas_call(
        body, name=name, out_shape=[pltpu.HBM(a.shape, a.dtype) for a in bufs],
        in_specs=[HBM_SPEC] * nb + [SEM_SPEC, SEM_SPEC] + [ANY_SPEC] * len(after),
        out_specs=[HBM_SPEC] * nb, input_output_aliases={i: i for i in range(nb)},
        compiler_params=pltpu.CompilerParams(has_side_effects=DATAFLOW),
    )(*bufs, send_sems, recv_sems, *after)


def share_halves(bufs, axes, name):
    nb = len(bufs)

    def body(*refs):
        outs = refs[nb:2 * nb]
        send_sems, recv_sems = refs[2 * nb:]
        core = lax.axis_index("c")
        me = _my_chip()
        sibling = (lax.axis_index("x"), lax.axis_index("y"), 1 - core)
        for k in range(N_CHIPS):
            for half in range(2):
                @pl.when((me == k) & (core == half))
                def _(k=k, half=half):
                    def copy(t, j, h):
                        part = _slot_half(outs[t], axes[t], j, h)
                        return pltpu.make_async_remote_copy(
                            src_ref=part, dst_ref=part, send_sem=send_sems.at[t * N_CHIPS + j],
                            recv_sem=recv_sems.at[t * N_CHIPS + j], device_id=sibling, device_id_type=MESH)

                    pairs = [(t, j) for t in range(nb) for j in range(N_CHIPS) if j != k]
                    for t, j in pairs:
                        copy(t, j, half).start()
                    for t, j in pairs:
                        copy(t, j, 1 - half).wait_recv()
                    for t, j in pairs:
                        copy(t, j, half).wait_send()

    return pl.pallas_call(
        body, name=name, out_shape=[jax.ShapeDtypeStruct(a.shape, a.dtype) for a in bufs],
        in_specs=[HBM_SPEC] * nb, out_specs=[HBM_SPEC] * nb, input_output_aliases={i: i for i in range(nb)},
        scratch_shapes=[pltpu.SemaphoreType.DMA((nb * N_CHIPS,)), pltpu.SemaphoreType.DMA((nb * N_CHIPS,))],
        compiler_params=pltpu.CompilerParams(has_side_effects=True),
    )(*bufs)


def _share_copy(bufs, axes, send_sems, recv_sems, sibling, t, j, half):
    part = _slot_half(bufs[t], axes[t], j, half)
    return pltpu.make_async_remote_copy(src_ref=part, dst_ref=part, send_sem=send_sems.at[t * N_CHIPS + j],
                                        recv_sem=recv_sems.at[t * N_CHIPS + j], device_id=sibling, device_id_type=MESH)


def share_start(bufs, axes, name):
    nb = len(bufs)

    def body(*refs):
        ins, outs = refs[:nb], refs[nb:]
        send_sems, recv_sems, token = outs[0], outs[1], outs[-1]
        core = lax.axis_index("c")
        me = _my_chip()
        sibling = (lax.axis_index("x"), lax.axis_index("y"), 1 - core)
        for k in range(N_CHIPS):
            for half in range(2):
                @pl.when((me == k) & (core == half))
                def _(k=k, half=half):
                    for t in range(nb):
                        for j in range(N_CHIPS):
                            if j != k:
                                _share_copy(ins, axes, send_sems, recv_sems, sibling, t, j, half).start()
        token[...] = jnp.zeros_like(token)

    outs = pl.pallas_call(
        body, name=name,
        out_shape=[pltpu.SemaphoreType.DMA((nb * N_CHIPS,))] * 2 + [pltpu.HBM(a.shape, a.dtype) for a in bufs]
        + [jax.ShapeDtypeStruct((SUBLANE, LANE), F32)],
        in_specs=[HBM_SPEC] * nb,
        out_specs=[SEM_SPEC, SEM_SPEC] + [HBM_SPEC] * nb + [pl.BlockSpec(memory_space=pltpu.VMEM)],
        input_output_aliases={i: 2 + i for i in range(nb)},
        compiler_params=pltpu.CompilerParams(has_side_effects=DATAFLOW),
    )(*[_hbm(a) for a in bufs])
    return outs[0], outs[1], outs[2:2 + nb], outs[-1]


def share_wait(send_sems, recv_sems, bufs, axes, after, name):
    nb = len(bufs)

    def body(*refs):
        ins, send_ref, recv_ref = refs[:nb], refs[nb], refs[nb + 1]
        core = lax.axis_index("c")
        me = _my_chip()
        sibling = (lax.axis_index("x"), lax.axis_index("y"), 1 - core)
        for k in range(N_CHIPS):
            for half in range(2):
                @pl.when((me == k) & (core == half))
                def _(k=k, half=half):
                    for t in range(nb):
                        for j in range(N_CHIPS):
                            if j != k:
                                _share_copy(ins, axes, send_ref, recv_ref, sibling, t, j, 1 - half).wait_recv()
                                _share_copy(ins, axes, send_ref, recv_ref, sibling, t, j, half).wait_send()

    return pl.pallas_call(
        body, name=name, out_shape=[pltpu.HBM(a.shape, a.dtype) for a in bufs],
        in_specs=[HBM_SPEC] * nb + [SEM_SPEC, SEM_SPEC] + [ANY_SPEC] * len(after),
        out_specs=[HBM_SPEC] * nb, input_output_aliases={i: i for i in range(nb)},
        compiler_params=pltpu.CompilerParams(has_side_effects=DATAFLOW),
    )(*bufs, send_sems, recv_sems, *after)


def swap_start(arrs, name):
    nt = len(arrs)

    def body(*refs):
        ins, lands, outs = refs[:nt], refs[nt:2 * nt], refs[2 * nt:]
        send_sems, recv_sems, token = outs[0], outs[1], outs[-1]
        sibling = (lax.axis_index("x"), lax.axis_index("y"), 1 - lax.axis_index("c"))
        for t in range(nt):
            pltpu.make_async_remote_copy(src_ref=ins[t], dst_ref=lands[t], send_sem=send_sems.at[t],
                                         recv_sem=recv_sems.at[t], device_id=sibling, device_id_type=MESH).start()
        token[...] = jnp.zeros_like(token)

    thru = [pltpu.HBM(a.shape, a.dtype) for a in arrs] * 2
    outs = pl.pallas_call(
        body, name=name,
        out_shape=[pltpu.SemaphoreType.DMA((nt,))] * 2 + thru + [jax.ShapeDtypeStruct((SUBLANE, LANE), F32)],
        in_specs=[HBM_SPEC] * (2 * nt),
        out_specs=[SEM_SPEC, SEM_SPEC] + [HBM_SPEC] * (2 * nt) + [pl.BlockSpec(memory_space=pltpu.VMEM)],
        input_output_aliases={i: 2 + i for i in range(2 * nt)},
        compiler_params=pltpu.CompilerParams(has_side_effects=DATAFLOW),
    )(*[_hbm(a) for a in arrs], *[_hbm(lax.empty(a.shape, a.dtype)) for a in arrs])
    return outs[0], outs[1], outs[2:2 + nt], outs[2 + nt:2 + 2 * nt], outs[-1]


def swap_wait(send_sems, recv_sems, arrs, lands, after, name):
    nt = len(arrs)

    def body(*refs):
        ins, land_refs, send_ref, recv_ref = refs[:nt], refs[nt:2 * nt], refs[2 * nt], refs[2 * nt + 1]
        sibling = (lax.axis_index("x"), lax.axis_index("y"), 1 - lax.axis_index("c"))
        for t in range(nt):
            copy = pltpu.make_async_remote_copy(src_ref=ins[t], dst_ref=land_refs[t], send_sem=send_ref.at[t],
                                                recv_sem=recv_ref.at[t], device_id=sibling,
                                                device_id_type=MESH)
            copy.wait_send()
            copy.wait_recv()

    outs = pl.pallas_call(
        body, name=name, out_shape=[pltpu.HBM(a.shape, a.dtype) for a in list(arrs) + list(lands)],
        in_specs=[HBM_SPEC] * (2 * nt) + [SEM_SPEC, SEM_SPEC] + [ANY_SPEC] * len(after),
        out_specs=[HBM_SPEC] * (2 * nt), input_output_aliases={i: i for i in range(2 * nt)},
        compiler_params=pltpu.CompilerParams(has_side_effects=DATAFLOW),
    )(*arrs, *lands, send_sems, recv_sems, *after)
    return outs[:nt], outs[nt:]


def place_shard(w_stack, layer, axis, me, name, deps=()):
    _, a_dim, r, c = w_stack.shape
    tr, tc = _stream_tiles(r, c)
    nr, nc = r // tr, c // tc
    full = (a_dim, r * N_CHIPS, c) if axis == 1 else (a_dim, r, c * N_CHIPS)

    def body(me_ref, w_ref, *rest):
        o_ref = rest[-1]
        o_ref[...] = w_ref[...].astype(BF16)

    def own_map(a, i, j, me_ref):
        return (a, me_ref[0] * nr + i, j) if axis == 1 else (a, i, me_ref[0] * nc + j)

    return pl.pallas_call(
        body, name=name, out_shape=jax.ShapeDtypeStruct(full, BF16),
        grid_spec=pltpu.PrefetchScalarGridSpec(
            num_scalar_prefetch=1, grid=(a_dim, nr, nc),
            in_specs=[pl.BlockSpec((None, None, tr, tc), lambda a, i, j, me_ref: (layer, a, i, j))]
            + [ANY_SPEC] * len(deps),
            out_specs=pl.BlockSpec((None, tr, tc), own_map)),
        compiler_params=_params("parallel", "parallel", "parallel"),
    )(me, w_stack, *deps)


def sum_pieces(stack, full, axis, me, name):
    _, a_dim, r, c = stack.shape
    tr, tc = _stream_tiles(r, c)
    nr, nc = r // tr, c // tc

    def body(me_ref, own_ref, s1_ref, s2_ref, s3_ref, o_ref):
        acc = own_ref[...].astype(F32)
        for ref in (s1_ref, s2_ref, s3_ref):
            acc = acc + ref[...].astype(F32)
        o_ref[...] = acc.astype(BF16)

    def own_map(a, i, j, me_ref):
        return (a, me_ref[0] * nr + i, j) if axis == 1 else (a, i, me_ref[0] * nc + j)

    def from_chip(step):
        return pl.BlockSpec((None, None, tr, tc), lambda a, i, j, me_ref: ((me_ref[0] + step) % N_CHIPS, a, i, j))

    return pl.pallas_call(
        body, name=name, out_shape=jax.ShapeDtypeStruct((a_dim, r, c), BF16),
        grid_spec=pltpu.PrefetchScalarGridSpec(
            num_scalar_prefetch=1, grid=(a_dim, nr, nc),
            in_specs=[pl.BlockSpec((None, tr, tc), own_map), from_chip(1), from_chip(2), from_chip(3)],
            out_specs=pl.BlockSpec((None, tr, tc), lambda a, i, j, me_ref: (a, i, j))),
        compiler_params=_params("parallel", "parallel", "parallel"),
    )(me, full, stack, stack, stack)


def all_reduce_small(x, name):
    rows, lanes = x.shape

    def body(x_ref, sum_ref, gath_ref, send_sems, recv_sems, local_sem):
        cx, cy, cc = lax.axis_index("x"), lax.axis_index("y"), lax.axis_index("c")
        me, sibling = (cx, cy, cc), (cx, cy, 1 - cc)
        chips = [(1 - cx, cy), (cx, 1 - cy), (1 - cx, 1 - cy)]

        def block(px, py, pc):
            return gath_ref.at[pl.ds(pl.multiple_of((4 * px + 2 * py + pc) * rows, SUBLANE), rows), :]

        def copy(k, blk, to, src=None):
            return pltpu.make_async_remote_copy(
                src_ref=block(*blk) if src is None else src, dst_ref=block(*blk),
                send_sem=send_sems.at[k], recv_sem=recv_sems.at[k], device_id=to, device_id_type=MESH)

        mine = pltpu.make_async_copy(x_ref, block(*me), local_sem)
        mine.start()
        first = [copy(0, me, sibling, src=x_ref)]
        first += [copy(1 + j, me, (*chip, cc), src=x_ref) for j, chip in enumerate(chips)]
        for cp in first:
            cp.start()
        passed = [copy(4 + j, (*chip, cc), sibling) for j, chip in enumerate(chips)]
        for j, chip in enumerate(chips):
            copy(1 + j, (*chip, cc), me).wait_recv()
            passed[j].start()
        copy(0, sibling, me).wait_recv()
        for j, chip in enumerate(chips):
            copy(4 + j, (*chip, 1 - cc), me).wait_recv()
        for cp in first + passed:
            cp.wait_send()
        mine.wait()
        acc = gath_ref[pl.ds(0, rows), :]
        for k in range(1, N_DEV):
            acc = acc + gath_ref[pl.ds(k * rows, rows), :]
        sum_ref[...] = acc

    return pl.pallas_call(
        body, name=name, out_shape=jax.ShapeDtypeStruct((rows, lanes), F32),
        in_specs=[pl.BlockSpec(memory_space=pltpu.VMEM)], out_specs=pl.BlockSpec(memory_space=pltpu.VMEM),
        scratch_shapes=[pltpu.VMEM((N_DEV * rows, lanes), F32), pltpu.SemaphoreType.DMA((7,)),
                        pltpu.SemaphoreType.DMA((7,)), pltpu.SemaphoreType.DMA],
        compiler_params=pltpu.CompilerParams(has_side_effects=True, vmem_limit_bytes=VMEM_LIMIT),
    )(x)


def _adamw(w, g, m, v):
    m = ADAM_B1 * m + (1.0 - ADAM_B1) * g
    v = ADAM_B2 * v + (1.0 - ADAM_B2) * (g * g)
    m_hat = m / (1.0 - ADAM_B1 ** ADAM_STEP)
    v_hat = v / (1.0 - ADAM_B2 ** ADAM_STEP)
    delta = -ADAM_LR * (m_hat / (jnp.sqrt(v_hat) + ADAM_EPS) + ADAM_WD * w)
    return delta, m, v


def adam_big(p_own, p_sib, w, m, v, layer, stacks, name):
    r, c = p_own.shape
    tr, tc = _stream_tiles(r, c)

    def body(p_ref, q_ref, w_ref, m_ref, v_ref, *rest):
        g_out, d_out, m_out, v_out = rest[4:]
        g = p_ref[...].astype(F32) + q_ref[...].astype(F32)
        delta, m_new, v_new = _adamw(w_ref[...], g, m_ref[...], v_ref[...])
        g_out[...] = g
        d_out[...] = delta
        m_out[...] = m_new
        v_out[...] = v_new

    flat = pl.BlockSpec((tr, tc), lambda i, j: (i, j))
    layered = pl.BlockSpec((None, tr, tc), lambda i, j: (layer, i, j))
    anyspace = pl.BlockSpec(memory_space=pl.ANY)
    out = jax.ShapeDtypeStruct(w.shape, F32)
    return pl.pallas_call(
        body, name=name, grid=(r // tr, c // tc),
        in_specs=[flat, flat, layered, layered, layered] + [anyspace] * 4,
        out_specs=[layered] * 4, out_shape=[out] * 4, input_output_aliases={5: 0, 6: 1, 7: 2, 8: 3},
        compiler_params=_params("parallel", "parallel"),
    )(p_own, p_sib, w, m, v, *stacks)


def adam_small(g, w, m, v, name):
    rows, lanes = g.shape
    tr = _pick(rows, (512,) + ROW_TILES)

    def body(g_ref, w_ref, m_ref, v_ref, d_out, m_out, v_out):
        delta, m_new, v_new = _adamw(w_ref[...], g_ref[...], m_ref[...], v_ref[...])
        d_out[...] = delta
        m_out[...] = m_new
        v_out[...] = v_new

    out = jax.ShapeDtypeStruct((rows, lanes), F32)
    return pl.pallas_call(
        body, name=name, grid=(rows // tr,), in_specs=[_row(tr, lanes)] * 4, out_specs=[_row(tr, lanes)] * 3,
        out_shape=[out] * 3, compiler_params=_params("parallel"),
    )(g, w, m, v)


def ffn_fwd(x, h, w_up, w_down, g_post, g_next, tag, deps=()):
    gate, up, a = matmul_swiglu(h, w_up, name=f"{tag}_up", deps=deps)
    f = matmul(a, w_down, out_dtype=F32, name=f"{tag}_down")
    y, h_next = res_rms_fwd(x, f, g_post, MACARON_WEIGHT, g_next, name=f"{tag}_res_fwd")
    return y, h_next, (x, h, gate, up, a, f)


def ffn_bwd(dy, head, saved, g_pre, w_up, w_down, g_post, tag, deps=(), chain=None):
    x, h, gate, up, a, f = saved
    if head is None:
        df, dg_post = rms_bwd(f, g_post, dy, MACARON_WEIGHT, None, BF16, name=f"{tag}_res_bwd", deps=deps)
        deps = ()
    else:
        df, dg_post = head
    da = matmul(df, w_down, tb=True, out_dtype=BF16, name=f"{tag}_down_dx", deps=deps)
    dw_down = matmul(a, df, ta=True, out_dtype=BF16, name=f"{tag}_down_dw")
    dz = swiglu_bwd(gate, up, da, name=f"{tag}_swiglu_bwd")
    dh = matmul(dz, w_up, tb=True, out_dtype=F32, name=f"{tag}_up_dx")
    dw_up = matmul(h, dz, ta=True, out_dtype=BF16, name=f"{tag}_up_dw")
    dx, dg_pre, *tail = rms_bwd(x, g_pre, dh, 1.0, dy, F32, name=f"{tag}_rms_bwd", chain=chain)
    return dx, dw_up, dw_down, dg_pre, dg_post, (tail[0] if tail else None)


def mixer_fwd(x, h, wt, sm, g_next, dims, deps=()):
    pw, sw, d = dims
    z = matmul(h, wt['w_in'], out_dtype=F32, name="mix_in", deps=deps, cols=(0, pw + 2 * sw))
    zg = matmul(h, wt['w_in'], out_dtype=BF16, name="mix_in_gates", cols=(pw + 2 * sw, 2 * d))
    dd, e, yp = pool_fwd(z, wt['pool_group_w'], sm['pool_scale'], pw, name="mix_pool_fwd")
    ya = matmul(yp, wt['w_pool_out'], out_dtype=BF16, name="mix_pool_out")
    sg = sgu_fwd(z, sm['sgu_v_gain'], sm['sgu_w_s'], sm['sgu_b_s'], pw, sw, name="mix_sgu_fwd")
    yb = matmul(sg, wt['w_sgu_out'], out_dtype=BF16, name="mix_sgu_out")
    m = gate_fwd(zg, ya, yb, 0, d, name="mix_gate_fwd")
    o = matmul(m, wt['w_out'], out_dtype=F32, name="mix_out")
    y, h_next = res_rms_fwd(x, o, sm['g_mix_post'], 1.0, g_next, name="mix_res_fwd")
    return y, h_next, (x, h, z, zg, dd, e, yp, sg, ya, yb, m, o)


def mixer_bwd(dy, head, saved, wt, sm, dims, deps=(), chain=None):
    pw, sw, d = dims
    x, h, z, zg, dd, e, yp, sg, ya, yb, m, o = saved
    grads = {}
    if head is None:
        do, grads['g_mix_post'] = rms_bwd(o, sm['g_mix_post'], dy, 1.0, None, BF16, name="mix_res_bwd", deps=deps)
        deps = ()
    else:
        do, grads['g_mix_post'] = head
    dm = matmul(do, wt['w_out'], tb=True, out_dtype=BF16, name="mix_out_dx", deps=deps)
    grads['w_out'] = matmul(m, do, ta=True, out_dtype=BF16, name="mix_out_dw")
    dya, dyb, dga, dgb = gate_bwd(zg, ya, yb, dm, 0, d, name="mix_gate_bwd")
    dyp = matmul(dya, wt['w_pool_out'], tb=True, out_dtype=BF16, name="mix_pool_out_dx")
    grads['w_pool_out'] = matmul(yp, dya, ta=True, out_dtype=BF16, name="mix_pool_out_dw")
    dp, dwg, dscale = pool_bwd(dyp, e, dd, wt['pool_group_w'], sm['pool_scale'], name="mix_pool_bwd")
    grads['pool_group_w'] = dwg.astype(BF16)
    grads['pool_scale'] = dscale.reshape(pw)
    dsg = matmul(dyb, wt['w_sgu_out'], tb=True, out_dtype=BF16, name="mix_sgu_out_dx")
    grads['w_sgu_out'] = matmul(sg, dyb, ta=True, out_dtype=BF16, name="mix_sgu_out_dw")
    du, dv, grads['sgu_w_s'], grads['sgu_b_s'], grads['sgu_v_gain'] = sgu_bwd(
        z, dsg, sm['sgu_v_gain'], sm['sgu_w_s'], sm['sgu_b_s'], pw, sw, name="mix_sgu_bwd")
    dz = jnp.concatenate([dp, du, dv, dga, dgb], axis=1)
    dh = matmul(dz, wt['w_in'], tb=True, out_dtype=F32, name="mix_in_dx")
    grads['w_in'] = matmul(h, dz, ta=True, out_dtype=BF16, name="mix_in_dw")
    dx, grads['g_mix_pre'], *tail = rms_bwd(x, sm['g_mix_pre'], dh, 1.0, dy, F32, name="mix_rms_bwd", chain=chain)
    return dx, grads, (tail[0] if tail else None)


def _as_rows(a):
    return a.reshape(a.shape[0], -1, a.shape[-1])


def kernel(x, g_ffn1_pre, w_ffn1_up, w_ffn1_down, g_ffn1_post, g_mix_pre, w_in, pool_group_w, pool_scale, w_pool_out, sgu_v_gain, sgu_w_s, sgu_b_s, w_sgu_out, w_out, g_mix_post, g_ffn2_pre, w_ffn2_up, w_ffn2_down, g_ffn2_post, loss_target, m_g_ffn1_pre, m_w_ffn1_up, m_w_ffn1_down, m_g_ffn1_post, m_g_mix_pre, m_w_in, m_pool_group_w, m_pool_scale, m_w_pool_out, m_sgu_v_gain, m_sgu_w_s, m_sgu_b_s, m_w_sgu_out, m_w_out, m_g_mix_post, m_g_ffn2_pre, m_w_ffn2_up, m_w_ffn2_down, m_g_ffn2_post, v_g_ffn1_pre, v_w_ffn1_up, v_w_ffn1_down, v_g_ffn1_post, v_g_mix_pre, v_w_in, v_pool_group_w, v_pool_scale, v_w_pool_out, v_sgu_v_gain, v_sgu_w_s, v_sgu_b_s, v_w_sgu_out, v_w_out, v_g_mix_post, v_g_ffn2_pre, v_w_ffn2_up, v_w_ffn2_down, v_g_ffn2_post):
    w = dict(zip(WEIGHTS, (g_ffn1_pre, w_ffn1_up, w_ffn1_down, g_ffn1_post, g_mix_pre, w_in, pool_group_w, pool_scale, w_pool_out, sgu_v_gain, sgu_w_s, sgu_b_s, w_sgu_out, w_out, g_mix_post, g_ffn2_pre, w_ffn2_up, w_ffn2_down, g_ffn2_post)))
    mom = dict(zip(WEIGHTS, (m_g_ffn1_pre, m_w_ffn1_up, m_w_ffn1_down, m_g_ffn1_post, m_g_mix_pre, m_w_in, m_pool_group_w, m_pool_scale, m_w_pool_out, m_sgu_v_gain, m_sgu_w_s, m_sgu_b_s, m_w_sgu_out, m_w_out, m_g_mix_post, m_g_ffn2_pre, m_w_ffn2_up, m_w_ffn2_down, m_g_ffn2_post)))
    var = dict(zip(WEIGHTS, (v_g_ffn1_pre, v_w_ffn1_up, v_w_ffn1_down, v_g_ffn1_post, v_g_mix_pre, v_w_in, v_pool_group_w, v_pool_scale, v_w_pool_out, v_sgu_v_gain, v_sgu_w_s, v_sgu_b_s, v_w_sgu_out, v_w_out, v_g_mix_post, v_g_ffn2_pre, v_w_ffn2_up, v_w_ffn2_down, v_g_ffn2_post)))
    depth = g_ffn1_pre.shape[0]
    d = x.shape[-1]
    pw = pool_scale.shape[-1]
    sw = sgu_v_gain.shape[-1]
    dims = (pw, sw, d)
    axes = [BIG_AXIS[n] for n in BIG]

    me = 2 * lax.axis_index("x") + lax.axis_index("y")
    gather = Exchange(True, axes)
    group_ids = [[BIG.index(n) for n in group] for group in BLOCK_WEIGHTS]

    me1 = me.reshape(1).astype(jnp.int32)

    def start_gather(l, deps=()):
        lands = []
        for n, ax in zip(BIG, axes):
            shards = w[n] if w[n].ndim == 4 else w[n][:, None]
            full = place_shard(shards, l, ax + 4 - w[n].ndim, me1, name=f"place_{n}", deps=deps)
            lands.append(full if w[n].ndim == 4 else full[0])
        return exchange_start(lands, gather, group_ids, name=f"gather_start_l{l}", deps=deps)

    def arrived(l, gi, after):
        sems, lands, _ = gathers[l]
        ids = group_ids[gi]
        return exchange_wait([lands[t] for t in ids], ids, gather, sems[gi], (after,), name=f"gather_wait_l{l}_b{gi}")

    def block_axes(gi):
        return [axes[t] for t in group_ids[gi]]

    small = [{n: w[n][l] for n in SMALL} for l in range(depth)]

    act = x[0]
    saved, full = [], []
    gathers = []
    for l in range(depth):
        gathers.append(start_gather(l, (gathers[-1][2],) if gathers else ()))
    h = rms_fwd(act, small[0]['g_ffn1_pre'], name="first_rms_fwd", deps=(gathers[-1][2],))
    blocks = [(l, gi) for l in range(depth) for gi in range(len(BLOCK_WEIGHTS))]
    sharing = None
    for n, (l, gi) in enumerate(blocks):
        sm = small[l]
        if sharing is None:
            got = share_halves(arrived(l, gi, act), block_axes(gi), name=f"share_halves_b{gi}")
        else:
            got = share_wait(*sharing, block_axes(gi), (act,), name=f"share_wait_l{l}_b{gi}")
        wt = dict(zip(BLOCK_WEIGHTS[gi], got))
        sharing, deps = None, ()
        if SHARE_AHEAD_FROM <= n + 1 < len(blocks):
            l2, gi2 = blocks[n + 1]
            started = share_start(arrived(l2, gi2, act), block_axes(gi2), name=f"share_start_l{l2}_b{gi2}")
            sharing, deps = started[:3], (started[3],)
        if gi == 0:
            full.append({})
            saved.append([])
            act, h, s = ffn_fwd(act, h, wt['w_ffn1_up'], wt['w_ffn1_down'], sm['g_ffn1_post'], sm['g_mix_pre'],
                                "ffn1", deps)
        elif gi == 1:
            act, h, s = mixer_fwd(act, h, wt, sm, sm['g_ffn2_pre'], dims, deps)
        else:
            g_after = small[l + 1]['g_ffn1_pre'] if l + 1 < depth else None
            act, h, s = ffn_fwd(act, h, wt['w_ffn2_up'], wt['w_ffn2_down'], sm['g_ffn2_post'], g_after, "ffn2", deps)
        full[l].update(wt)
        saved[l].append(s)
    dact, sq_sum = loss_grad(act, loss_target[0], name="loss_grad")
    loss = lax.psum(0.5 * sq_sum / d, ("x", "y", "c"))

    rows3 = {n: (_as_rows(w[n]), _as_rows(mom[n]), _as_rows(var[n])) for n in BIG}
    stacks = {n: tuple(lax.empty(rows3[n][0].shape, F32) for _ in range(4)) for n in BIG}
    small_grads = [{} for _ in range(depth)]

    def start_scatter(names, grads, l, block):
        ex = Exchange(False, [BIG_AXIS[n] for n in names])
        bufs = []
        for n in names:
            piece = list(grads[n].shape)
            piece[BIG_AXIS[n]] //= N_CHIPS
            bufs += [grads[n], lax.empty((N_CHIPS, *piece), BF16)]
        return (names, ex, l, block) + exchange_start(bufs, ex, [list(range(len(names)))],
                                                      name=f"scatter_start_l{l}_b{block}")

    def finish_scatter(pending, after):
        names, ex, l, block, sems, bufs, _ = pending
        got = exchange_wait(bufs, list(range(len(names))), ex, sems[0], after, name=f"scatter_wait_l{l}_b{block}")
        plane = []
        for ti, n in enumerate(names):
            full_g, stack = got[2 * ti], got[2 * ti + 1]
            if full_g.ndim == 2:
                full_g, stack = full_g[None], stack[:, None]
            p = sum_pieces(stack, full_g, BIG_AXIS[n] + 3 - got[2 * ti].ndim, me1, name=f"sum_{n}")
            plane.append(p.reshape(-1, p.shape[-1]))
        return (names, l, block) + swap_start(plane, name=f"swap_start_l{l}_b{block}")

    def finish_swap(swapping, after):
        names, l, block, send_sems, recv_sems, plane, lands, _ = swapping
        plane, other = swap_wait(send_sems, recv_sems, plane, lands, after, name=f"swap_wait_l{l}_b{block}")
        for n, p_own, p_sib in zip(names, plane, other):
            stacks[n] = tuple(adam_big(p_own, p_sib, *rows3[n], l, stacks[n], name=f"adam_{n}"))
        return stacks[names[-1]][0]

    def advance(pending, swapping, done, after):
        started = finish_scatter(pending, (after,) + done)
        if swapping is not None:
            done = (finish_swap(swapping, (started[-1],)),)
        return started, done

    pending, swapping, done, head = None, None, (), None
    for l in reversed(range(depth)):
        wt, sm = full[l], small[l]
        s1, s2, s3 = saved[l]
        for block in (2, 1, 0):
            deps = (pending[-1],) if pending is not None else ()
            g = {}
            if block == 2:
                chain = (s2[-1], sm['g_mix_post'], 1.0)
                dnew, g['w_ffn2_up'], g['w_ffn2_down'], g['g_ffn2_pre'], g['g_ffn2_post'], head = ffn_bwd(
                    dact, head, s3, sm['g_ffn2_pre'], wt['w_ffn2_up'], wt['w_ffn2_down'], sm['g_ffn2_post'], "ffn2",
                    deps, chain)
            elif block == 1:
                chain = (s1[-1], sm['g_ffn1_post'], MACARON_WEIGHT)
                dnew, g, head = mixer_bwd(dact, head, s2, wt, sm, dims, deps, chain)
            else:
                chain = (saved[l - 1][2][-1], small[l - 1]['g_ffn2_post'], MACARON_WEIGHT) if l > 0 else None
                dnew, g['w_ffn1_up'], g['w_ffn1_down'], g['g_ffn1_pre'], g['g_ffn1_post'], head = ffn_bwd(
                    dact, head, s1, sm['g_ffn1_pre'], wt['w_ffn1_up'], wt['w_ffn1_down'], sm['g_ffn1_post'], "ffn1",
                    deps, chain)
            small_grads[l].update({n: g[n] for n in g if n in SMALL})
            if pending is not None:
                swapping, done = advance(pending, swapping, done, dnew)
            pending = start_scatter(BLOCK_WEIGHTS[block], g, l, block)
            dact = dnew
    swapping, done = advance(pending, swapping, done, dact)
    finish_swap(swapping, done)

    def flat(tree):
        v = jnp.concatenate([tree[n].reshape(-1).astype(F32) for n in SMALL])
        pad = (-v.shape[0]) % (SUBLANE * LANE)
        return jnp.pad(v, (0, pad)).reshape(-1, LANE)

    g_small = all_reduce_small(flat({n: jnp.stack([small_grads[l][n] for l in range(depth)]) for n in SMALL}),
                               name="all_reduce_small")
    d_small, m_small, v_small = adam_small(g_small, flat(w), flat(mom), flat(var), name="adam_small")

    def unflat(block):
        v, out, at = block.reshape(-1), {}, 0
        for n in SMALL:
            out[n] = v[at:at + w[n].size].reshape(w[n].shape)
            at += w[n].size
        return out

    result = [{}, {}, {}, {}]
    for tree, block in zip(result, (g_small, d_small, m_small, v_small)):
        tree.update(unflat(block))
    for n in BIG:
        for tree, stack in zip(result, stacks[n]):
            tree[n] = stack.reshape(w[n].shape)
    return (loss, dact.reshape(x.shape), *[tree[n] for tree in result for n in WEIGHTS])
```
